```python
import jax, jax.numpy as jnp
from jax import lax
import numpy as np

D_MODEL = 1024
BATCH = 4
SEQ = 8192
DEPTH = 1

D_MIX = D_MODEL
ATTN_HEADS = 4
QK_NOPE_DIM = 128
QK_ROPE_DIM = 64
QK_HEAD_DIM = QK_NOPE_DIM + QK_ROPE_DIM
V_HEAD_DIM = 128
ATTN_WIDTH = ATTN_HEADS * V_HEAD_DIM
Q_LORA_RANK = 256
KV_LORA_RANK = 128
ROPE_THETA = 10000.0
Q_BLOCK = 128
LRU_WIDTH = D_MIX - ATTN_WIDTH
LRU_HEADS = 4
LRU_BLOCK = LRU_WIDTH // LRU_HEADS
CONV_WIDTH = 4
LRU_C = 8.0
IN_SPLITS = [Q_LORA_RANK,
             Q_LORA_RANK + KV_LORA_RANK,
             Q_LORA_RANK + KV_LORA_RANK + QK_ROPE_DIM,
             Q_LORA_RANK + KV_LORA_RANK + QK_ROPE_DIM + LRU_WIDTH]
IN_PROJ_WIDTH = Q_LORA_RANK + KV_LORA_RANK + QK_ROPE_DIM + 2 * LRU_WIDTH
N_GROUPS = 4
EXPERTS_PER_GROUP = 8
N_EXPERTS = N_GROUPS * EXPERTS_PER_GROUP
TOP_K = 2
D_FF_EXPERT = 256
MOE_BLOCK = 128
EPS = 1e-6

kernel_name = 'hymba_mla_rglru_hmoe_adaln_layer'


def rms_norm(x, g):
    xf = x.astype(jnp.float32)
    y = xf * lax.rsqrt(jnp.mean(xf * xf, axis=-1, keepdims=True) + EPS)
    return (y * g.astype(jnp.float32)).astype(x.dtype)


def rope_cos_sin(positions):
    inv = 1.0 / (ROPE_THETA ** (jnp.arange(0, QK_ROPE_DIM, 2, dtype=jnp.float32) / QK_ROPE_DIM))
    ang = positions.astype(jnp.float32)[..., None] * inv
    return jnp.cos(ang), jnp.sin(ang)


def apply_rope(x, cos, sin):
    xf = x.astype(jnp.float32)
    x1, x2 = jnp.split(xf, 2, axis=-1)
    c = cos[:, :, None, :]
    s = sin[:, :, None, :]
    return jnp.concatenate([x1 * c - x2 * s, x2 * c + x1 * s], axis=-1).astype(x.dtype)


def mla_attention(q_lat, kv_lat, k_rope, cos, sin, q_a_norm, w_q_b, kv_a_norm, w_kv_b, q_norm, k_norm):
    B, S, _ = q_lat.shape
    q = (rms_norm(q_lat, q_a_norm) @ w_q_b).reshape(B, S, ATTN_HEADS, QK_HEAD_DIM)
    kv = (rms_norm(kv_lat, kv_a_norm) @ w_kv_b).reshape(B, S, ATTN_HEADS, QK_NOPE_DIM + V_HEAD_DIM)
    k_nope, v = kv[..., :QK_NOPE_DIM], kv[..., QK_NOPE_DIM:]
    k_pe = jnp.broadcast_to(k_rope[:, :, None, :], (B, S, ATTN_HEADS, QK_ROPE_DIM))
    k = jnp.concatenate([k_nope, k_pe], axis=-1)
    q = rms_norm(q, q_norm)
    k = rms_norm(k, k_norm)
    q = jnp.concatenate([q[..., :QK_NOPE_DIM], apply_rope(q[..., QK_NOPE_DIM:], cos, sin)], axis=-1)
    k = jnp.concatenate([k[..., :QK_NOPE_DIM], apply_rope(k[..., QK_NOPE_DIM:], cos, sin)], axis=-1)
    qh = q.transpose(0, 2, 1, 3)
    kh = k.transpose(0, 2, 1, 3)
    vh = v.transpose(0, 2, 1, 3)
    n_blocks = S // Q_BLOCK
    q_blocks = qh.reshape(B, ATTN_HEADS, n_blocks, Q_BLOCK, QK_HEAD_DIM).transpose(2, 0, 1, 3, 4)
    key_idx = jnp.arange(S)
    scale = QK_HEAD_DIM ** -0.5

    def block(args):
        qb, bi = args
        s = jnp.einsum('bhqd,bhkd->bhqk', qb, kh, preferred_element_type=jnp.float32) * scale
        q_idx = bi * Q_BLOCK + jnp.arange(Q_BLOCK)
        s = jnp.where(key_idx[None, :] <= q_idx[:, None], s, -jnp.inf)
        p = jax.nn.softmax(s, axis=-1)
        return jnp.einsum('bhqk,bhkd->bhqd', p.astype(vh.dtype), vh)

    out = lax.map(block, (q_blocks, jnp.arange(n_blocks)))
    return out.transpose(1, 0, 3, 2, 4).reshape(B, S, ATTN_WIDTH)


def rg_lru_branch(x_lru, g_lru, positions, conv_w, conv_b, w_a, b_a, w_x, b_x, lam):
    B, S, C = x_lru.shape
    xc = lax.conv_general_dilated(x_lru, conv_w[:, None, :], window_strides=(1,),
                                  padding=[(CONV_WIDTH - 1, 0)],
                                  dimension_numbers=('NWC', 'WIO', 'NWC'),
                                  feature_group_count=C) + conv_b
    xb = xc.reshape(B, S, LRU_HEADS, LRU_BLOCK)
    r = jax.nn.sigmoid(jnp.einsum('bshi,hij->bshj', xb, w_a).reshape(B, S, C) + b_a)
    i = jax.nn.sigmoid(jnp.einsum('bshi,hij->bshj', xb, w_x).reshape(B, S, C) + b_x)
    log_a = -LRU_C * r.astype(jnp.float32) * jax.nn.softplus(-lam.astype(jnp.float32))
    a = jnp.exp(log_a)
    mult = jnp.sqrt(jnp.maximum(1.0 - jnp.exp(2.0 * log_a), 0.0))
    reset = (positions == 0)[..., None]
    a = jnp.where(reset, 0.0, a)
    mult = jnp.where(reset, 1.0, mult)
    b = mult * (i.astype(jnp.float32) * xc.astype(jnp.float32))

    def combine(lhs, rhs):
        a1, b1 = lhs
        a2, b2 = rhs
        return a1 * a2, a2 * b1 + b2

    _, h = lax.associative_scan(combine, (a, b), axis=1)
    return h.astype(x_lru.dtype) * jax.nn.gelu(g_lru)


def hierarchical_moe(h, w_rg, b_rg, w_re, b_re, w_gate, w_up, w_down):
    B, S, D = h.shape
    N = B * S
    t = h.reshape(N, D)
    g_prob = jax.nn.softmax((t @ w_rg).astype(jnp.float32) + b_rg, axis=-1)
    g_p, g_idx = lax.top_k(g_prob, 1)
    e_logits = ((t @ w_re).astype(jnp.float32) + b_re).reshape(N, N_GROUPS, EXPERTS_PER_GROUP)
    e_logits = jnp.take_along_axis(e_logits, g_idx[:, :, None], axis=1)[:, 0]
    e_p, e_idx = lax.top_k(jax.nn.softmax(e_logits, axis=-1), TOP_K)
    e_p = e_p / jnp.sum(e_p, axis=-1, keepdims=True)
    weights = g_p * e_p
    expert = g_idx * EXPERTS_PER_GROUP + e_idx
    NK = N * TOP_K
    flat_e = expert.reshape(NK)
    flat_tok = jnp.repeat(jnp.arange(N, dtype=jnp.int32), TOP_K)
    flat_w = weights.reshape(NK)
    order = jnp.argsort(flat_e)
    sorted_e = flat_e[order]
    counts = jnp.bincount(flat_e, length=N_EXPERTS)
    padded = (counts + MOE_BLOCK - 1) // MOE_BLOCK * MOE_BLOCK
    starts = jnp.cumsum(counts) - counts
    pad_ends = jnp.cumsum(padded)
    pad_starts = pad_ends - padded
    dest = pad_starts[sorted_e] + (jnp.arange(NK) - starts[sorted_e])
    cap = (NK + N_EXPERTS * (MOE_BLOCK - 1) + MOE_BLOCK - 1) // MOE_BLOCK * MOE_BLOCK
    n_blocks = cap // MOE_BLOCK
    row_tok = jnp.full((cap,), N, jnp.int32).at[dest].set(flat_tok[order])
    row_w = jnp.zeros((cap,), jnp.float32).at[dest].set(flat_w[order])
    block_e = jnp.minimum(jnp.searchsorted(pad_ends, jnp.arange(n_blocks) * MOE_BLOCK, side='right'),
                          N_EXPERTS - 1)
    t_pad = jnp.concatenate([t, jnp.zeros((1, D), t.dtype)], axis=0)

    def expert_block(args):
        rows, e = args
        xb = t_pad[rows]
        hid = jax.nn.silu(xb @ w_gate[e]) * (xb @ w_up[e])
        return hid @ w_down[e]

    y_rows = lax.map(expert_block, (row_tok.reshape(n_blocks, MOE_BLOCK), block_e))
    y_rows = y_rows.reshape(cap, D).astype(jnp.float32) * row_w[:, None]
    y = jnp.zeros((N + 1, D), jnp.float32).at[row_tok].add(y_rows)[:N]
    return y.reshape(B, S, D).astype(h.dtype)


def hybrid_layer(x, c, positions, cos, sin, w_ada, b_ada, norm1_g, w_in, q_a_norm, w_q_b, kv_a_norm,
                 w_kv_b, q_norm, k_norm, conv_w, conv_b, lru_wa, lru_ba, lru_wx, lru_bx, lru_lambda,
                 attn_out_norm, lru_out_norm, w_out, norm2_g, w_router_group, b_router_group,
                 w_router_expert, b_router_expert, w_gate, w_up, w_down):
    mod = jax.nn.silu(c) @ w_ada + b_ada
    shift1, scale1, gate1, shift2, scale2, gate2 = jnp.split(mod[:, None, :], 6, axis=-1)
    h = rms_norm(x, norm1_g) * (1.0 + scale1) + shift1
    proj = h @ w_in
    q_lat, kv_lat, k_rope, x_lru, g_lru = jnp.split(proj, IN_SPLITS, axis=-1)
    attn = mla_attention(q_lat, kv_lat, k_rope, cos, sin, q_a_norm, w_q_b, kv_a_norm, w_kv_b,
                         q_norm, k_norm)
    lru = rg_lru_branch(x_lru, g_lru, positions, conv_w, conv_b, lru_wa, lru_ba, lru_wx, lru_bx,
                        lru_lambda)
    mixed = jnp.concatenate([rms_norm(attn, attn_out_norm), rms_norm(lru, lru_out_norm)], axis=-1)
    x = x + gate1 * (mixed @ w_out)
    h2 = rms_norm(x, norm2_g) * (1.0 + scale2) + shift2
    x = x + gate2 * hierarchical_moe(h2, w_router_group, b_router_group, w_router_expert,
                                     b_router_expert, w_gate, w_up, w_down)
    return x


def setup_inputs(seed: int = 0) -> dict:
    key = jax.random.key(seed)
    ks = iter(jax.random.split(key, 40))
    L = DEPTH

    def normal(shape, scale):
        return jax.random.normal(next(ks), shape, jnp.float32) * scale

    def gain(n):
        return 1.0 + normal((L, n), 0.02)

    u = jax.random.uniform(next(ks), (L, LRU_WIDTH), jnp.float32, 0.9, 0.999)
    s = u ** (1.0 / LRU_C)
    lam = jnp.log(s) - jnp.log1p(-s)
    return {
        'x': normal((BATCH, SEQ, D_MODEL), 1.0),
        'c': normal((BATCH, D_MODEL), 1.0),
        'positions': jnp.tile(jnp.arange(SEQ, dtype=jnp.int32)[None, :], (BATCH, 1)),
        'w_ada': normal((L, D_MODEL, 6 * D_MODEL), D_MODEL ** -0.5),
        'b_ada': normal((L, 6 * D_MODEL), 0.02),
        'norm1_g': gain(D_MODEL),
        'w_in': normal((L, D_MODEL, IN_PROJ_WIDTH), D_MODEL ** -0.5),
        'q_a_norm': gain(Q_LORA_RANK),
        'w_q_b': normal((L, Q_LORA_RANK, ATTN_HEADS * QK_HEAD_DIM), Q_LORA_RANK ** -0.5),
        'kv_a_norm': gain(KV_LORA_RANK),
        'w_kv_b': normal((L, KV_LORA_RANK, ATTN_HEADS * (QK_NOPE_DIM + V_HEAD_DIM)), KV_LORA_RANK ** -0.5),
        'q_norm': gain(QK_HEAD_DIM),
        'k_norm': gain(QK_HEAD_DIM),
        'conv_w': normal((L, CONV_WIDTH, LRU_WIDTH), CONV_WIDTH ** -0.5),
        'conv_b': normal((L, LRU_WIDTH), 0.02),
        'lru_wa': normal((L, LRU_HEADS, LRU_BLOCK, LRU_BLOCK), LRU_BLOCK ** -0.5),
        'lru_ba': normal((L, LRU_WIDTH), 0.02),
        'lru_wx': normal((L, LRU_HEADS, LRU_BLOCK, LRU_BLOCK), LRU_BLOCK ** -0.5),
        'lru_bx': normal((L, LRU_WIDTH), 0.02),
        'lru_lambda': lam,
        'attn_out_norm': gain(ATTN_WIDTH),
        'lru_out_norm': gain(LRU_WIDTH),
        'w_out': normal((L, D_MIX, D_MODEL), D_MIX ** -0.5),
        'norm2_g': gain(D_MODEL),
        'w_router_group': normal((L, D_MODEL, N_GROUPS), D_MODEL ** -0.5),
        'b_router_group': normal((L, N_GROUPS), 0.01),
        'w_router_expert': normal((L, D_MODEL, N_EXPERTS), D_MODEL ** -0.5),
        'b_router_expert': normal((L, N_EXPERTS), 0.01),
        'w_gate': normal((L, N_EXPERTS, D_MODEL, D_FF_EXPERT), D_MODEL ** -0.5),
        'w_up': normal((L, N_EXPERTS, D_MODEL, D_FF_EXPERT), D_MODEL ** -0.5),
        'w_down': normal((L, N_EXPERTS, D_FF_EXPERT, D_MODEL), D_FF_EXPERT ** -0.5),
    }


def reference(x, c, positions, w_ada, b_ada, norm1_g, w_in, q_a_norm, w_q_b, kv_a_norm, w_kv_b,
              q_norm, k_norm, conv_w, conv_b, lru_wa, lru_ba, lru_wx, lru_bx, lru_lambda,
              attn_out_norm, lru_out_norm, w_out, norm2_g, w_router_group, b_router_group,
              w_router_expert, b_router_expert, w_gate, w_up, w_down):
    cos, sin = rope_cos_sin(positions)
    for l in range(DEPTH):
        x = hybrid_layer(x, c, positions, cos, sin, w_ada[l], b_ada[l], norm1_g[l], w_in[l],
                         q_a_norm[l], w_q_b[l], kv_a_norm[l], w_kv_b[l], q_norm[l], k_norm[l],
                         conv_w[l], conv_b[l], lru_wa[l], lru_ba[l], lru_wx[l], lru_bx[l],
                         lru_lambda[l], attn_out_norm[l], lru_out_norm[l], w_out[l], norm2_g[l],
                         w_router_group[l], b_router_group[l], w_router_expert[l],
                         b_router_expert[l], w_gate[l], w_up[l], w_down[l])
    return x
```

```python
import functools
import math

import numpy as np
import jax
import jax.numpy as jnp
from jax import lax
from jax.experimental import pallas as pl
from jax.experimental.pallas import tpu as pltpu

F32 = jnp.float32
BF16 = jnp.bfloat16

ATTN_HEADS = 4
QK_NOPE_DIM = 128
QK_ROPE_DIM = 64
QK_HEAD_DIM = QK_NOPE_DIM + QK_ROPE_DIM
V_HEAD_DIM = 128
ROPE_THETA = 10000.0
LRU_HEADS = 4
CONV_WIDTH = 4
LRU_C = 8.0
N_GROUPS = 4
TOP_K = 2
EPS = 1e-6

LANES = 128
SUBLANES = 8
QK_PAD_DIM = 256
VMEM_LIMIT = 56 * 1024 * 1024

PROJ_TILE = 512
ATTN_TILE = 512
LRU_TILE = 512
ROUTE_TILE = 512
MOVE_TILE = 512
EXPERT_BLOCK = 256


def _cparams(*sem):
    return pltpu.CompilerParams(dimension_semantics=sem, vmem_limit_bytes=VMEM_LIMIT)


def _rms(x, n):
    return lax.rsqrt(jnp.sum(x * x, axis=-1, keepdims=True) * (1.0 / n) + EPS)


def _column(row, width=LANES):
    t = row.shape[-1]
    return jnp.broadcast_to(row, (width, t)).T


def _adaln_kernel(c_ref, w_ref, b_ref, o_ref):
    c = c_ref[...]
    sc = c * jax.nn.sigmoid(c)
    o_ref[...] = jnp.dot(sc.astype(BF16), w_ref[...].astype(BF16),
                         preferred_element_type=F32) + b_ref[...]


def _adaln(c, w_ada, b_ada):
    b, d = c.shape
    n_out = w_ada.shape[1]
    return pl.pallas_call(
        _adaln_kernel,
        grid=(n_out // d,),
        in_specs=[pl.BlockSpec((b, d), lambda j: (0, 0)),
                  pl.BlockSpec((d, d), lambda j: (0, j)),
                  pl.BlockSpec((1, d), lambda j: (0, j))],
        out_specs=pl.BlockSpec((b, d), lambda j: (0, j)),
        out_shape=jax.ShapeDtypeStruct((b, n_out), F32),
        compiler_params=_cparams("parallel"),
        name="adaln",
    )(c, w_ada, b_ada.reshape(1, n_out))


def _rope_tables(pos_row, inv_col):
    ang = inv_col * pos_row
    row = lax.broadcasted_iota(jnp.int32, ang.shape, 0)
    half = QK_ROPE_DIM // 2
    cos = jnp.cos(ang)
    sin = jnp.sin(ang)
    c = jnp.where(row < QK_ROPE_DIM, cos, 0.0)
    sa = jnp.where(row < half, -sin, 0.0)
    sb = jnp.where((row >= half) & (row < QK_ROPE_DIM), sin, 0.0)
    return c.T, sa.T, sb.T


def _rope(x, c, sa, sb):
    return x * c + pltpu.roll(x, LANES - QK_ROPE_DIM // 2, 1) * sa + pltpu.roll(x, QK_ROPE_DIM // 2, 1) * sb


def _in_proj_kernel(x_ref, pos_ref, mod_ref, g1_ref, w_in_ref, qa_ref, wq_ref, kva_ref, wkv_ref,
                    qn_ref, kn_nope_ref, kn_rope_ref, inv_ref,
                    q_ref, k_ref, v_ref, xl_ref, gl_ref, *, q_lora, kv_lora, lru_w):
    x = x_ref[0]
    d = x.shape[-1]
    shift = mod_ref[0, 0:1, :]
    scale = mod_ref[0, 1:2, :]
    h = x * _rms(x, d) * g1_ref[...] * (1.0 + scale) + shift
    proj = jnp.dot(h.astype(BF16), w_in_ref[...], preferred_element_type=F32)
    o = 0
    q_lat = proj[:, o:o + q_lora]; o += q_lora
    kv_lat = proj[:, o:o + kv_lora]; o += kv_lora
    x_lru = proj[:, o:o + lru_w]; o += lru_w
    g_lru = proj[:, o:o + lru_w]; o += lru_w
    k_pe = proj[:, o:o + LANES]

    xl_ref[0] = x_lru
    gl_ref[0] = jax.nn.gelu(g_lru)

    c, sa, sb = _rope_tables(pos_ref[0].astype(F32), inv_ref[...])

    qn = q_lat * _rms(q_lat, q_lora) * qa_ref[...]
    q_full = jnp.dot(qn.astype(BF16), wq_ref[...], preferred_element_type=F32)
    kvn = kv_lat * _rms(kv_lat, kv_lora) * kva_ref[...]
    kv_full = jnp.dot(kvn.astype(BF16), wkv_ref[...], preferred_element_type=F32)

    sm_scale = QK_HEAD_DIM ** -0.5
    pe_ss = jnp.sum(k_pe * k_pe, axis=-1, keepdims=True)
    pe_rot = _rope(k_pe * kn_rope_ref[...], c, sa, sb)
    for hd in range(ATTN_HEADS):
        qh = q_full[:, hd * QK_PAD_DIM:(hd + 1) * QK_PAD_DIM]
        qh = qh * (_rms(qh, QK_HEAD_DIM) * sm_scale) * qn_ref[...]
        q_ref[0, hd, :, 0:QK_NOPE_DIM] = qh[:, 0:QK_NOPE_DIM].astype(BF16)
        q_ref[0, hd, :, QK_NOPE_DIM:QK_PAD_DIM] = _rope(qh[:, QK_NOPE_DIM:QK_PAD_DIM], c, sa, sb).astype(BF16)
        base = hd * (QK_NOPE_DIM + V_HEAD_DIM)
        kn = kv_full[:, base:base + QK_NOPE_DIM]
        vv = kv_full[:, base + QK_NOPE_DIM:base + QK_NOPE_DIM + V_HEAD_DIM]
        r = lax.rsqrt((jnp.sum(kn * kn, axis=-1, keepdims=True) + pe_ss) * (1.0 / QK_HEAD_DIM) + EPS)
        k_ref[0, hd, :, 0:QK_NOPE_DIM] = (kn * r * kn_nope_ref[...]).astype(BF16)
        k_ref[0, hd, :, QK_NOPE_DIM:QK_PAD_DIM] = (pe_rot * r).astype(BF16)
        v_ref[0, hd] = vv.astype(BF16)


def _in_proj(x, pos_row, mod, norm1_g, w_in_p, q_a_norm, w_q_p, kv_a_norm, w_kv, q_norm_p,
             k_norm_nope, k_norm_rope_p, inv_col, *, q_lora, kv_lora, lru_w):
    b, s, d = x.shape
    t = min(PROJ_TILE, s)
    full = lambda a: pl.BlockSpec(a.shape, lambda i, j: (0,) * a.ndim)
    kern = functools.partial(_in_proj_kernel, q_lora=q_lora, kv_lora=kv_lora, lru_w=lru_w)
    return pl.pallas_call(
        kern,
        grid=(b, s // t),
        in_specs=[pl.BlockSpec((1, t, d), lambda i, j: (i, j, 0)),
                  pl.BlockSpec((1, 1, t), lambda i, j: (i, 0, j)),
                  pl.BlockSpec((1,) + mod.shape[1:], lambda i, j: (i, 0, 0)),
                  full(norm1_g), full(w_in_p), full(q_a_norm), full(w_q_p), full(kv_a_norm),
                  full(w_kv), full(q_norm_p), full(k_norm_nope), full(k_norm_rope_p), full(inv_col)],
        out_specs=[pl.BlockSpec((1, ATTN_HEADS, t, QK_PAD_DIM), lambda i, j: (i, 0, j, 0)),
                   pl.BlockSpec((1, ATTN_HEADS, t, QK_PAD_DIM), lambda i, j: (i, 0, j, 0)),
                   pl.BlockSpec((1, ATTN_HEADS, t, V_HEAD_DIM), lambda i, j: (i, 0, j, 0)),
                   pl.BlockSpec((1, t, lru_w), lambda i, j: (i, j, 0)),
                   pl.BlockSpec((1, t, lru_w), lambda i, j: (i, j, 0))],
        out_shape=[jax.ShapeDtypeStruct((b, ATTN_HEADS, s, QK_PAD_DIM), BF16),
                   jax.ShapeDtypeStruct((b, ATTN_HEADS, s, QK_PAD_DIM), BF16),
                   jax.ShapeDtypeStruct((b, ATTN_HEADS, s, V_HEAD_DIM), BF16),
                   jax.ShapeDtypeStruct((b, s, lru_w), F32),
                   jax.ShapeDtypeStruct((b, s, lru_w), F32)],
        compiler_params=_cparams("parallel", "parallel"),
        name="in_proj",
    )(x, pos_row, mod, norm1_g, w_in_p, q_a_norm, w_q_p, kv_a_norm, w_kv, q_norm_p,
      k_norm_nope, k_norm_rope_p, inv_col)


def _attn_kernel(q_ref, k_ref, v_ref, o_ref, m_sc, l_sc, acc_sc, *, tile):
    qi = pl.program_id(2)
    q = q_ref[0, 0]
    m_sc[...] = jnp.full(m_sc.shape, -jnp.inf, F32)
    l_sc[...] = jnp.zeros(l_sc.shape, F32)
    acc_sc[...] = jnp.zeros(acc_sc.shape, F32)

    def step(j, masked):
        start = pl.multiple_of(j * tile, tile)
        k = k_ref[0, 0, pl.ds(start, tile), :]
        v = v_ref[0, 0, pl.ds(start, tile), :]
        s = lax.dot_general(q, k, (((1,), (1,)), ((), ())), preferred_element_type=F32)
        if masked:
            row = lax.broadcasted_iota(jnp.int32, s.shape, 0)
            col = lax.broadcasted_iota(jnp.int32, s.shape, 1)
            s = jnp.where(col <= row, s, -jnp.inf)
        m_old = m_sc[...]
        m_new = jnp.maximum(m_old, jnp.max(s, axis=-1, keepdims=True))
        alpha = jnp.exp(m_old - m_new)
        p = jnp.exp(s - m_new)
        l_sc[...] = alpha * l_sc[...] + jnp.sum(p, axis=-1, keepdims=True)
        acc_sc[...] = alpha * acc_sc[...] + jnp.dot(p.astype(BF16), v, preferred_element_type=F32)
        m_sc[...] = m_new

    def body(j, carry):
        step(j, False)
        return carry

    lax.fori_loop(0, qi, body, 0)
    step(qi, True)
    o_ref[0] = (acc_sc[...] / l_sc[...]).astype(o_ref.dtype)


def _attention(q, k, v):
    b, h, s, _ = q.shape
    t = min(ATTN_TILE, s)
    kern = functools.partial(_attn_kernel, tile=t)
    return pl.pallas_call(
        kern,
        grid=(b, h, s // t),
        in_specs=[pl.BlockSpec((1, 1, t, QK_PAD_DIM), lambda i, j, n: (i, j, n, 0)),
                  pl.BlockSpec((1, 1, s, QK_PAD_DIM), lambda i, j, n: (i, j, 0, 0)),
                  pl.BlockSpec((1, 1, s, V_HEAD_DIM), lambda i, j, n: (i, j, 0, 0))],
        out_specs=pl.BlockSpec((1, t, V_HEAD_DIM), lambda i, j, n: (i, n, j)),
        out_shape=jax.ShapeDtypeStruct((b, s, h * V_HEAD_DIM), BF16),
        scratch_shapes=[pltpu.VMEM((t, 1), F32), pltpu.VMEM((t, 1), F32),
                        pltpu.VMEM((t, V_HEAD_DIM), F32)],
        compiler_params=_cparams("parallel", "parallel", "arbitrary"),
        name="attention",
    )(q, k, v)


def _lru_kernel(xl_ref, gl_ref, pos_ref, cw_ref, cb_ref, wax_ref, ba_ref, bx_ref, lam_ref, gn_ref,
                o_ref, ext_sc, a_sc, b_sc, h_sc, carry_sc):
    t, c = a_sc.shape
    head_w = c // LRU_HEADS

    @pl.when(pl.program_id(1) == 0)
    def _():
        ext_sc[0:SUBLANES, :] = jnp.zeros((SUBLANES, c), F32)
        carry_sc[...] = jnp.zeros(carry_sc.shape, F32)

    x = xl_ref[0]
    ext_sc[SUBLANES:SUBLANES + t, :] = x
    xc = cb_ref[...] + x * cw_ref[CONV_WIDTH - 1:CONV_WIDTH, :]
    for back in range(1, CONV_WIDTH):
        tap = CONV_WIDTH - 1 - back
        xc = xc + ext_sc[SUBLANES - back:SUBLANES - back + t, :] * cw_ref[tap:tap + 1, :]
    ext_sc[0:SUBLANES, :] = x[t - SUBLANES:t, :]

    reset = _column((pos_ref[0] == 0).astype(F32)) > 0.5
    lam = lam_ref[...]
    neg_sp = -LRU_C * (jnp.maximum(-lam, 0.0) + jnp.log1p(jnp.exp(-jnp.abs(lam))))
    for hd in range(LRU_HEADS):
        sl = slice(hd * head_w, (hd + 1) * head_w)
        xh = xc[:, sl]
        gates = jnp.dot(xh.astype(BF16), wax_ref[hd], preferred_element_type=F32)
        r = jax.nn.sigmoid(gates[:, :head_w] + ba_ref[:, sl])
        i = jax.nn.sigmoid(gates[:, head_w:] + bx_ref[:, sl])
        log_a = r * neg_sp[:, sl]
        a = jnp.exp(log_a)
        mult = jnp.sqrt(jnp.maximum(1.0 - jnp.exp(2.0 * log_a), 0.0))
        a = jnp.where(reset, 0.0, a)
        mult = jnp.where(reset, 1.0, mult)
        a_sc[:, sl] = a
        b_sc[:, sl] = mult * (i * xh)

    srow = lax.broadcasted_iota(jnp.int32, (SUBLANES, c), 0)

    def group(g, h_prev):
        start = pl.multiple_of(g * SUBLANES, SUBLANES)
        a = a_sc[pl.ds(start, SUBLANES), :]
        bb = b_sc[pl.ds(start, SUBLANES), :]
        sh = 1
        while sh < SUBLANES:
            keep = srow >= sh
            a_prev = jnp.where(keep, pltpu.roll(a, sh, 0), 1.0)
            b_prev = jnp.where(keep, pltpu.roll(bb, sh, 0), 0.0)
            bb = bb + a * b_prev
            a = a * a_prev
            sh *= 2
        hh = bb + a * h_prev
        h_sc[pl.ds(start, SUBLANES), :] = hh
        return jnp.broadcast_to(hh[SUBLANES - 1:SUBLANES, :], (SUBLANES, c))

    carry_sc[...] = lax.fori_loop(0, t // SUBLANES, group, carry_sc[...], unroll=4)

    y = h_sc[...] * gl_ref[0]
    o_ref[0] = (y * _rms(y, c) * gn_ref[...]).astype(o_ref.dtype)


def _rg_lru(x_lru, gelu_g, pos_row, conv_w, conv_b, w_ax, b_a, b_x, lam, out_norm):
    b, s, c = x_lru.shape
    t = min(LRU_TILE, s)
    full = lambda a: pl.BlockSpec(a.shape, lambda i, j: (0,) * a.ndim)
    return pl.pallas_call(
        _lru_kernel,
        grid=(b, s // t),
        in_specs=[pl.BlockSpec((1, t, c), lambda i, j: (i, j, 0)),
                  pl.BlockSpec((1, t, c), lambda i, j: (i, j, 0)),
                  pl.BlockSpec((1, 1, t), lambda i, j: (i, 0, j)),
                  full(conv_w), full(conv_b), full(w_ax), full(b_a), full(b_x), full(lam),
                  full(out_norm)],
        out_specs=pl.BlockSpec((1, t, c), lambda i, j: (i, j, 0)),
        out_shape=jax.ShapeDtypeStruct((b, s, c), BF16),
        scratch_shapes=[pltpu.VMEM((t + SUBLANES, c), F32), pltpu.VMEM((t, c), F32),
                        pltpu.VMEM((t, c), F32), pltpu.VMEM((t, c), F32),
                        pltpu.VMEM((SUBLANES, c), F32)],
        compiler_params=_cparams("parallel", "arbitrary"),
        name="rg_lru",
    )(x_lru, gelu_g, pos_row, conv_w, conv_b, w_ax, b_a, b_x, lam, out_norm)


def _route_kernel(x_ref, at_ref, lr_ref, mod_ref, an_ref, wa_ref, wl_ref, g2_ref, wr_ref, br_ref,
                  x1_ref, h2_ref, ids_ref, rank_ref, wts_ref, cnt_ref, carry_sc,
                  *, n_experts, group_row0, expert_row0):
    @pl.when(pl.program_id(0) == 0)
    def _():
        carry_sc[...] = jnp.zeros(carry_sc.shape, F32)

    x = x_ref[0]
    t, d = x.shape
    gate1 = mod_ref[0, 2:3, :]
    shift2 = mod_ref[0, 3:4, :]
    scale2 = mod_ref[0, 4:5, :]
    at = at_ref[0].astype(F32)
    at_n = at * _rms(at, at.shape[-1]) * an_ref[...]
    mix = jnp.dot(at_n.astype(BF16), wa_ref[...], preferred_element_type=F32)
    mix = mix + jnp.dot(lr_ref[0], wl_ref[...], preferred_element_type=F32)
    x1 = x + gate1 * mix
    x1_ref[0] = x1
    h2 = x1 * _rms(x1, d) * g2_ref[...] * (1.0 + scale2) + shift2
    h2_ref[0] = h2

    logits = lax.dot_general(wr_ref[...], h2.astype(BF16), (((1,), (1,)), ((), ())),
                             preferred_element_type=F32) + br_ref[...]
    per_group = n_experts // N_GROUPS
    lg = logits[group_row0:group_row0 + N_GROUPS, :]
    grow = lax.broadcasted_iota(jnp.int32, lg.shape, 0)
    gmax = jnp.max(lg, axis=0, keepdims=True)
    g_idx = jnp.min(jnp.where(lg == gmax, grow, N_GROUPS), axis=0, keepdims=True)
    g_p = 1.0 / jnp.sum(jnp.exp(lg - gmax), axis=0, keepdims=True)
    sel = jnp.zeros((per_group, t), F32)
    for g in range(N_GROUPS):
        r0 = expert_row0 + g * per_group
        sel = sel + jnp.where(g_idx == g, logits[r0:r0 + per_group, :], 0.0)
    erow = lax.broadcasted_iota(jnp.int32, sel.shape, 0)
    m1 = jnp.max(sel, axis=0, keepdims=True)
    i1 = jnp.min(jnp.where(sel == m1, erow, per_group), axis=0, keepdims=True)
    sel2 = jnp.where(erow == i1, -jnp.inf, sel)
    m2 = jnp.max(sel2, axis=0, keepdims=True)
    i2 = jnp.min(jnp.where(sel2 == m2, erow, per_group), axis=0, keepdims=True)
    e21 = jnp.exp(m2 - m1)
    w1 = g_p / (1.0 + e21)
    w2 = g_p * e21 / (1.0 + e21)
    e1 = g_idx * per_group + i1
    e2 = g_idx * per_group + i2
    ids_ref[0:1, :] = e1
    ids_ref[1:2, :] = e2
    wts_ref[0:1, :] = w1
    wts_ref[1:2, :] = w2

    xrow = lax.broadcasted_iota(jnp.int32, (n_experts, t), 0)
    o1 = (xrow == e1).astype(F32)
    o2 = (xrow == e2).astype(F32)
    both = o1 + o2
    tr = lax.broadcasted_iota(jnp.int32, (t, t), 0)
    tc = lax.broadcasted_iota(jnp.int32, (t, t), 1)
    upper = jnp.where(tr < tc, 1.0, 0.0).astype(BF16)
    before = jnp.dot(both.astype(BF16), upper, preferred_element_type=F32) + carry_sc[...]
    rank_ref[0:1, :] = jnp.sum(o1 * before, axis=0, keepdims=True).astype(jnp.int32)
    rank_ref[1:2, :] = jnp.sum(o2 * before, axis=0, keepdims=True).astype(jnp.int32)
    carry_sc[...] = carry_sc[...] + jnp.sum(both, axis=1, keepdims=True)
    cnt_ref[...] = jnp.broadcast_to(carry_sc[...], cnt_ref.shape)


def _out_route(x, attn, lru_n, mod, attn_norm, w_out_a, w_out_l, norm2_g, w_r_t, b_r_col,
               *, n_experts, group_row0, expert_row0):
    b, s, d = x.shape
    t = min(ROUTE_TILE, s)
    nt = s // t
    n = b * s
    full = lambda a: pl.BlockSpec(a.shape, lambda i: (0,) * a.ndim)
    tok = lambda w: pl.BlockSpec((1, t, w), lambda i: (i // nt, i % nt, 0))
    kern = functools.partial(_route_kernel, n_experts=n_experts, group_row0=group_row0,
                             expert_row0=expert_row0)
    return pl.pallas_call(
        kern,
        grid=(b * nt,),
        in_specs=[tok(d), tok(attn.shape[-1]), tok(lru_n.shape[-1]),
                  pl.BlockSpec((1,) + mod.shape[1:], lambda i: (i // nt, 0, 0)),
                  full(attn_norm), full(w_out_a), full(w_out_l), full(norm2_g), full(w_r_t),
                  full(b_r_col)],
        out_specs=[tok(d), tok(d),
                   pl.BlockSpec((TOP_K, t), lambda i: (0, i)),
                   pl.BlockSpec((TOP_K, t), lambda i: (0, i)),
                   pl.BlockSpec((TOP_K, t), lambda i: (0, i)),
                   pl.BlockSpec((n_experts, LANES), lambda i: (0, 0))],
        out_shape=[jax.ShapeDtypeStruct((b, s, d), F32),
                   jax.ShapeDtypeStruct((b, s, d), F32),
                   jax.ShapeDtypeStruct((TOP_K, n), jnp.int32),
                   jax.ShapeDtypeStruct((TOP_K, n), jnp.int32),
                   jax.ShapeDtypeStruct((TOP_K, n), F32),
                   jax.ShapeDtypeStruct((n_experts, LANES), F32)],
        scratch_shapes=[pltpu.VMEM((n_experts, 1), F32)],
        compiler_params=_cparams("arbitrary"),
        name="out_route",
    )(x, attn, lru_n, mod, attn_norm, w_out_a, w_out_l, norm2_g, w_r_t, b_r_col)


def _row_copy(src, src_row, dst, dst_row, sem):
    return pltpu.make_async_copy(src.at[pl.ds(src_row, 1), :], dst.at[pl.ds(dst_row, 1), :], sem)


def _dispatch_kernel(cnt_ref, start_ref, nb_ref, dest_ref, h2_ref, xs_ref, zero_sc, sem, *, tile,
                     block, n_experts):
    i = pl.program_id(0)

    @pl.when(i == 0)
    def _():
        zero_sc[...] = jnp.zeros(zero_sc.shape, zero_sc.dtype)

        def tail_copy(j):
            return pltpu.make_async_copy(zero_sc, xs_ref.at[pl.ds(j * block, block), :], sem)

        def tail_issue(j, carry):
            tail_copy(j).start()
            return carry

        def tail_drain(j, carry):
            tail_copy(j).wait()
            return carry

        n_blocks = xs_ref.shape[0] // block
        lax.fori_loop(nb_ref[0], n_blocks, tail_issue, 0)
        lax.fori_loop(nb_ref[0], n_blocks, tail_drain, 0)

        def per_expert(e, carry):
            cnt = cnt_ref[e]
            first = start_ref[e] + cnt
            n_pad = (block - cnt % block) % block

            def issue(r, c2):
                _row_copy(zero_sc, 0, xs_ref, first + r, sem).start()
                return c2

            lax.fori_loop(0, n_pad, issue, 0)

            def drain(r, c2):
                _row_copy(zero_sc, 0, xs_ref, first, sem).wait()
                return c2

            lax.fori_loop(0, n_pad, drain, 0)
            return carry

        lax.fori_loop(0, n_experts, per_expert, 0)

    base = i * tile

    def issue(tk, carry):
        for k in range(TOP_K):
            _row_copy(h2_ref, base + tk, xs_ref, dest_ref[k, tk], sem).start()
        return carry

    lax.fori_loop(0, tile, issue, 0)

    def drain(tk, carry):
        for k in range(TOP_K):
            _row_copy(h2_ref, base, xs_ref, 0, sem).wait()
        return carry

    lax.fori_loop(0, tile, drain, 0)


def _dispatch(counts, starts, n_blocks_used, dest, h2, cap, *, block, n_experts):
    n, d = h2.shape
    t = min(MOVE_TILE, n)
    kern = functools.partial(_dispatch_kernel, tile=t, block=block, n_experts=n_experts)
    return pl.pallas_call(
        kern,
        grid_spec=pltpu.PrefetchScalarGridSpec(
            num_scalar_prefetch=3,
            grid=(n // t,),
            in_specs=[pl.BlockSpec((TOP_K, t), lambda i, c, s, nb: (0, i), memory_space=pltpu.SMEM),
                      pl.BlockSpec(memory_space=pl.ANY)],
            out_specs=pl.BlockSpec(memory_space=pl.ANY),
            scratch_shapes=[pltpu.VMEM((block, d), h2.dtype), pltpu.SemaphoreType.DMA(())]),
        out_shape=jax.ShapeDtypeStruct((cap, d), h2.dtype),
        compiler_params=_cparams("arbitrary"),
        name="dispatch",
    )(counts, starts, n_blocks_used, dest, h2)


def _expert_kernel(be_ref, nb_ref, xs_ref, wgu_ref, wdn_ref, ys_ref, *, d_ff):
    @pl.when(pl.program_id(0) < nb_ref[0])
    def _():
        xb = xs_ref[...].astype(BF16)
        gu = jnp.dot(xb, wgu_ref[0], preferred_element_type=F32)
        gt = gu[:, :d_ff]
        hid = gt * jax.nn.sigmoid(gt) * gu[:, d_ff:]
        ys_ref[...] = jnp.dot(hid.astype(BF16), wdn_ref[0], preferred_element_type=F32)

    @pl.when(pl.program_id(0) >= nb_ref[0])
    def _():
        ys_ref[...] = jnp.zeros(ys_ref.shape, ys_ref.dtype)


def _experts(block_e, n_blocks_used, xs, w_gu, w_dn, *, block):
    cap, d = xs.shape
    d_ff = w_dn.shape[1]
    kern = functools.partial(_expert_kernel, d_ff=d_ff)
    live = lambda j, nb: jnp.minimum(j, nb[0] - 1)
    return pl.pallas_call(
        kern,
        grid_spec=pltpu.PrefetchScalarGridSpec(
            num_scalar_prefetch=2,
            grid=(cap // block,),
            in_specs=[pl.BlockSpec((block, d), lambda j, be, nb: (live(j, nb), 0)),
                      pl.BlockSpec((1, d, 2 * d_ff), lambda j, be, nb: (be[live(j, nb)], 0, 0)),
                      pl.BlockSpec((1, d_ff, d), lambda j, be, nb: (be[live(j, nb)], 0, 0))],
            out_specs=pl.BlockSpec((block, d), lambda j, be, nb: (j, 0))),
        out_shape=jax.ShapeDtypeStruct((cap, d), F32),
        compiler_params=_cparams("arbitrary"),
        name="experts",
    )(block_e, n_blocks_used, xs, w_gu, w_dn)


def _combine_kernel(dest_ref, wts_ref, x1_ref, mod_ref, ys_ref, o_ref, buf0, buf1, sem, *, tile):
    def issue(tk, carry):
        _row_copy(ys_ref, dest_ref[0, tk], buf0, tk, sem).start()
        _row_copy(ys_ref, dest_ref[1, tk], buf1, tk, sem).start()
        return carry

    lax.fori_loop(0, tile, issue, 0)

    def drain(tk, carry):
        _row_copy(ys_ref, 0, buf0, 0, sem).wait()
        _row_copy(ys_ref, 0, buf1, 0, sem).wait()
        return carry

    lax.fori_loop(0, tile, drain, 0)

    d = o_ref.shape[-1]
    w0 = jnp.tile(_column(wts_ref[0:1, :]), (1, d // LANES))
    w1 = jnp.tile(_column(wts_ref[1:2, :]), (1, d // LANES))
    gate2 = mod_ref[0, 5:6, :]
    o_ref[0] = x1_ref[0] + gate2 * (w0 * buf0[...] + w1 * buf1[...])


def _combine(dest, wts, x1, mod, ys):
    b, s, d = x1.shape
    t = min(MOVE_TILE, s)
    nt = s // t
    kern = functools.partial(_combine_kernel, tile=t)
    return pl.pallas_call(
        kern,
        grid=(b * nt,),
        in_specs=[pl.BlockSpec((TOP_K, t), lambda i: (0, i), memory_space=pltpu.SMEM),
                  pl.BlockSpec((TOP_K, t), lambda i: (0, i)),
                  pl.BlockSpec((1, t, d), lambda i: (i // nt, i % nt, 0)),
                  pl.BlockSpec((1,) + mod.shape[1:], lambda i: (i // nt, 0, 0)),
                  pl.BlockSpec(memory_space=pl.ANY)],
        out_specs=pl.BlockSpec((1, t, d), lambda i: (i // nt, i % nt, 0)),
        out_shape=jax.ShapeDtypeStruct((b, s, d), F32),
        scratch_shapes=[pltpu.VMEM((t, d), F32), pltpu.VMEM((t, d), F32),
                        pltpu.SemaphoreType.DMA(())],
        compiler_params=_cparams("arbitrary"),
        name="combine",
    )(dest, wts, x1, mod, ys)


def _layer(x, c, pos_row, w_ada, b_ada, norm1_g, w_in, q_a_norm, w_q_b, kv_a_norm, w_kv_b, q_norm,
           k_norm, conv_w, conv_b, lru_wa, lru_ba, lru_wx, lru_bx, lru_lambda, attn_out_norm,
           lru_out_norm, w_out, norm2_g, w_rg, b_rg, w_re, b_re, w_gate, w_up, w_down):
    b, s, d = x.shape
    n = b * s
    q_lora = q_a_norm.shape[0]
    kv_lora = kv_a_norm.shape[0]
    lru_w = lru_lambda.shape[0]
    attn_w = ATTN_HEADS * V_HEAD_DIM
    n_experts = w_gate.shape[0]
    row = lambda v: v.reshape(1, -1)

    mod = _adaln(c, w_ada, b_ada).reshape(b, 6, d)

    o1 = q_lora
    o2 = o1 + kv_lora
    o3 = o2 + QK_ROPE_DIM
    o4 = o3 + lru_w
    w_in_p = jnp.concatenate(
        [w_in[:, :o2], w_in[:, o3:o4], w_in[:, o4:], w_in[:, o2:o3],
         jnp.zeros((d, LANES - QK_ROPE_DIM), w_in.dtype)], axis=1).astype(BF16)
    pad_h = QK_PAD_DIM - QK_HEAD_DIM
    w_q_p = jnp.pad(w_q_b.reshape(q_lora, ATTN_HEADS, QK_HEAD_DIM),
                    ((0, 0), (0, 0), (0, pad_h))).reshape(q_lora, ATTN_HEADS * QK_PAD_DIM).astype(BF16)
    q_norm_p = row(jnp.pad(q_norm, (0, pad_h)))
    k_norm_nope = row(k_norm[:QK_NOPE_DIM])
    k_norm_rope_p = row(jnp.pad(k_norm[QK_NOPE_DIM:], (0, LANES - QK_ROPE_DIM)))
    inv = 1.0 / (ROPE_THETA ** (np.arange(0, QK_ROPE_DIM, 2, dtype=np.float32) / QK_ROPE_DIM))
    inv_col = jnp.asarray(np.concatenate([inv, inv, np.zeros(LANES - QK_ROPE_DIM, np.float32)])
                          .reshape(LANES, 1))

    q, k, v, x_lru, gelu_g = _in_proj(
        x, pos_row, mod, row(norm1_g), w_in_p, row(q_a_norm), w_q_p, row(kv_a_norm),
        w_kv_b.astype(BF16), q_norm_p, k_norm_nope, k_norm_rope_p, inv_col,
        q_lora=q_lora, kv_lora=kv_lora, lru_w=lru_w)

    attn = _attention(q, k, v)

    w_ax = jnp.concatenate([lru_wa, lru_wx], axis=-1).astype(BF16)
    lru_n = _rg_lru(x_lru, gelu_g, pos_row, conv_w, row(conv_b), w_ax, row(lru_ba), row(lru_bx),
                    row(lru_lambda), row(lru_out_norm))

    group_row0, expert_row0 = 0, SUBLANES
    w_r_t = jnp.concatenate([w_rg.T, jnp.zeros((expert_row0 - N_GROUPS, d), w_rg.dtype), w_re.T],
                            axis=0).astype(BF16)
    b_r_col = jnp.concatenate([b_rg, jnp.zeros((expert_row0 - N_GROUPS,), b_rg.dtype),
                               b_re]).reshape(-1, 1)
    x1, h2, ids, ranks, wts, cnt = _out_route(
        x, attn, lru_n, mod, row(attn_out_norm), w_out[:attn_w].astype(BF16),
        w_out[attn_w:].astype(BF16), row(norm2_g), w_r_t, b_r_col,
        n_experts=n_experts, group_row0=group_row0, expert_row0=expert_row0)

    blk = EXPERT_BLOCK
    counts = cnt[:, 0].astype(jnp.int32)
    padded = (counts + blk - 1) // blk * blk
    pad_ends = jnp.cumsum(padded)
    starts = pad_ends - padded
    dest = starts[ids] + ranks
    cap = (n * TOP_K + n_experts * (blk - 1) + blk - 1) // blk * blk
    n_blocks = cap // blk
    block_e = jnp.minimum(jnp.searchsorted(pad_ends, jnp.arange(n_blocks, dtype=jnp.int32) * blk,
                                           side='right'), n_experts - 1).astype(jnp.int32)
    n_used = (pad_ends[-1:] // blk).astype(jnp.int32)

    xs = _dispatch(counts, starts.astype(jnp.int32), n_used, dest, h2.reshape(n, d), cap,
                   block=blk, n_experts=n_experts)
    w_gu = jnp.concatenate([w_gate, w_up], axis=-1).astype(BF16)
    ys = _experts(block_e, n_used, xs, w_gu, w_down.astype(BF16), block=blk)
    return _combine(dest, wts, x1, mod, ys)


def kernel(x, c, positions, w_ada, b_ada, norm1_g, w_in, q_a_norm, w_q_b, kv_a_norm, w_kv_b, q_norm, k_norm, conv_w, conv_b, lru_wa, lru_ba, lru_wx, lru_bx, lru_lambda, attn_out_norm, lru_out_norm, w_out, norm2_g, w_router_group, b_router_group, w_router_expert, b_router_expert, w_gate, w_up, w_down):
    b, s, _ = x.shape
    pos_row = positions.reshape(b, 1, s)
    for l in range(w_ada.shape[0]):
        x = _layer(x, c, pos_row, w_ada[l], b_ada[l], norm1_g[l], w_in[l], q_a_norm[l], w_q_b[l],
                   kv_a_norm[l], w_kv_b[l], q_norm[l], k_norm[l], conv_w[l], conv_b[l], lru_wa[l],
                   lru_ba[l], lru_wx[l], lru_bx[l], lru_lambda[l], attn_out_norm[l], lru_out_norm[l],
                   w_out[l], norm2_g[l], w_router_group[l], b_router_group[l], w_router_expert[l],
                   b_router_expert[l], w_gate[l], w_up[l], w_down[l])
    return x
```

```python
import functools
import math

import numpy as np
import jax
import jax.numpy as jnp
from jax import lax
from jax.experimental import pallas as pl
from jax.experimental.pallas import tpu as pltpu

F32 = jnp.float32
BF16 = jnp.bfloat16

ATTN_HEADS = 4
QK_NOPE_DIM = 128
QK_ROPE_DIM = 64
QK_HEAD_DIM = QK_NOPE_DIM + QK_ROPE_DIM
V_HEAD_DIM = 128
ROPE_THETA = 10000.0
LRU_HEADS = 4
CONV_WIDTH = 4
LRU_C = 8.0
N_GROUPS = 4
TOP_K = 2
EPS = 1e-6

LANES = 128
SUBLANES = 8
QK_PAD_DIM = 256
VMEM_LIMIT = 56 * 1024 * 1024

PROJ_TILE = 512
ATTN_TILE = 512
ATTN_CHUNK = 128
LRU_TILE = 512
ROUTE_TILE = 512
MOVE_TILE = 512
EXPERT_BLOCK = 256


def _cparams(*sem):
    return pltpu.CompilerParams(dimension_semantics=sem, vmem_limit_bytes=VMEM_LIMIT)


def _rms(x, n):
    return lax.rsqrt(jnp.sum(x * x, axis=-1, keepdims=True) * (1.0 / n) + EPS)


def _column(row, width=LANES):
    t = row.shape[-1]
    return jnp.broadcast_to(row, (width, t)).T


def _adaln_kernel(c_ref, w_ref, b_ref, o_ref):
    c = c_ref[...]
    sc = c * jax.nn.sigmoid(c)
    o_ref[...] = jnp.dot(sc.astype(BF16), w_ref[...].astype(BF16),
                         preferred_element_type=F32) + b_ref[...]


def _adaln(c, w_ada, b_ada):
    b, d = c.shape
    n_out = w_ada.shape[1]
    return pl.pallas_call(
        _adaln_kernel,
        grid=(n_out // d,),
        in_specs=[pl.BlockSpec((b, d), lambda j: (0, 0)),
                  pl.BlockSpec((d, d), lambda j: (0, j)),
                  pl.BlockSpec((1, d), lambda j: (0, j))],
        out_specs=pl.BlockSpec((b, d), lambda j: (0, j)),
        out_shape=jax.ShapeDtypeStruct((b, n_out), F32),
        compiler_params=_cparams("parallel"),
        name="adaln",
    )(c, w_ada, b_ada.reshape(1, n_out))


def _rope_tables(pos_row, inv_col):
    ang = inv_col * pos_row
    row = lax.broadcasted_iota(jnp.int32, ang.shape, 0)
    half = QK_ROPE_DIM // 2
    cos = jnp.cos(ang)
    sin = jnp.sin(ang)
    c = jnp.where(row < QK_ROPE_DIM, cos, 0.0)
    sa = jnp.where(row < half, -sin, 0.0)
    sb = jnp.where((row >= half) & (row < QK_ROPE_DIM), sin, 0.0)
    return c.T, sa.T, sb.T


def _rope(x, c, sa, sb):
    return x * c + pltpu.roll(x, LANES - QK_ROPE_DIM // 2, 1) * sa + pltpu.roll(x, QK_ROPE_DIM // 2, 1) * sb


def _in_proj_kernel(x_ref, pos_ref, mod_ref, g1_ref, w_in_ref, qa_ref, wq_ref, kva_ref, wkv_ref,
                    qn_ref, kn_nope_ref, kn_rope_ref, inv_ref,
                    q_ref, k_ref, v_ref, xl_ref, gl_ref, *, q_lora, kv_lora, lru_w):
    x = x_ref[0]
    d = x.shape[-1]
    shift = mod_ref[0, 0:1, :]
    scale = mod_ref[0, 1:2, :]
    h = x * _rms(x, d) * g1_ref[...] * (1.0 + scale) + shift
    proj = jnp.dot(h.astype(BF16), w_in_ref[...], preferred_element_type=F32)
    o = 0
    q_lat = proj[:, o:o + q_lora]; o += q_lora
    kv_lat = proj[:, o:o + kv_lora]; o += kv_lora
    x_lru = proj[:, o:o + lru_w]; o += lru_w
    g_lru = proj[:, o:o + lru_w]; o += lru_w
    k_pe = proj[:, o:o + LANES]

    xl_ref[0] = x_lru
    gl_ref[0] = jax.nn.gelu(g_lru)

    c, sa, sb = _rope_tables(pos_ref[0].astype(F32), inv_ref[...])

    qn = q_lat * _rms(q_lat, q_lora) * qa_ref[...]
    q_full = jnp.dot(qn.astype(BF16), wq_ref[...], preferred_element_type=F32)
    kvn = kv_lat * _rms(kv_lat, kv_lora) * kva_ref[...]
    kv_full = jnp.dot(kvn.astype(BF16), wkv_ref[...], preferred_element_type=F32)

    sm_scale = QK_HEAD_DIM ** -0.5 * math.log2(math.e)
    pe_ss = jnp.sum(k_pe * k_pe, axis=-1, keepdims=True)
    pe_rot = _rope(k_pe * kn_rope_ref[...], c, sa, sb)
    for hd in range(ATTN_HEADS):
        qh = q_full[:, hd * QK_PAD_DIM:(hd + 1) * QK_PAD_DIM]
        qh = qh * (_rms(qh, QK_HEAD_DIM) * sm_scale) * qn_ref[...]
        q_ref[0, hd, :, 0:QK_NOPE_DIM] = qh[:, 0:QK_NOPE_DIM].astype(BF16)
        q_ref[0, hd, :, QK_NOPE_DIM:QK_PAD_DIM] = _rope(qh[:, QK_NOPE_DIM:QK_PAD_DIM], c, sa, sb).astype(BF16)
        base = hd * (QK_NOPE_DIM + V_HEAD_DIM)
        kn = kv_full[:, base:base + QK_NOPE_DIM]
        vv = kv_full[:, base + QK_NOPE_DIM:base + QK_NOPE_DIM + V_HEAD_DIM]
        r = lax.rsqrt((jnp.sum(kn * kn, axis=-1, keepdims=True) + pe_ss) * (1.0 / QK_HEAD_DIM) + EPS)
        k_ref[0, hd, :, 0:QK_NOPE_DIM] = (kn * r * kn_nope_ref[...]).astype(BF16)
        k_ref[0, hd, :, QK_NOPE_DIM:QK_PAD_DIM] = (pe_rot * r).astype(BF16)
        v_ref[0, hd] = vv.astype(BF16)


def _in_proj(x, pos_row, mod, norm1_g, w_in_p, q_a_norm, w_q_p, kv_a_norm, w_kv, q_norm_p,
             k_norm_nope, k_norm_rope_p, inv_col, *, q_lora, kv_lora, lru_w):
    b, s, d = x.shape
    t = min(PROJ_TILE, s)
    full = lambda a: pl.BlockSpec(a.shape, lambda i, j: (0,) * a.ndim)
    kern = functools.partial(_in_proj_kernel, q_lora=q_lora, kv_lora=kv_lora, lru_w=lru_w)
    return pl.pallas_call(
        kern,
        grid=(b, s // t),
        in_specs=[pl.BlockSpec((1, t, d), lambda i, j: (i, j, 0)),
                  pl.BlockSpec((1, 1, t), lambda i, j: (i, 0, j)),
                  pl.BlockSpec((1,) + mod.shape[1:], lambda i, j: (i, 0, 0)),
                  full(norm1_g), full(w_in_p), full(q_a_norm), full(w_q_p), full(kv_a_norm),
                  full(w_kv), full(q_norm_p), full(k_norm_nope), full(k_norm_rope_p), full(inv_col)],
        out_specs=[pl.BlockSpec((1, ATTN_HEADS, t, QK_PAD_DIM), lambda i, j: (i, 0, j, 0)),
                   pl.BlockSpec((1, ATTN_HEADS, t, QK_PAD_DIM), lambda i, j: (i, 0, j, 0)),
                   pl.BlockSpec((1, ATTN_HEADS, t, V_HEAD_DIM), lambda i, j: (i, 0, j, 0)),
                   pl.BlockSpec((1, t, lru_w), lambda i, j: (i, j, 0)),
                   pl.BlockSpec((1, t, lru_w), lambda i, j: (i, j, 0))],
        out_shape=[jax.ShapeDtypeStruct((b, ATTN_HEADS, s, QK_PAD_DIM), BF16),
                   jax.ShapeDtypeStruct((b, ATTN_HEADS, s, QK_PAD_DIM), BF16),
                   jax.ShapeDtypeStruct((b, ATTN_HEADS, s, V_HEAD_DIM), BF16),
                   jax.ShapeDtypeStruct((b, s, lru_w), F32),
                   jax.ShapeDtypeStruct((b, s, lru_w), F32)],
        compiler_params=_cparams("parallel", "parallel"),
        name="in_proj",
    )(x, pos_row, mod, norm1_g, w_in_p, q_a_norm, w_q_p, kv_a_norm, w_kv, q_norm_p,
      k_norm_nope, k_norm_rope_p, inv_col)


def _attn_kernel(q_ref, k_ref, v_ref, o_ref, m_sc, l_sc, acc_sc, s_sc, *, tile, chunk):
    qi = pl.program_id(2)
    m_sc[...] = jnp.full(m_sc.shape, -jnp.inf, F32)
    l_sc[...] = jnp.zeros(l_sc.shape, F32)
    acc_sc[...] = jnp.zeros(acc_sc.shape, F32)

    def scores(j, slot):
        start = pl.multiple_of(j * tile, tile)
        k = k_ref[0, 0, pl.ds(start, tile), :]
        s_sc[slot] = lax.dot_general(q_ref[0, 0], k, (((1,), (1,)), ((), ())),
                                     preferred_element_type=F32)

    def accumulate(j, slot, masked):
        start = pl.multiple_of(j * tile, tile)
        for r in range(tile // chunk):
            rows = slice(r * chunk, (r + 1) * chunk)
            cols = (r + 1) * chunk if masked else tile
            s = s_sc[slot, rows, 0:cols]
            if masked:
                row = lax.broadcasted_iota(jnp.int32, s.shape, 0) + r * chunk
                col = lax.broadcasted_iota(jnp.int32, s.shape, 1)
                s = jnp.where(col <= row, s, -jnp.inf)
            v = v_ref[0, 0, pl.ds(start, cols), :]
            m_old = m_sc[rows, :]
            m_new = jnp.maximum(m_old, jnp.max(s, axis=-1, keepdims=True))
            alpha = jnp.exp2(m_old - m_new)
            p = jnp.exp2(s - jnp.tile(m_new, (1, cols // LANES)))
            l_sc[rows, :] = alpha * l_sc[rows, :] + jnp.sum(p, axis=-1, keepdims=True)
            acc_sc[rows, :] = alpha * acc_sc[rows, :] + jnp.dot(p.astype(BF16), v,
                                                                preferred_element_type=F32)
            m_sc[rows, :] = m_new

    scores(0, 0)

    def pair(p, carry):
        scores(2 * p + 1, 1)
        accumulate(2 * p, 0, False)
        scores(2 * p + 2, 0)
        accumulate(2 * p + 1, 1, False)
        return carry

    lax.fori_loop(0, qi // 2, pair, 0)

    @pl.when(qi % 2 == 0)
    def _():
        accumulate(qi, 0, True)

    @pl.when(qi % 2 == 1)
    def _():
        scores(qi, 1)
        accumulate(qi - 1, 0, False)
        accumulate(qi, 1, True)

    o_ref[0] = (acc_sc[...] / l_sc[...]).astype(o_ref.dtype)


def _attention(q, k, v):
    b, h, s, _ = q.shape
    t = min(ATTN_TILE, s)
    assert V_HEAD_DIM == LANES
    kern = functools.partial(_attn_kernel, tile=t, chunk=min(ATTN_CHUNK, t))
    return pl.pallas_call(
        kern,
        grid=(b, h, s // t),
        in_specs=[pl.BlockSpec((1, 1, t, QK_PAD_DIM), lambda i, j, n: (i, j, n, 0)),
                  pl.BlockSpec((1, 1, s, QK_PAD_DIM), lambda i, j, n: (i, j, 0, 0)),
                  pl.BlockSpec((1, 1, s, V_HEAD_DIM), lambda i, j, n: (i, j, 0, 0))],
        out_specs=pl.BlockSpec((1, t, V_HEAD_DIM), lambda i, j, n: (i, n, j)),
        out_shape=jax.ShapeDtypeStruct((b, s, h * V_HEAD_DIM), BF16),
        scratch_shapes=[pltpu.VMEM((t, LANES), F32), pltpu.VMEM((t, LANES), F32),
                        pltpu.VMEM((t, V_HEAD_DIM), F32), pltpu.VMEM((2, t, t), F32)],
        compiler_params=_cparams("parallel", "parallel", "arbitrary"),
        name="attention",
    )(q, k, v)


def _lru_kernel(xl_ref, gl_ref, pos_ref, cw_ref, cb_ref, wax_ref, ba_ref, bx_ref, lam_ref, gn_ref,
                o_ref, ext_sc, a_sc, b_sc, h_sc, carry_sc):
    t, c = a_sc.shape
    head_w = c // LRU_HEADS

    @pl.when(pl.program_id(1) == 0)
    def _():
        ext_sc[0:SUBLANES, :] = jnp.zeros((SUBLANES, c), F32)
        carry_sc[...] = jnp.zeros(carry_sc.shape, F32)

    x = xl_ref[0]
    ext_sc[SUBLANES:SUBLANES + t, :] = x
    xc = cb_ref[...] + x * cw_ref[CONV_WIDTH - 1:CONV_WIDTH, :]
    for back in range(1, CONV_WIDTH):
        tap = CONV_WIDTH - 1 - back
        xc = xc + ext_sc[SUBLANES - back:SUBLANES - back + t, :] * cw_ref[tap:tap + 1, :]
    ext_sc[0:SUBLANES, :] = x[t - SUBLANES:t, :]

    reset = _column((pos_ref[0] == 0).astype(F32)) > 0.5
    lam = lam_ref[...]
    neg_sp = -LRU_C * (jnp.maximum(-lam, 0.0) + jnp.log1p(jnp.exp(-jnp.abs(lam))))
    for hd in range(LRU_HEADS):
        sl = slice(hd * head_w, (hd + 1) * head_w)
        xh = xc[:, sl]
        gates = jnp.dot(xh.astype(BF16), wax_ref[hd], preferred_element_type=F32)
        r = jax.nn.sigmoid(gates[:, :head_w] + ba_ref[:, sl])
        i = jax.nn.sigmoid(gates[:, head_w:] + bx_ref[:, sl])
        log_a = r * neg_sp[:, sl]
        a = jnp.exp(log_a)
        mult = jnp.sqrt(jnp.maximum(1.0 - jnp.exp(2.0 * log_a), 0.0))
        a = jnp.where(reset, 0.0, a)
        mult = jnp.where(reset, 1.0, mult)
        a_sc[:, sl] = a
        b_sc[:, sl] = mult * (i * xh)

    srow = lax.broadcasted_iota(jnp.int32, (SUBLANES, c), 0)

    def group(g, h_prev):
        start = pl.multiple_of(g * SUBLANES, SUBLANES)
        a = a_sc[pl.ds(start, SUBLANES), :]
        bb = b_sc[pl.ds(start, SUBLANES), :]
        sh = 1
        while sh < SUBLANES:
            keep = srow >= sh
            a_prev = jnp.where(keep, pltpu.roll(a, sh, 0), 1.0)
            b_prev = jnp.where(keep, pltpu.roll(bb, sh, 0), 0.0)
            bb = bb + a * b_prev
            a = a * a_prev
            sh *= 2
        hh = bb + a * h_prev
        h_sc[pl.ds(start, SUBLANES), :] = hh
        return jnp.broadcast_to(hh[SUBLANES - 1:SUBLANES, :], (SUBLANES, c))

    carry_sc[...] = lax.fori_loop(0, t // SUBLANES, group, carry_sc[...], unroll=4)

    y = h_sc[...] * gl_ref[0]
    o_ref[0] = (y * _rms(y, c) * gn_ref[...]).astype(o_ref.dtype)


def _rg_lru(x_lru, gelu_g, pos_row, conv_w, conv_b, w_ax, b_a, b_x, lam, out_norm):
    b, s, c = x_lru.shape
    t = min(LRU_TILE, s)
    full = lambda a: pl.BlockSpec(a.shape, lambda i, j: (0,) * a.ndim)
    return pl.pallas_call(
        _lru_kernel,
        grid=(b, s // t),
        in_specs=[pl.BlockSpec((1, t, c), lambda i, j: (i, j, 0)),
                  pl.BlockSpec((1, t, c), lambda i, j: (i, j, 0)),
                  pl.BlockSpec((1, 1, t), lambda i, j: (i, 0, j)),
                  full(conv_w), full(conv_b), full(w_ax), full(b_a), full(b_x), full(lam),
                  full(out_norm)],
        out_specs=pl.BlockSpec((1, t, c), lambda i, j: (i, j, 0)),
        out_shape=jax.ShapeDtypeStruct((b, s, c), BF16),
        scratch_shapes=[pltpu.VMEM((t + SUBLANES, c), F32), pltpu.VMEM((t, c), F32),
                        pltpu.VMEM((t, c), F32), pltpu.VMEM((t, c), F32),
                        pltpu.VMEM((SUBLANES, c), F32)],
        compiler_params=_cparams("parallel", "arbitrary"),
        name="rg_lru",
    )(x_lru, gelu_g, pos_row, conv_w, conv_b, w_ax, b_a, b_x, lam, out_norm)


def _route_kernel(x_ref, at_ref, lr_ref, mod_ref, an_ref, wa_ref, wl_ref, g2_ref, wr_ref, br_ref,
                  x1_ref, h2_ref, ids_ref, rank_ref, wts_ref, cnt_ref, carry_sc,
                  *, n_experts, group_row0, expert_row0):
    @pl.when(pl.program_id(0) == 0)
    def _():
        carry_sc[...] = jnp.zeros(carry_sc.shape, F32)

    x = x_ref[0]
    t, d = x.shape
    gate1 = mod_ref[0, 2:3, :]
    shift2 = mod_ref[0, 3:4, :]
    scale2 = mod_ref[0, 4:5, :]
    at = at_ref[0].astype(F32)
    at_n = at * _rms(at, at.shape[-1]) * an_ref[...]
    mix = jnp.dot(at_n.astype(BF16), wa_ref[...], preferred_element_type=F32)
    mix = mix + jnp.dot(lr_ref[0], wl_ref[...], preferred_element_type=F32)
    x1 = x + gate1 * mix
    x1_ref[0] = x1
    h2 = x1 * _rms(x1, d) * g2_ref[...] * (1.0 + scale2) + shift2
    h2_ref[0] = h2

    logits = lax.dot_general(wr_ref[...], h2.astype(BF16), (((1,), (1,)), ((), ())),
                             preferred_element_type=F32) + br_ref[...]
    per_group = n_experts // N_GROUPS
    lg = logits[group_row0:group_row0 + N_GROUPS, :]
    grow = lax.broadcasted_iota(jnp.int32, lg.shape, 0)
    gmax = jnp.max(lg, axis=0, keepdims=True)
    g_idx = jnp.min(jnp.where(lg == gmax, grow, N_GROUPS), axis=0, keepdims=True)
    g_p = 1.0 / jnp.sum(jnp.exp(lg - gmax), axis=0, keepdims=True)
    sel = jnp.zeros((per_group, t), F32)
    for g in range(N_GROUPS):
        r0 = expert_row0 + g * per_group
        sel = sel + jnp.where(g_idx == g, logits[r0:r0 + per_group, :], 0.0)
    erow = lax.broadcasted_iota(jnp.int32, sel.shape, 0)
    m1 = jnp.max(sel, axis=0, keepdims=True)
    i1 = jnp.min(jnp.where(sel == m1, erow, per_group), axis=0, keepdims=True)
    sel2 = jnp.where(erow == i1, -jnp.inf, sel)
    m2 = jnp.max(sel2, axis=0, keepdims=True)
    i2 = jnp.min(jnp.where(sel2 == m2, erow, per_group), axis=0, keepdims=True)
    e21 = jnp.exp(m2 - m1)
    w1 = g_p / (1.0 + e21)
    w2 = g_p * e21 / (1.0 + e21)
    e1 = g_idx * per_group + i1
    e2 = g_idx * per_group + i2
    ids_ref[0:1, :] = e1
    ids_ref[1:2, :] = e2
    wts_ref[0:1, :] = w1
    wts_ref[1:2, :] = w2

    xrow = lax.broadcasted_iota(jnp.int32, (n_experts, t), 0)
    o1 = (xrow == e1).astype(F32)
    o2 = (xrow == e2).astype(F32)
    both = o1 + o2
    tr = lax.broadcasted_iota(jnp.int32, (t, t), 0)
    tc = lax.broadcasted_iota(jnp.int32, (t, t), 1)
    upper = jnp.where(tr < tc, 1.0, 0.0).astype(BF16)
    before = jnp.dot(both.astype(BF16), upper, preferred_element_type=F32) + carry_sc[...]
    rank_ref[0:1, :] = jnp.sum(o1 * before, axis=0, keepdims=True).astype(jnp.int32)
    rank_ref[1:2, :] = jnp.sum(o2 * before, axis=0, keepdims=True).astype(jnp.int32)
    carry_sc[...] = carry_sc[...] + jnp.sum(both, axis=1, keepdims=True)
    cnt_ref[...] = jnp.broadcast_to(carry_sc[...], cnt_ref.shape)


def _out_route(x, attn, lru_n, mod, attn_norm, w_out_a, w_out_l, norm2_g, w_r_t, b_r_col,
               *, n_experts, group_row0, expert_row0):
    b, s, d = x.shape
    t = min(ROUTE_TILE, s)
    nt = s // t
    n = b * s
    full = lambda a: pl.BlockSpec(a.shape, lambda i: (0,) * a.ndim)
    tok = lambda w: pl.BlockSpec((1, t, w), lambda i: (i // nt, i % nt, 0))
    kern = functools.partial(_route_kernel, n_experts=n_experts, group_row0=group_row0,
                             expert_row0=expert_row0)
    return pl.pallas_call(
        kern,
        grid=(b * nt,),
        in_specs=[tok(d), tok(attn.shape[-1]), tok(lru_n.shape[-1]),
                  pl.BlockSpec((1,) + mod.shape[1:], lambda i: (i // nt, 0, 0)),
                  full(attn_norm), full(w_out_a), full(w_out_l), full(norm2_g), full(w_r_t),
                  full(b_r_col)],
        out_specs=[tok(d), tok(d),
                   pl.BlockSpec((TOP_K, t), lambda i: (0, i)),
                   pl.BlockSpec((TOP_K, t), lambda i: (0, i)),
                   pl.BlockSpec((TOP_K, t), lambda i: (0, i)),
                   pl.BlockSpec((n_experts, LANES), lambda i: (0, 0))],
        out_shape=[jax.ShapeDtypeStruct((b, s, d), F32),
                   jax.ShapeDtypeStruct((b, s, d), F32),
                   jax.ShapeDtypeStruct((TOP_K, n), jnp.int32),
                   jax.ShapeDtypeStruct((TOP_K, n), jnp.int32),
                   jax.ShapeDtypeStruct((TOP_K, n), F32),
                   jax.ShapeDtypeStruct((n_experts, LANES), F32)],
        scratch_shapes=[pltpu.VMEM((n_experts, 1), F32)],
        compiler_params=_cparams("arbitrary"),
        name="out_route",
    )(x, attn, lru_n, mod, attn_norm, w_out_a, w_out_l, norm2_g, w_r_t, b_r_col)


def _row_copy(src, src_row, dst, dst_row, sem):
    return pltpu.make_async_copy(src.at[pl.ds(src_row, 1), :], dst.at[pl.ds(dst_row, 1), :], sem)


def _dispatch_kernel(cnt_ref, start_ref, nb_ref, ids_ref, rank_ref, h2_ref, xs_ref, zero_sc, sem, *,
                     tile, block, n_experts):
    i = pl.program_id(0)

    @pl.when(i == 0)
    def _():
        zero_sc[...] = jnp.zeros(zero_sc.shape, zero_sc.dtype)

        def tail_copy(j):
            return pltpu.make_async_copy(zero_sc, xs_ref.at[pl.ds(j * block, block), :], sem)

        def tail_issue(j, carry):
            tail_copy(j).start()
            return carry

        def tail_drain(j, carry):
            tail_copy(j).wait()
            return carry

        n_blocks = xs_ref.shape[0] // block
        lax.fori_loop(nb_ref[0], n_blocks, tail_issue, 0)
        lax.fori_loop(nb_ref[0], n_blocks, tail_drain, 0)

        def per_expert(e, carry):
            cnt = cnt_ref[e]
            first = start_ref[e] + cnt
            n_pad = (block - cnt % block) % block

            def issue(r, c2):
                _row_copy(zero_sc, 0, xs_ref, first + r, sem).start()
                return c2

            lax.fori_loop(0, n_pad, issue, 0)

            def drain(r, c2):
                _row_copy(zero_sc, 0, xs_ref, first, sem).wait()
                return c2

            lax.fori_loop(0, n_pad, drain, 0)
            return carry

        lax.fori_loop(0, n_experts, per_expert, 0)

    def issue(tk, carry):
        for k in range(TOP_K):
            dst_row = start_ref[ids_ref[k, tk]] + rank_ref[k, tk]
            _row_copy(h2_ref, tk, xs_ref, dst_row, sem).start()
        return carry

    lax.fori_loop(0, tile, issue, 0, unroll=8)

    def drain(tk, carry):
        for k in range(TOP_K):
            _row_copy(h2_ref, 0, xs_ref, 0, sem).wait()
        return carry

    lax.fori_loop(0, tile, drain, 0, unroll=8)


def _dispatch(counts, starts, n_blocks_used, ids, ranks, h2, cap, *, block, n_experts):
    n, d = h2.shape
    t = min(MOVE_TILE, n)
    kern = functools.partial(_dispatch_kernel, tile=t, block=block, n_experts=n_experts)
    smem_tile = pl.BlockSpec((TOP_K, t), lambda i, c, s, nb: (0, i), memory_space=pltpu.SMEM)
    return pl.pallas_call(
        kern,
        grid_spec=pltpu.PrefetchScalarGridSpec(
            num_scalar_prefetch=3,
            grid=(n // t,),
            in_specs=[smem_tile, smem_tile, pl.BlockSpec((t, d), lambda i, c, s, nb: (i, 0))],
            out_specs=pl.BlockSpec(memory_space=pl.ANY),
            scratch_shapes=[pltpu.VMEM((block, d), h2.dtype), pltpu.SemaphoreType.DMA(())]),
        out_shape=jax.ShapeDtypeStruct((cap, d), h2.dtype),
        compiler_params=_cparams("arbitrary"),
        name="dispatch",
    )(counts, starts, n_blocks_used, ids, ranks, h2)


def _expert_kernel(be_ref, nb_ref, xs_ref, wgu_ref, wdn_ref, ys_ref, *, d_ff):
    @pl.when(pl.program_id(0) < nb_ref[0])
    def _():
        xb = xs_ref[...].astype(BF16)
        gu = jnp.dot(xb, wgu_ref[0], preferred_element_type=F32)
        gt = gu[:, :d_ff]
        hid = gt * jax.nn.sigmoid(gt) * gu[:, d_ff:]
        ys_ref[...] = jnp.dot(hid.astype(BF16), wdn_ref[0], preferred_element_type=F32)

    @pl.when(pl.program_id(0) >= nb_ref[0])
    def _():
        ys_ref[...] = jnp.zeros(ys_ref.shape, ys_ref.dtype)


def _experts(block_e, n_blocks_used, xs, w_gu, w_dn, *, block):
    cap, d = xs.shape
    d_ff = w_dn.shape[1]
    kern = functools.partial(_expert_kernel, d_ff=d_ff)
    live = lambda j, nb: jnp.minimum(j, nb[0] - 1)
    return pl.pallas_call(
        kern,
        grid_spec=pltpu.PrefetchScalarGridSpec(
            num_scalar_prefetch=2,
            grid=(cap // block,),
            in_specs=[pl.BlockSpec((block, d), lambda j, be, nb: (live(j, nb), 0)),
                      pl.BlockSpec((1, d, 2 * d_ff), lambda j, be, nb: (be[live(j, nb)], 0, 0)),
                      pl.BlockSpec((1, d_ff, d), lambda j, be, nb: (be[live(j, nb)], 0, 0))],
            out_specs=pl.BlockSpec((block, d), lambda j, be, nb: (j, 0))),
        out_shape=jax.ShapeDtypeStruct((cap, d), F32),
        compiler_params=_cparams("arbitrary"),
        name="experts",
    )(block_e, n_blocks_used, xs, w_gu, w_dn)


def _combine_kernel(start_ref, ids_ref, rank_ref, ids_nx_ref, rank_nx_ref, wts_ref, x1_ref, mod_ref,
                    ys_ref, o_ref, buf, sem, *, tile):
    i = pl.program_id(0)
    slot = i % 2

    def gather(id_ref, rk_ref, sl):
        def issue(tk, carry):
            for k in range(TOP_K):
                src_row = start_ref[id_ref[k, tk]] + rk_ref[k, tk]
                _row_copy(ys_ref, src_row, buf.at[sl, k], tk, sem.at[sl]).start()
            return carry

        lax.fori_loop(0, tile, issue, 0, unroll=8)

    @pl.when(i == 0)
    def _():
        gather(ids_ref, rank_ref, slot)

    @pl.when(i + 1 < pl.num_programs(0))
    def _():
        gather(ids_nx_ref, rank_nx_ref, 1 - slot)

    for k in range(TOP_K):
        pltpu.make_async_copy(ys_ref.at[pl.ds(0, tile), :], buf.at[slot, k], sem.at[slot]).wait()

    d = o_ref.shape[-1]
    w0 = jnp.tile(_column(wts_ref[0:1, :]), (1, d // LANES))
    w1 = jnp.tile(_column(wts_ref[1:2, :]), (1, d // LANES))
    gate2 = mod_ref[0, 5:6, :]
    o_ref[0] = x1_ref[0] + gate2 * (w0 * buf[slot, 0] + w1 * buf[slot, 1])


def _combine(starts, ids, ranks, wts, x1, mod, ys):
    b, s, d = x1.shape
    t = min(MOVE_TILE, s)
    nt = s // t
    last = b * nt - 1
    kern = functools.partial(_combine_kernel, tile=t)
    smem_now = pl.BlockSpec((TOP_K, t), lambda i, st: (0, i), memory_space=pltpu.SMEM)
    smem_next = pl.BlockSpec((TOP_K, t), lambda i, st: (0, jnp.minimum(i + 1, last)),
                             memory_space=pltpu.SMEM)
    return pl.pallas_call(
        kern,
        grid_spec=pltpu.PrefetchScalarGridSpec(
            num_scalar_prefetch=1,
            grid=(b * nt,),
            in_specs=[smem_now, smem_now, smem_next, smem_next,
                      pl.BlockSpec((TOP_K, t), lambda i, st: (0, i)),
                      pl.BlockSpec((1, t, d), lambda i, st: (i // nt, i % nt, 0)),
                      pl.BlockSpec((1,) + mod.shape[1:], lambda i, st: (i // nt, 0, 0)),
                      pl.BlockSpec(memory_space=pl.ANY)],
            out_specs=pl.BlockSpec((1, t, d), lambda i, st: (i // nt, i % nt, 0)),
            scratch_shapes=[pltpu.VMEM((2, TOP_K, t, d), F32), pltpu.SemaphoreType.DMA((2,))]),
        out_shape=jax.ShapeDtypeStruct((b, s, d), F32),
        compiler_params=_cparams("arbitrary"),
        name="combine",
    )(starts, ids, ranks, ids, ranks, wts, x1, mod, ys)


def _layer(x, c, pos_row, w_ada, b_ada, norm1_g, w_in, q_a_norm, w_q_b, kv_a_norm, w_kv_b, q_norm,
           k_norm, conv_w, conv_b, lru_wa, lru_ba, lru_wx, lru_bx, lru_lambda, attn_out_norm,
           lru_out_norm, w_out, norm2_g, w_rg, b_rg, w_re, b_re, w_gate, w_up, w_down):
    b, s, d = x.shape
    n = b * s
    q_lora = q_a_norm.shape[0]
    kv_lora = kv_a_norm.shape[0]
    lru_w = lru_lambda.shape[0]
    attn_w = ATTN_HEADS * V_HEAD_DIM
    n_experts = w_gate.shape[0]
    row = lambda v: v.reshape(1, -1)

    mod = _adaln(c, w_ada, b_ada).reshape(b, 6, d)

    o1 = q_lora
    o2 = o1 + kv_lora
    o3 = o2 + QK_ROPE_DIM
    o4 = o3 + lru_w
    w_in_p = jnp.concatenate(
        [w_in[:, :o2], w_in[:, o3:o4], w_in[:, o4:], w_in[:, o2:o3],
         jnp.zeros((d, LANES - QK_ROPE_DIM), w_in.dtype)], axis=1).astype(BF16)
    pad_h = QK_PAD_DIM - QK_HEAD_DIM
    w_q_p = jnp.pad(w_q_b.reshape(q_lora, ATTN_HEADS, QK_HEAD_DIM),
                    ((0, 0), (0, 0), (0, pad_h))).reshape(q_lora, ATTN_HEADS * QK_PAD_DIM).astype(BF16)
    q_norm_p = row(jnp.pad(q_norm, (0, pad_h)))
    k_norm_nope = row(k_norm[:QK_NOPE_DIM])
    k_norm_rope_p = row(jnp.pad(k_norm[QK_NOPE_DIM:], (0, LANES - QK_ROPE_DIM)))
    inv = 1.0 / (ROPE_THETA ** (np.arange(0, QK_ROPE_DIM, 2, dtype=np.float32) / QK_ROPE_DIM))
    inv_col = jnp.asarray(np.concatenate([inv, inv, np.zeros(LANES - QK_ROPE_DIM, np.float32)])
                          .reshape(LANES, 1))

    q, k, v, x_lru, gelu_g = _in_proj(
        x, pos_row, mod, row(norm1_g), w_in_p, row(q_a_norm), w_q_p, row(kv_a_norm),
        w_kv_b.astype(BF16), q_norm_p, k_norm_nope, k_norm_rope_p, inv_col,
        q_lora=q_lora, kv_lora=kv_lora, lru_w=lru_w)

    attn = _attention(q, k, v)

    w_ax = jnp.concatenate([lru_wa, lru_wx], axis=-1).astype(BF16)
    lru_n = _rg_lru(x_lru, gelu_g, pos_row, conv_w, row(conv_b), w_ax, row(lru_ba), row(lru_bx),
                    row(lru_lambda), row(lru_out_norm))

    group_row0, expert_row0 = 0, SUBLANES
    w_r_t = jnp.concatenate([w_rg.T, jnp.zeros((expert_row0 - N_GROUPS, d), w_rg.dtype), w_re.T],
                            axis=0).astype(BF16)
    b_r_col = jnp.concatenate([b_rg, jnp.zeros((expert_row0 - N_GROUPS,), b_rg.dtype),
                               b_re]).reshape(-1, 1)
    x1, h2, ids, ranks, wts, cnt = _out_route(
        x, attn, lru_n, mod, row(attn_out_norm), w_out[:attn_w].astype(BF16),
        w_out[attn_w:].astype(BF16), row(norm2_g), w_r_t, b_r_col,
        n_experts=n_experts, group_row0=group_row0, expert_row0=expert_row0)

    blk = EXPERT_BLOCK
    counts = cnt[:, 0].astype(jnp.int32)
    padded = (counts + blk - 1) // blk * blk
    pad_ends = jnp.cumsum(padded)
    starts = (pad_ends - padded).astype(jnp.int32)
    cap = (n * TOP_K + n_experts * (blk - 1) + blk - 1) // blk * blk
    n_blocks = cap // blk
    block_first_row = jnp.arange(n_blocks, dtype=jnp.int32) * blk
    block_e = jnp.minimum(jnp.sum(pad_ends[None, :] <= block_first_row[:, None], axis=1),
                          n_experts - 1).astype(jnp.int32)
    n_used = (pad_ends[-1:] // blk).astype(jnp.int32)

    xs = _dispatch(counts, starts, n_used, ids, ranks, h2.reshape(n, d), cap,
                   block=blk, n_experts=n_experts)
    w_gu = jnp.concatenate([w_gate, w_up], axis=-1).astype(BF16)
    ys = _experts(block_e, n_used, xs, w_gu, w_down.astype(BF16), block=blk)
    return _combine(starts, ids, ranks, wts, x1, mod, ys)


def kernel(x, c, positions, w_ada, b_ada, norm1_g, w_in, q_a_norm, w_q_b, kv_a_norm, w_kv_b, q_norm, k_norm, conv_w, conv_b, lru_wa, lru_ba, lru_wx, lru_bx, lru_lambda, attn_out_norm, lru_out_norm, w_out, norm2_g, w_router_group, b_router_group, w_router_expert, b_router_expert, w_gate, w_up, w_down):
    b, s, _ = x.shape
    pos_row = positions.reshape(b, 1, s)
    for l in range(w_ada.shape[0]):
        x = _layer(x, c, pos_row, w_ada[l], b_ada[l], norm1_g[l], w_in[l], q_a_norm[l], w_q_b[l],
                   kv_a_norm[l], w_kv_b[l], q_norm[l], k_norm[l], conv_w[l], conv_b[l], lru_wa[l],
                   lru_ba[l], lru_wx[l], lru_bx[l], lru_lambda[l], attn_out_norm[l], lru_out_norm[l],
                   w_out[l], norm2_g[l], w_router_group[l], b_router_group[l], w_router_expert[l],
                   b_router_expert[l], w_gate[l], w_up[l], w_down[l])
    return x
```

```python
import functools
import math

import numpy as np
import jax
import jax.numpy as jnp
from jax import lax
from jax.experimental import pallas as pl
from jax.experimental.pallas import tpu as pltpu

F32 = jnp.float32
BF16 = jnp.bfloat16

ATTN_HEADS = 4
QK_NOPE_DIM = 128
QK_ROPE_DIM = 64
QK_HEAD_DIM = QK_NOPE_DIM + QK_ROPE_DIM
V_HEAD_DIM = 128
ROPE_THETA = 10000.0
LRU_HEADS = 4
CONV_WIDTH = 4
LRU_C = 8.0
N_GROUPS = 4
TOP_K = 2
EPS = 1e-6

LANES = 128
SUBLANES = 8
QK_PAD_DIM = 256
V_PAD_DIM = 256
VMEM_LIMIT = 56 * 1024 * 1024

PROJ_TILE = 512
PROJ_CHUNK = 256
ATTN_TILE = 512
ATTN_CHUNK = 128
LRU_TILE = 512
ROUTE_TILE = 512
DISPATCH_TILE = 512
COMBINE_TILE = 512
EXPERT_BLOCK = 256


def _cparams(*sem):
    return pltpu.CompilerParams(dimension_semantics=sem, vmem_limit_bytes=VMEM_LIMIT)


def _rms(x, n):
    return lax.rsqrt(jnp.sum(x * x, axis=-1, keepdims=True) * (1.0 / n) + EPS)


def _column(row, width=LANES):
    t = row.shape[-1]
    return jnp.broadcast_to(row, (width, t)).T


def _adaln_kernel(c_ref, w_ref, b_ref, o_ref):
    c = c_ref[...]
    sc = c * jax.nn.sigmoid(c)
    o_ref[...] = jnp.dot(sc.astype(BF16), w_ref[...].astype(BF16),
                         preferred_element_type=F32) + b_ref[...]


def _adaln(c, w_ada, b_ada):
    b, d = c.shape
    n_out = w_ada.shape[1]
    return pl.pallas_call(
        _adaln_kernel,
        grid=(n_out // d,),
        in_specs=[pl.BlockSpec((b, d), lambda j: (0, 0)),
                  pl.BlockSpec((d, d), lambda j: (0, j)),
                  pl.BlockSpec((1, d), lambda j: (0, j))],
        out_specs=pl.BlockSpec((b, d), lambda j: (0, j)),
        out_shape=jax.ShapeDtypeStruct((b, n_out), F32),
        compiler_params=_cparams("parallel"),
        name="adaln",
    )(c, w_ada, b_ada.reshape(1, n_out))


def _rope_tables(pos_row, inv_col):
    half = QK_ROPE_DIM // 2
    ang = inv_col * pos_row
    cos = jnp.cos(ang)
    sin = jnp.sin(ang)
    t = ang.shape[-1]
    zeros = lambda n: jnp.zeros((n, t), F32)
    c = jnp.concatenate([cos, cos, zeros(LANES - QK_ROPE_DIM)], axis=0)
    sa = jnp.concatenate([-sin, zeros(LANES - half)], axis=0)
    sb = jnp.concatenate([zeros(half), sin, zeros(LANES - QK_ROPE_DIM)], axis=0)
    return c.T, sa.T, sb.T


def _rope(x, c, sa, sb):
    return x * c + pltpu.roll(x, LANES - QK_ROPE_DIM // 2, 1) * sa + pltpu.roll(x, QK_ROPE_DIM // 2, 1) * sb


def _in_proj_kernel(x_ref, pos_ref, mod_ref, g1_ref, w_in_ref, qa_ref, wq_ref, kva_ref, wkv_ref,
                    qn_ref, kn_nope_ref, kn_rope_ref, inv_ref,
                    q_ref, k_ref, v_ref, xl_ref, gl_ref, *, q_lora, kv_lora, lru_w, chunk):
    t, d = x_ref.shape[1:]
    shift = mod_ref[0, 0:1, :]
    gain = g1_ref[...] * (1.0 + mod_ref[0, 1:2, :])
    c_all, sa_all, sb_all = _rope_tables(pos_ref[0].astype(F32), inv_ref[...])
    sm_scale = QK_HEAD_DIM ** -0.5 * math.log2(math.e)
    ones = jnp.ones((chunk, V_PAD_DIM - V_HEAD_DIM), BF16)

    for r0 in range(0, t, chunk):
        rows = slice(r0, r0 + chunk)
        x = x_ref[0, rows, :]
        h = x * _rms(x, d) * gain + shift
        proj = jnp.dot(h.astype(BF16), w_in_ref[...], preferred_element_type=F32)
        o = 0
        q_lat = proj[:, o:o + q_lora]; o += q_lora
        kv_lat = proj[:, o:o + kv_lora]; o += kv_lora
        x_lru = proj[:, o:o + lru_w]; o += lru_w
        g_lru = proj[:, o:o + lru_w]; o += lru_w
        k_pe = proj[:, o:o + LANES]

        xl_ref[0, rows, :] = x_lru
        gl_ref[0, rows, :] = jax.nn.gelu(g_lru)

        c, sa, sb = c_all[rows], sa_all[rows], sb_all[rows]
        qn = q_lat * _rms(q_lat, q_lora) * qa_ref[...]
        q_full = jnp.dot(qn.astype(BF16), wq_ref[...], preferred_element_type=F32)
        kvn = kv_lat * _rms(kv_lat, kv_lora) * kva_ref[...]
        kv_full = jnp.dot(kvn.astype(BF16), wkv_ref[...], preferred_element_type=F32)

        pe_ss = jnp.sum(k_pe * k_pe, axis=-1, keepdims=True)
        pe_rot = _rope(k_pe * kn_rope_ref[...], c, sa, sb)
        for hd in range(ATTN_HEADS):
            qh = q_full[:, hd * QK_PAD_DIM:(hd + 1) * QK_PAD_DIM]
            qh = qh * (_rms(qh, QK_HEAD_DIM) * sm_scale) * qn_ref[...]
            q_ref[0, hd, rows, 0:QK_NOPE_DIM] = qh[:, 0:QK_NOPE_DIM].astype(BF16)
            q_ref[0, hd, rows, QK_NOPE_DIM:QK_PAD_DIM] = _rope(qh[:, QK_NOPE_DIM:QK_PAD_DIM],
                                                               c, sa, sb).astype(BF16)
            base = hd * (QK_NOPE_DIM + V_HEAD_DIM)
            kn = kv_full[:, base:base + QK_NOPE_DIM]
            vv = kv_full[:, base + QK_NOPE_DIM:base + QK_NOPE_DIM + V_HEAD_DIM]
            r = lax.rsqrt((jnp.sum(kn * kn, axis=-1, keepdims=True) + pe_ss) * (1.0 / QK_HEAD_DIM) + EPS)
            k_ref[0, hd, rows, 0:QK_NOPE_DIM] = (kn * r * kn_nope_ref[...]).astype(BF16)
            k_ref[0, hd, rows, QK_NOPE_DIM:QK_PAD_DIM] = (pe_rot * r).astype(BF16)
            v_ref[0, hd, rows, 0:V_HEAD_DIM] = vv.astype(BF16)
            v_ref[0, hd, rows, V_HEAD_DIM:V_PAD_DIM] = ones


def _in_proj(x, pos_row, mod, norm1_g, w_in_p, q_a_norm, w_q_p, kv_a_norm, w_kv, q_norm_p,
             k_norm_nope, k_norm_rope_p, inv_col, *, q_lora, kv_lora, lru_w):
    b, s, d = x.shape
    t = min(PROJ_TILE, s)
    full = lambda a: pl.BlockSpec(a.shape, lambda i, j: (0,) * a.ndim)
    kern = functools.partial(_in_proj_kernel, q_lora=q_lora, kv_lora=kv_lora, lru_w=lru_w,
                             chunk=min(PROJ_CHUNK, t))
    head_out = lambda w: pl.BlockSpec((1, ATTN_HEADS, t, w), lambda i, j: (i, 0, j, 0))
    return pl.pallas_call(
        kern,
        grid=(b, s // t),
        in_specs=[pl.BlockSpec((1, t, d), lambda i, j: (i, j, 0)),
                  pl.BlockSpec((1, 1, t), lambda i, j: (i, 0, j)),
                  pl.BlockSpec((1,) + mod.shape[1:], lambda i, j: (i, 0, 0)),
                  full(norm1_g), full(w_in_p), full(q_a_norm), full(w_q_p), full(kv_a_norm),
                  full(w_kv), full(q_norm_p), full(k_norm_nope), full(k_norm_rope_p), full(inv_col)],
        out_specs=[head_out(QK_PAD_DIM), head_out(QK_PAD_DIM), head_out(V_PAD_DIM),
                   pl.BlockSpec((1, t, lru_w), lambda i, j: (i, j, 0)),
                   pl.BlockSpec((1, t, lru_w), lambda i, j: (i, j, 0))],
        out_shape=[jax.ShapeDtypeStruct((b, ATTN_HEADS, s, QK_PAD_DIM), BF16),
                   jax.ShapeDtypeStruct((b, ATTN_HEADS, s, QK_PAD_DIM), BF16),
                   jax.ShapeDtypeStruct((b, ATTN_HEADS, s, V_PAD_DIM), BF16),
                   jax.ShapeDtypeStruct((b, s, lru_w), F32),
                   jax.ShapeDtypeStruct((b, s, lru_w), F32)],
        compiler_params=_cparams("parallel", "parallel"),
        name="in_proj",
    )(x, pos_row, mod, norm1_g, w_in_p, q_a_norm, w_q_p, kv_a_norm, w_kv, q_norm_p,
      k_norm_nope, k_norm_rope_p, inv_col)


def _attn_kernel(q_ref, k_ref, v_ref, o_ref, m_sc, acc_sc, s_sc, *, tile, chunk):
    qi = pl.program_id(2)
    m_sc[...] = jnp.full(m_sc.shape, -jnp.inf, F32)
    acc_sc[...] = jnp.zeros(acc_sc.shape, F32)

    def scores(j, slot):
        start = pl.multiple_of(j * tile, tile)
        k = k_ref[0, 0, pl.ds(start, tile), :]
        s_sc[slot] = lax.dot_general(q_ref[0, 0], k, (((1,), (1,)), ((), ())),
                                     preferred_element_type=F32)

    def accumulate(j, slot, masked):
        start = pl.multiple_of(j * tile, tile)
        for r in range(tile // chunk):
            rows = slice(r * chunk, (r + 1) * chunk)
            cols = -(-(r + 1) * chunk // LANES) * LANES if masked else tile
            s = s_sc[slot, rows, 0:cols]
            if masked:
                row = lax.broadcasted_iota(jnp.int32, s.shape, 0) + r * chunk
                col = lax.broadcasted_iota(jnp.int32, s.shape, 1)
                s = jnp.where(col <= row, s, -jnp.inf)
            v = v_ref[0, 0, pl.ds(start, cols), :]
            m_old = m_sc[rows, :]
            m_new = jnp.maximum(m_old, jnp.max(s, axis=-1, keepdims=True))
            alpha = jnp.exp2(m_old - m_new)
            p = jnp.exp2(s - jnp.tile(m_new, (1, cols // LANES)))
            acc_sc[rows, :] = (jnp.tile(alpha, (1, V_PAD_DIM // LANES)) * acc_sc[rows, :]
                               + jnp.dot(p.astype(BF16), v, preferred_element_type=F32))
            m_sc[rows, :] = m_new

    scores(0, 0)

    def pair(p, carry):
        scores(2 * p + 1, 1)
        accumulate(2 * p, 0, False)
        scores(2 * p + 2, 0)
        accumulate(2 * p + 1, 1, False)
        return carry

    lax.fori_loop(0, qi // 2, pair, 0)

    @pl.when(qi % 2 == 0)
    def _():
        accumulate(qi, 0, True)

    @pl.when(qi % 2 == 1)
    def _():
        scores(qi, 1)
        accumulate(qi - 1, 0, False)
        accumulate(qi, 1, True)

    o_ref[0] = (acc_sc[:, 0:V_HEAD_DIM] / acc_sc[:, V_HEAD_DIM:2 * V_HEAD_DIM]).astype(o_ref.dtype)


def _attention(q, k, v):
    b, h, s, _ = q.shape
    t = min(ATTN_TILE, s)
    assert V_HEAD_DIM == LANES and V_PAD_DIM == 2 * V_HEAD_DIM
    kern = functools.partial(_attn_kernel, tile=t, chunk=min(ATTN_CHUNK, t))
    return pl.pallas_call(
        kern,
        grid=(b, h, s // t),
        in_specs=[pl.BlockSpec((1, 1, t, QK_PAD_DIM), lambda i, j, n: (i, j, n, 0)),
                  pl.BlockSpec((1, 1, s, QK_PAD_DIM), lambda i, j, n: (i, j, 0, 0)),
                  pl.BlockSpec((1, 1, s, V_PAD_DIM), lambda i, j, n: (i, j, 0, 0))],
        out_specs=pl.BlockSpec((1, t, V_HEAD_DIM), lambda i, j, n: (i, n, j)),
        out_shape=jax.ShapeDtypeStruct((b, s, h * V_HEAD_DIM), BF16),
        scratch_shapes=[pltpu.VMEM((t, LANES), F32), pltpu.VMEM((t, V_PAD_DIM), F32),
                        pltpu.VMEM((2, t, t), F32)],
        compiler_params=_cparams("parallel", "parallel", "arbitrary"),
        name="attention",
    )(q, k, v)


def _lru_kernel(xl_ref, gl_ref, pos_ref, cw_ref, cb_ref, wax_ref, ba_ref, bx_ref, lam_ref, gn_ref,
                o_ref, ext_sc, a_sc, b_sc, h_sc, carry_sc):
    t, c = a_sc.shape
    head_w = c // LRU_HEADS

    @pl.when(pl.program_id(1) == 0)
    def _():
        ext_sc[0:SUBLANES, :] = jnp.zeros((SUBLANES, c), F32)
        carry_sc[...] = jnp.zeros(carry_sc.shape, F32)

    x = xl_ref[0]
    ext_sc[SUBLANES:SUBLANES + t, :] = x
    xc = cb_ref[...] + x * cw_ref[CONV_WIDTH - 1:CONV_WIDTH, :]
    for back in range(1, CONV_WIDTH):
        tap = CONV_WIDTH - 1 - back
        xc = xc + ext_sc[SUBLANES - back:SUBLANES - back + t, :] * cw_ref[tap:tap + 1, :]
    ext_sc[0:SUBLANES, :] = x[t - SUBLANES:t, :]

    reset = _column((pos_ref[0] == 0).astype(F32)) > 0.5
    lam = lam_ref[...]
    neg_sp = -LRU_C * (jnp.maximum(-lam, 0.0) + jnp.log1p(jnp.exp(-jnp.abs(lam))))
    for hd in range(LRU_HEADS):
        sl = slice(hd * head_w, (hd + 1) * head_w)
        xh = xc[:, sl]
        gates = jnp.dot(xh.astype(BF16), wax_ref[hd], preferred_element_type=F32)
        r = jax.nn.sigmoid(gates[:, :head_w] + ba_ref[:, sl])
        i = jax.nn.sigmoid(gates[:, head_w:] + bx_ref[:, sl])
        log_a = r * neg_sp[:, sl]
        a = jnp.exp(log_a)
        mult = jnp.sqrt(jnp.maximum(1.0 - jnp.exp(2.0 * log_a), 0.0))
        a = jnp.where(reset, 0.0, a)
        mult = jnp.where(reset, 1.0, mult)
        a_sc[:, sl] = a
        b_sc[:, sl] = mult * (i * xh)

    srow = lax.broadcasted_iota(jnp.int32, (SUBLANES, c), 0)

    def group(g, h_prev):
        start = pl.multiple_of(g * SUBLANES, SUBLANES)
        a = a_sc[pl.ds(start, SUBLANES), :]
        bb = b_sc[pl.ds(start, SUBLANES), :]
        sh = 1
        while sh < SUBLANES:
            keep = srow >= sh
            a_prev = jnp.where(keep, pltpu.roll(a, sh, 0), 1.0)
            b_prev = jnp.where(keep, pltpu.roll(bb, sh, 0), 0.0)
            bb = bb + a * b_prev
            a = a * a_prev
            sh *= 2
        hh = bb + a * h_prev
        h_sc[pl.ds(start, SUBLANES), :] = hh
        return jnp.broadcast_to(hh[SUBLANES - 1:SUBLANES, :], (SUBLANES, c))

    carry_sc[...] = lax.fori_loop(0, t // SUBLANES, group, carry_sc[...], unroll=4)

    y = h_sc[...] * gl_ref[0]
    o_ref[0] = (y * _rms(y, c) * gn_ref[...]).astype(o_ref.dtype)


def _rg_lru(x_lru, gelu_g, pos_row, conv_w, conv_b, w_ax, b_a, b_x, lam, out_norm):
    b, s, c = x_lru.shape
    t = min(LRU_TILE, s)
    full = lambda a: pl.BlockSpec(a.shape, lambda i, j: (0,) * a.ndim)
    return pl.pallas_call(
        _lru_kernel,
        grid=(b, s // t),
        in_specs=[pl.BlockSpec((1, t, c), lambda i, j: (i, j, 0)),
                  pl.BlockSpec((1, t, c), lambda i, j: (i, j, 0)),
                  pl.BlockSpec((1, 1, t), lambda i, j: (i, 0, j)),
                  full(conv_w), full(conv_b), full(w_ax), full(b_a), full(b_x), full(lam),
                  full(out_norm)],
        out_specs=pl.BlockSpec((1, t, c), lambda i, j: (i, j, 0)),
        out_shape=jax.ShapeDtypeStruct((b, s, c), BF16),
        scratch_shapes=[pltpu.VMEM((t + SUBLANES, c), F32), pltpu.VMEM((t, c), F32),
                        pltpu.VMEM((t, c), F32), pltpu.VMEM((t, c), F32),
                        pltpu.VMEM((SUBLANES, c), F32)],
        compiler_params=_cparams("parallel", "arbitrary"),
        name="rg_lru",
    )(x_lru, gelu_g, pos_row, conv_w, conv_b, w_ax, b_a, b_x, lam, out_norm)


def _route_kernel(x_ref, at_ref, lr_ref, mod_ref, an_ref, wa_ref, wl_ref, g2_ref, wr_ref, br_ref,
                  x1_ref, h2_ref, ids_ref, rank_ref, wts_ref, cnt_ref, carry_sc,
                  *, n_experts, group_row0, expert_row0):
    @pl.when(pl.program_id(0) == 0)
    def _():
        carry_sc[...] = jnp.zeros(carry_sc.shape, F32)

    x = x_ref[0]
    t, d = x.shape
    gate1 = mod_ref[0, 2:3, :]
    shift2 = mod_ref[0, 3:4, :]
    gain2 = g2_ref[...] * (1.0 + mod_ref[0, 4:5, :])
    at = at_ref[0].astype(F32)
    at_n = at * _rms(at, at.shape[-1]) * an_ref[...]
    mix = jnp.dot(at_n.astype(BF16), wa_ref[...], preferred_element_type=F32)
    mix = mix + jnp.dot(lr_ref[0], wl_ref[...], preferred_element_type=F32)
    x1 = x + gate1 * mix
    x1_ref[0] = x1
    h2 = x1 * _rms(x1, d) * gain2 + shift2
    h2_ref[0] = h2

    logits = lax.dot_general(wr_ref[...], h2.astype(BF16), (((1,), (1,)), ((), ())),
                             preferred_element_type=F32) + br_ref[...]
    per_group = n_experts // N_GROUPS
    lg = logits[group_row0:group_row0 + N_GROUPS, :]
    grow = lax.broadcasted_iota(jnp.int32, lg.shape, 0)
    gmax = jnp.max(lg, axis=0, keepdims=True)
    g_idx = jnp.min(jnp.where(lg == gmax, grow, N_GROUPS), axis=0, keepdims=True)
    g_p = 1.0 / jnp.sum(jnp.exp(lg - gmax), axis=0, keepdims=True)
    sel = jnp.zeros((per_group, t), F32)
    for g in range(N_GROUPS):
        r0 = expert_row0 + g * per_group
        sel = sel + jnp.where(g_idx == g, logits[r0:r0 + per_group, :], 0.0)
    erow = lax.broadcasted_iota(jnp.int32, sel.shape, 0)
    m1 = jnp.max(sel, axis=0, keepdims=True)
    i1 = jnp.min(jnp.where(sel == m1, erow, per_group), axis=0, keepdims=True)
    sel2 = jnp.where(erow == i1, -jnp.inf, sel)
    m2 = jnp.max(sel2, axis=0, keepdims=True)
    i2 = jnp.min(jnp.where(sel2 == m2, erow, per_group), axis=0, keepdims=True)
    e21 = jnp.exp(m2 - m1)
    w1 = g_p / (1.0 + e21)
    w2 = g_p * e21 / (1.0 + e21)
    e1 = g_idx * per_group + i1
    e2 = g_idx * per_group + i2
    ids_ref[0:1, :] = e1
    ids_ref[1:2, :] = e2
    wts_ref[0:1, :] = w1
    wts_ref[1:2, :] = w2

    xrow = lax.broadcasted_iota(jnp.int32, (n_experts, t), 0)
    o1 = (xrow == e1).astype(F32)
    o2 = (xrow == e2).astype(F32)
    both = o1 + o2
    tr = lax.broadcasted_iota(jnp.int32, (t, t), 0)
    tc = lax.broadcasted_iota(jnp.int32, (t, t), 1)
    upper = jnp.where(tr < tc, 1.0, 0.0).astype(BF16)
    before = jnp.dot(both.astype(BF16), upper, preferred_element_type=F32) + carry_sc[...]
    rank_ref[0:1, :] = jnp.sum(o1 * before, axis=0, keepdims=True).astype(jnp.int32)
    rank_ref[1:2, :] = jnp.sum(o2 * before, axis=0, keepdims=True).astype(jnp.int32)
    carry_sc[...] = carry_sc[...] + jnp.sum(both, axis=1, keepdims=True)
    cnt_ref[...] = jnp.broadcast_to(carry_sc[...], cnt_ref.shape)


def _out_route(x, attn, lru_n, mod, attn_norm, w_out_a, w_out_l, norm2_g, w_r_t, b_r_col,
               *, n_experts, group_row0, expert_row0):
    b, s, d = x.shape
    t = min(ROUTE_TILE, s)
    nt = s // t
    n = b * s
    full = lambda a: pl.BlockSpec(a.shape, lambda i: (0,) * a.ndim)
    tok = lambda w: pl.BlockSpec((1, t, w), lambda i: (i // nt, i % nt, 0))
    kern = functools.partial(_route_kernel, n_experts=n_experts, group_row0=group_row0,
                             expert_row0=expert_row0)
    return pl.pallas_call(
        kern,
        grid=(b * nt,),
        in_specs=[tok(d), tok(attn.shape[-1]), tok(lru_n.shape[-1]),
                  pl.BlockSpec((1,) + mod.shape[1:], lambda i: (i // nt, 0, 0)),
                  full(attn_norm), full(w_out_a), full(w_out_l), full(norm2_g), full(w_r_t),
                  full(b_r_col)],
        out_specs=[tok(d), tok(d),
                   pl.BlockSpec((TOP_K, t), lambda i: (0, i)),
                   pl.BlockSpec((TOP_K, t), lambda i: (0, i)),
                   pl.BlockSpec((TOP_K, t), lambda i: (0, i)),
                   pl.BlockSpec((n_experts, LANES), lambda i: (0, 0))],
        out_shape=[jax.ShapeDtypeStruct((b, s, d), F32),
                   jax.ShapeDtypeStruct((b, s, d), F32),
                   jax.ShapeDtypeStruct((TOP_K, n), jnp.int32),
                   jax.ShapeDtypeStruct((TOP_K, n), jnp.int32),
                   jax.ShapeDtypeStruct((TOP_K, n), F32),
                   jax.ShapeDtypeStruct((n_experts, LANES), F32)],
        scratch_shapes=[pltpu.VMEM((n_experts, 1), F32)],
        compiler_params=_cparams("arbitrary"),
        name="out_route",
    )(x, attn, lru_n, mod, attn_norm, w_out_a, w_out_l, norm2_g, w_r_t, b_r_col)


def _row_copy(src, src_row, dst, dst_row, sem):
    return pltpu.make_async_copy(src.at[pl.ds(src_row, 1), :], dst.at[pl.ds(dst_row, 1), :], sem)


def _dispatch_kernel(cnt_ref, start_ref, nb_ref, dest_ref, h2_ref, xs_ref, zero_sc, sem,
                     *, tile, block, n_experts):
    i = pl.program_id(0)

    @pl.when(i == 0)
    def _():
        zero_sc[...] = jnp.zeros(zero_sc.shape, zero_sc.dtype)

        def tail_copy(j):
            return pltpu.make_async_copy(zero_sc, xs_ref.at[pl.ds(j * block, block), :], sem)

        def tail_issue(j, carry):
            tail_copy(j).start()
            return carry

        def tail_drain(j, carry):
            tail_copy(j).wait()
            return carry

        n_blocks = xs_ref.shape[0] // block
        lax.fori_loop(nb_ref[0], n_blocks, tail_issue, 0)
        lax.fori_loop(nb_ref[0], n_blocks, tail_drain, 0)

        def per_expert(e, carry):
            cnt = cnt_ref[e]
            first = start_ref[e] + cnt
            n_pad = (block - cnt % block) % block

            def issue(r, c2):
                _row_copy(zero_sc, 0, xs_ref, first + r, sem).start()
                return c2

            lax.fori_loop(0, n_pad, issue, 0)

            def drain(r, c2):
                _row_copy(zero_sc, 0, xs_ref, first, sem).wait()
                return c2

            lax.fori_loop(0, n_pad, drain, 0)
            return carry

        lax.fori_loop(0, n_experts, per_expert, 0)

    for tk in range(tile):
        for k in range(TOP_K):
            _row_copy(h2_ref, tk, xs_ref, dest_ref[k, tk], sem).start()

    def drain(tk, carry):
        for _ in range(TOP_K):
            _row_copy(h2_ref, 0, xs_ref, 0, sem).wait()
        return carry

    lax.fori_loop(0, tile, drain, 0, unroll=8)


def _dispatch(counts, starts, n_blocks_used, dest, h2, cap, *, block, n_experts):
    n, w = h2.shape
    t = min(DISPATCH_TILE, n)
    kern = functools.partial(_dispatch_kernel, tile=t, block=block, n_experts=n_experts)
    smem_tile = pl.BlockSpec((TOP_K, t), lambda i, c, s, nb: (0, i), memory_space=pltpu.SMEM)
    return pl.pallas_call(
        kern,
        grid_spec=pltpu.PrefetchScalarGridSpec(
            num_scalar_prefetch=3,
            grid=(n // t,),
            in_specs=[smem_tile, pl.BlockSpec((t, w), lambda i, c, s, nb: (i, 0))],
            out_specs=pl.BlockSpec(memory_space=pl.ANY),
            scratch_shapes=[pltpu.VMEM((block, w), h2.dtype), pltpu.SemaphoreType.DMA(())]),
        out_shape=jax.ShapeDtypeStruct((cap, w), h2.dtype),
        compiler_params=_cparams("arbitrary"),
        name="dispatch",
    )(counts, starts, n_blocks_used, dest, h2)


def _expert_kernel(be_ref, nb_ref, xs_ref, wgu_ref, wdn_ref, ys_ref, *, d_ff):
    @pl.when(pl.program_id(0) < nb_ref[0])
    def _():
        xb = xs_ref[...].astype(BF16)
        gu = jnp.dot(xb, wgu_ref[0], preferred_element_type=F32)
        gt = gu[:, :d_ff]
        hid = gt * jax.nn.sigmoid(gt) * gu[:, d_ff:]
        ys_ref[...] = jnp.dot(hid.astype(BF16), wdn_ref[0], preferred_element_type=F32)

    @pl.when(pl.program_id(0) >= nb_ref[0])
    def _():
        ys_ref[...] = jnp.zeros(ys_ref.shape, ys_ref.dtype)


def _experts(block_e, n_blocks_used, xs, w_gu, w_dn, *, block):
    cap, w = xs.shape
    d = w_dn.shape[2]
    d_ff = w_dn.shape[1]
    kern = functools.partial(_expert_kernel, d_ff=d_ff)
    live = lambda j, nb: jnp.minimum(j, nb[0] - 1)
    return pl.pallas_call(
        kern,
        grid_spec=pltpu.PrefetchScalarGridSpec(
            num_scalar_prefetch=2,
            grid=(cap // block,),
            in_specs=[pl.BlockSpec((block, w), lambda j, be, nb: (live(j, nb), 0)),
                      pl.BlockSpec((1, d, 2 * d_ff), lambda j, be, nb: (be[live(j, nb)], 0, 0)),
                      pl.BlockSpec((1, d_ff, d), lambda j, be, nb: (be[live(j, nb)], 0, 0))],
            out_specs=pl.BlockSpec((block, w), lambda j, be, nb: (j, 0))),
        out_shape=jax.ShapeDtypeStruct((cap, w), F32),
        compiler_params=_cparams("arbitrary"),
        name="experts",
    )(block_e, n_blocks_used, xs, w_gu, w_dn)


def _combine_kernel(dest_ref, dest_nx_ref, wts_ref, x1_ref, mod_ref, ys_ref, o_ref, buf, sem, *,
                    tile):
    i = pl.program_id(0)
    slot = i % 2

    def gather(d_ref, sl):
        for tk in range(tile):
            for k in range(TOP_K):
                _row_copy(ys_ref, d_ref[k, tk], buf.at[sl, k], tk, sem.at[sl]).start()

    @pl.when(i == 0)
    def _():
        gather(dest_ref, slot)

    @pl.when(i + 1 < pl.num_programs(0))
    def _():
        gather(dest_nx_ref, 1 - slot)

    for k in range(TOP_K):
        pltpu.make_async_copy(ys_ref.at[pl.ds(0, tile), :], buf.at[slot, k], sem.at[slot]).wait()

    d = o_ref.shape[-1]
    w0 = jnp.tile(_column(wts_ref[0:1, :]), (1, d // LANES))
    w1 = jnp.tile(_column(wts_ref[1:2, :]), (1, d // LANES))
    gate2 = mod_ref[0, 5:6, :]
    o_ref[0] = x1_ref[0] + gate2 * (w0 * buf[slot, 0] + w1 * buf[slot, 1])


def _combine(dest, wts, x1, mod, ys):
    b, s, d = x1.shape
    w = ys.shape[1]
    t = min(COMBINE_TILE, s)
    nt = s // t
    last = b * nt - 1
    kern = functools.partial(_combine_kernel, tile=t)
    smem_now = pl.BlockSpec((TOP_K, t), lambda i: (0, i), memory_space=pltpu.SMEM)
    smem_next = pl.BlockSpec((TOP_K, t), lambda i: (0, jnp.minimum(i + 1, last)),
                             memory_space=pltpu.SMEM)
    return pl.pallas_call(
        kern,
        grid=(b * nt,),
        in_specs=[smem_now, smem_next,
                  pl.BlockSpec((TOP_K, t), lambda i: (0, i)),
                  pl.BlockSpec((1, t, d), lambda i: (i // nt, i % nt, 0)),
                  pl.BlockSpec((1,) + mod.shape[1:], lambda i: (i // nt, 0, 0)),
                  pl.BlockSpec(memory_space=pl.ANY)],
        out_specs=pl.BlockSpec((1, t, d), lambda i: (i // nt, i % nt, 0)),
        out_shape=jax.ShapeDtypeStruct((b, s, d), F32),
        scratch_shapes=[pltpu.VMEM((2, TOP_K, t, w), F32), pltpu.SemaphoreType.DMA((2,))],
        compiler_params=_cparams("arbitrary"),
        name="combine",
    )(dest, dest, wts, x1, mod, ys)


def _layer(x, c, pos_row, w_ada, b_ada, norm1_g, w_in, q_a_norm, w_q_b, kv_a_norm, w_kv_b, q_norm,
           k_norm, conv_w, conv_b, lru_wa, lru_ba, lru_wx, lru_bx, lru_lambda, attn_out_norm,
           lru_out_norm, w_out, norm2_g, w_rg, b_rg, w_re, b_re, w_gate, w_up, w_down):
    b, s, d = x.shape
    n = b * s
    q_lora = q_a_norm.shape[0]
    kv_lora = kv_a_norm.shape[0]
    lru_w = lru_lambda.shape[0]
    attn_w = ATTN_HEADS * V_HEAD_DIM
    n_experts = w_gate.shape[0]
    row = lambda v: v.reshape(1, -1)

    mod = _adaln(c, w_ada, b_ada).reshape(b, 6, d)

    o1 = q_lora
    o2 = o1 + kv_lora
    o3 = o2 + QK_ROPE_DIM
    o4 = o3 + lru_w
    w_in_p = jnp.concatenate(
        [w_in[:, :o2], w_in[:, o3:o4], w_in[:, o4:], w_in[:, o2:o3],
         jnp.zeros((d, LANES - QK_ROPE_DIM), w_in.dtype)], axis=1).astype(BF16)
    pad_h = QK_PAD_DIM - QK_HEAD_DIM
    w_q_p = jnp.pad(w_q_b.reshape(q_lora, ATTN_HEADS, QK_HEAD_DIM),
                    ((0, 0), (0, 0), (0, pad_h))).reshape(q_lora, ATTN_HEADS * QK_PAD_DIM).astype(BF16)
    q_norm_p = row(jnp.pad(q_norm, (0, pad_h)))
    k_norm_nope = row(k_norm[:QK_NOPE_DIM])
    k_norm_rope_p = row(jnp.pad(k_norm[QK_NOPE_DIM:], (0, LANES - QK_ROPE_DIM)))
    inv = 1.0 / (ROPE_THETA ** (np.arange(0, QK_ROPE_DIM, 2, dtype=np.float32) / QK_ROPE_DIM))
    inv_col = jnp.asarray(inv.reshape(-1, 1))

    q, k, v, x_lru, gelu_g = _in_proj(
        x, pos_row, mod, row(norm1_g), w_in_p, row(q_a_norm), w_q_p, row(kv_a_norm),
        w_kv_b.astype(BF16), q_norm_p, k_norm_nope, k_norm_rope_p, inv_col,
        q_lora=q_lora, kv_lora=kv_lora, lru_w=lru_w)

    attn = _attention(q, k, v)

    w_ax = jnp.concatenate([lru_wa, lru_wx], axis=-1).astype(BF16)
    lru_n = _rg_lru(x_lru, gelu_g, pos_row, conv_w, row(conv_b), w_ax, row(lru_ba), row(lru_bx),
                    row(lru_lambda), row(lru_out_norm))

    group_row0, expert_row0 = 0, SUBLANES
    w_r_t = jnp.concatenate([w_rg.T, jnp.zeros((expert_row0 - N_GROUPS, d), w_rg.dtype), w_re.T],
                            axis=0).astype(BF16)
    b_r_col = jnp.concatenate([b_rg, jnp.zeros((expert_row0 - N_GROUPS,), b_rg.dtype),
                               b_re]).reshape(-1, 1)
    x1, h2, ids, ranks, wts, cnt = _out_route(
        x, attn, lru_n, mod, row(attn_out_norm), w_out[:attn_w].astype(BF16),
        w_out[attn_w:].astype(BF16), row(norm2_g), w_r_t, b_r_col,
        n_experts=n_experts, group_row0=group_row0, expert_row0=expert_row0)

    blk = EXPERT_BLOCK
    counts = cnt[:, 0].astype(jnp.int32)
    padded = (counts + blk - 1) // blk * blk
    pad_ends = jnp.cumsum(padded)
    starts = (pad_ends - padded).astype(jnp.int32)
    dest = ranks
    for e in range(n_experts):
        dest = dest + jnp.where(ids == e, starts[e], 0)
    cap = (n * TOP_K + n_experts * (blk - 1) + blk - 1) // blk * blk
    n_blocks = cap // blk
    block_first_row = jnp.arange(n_blocks, dtype=jnp.int32) * blk
    block_e = jnp.minimum(jnp.sum(pad_ends[None, :] <= block_first_row[:, None], axis=1),
                          n_experts - 1).astype(jnp.int32)
    n_used = (pad_ends[-1:] // blk).astype(jnp.int32)

    xs = _dispatch(counts, starts, n_used, dest, h2.reshape(n, d), cap,
                   block=blk, n_experts=n_experts)
    w_gu = jnp.concatenate([w_gate, w_up], axis=-1).astype(BF16)
    ys = _experts(block_e, n_used, xs, w_gu, w_down.astype(BF16), block=blk)
    return _combine(dest, wts, x1, mod, ys)


def kernel(x, c, positions, w_ada, b_ada, norm1_g, w_in, q_a_norm, w_q_b, kv_a_norm, w_kv_b, q_norm, k_norm, conv_w, conv_b, lru_wa, lru_ba, lru_wx, lru_bx, lru_lambda, attn_out_norm, lru_out_norm, w_out, norm2_g, w_router_group, b_router_group, w_router_expert, b_router_expert, w_gate, w_up, w_down):
    b, s, _ = x.shape
    pos_row = positions.reshape(b, 1, s)
    for l in range(w_ada.shape[0]):
        x = _layer(x, c, pos_row, w_ada[l], b_ada[l], norm1_g[l], w_in[l], q_a_norm[l], w_q_b[l],
                   kv_a_norm[l], w_kv_b[l], q_norm[l], k_norm[l], conv_w[l], conv_b[l], lru_wa[l],
                   lru_ba[l], lru_wx[l], lru_bx[l], lru_lambda[l], attn_out_norm[l], lru_out_norm[l],
                   w_out[l], norm2_g[l], w_router_group[l], b_router_group[l], w_router_expert[l],
                   b_router_expert[l], w_gate[l], w_up[l], w_down[l])
    return x
```

```python
import functools
import math

import numpy as np
import jax
import jax.numpy as jnp
from jax import lax
from jax.experimental import pallas as pl
from jax.experimental.pallas import tpu as pltpu

F32 = jnp.float32
BF16 = jnp.bfloat16

ATTN_HEADS = 4
QK_NOPE_DIM = 128
QK_ROPE_DIM = 64
QK_HEAD_DIM = QK_NOPE_DIM + QK_ROPE_DIM
V_HEAD_DIM = 128
ROPE_THETA = 10000.0
LRU_HEADS = 4
CONV_WIDTH = 4
LRU_C = 8.0
N_GROUPS = 4
TOP_K = 2
EPS = 1e-6

LANES = 128
SUBLANES = 8
QK_PAD_DIM = 256
V_PAD_DIM = 256
VMEM_LIMIT = 56 * 1024 * 1024

PROJ_TILE = 512
PROJ_CHUNK = 128
ATTN_TILE = 1024
ATTN_CHUNK = 128
LRU_TILE = 512
ROUTE_TILE = 512
DISPATCH_TILE = 512
COMBINE_TILE = 512
EXPERT_BLOCK = 256


def _cparams(*sem):
    return pltpu.CompilerParams(dimension_semantics=sem, vmem_limit_bytes=VMEM_LIMIT)


def _rms(x, n):
    return lax.rsqrt(jnp.sum(x * x, axis=-1, keepdims=True) * (1.0 / n) + EPS)


def _column(row, width=LANES):
    t = row.shape[-1]
    return jnp.broadcast_to(row, (width, t)).T


def _adaln_kernel(c_ref, w_ref, b_ref, o_ref):
    c = c_ref[...]
    sc = c * jax.nn.sigmoid(c)
    o_ref[...] = jnp.dot(sc.astype(BF16), w_ref[...].astype(BF16),
                         preferred_element_type=F32) + b_ref[...]


def _adaln(c, w_ada, b_ada):
    b, d = c.shape
    n_out = w_ada.shape[1]
    return pl.pallas_call(
        _adaln_kernel,
        grid=(n_out // d,),
        in_specs=[pl.BlockSpec((b, d), lambda j: (0, 0)),
                  pl.BlockSpec((d, d), lambda j: (0, j)),
                  pl.BlockSpec((1, d), lambda j: (0, j))],
        out_specs=pl.BlockSpec((b, d), lambda j: (0, j)),
        out_shape=jax.ShapeDtypeStruct((b, n_out), F32),
        compiler_params=_cparams("parallel"),
        name="adaln",
    )(c, w_ada, b_ada.reshape(1, n_out))


def _rope_tables(pos_row, inv_col):
    half = QK_ROPE_DIM // 2
    ang = inv_col * pos_row
    cos = jnp.cos(ang)
    sin = jnp.sin(ang)
    t = ang.shape[-1]
    zeros = lambda n: jnp.zeros((n, t), F32)
    c = jnp.concatenate([cos, cos, zeros(LANES - QK_ROPE_DIM)], axis=0)
    sa = jnp.concatenate([-sin, zeros(LANES - half)], axis=0)
    sb = jnp.concatenate([zeros(half), sin, zeros(LANES - QK_ROPE_DIM)], axis=0)
    return c.T, sa.T, sb.T


def _rope(x, c, sa, sb):
    return x * c + pltpu.roll(x, LANES - QK_ROPE_DIM // 2, 1) * sa + pltpu.roll(x, QK_ROPE_DIM // 2, 1) * sb


def _in_proj_kernel(x_ref, pos_ref, mod_ref, g1_ref, w_in_ref, qa_ref, wq_ref, kva_ref, wkv_ref,
                    qn_ref, kn_nope_ref, kn_rope_ref, inv_ref,
                    q_ref, k_ref, v_ref, xl_ref, gl_ref, *, q_lora, kv_lora, lru_w, chunk):
    t, d = x_ref.shape[1:]
    shift = mod_ref[0, 0:1, :]
    gain = g1_ref[...] * (1.0 + mod_ref[0, 1:2, :])
    c_all, sa_all, sb_all = _rope_tables(pos_ref[0].astype(F32), inv_ref[...])
    sm_scale = QK_HEAD_DIM ** -0.5 * math.log2(math.e)
    ones = jnp.ones((chunk, V_PAD_DIM - V_HEAD_DIM), BF16)

    for r0 in range(0, t, chunk):
        rows = slice(r0, r0 + chunk)
        x = x_ref[0, rows, :]
        h = x * _rms(x, d) * gain + shift
        proj = jnp.dot(h.astype(BF16), w_in_ref[...], preferred_element_type=F32)
        o = 0
        q_lat = proj[:, o:o + q_lora]; o += q_lora
        kv_lat = proj[:, o:o + kv_lora]; o += kv_lora
        x_lru = proj[:, o:o + lru_w]; o += lru_w
        g_lru = proj[:, o:o + lru_w]; o += lru_w
        k_pe = proj[:, o:o + LANES]

        xl_ref[0, rows, :] = x_lru
        gl_ref[0, rows, :] = jax.nn.gelu(g_lru)

        c, sa, sb = c_all[rows], sa_all[rows], sb_all[rows]
        qn = q_lat * _rms(q_lat, q_lora) * qa_ref[...]
        q_full = jnp.dot(qn.astype(BF16), wq_ref[...], preferred_element_type=F32)
        kvn = kv_lat * _rms(kv_lat, kv_lora) * kva_ref[...]
        kv_full = jnp.dot(kvn.astype(BF16), wkv_ref[...], preferred_element_type=F32)

        pe_ss = jnp.sum(k_pe * k_pe, axis=-1, keepdims=True)
        pe_rot = _rope(k_pe * kn_rope_ref[...], c, sa, sb)
        for hd in range(ATTN_HEADS):
            qh = q_full[:, hd * QK_PAD_DIM:(hd + 1) * QK_PAD_DIM]
            qh = qh * (_rms(qh, QK_HEAD_DIM) * sm_scale) * qn_ref[...]
            q_ref[0, hd, rows, 0:QK_NOPE_DIM] = qh[:, 0:QK_NOPE_DIM].astype(BF16)
            q_ref[0, hd, rows, QK_NOPE_DIM:QK_PAD_DIM] = _rope(qh[:, QK_NOPE_DIM:QK_PAD_DIM],
                                                               c, sa, sb).astype(BF16)
            base = hd * (QK_NOPE_DIM + V_HEAD_DIM)
            kn = kv_full[:, base:base + QK_NOPE_DIM]
            vv = kv_full[:, base + QK_NOPE_DIM:base + QK_NOPE_DIM + V_HEAD_DIM]
            r = lax.rsqrt((jnp.sum(kn * kn, axis=-1, keepdims=True) + pe_ss) * (1.0 / QK_HEAD_DIM) + EPS)
            k_ref[0, hd, rows, 0:QK_NOPE_DIM] = (kn * r * kn_nope_ref[...]).astype(BF16)
            k_ref[0, hd, rows, QK_NOPE_DIM:QK_PAD_DIM] = (pe_rot * r).astype(BF16)
            v_ref[0, hd, rows, 0:V_HEAD_DIM] = vv.astype(BF16)
            v_ref[0, hd, rows, V_HEAD_DIM:V_PAD_DIM] = ones


def _in_proj(x, pos_row, mod, norm1_g, w_in_p, q_a_norm, w_q_p, kv_a_norm, w_kv, q_norm_p,
             k_norm_nope, k_norm_rope_p, inv_col, *, q_lora, kv_lora, lru_w):
    b, s, d = x.shape
    t = min(PROJ_TILE, s)
    full = lambda a: pl.BlockSpec(a.shape, lambda i, j: (0,) * a.ndim)
    kern = functools.partial(_in_proj_kernel, q_lora=q_lora, kv_lora=kv_lora, lru_w=lru_w,
                             chunk=min(PROJ_CHUNK, t))
    head_out = lambda w: pl.BlockSpec((1, ATTN_HEADS, t, w), lambda i, j: (i, 0, j, 0))
    return pl.pallas_call(
        kern,
        grid=(b, s // t),
        in_specs=[pl.BlockSpec((1, t, d), lambda i, j: (i, j, 0)),
                  pl.BlockSpec((1, 1, t), lambda i, j: (i, 0, j)),
                  pl.BlockSpec((1,) + mod.shape[1:], lambda i, j: (i, 0, 0)),
                  full(norm1_g), full(w_in_p), full(q_a_norm), full(w_q_p), full(kv_a_norm),
                  full(w_kv), full(q_norm_p), full(k_norm_nope), full(k_norm_rope_p), full(inv_col)],
        out_specs=[head_out(QK_PAD_DIM), head_out(QK_PAD_DIM), head_out(V_PAD_DIM),
                   pl.BlockSpec((1, t, lru_w), lambda i, j: (i, j, 0)),
                   pl.BlockSpec((1, t, lru_w), lambda i, j: (i, j, 0))],
        out_shape=[jax.ShapeDtypeStruct((b, ATTN_HEADS, s, QK_PAD_DIM), BF16),
                   jax.ShapeDtypeStruct((b, ATTN_HEADS, s, QK_PAD_DIM), BF16),
                   jax.ShapeDtypeStruct((b, ATTN_HEADS, s, V_PAD_DIM), BF16),
                   jax.ShapeDtypeStruct((b, s, lru_w), F32),
                   jax.ShapeDtypeStruct((b, s, lru_w), F32)],
        compiler_params=_cparams("parallel", "parallel"),
        name="in_proj",
    )(x, pos_row, mod, norm1_g, w_in_p, q_a_norm, w_q_p, kv_a_norm, w_kv, q_norm_p,
      k_norm_nope, k_norm_rope_p, inv_col)


def _attn_kernel(q_ref, k_ref, v_ref, o_ref, m_sc, acc_sc, s_sc, *, tile, chunk):
    qi = pl.program_id(2)
    m_sc[...] = jnp.full(m_sc.shape, -jnp.inf, F32)
    acc_sc[...] = jnp.zeros(acc_sc.shape, F32)

    def scores(j, slot):
        start = pl.multiple_of(j * tile, tile)
        k = k_ref[0, 0, pl.ds(start, tile), :]
        s_sc[slot] = lax.dot_general(q_ref[0, 0], k, (((1,), (1,)), ((), ())),
                                     preferred_element_type=F32)

    def accumulate(j, slot, masked):
        start = pl.multiple_of(j * tile, tile)
        for r in range(tile // chunk):
            rows = slice(r * chunk, (r + 1) * chunk)
            cols = -(-(r + 1) * chunk // LANES) * LANES if masked else tile
            s = s_sc[slot, rows, 0:cols]
            if masked:
                row = lax.broadcasted_iota(jnp.int32, s.shape, 0) + r * chunk
                col = lax.broadcasted_iota(jnp.int32, s.shape, 1)
                s = jnp.where(col <= row, s, -jnp.inf)
            v = v_ref[0, 0, pl.ds(start, cols), :]
            m_old = m_sc[rows, :]
            m_new = jnp.maximum(m_old, jnp.max(s, axis=-1, keepdims=True))
            alpha = jnp.exp2(m_old - m_new)
            p = jnp.exp2(s - jnp.tile(m_new, (1, cols // LANES)))
            acc_sc[rows, :] = (jnp.tile(alpha, (1, V_PAD_DIM // LANES)) * acc_sc[rows, :]
                               + jnp.dot(p.astype(BF16), v, preferred_element_type=F32))
            m_sc[rows, :] = m_new

    scores(0, 0)

    def pair(p, carry):
        scores(2 * p + 1, 1)
        accumulate(2 * p, 0, False)
        scores(2 * p + 2, 0)
        accumulate(2 * p + 1, 1, False)
        return carry

    lax.fori_loop(0, qi // 2, pair, 0)

    @pl.when(qi % 2 == 0)
    def _():
        accumulate(qi, 0, True)

    @pl.when(qi % 2 == 1)
    def _():
        scores(qi, 1)
        accumulate(qi - 1, 0, False)
        accumulate(qi, 1, True)

    o_ref[0] = (acc_sc[:, 0:V_HEAD_DIM] / acc_sc[:, V_HEAD_DIM:2 * V_HEAD_DIM]).astype(o_ref.dtype)


def _attention(q, k, v):
    b, h, s, _ = q.shape
    t = min(ATTN_TILE, s)
    assert V_HEAD_DIM == LANES and V_PAD_DIM == 2 * V_HEAD_DIM
    kern = functools.partial(_attn_kernel, tile=t, chunk=min(ATTN_CHUNK, t))
    return pl.pallas_call(
        kern,
        grid=(b, h, s // t),
        in_specs=[pl.BlockSpec((1, 1, t, QK_PAD_DIM), lambda i, j, n: (i, j, n, 0)),
                  pl.BlockSpec((1, 1, s, QK_PAD_DIM), lambda i, j, n: (i, j, 0, 0)),
                  pl.BlockSpec((1, 1, s, V_PAD_DIM), lambda i, j, n: (i, j, 0, 0))],
        out_specs=pl.BlockSpec((1, t, V_HEAD_DIM), lambda i, j, n: (i, n, j)),
        out_shape=jax.ShapeDtypeStruct((b, s, h * V_HEAD_DIM), BF16),
        scratch_shapes=[pltpu.VMEM((t, LANES), F32), pltpu.VMEM((t, V_PAD_DIM), F32),
                        pltpu.VMEM((2, t, t), F32)],
        compiler_params=_cparams("parallel", "parallel", "arbitrary"),
        name="attention",
    )(q, k, v)


def _lru_kernel(xl_ref, gl_ref, pos_ref, cw_ref, cb_ref, wax_ref, ba_ref, bx_ref, lam_ref, gn_ref,
                o_ref, ext_sc, a_sc, b_sc, h_sc, carry_sc):
    t, c = a_sc.shape
    head_w = c // LRU_HEADS

    @pl.when(pl.program_id(1) == 0)
    def _():
        ext_sc[0:SUBLANES, :] = jnp.zeros((SUBLANES, c), F32)
        carry_sc[...] = jnp.zeros(carry_sc.shape, F32)

    x = xl_ref[0]
    ext_sc[SUBLANES:SUBLANES + t, :] = x
    xc = cb_ref[...] + x * cw_ref[CONV_WIDTH - 1:CONV_WIDTH, :]
    for back in range(1, CONV_WIDTH):
        tap = CONV_WIDTH - 1 - back
        xc = xc + ext_sc[SUBLANES - back:SUBLANES - back + t, :] * cw_ref[tap:tap + 1, :]
    ext_sc[0:SUBLANES, :] = x[t - SUBLANES:t, :]

    reset = _column((pos_ref[0] == 0).astype(F32)) > 0.5
    lam = lam_ref[...]
    neg_sp = -LRU_C * (jnp.maximum(-lam, 0.0) + jnp.log1p(jnp.exp(-jnp.abs(lam))))
    for hd in range(LRU_HEADS):
        sl = slice(hd * head_w, (hd + 1) * head_w)
        xh = xc[:, sl]
        gates = jnp.dot(xh.astype(BF16), wax_ref[hd], preferred_element_type=F32)
        r = jax.nn.sigmoid(gates[:, :head_w] + ba_ref[:, sl])
        i = jax.nn.sigmoid(gates[:, head_w:] + bx_ref[:, sl])
        log_a = r * neg_sp[:, sl]
        a = jnp.exp(log_a)
        mult = jnp.sqrt(jnp.maximum(1.0 - jnp.exp(2.0 * log_a), 0.0))
        a = jnp.where(reset, 0.0, a)
        mult = jnp.where(reset, 1.0, mult)
        a_sc[:, sl] = a
        b_sc[:, sl] = mult * (i * xh)

    srow = lax.broadcasted_iota(jnp.int32, (SUBLANES, c), 0)

    def group(g, h_prev):
        start = pl.multiple_of(g * SUBLANES, SUBLANES)
        a = a_sc[pl.ds(start, SUBLANES), :]
        bb = b_sc[pl.ds(start, SUBLANES), :]
        sh = 1
        while sh < SUBLANES:
            keep = srow >= sh
            a_prev = jnp.where(keep, pltpu.roll(a, sh, 0), 1.0)
            b_prev = jnp.where(keep, pltpu.roll(bb, sh, 0), 0.0)
            bb = bb + a * b_prev
            a = a * a_prev
            sh *= 2
        hh = bb + a * h_prev
        h_sc[pl.ds(start, SUBLANES), :] = hh
        return jnp.broadcast_to(hh[SUBLANES - 1:SUBLANES, :], (SUBLANES, c))

    carry_sc[...] = lax.fori_loop(0, t // SUBLANES, group, carry_sc[...], unroll=4)

    y = h_sc[...] * gl_ref[0]
    o_ref[0] = (y * _rms(y, c) * gn_ref[...]).astype(o_ref.dtype)


def _rg_lru(x_lru, gelu_g, pos_row, conv_w, conv_b, w_ax, b_a, b_x, lam, out_norm):
    b, s, c = x_lru.shape
    t = min(LRU_TILE, s)
    full = lambda a: pl.BlockSpec(a.shape, lambda i, j: (0,) * a.ndim)
    return pl.pallas_call(
        _lru_kernel,
        grid=(b, s // t),
        in_specs=[pl.BlockSpec((1, t, c), lambda i, j: (i, j, 0)),
                  pl.BlockSpec((1, t, c), lambda i, j: (i, j, 0)),
                  pl.BlockSpec((1, 1, t), lambda i, j: (i, 0, j)),
                  full(conv_w), full(conv_b), full(w_ax), full(b_a), full(b_x), full(lam),
                  full(out_norm)],
        out_specs=pl.BlockSpec((1, t, c), lambda i, j: (i, j, 0)),
        out_shape=jax.ShapeDtypeStruct((b, s, c), BF16),
        scratch_shapes=[pltpu.VMEM((t + SUBLANES, c), F32), pltpu.VMEM((t, c), F32),
                        pltpu.VMEM((t, c), F32), pltpu.VMEM((t, c), F32),
                        pltpu.VMEM((SUBLANES, c), F32)],
        compiler_params=_cparams("parallel", "arbitrary"),
        name="rg_lru",
    )(x_lru, gelu_g, pos_row, conv_w, conv_b, w_ax, b_a, b_x, lam, out_norm)


def _route_kernel(x_ref, at_ref, lr_ref, mod_ref, an_ref, wa_ref, wl_ref, g2_ref, wr_ref, br_ref,
                  x1_ref, h2_ref, ids_ref, rank_ref, wts_ref, cnt_ref, carry_sc,
                  *, n_experts, group_row0, expert_row0):
    @pl.when(pl.program_id(0) == 0)
    def _():
        carry_sc[...] = jnp.zeros(carry_sc.shape, F32)

    x = x_ref[0]
    t, d = x.shape
    gate1 = mod_ref[0, 2:3, :]
    shift2 = mod_ref[0, 3:4, :]
    gain2 = g2_ref[...] * (1.0 + mod_ref[0, 4:5, :])
    at = at_ref[0].astype(F32)
    at_n = at * _rms(at, at.shape[-1]) * an_ref[...]
    mix = jnp.dot(at_n.astype(BF16), wa_ref[...], preferred_element_type=F32)
    mix = mix + jnp.dot(lr_ref[0], wl_ref[...], preferred_element_type=F32)
    x1 = x + gate1 * mix
    x1_ref[0] = x1
    h2 = x1 * _rms(x1, d) * gain2 + shift2
    h2_ref[0] = h2

    logits = lax.dot_general(wr_ref[...], h2.astype(BF16), (((1,), (1,)), ((), ())),
                             preferred_element_type=F32) + br_ref[...]
    per_group = n_experts // N_GROUPS
    lg = logits[group_row0:group_row0 + N_GROUPS, :]
    grow = lax.broadcasted_iota(jnp.int32, lg.shape, 0)
    gmax = jnp.max(lg, axis=0, keepdims=True)
    g_idx = jnp.min(jnp.where(lg == gmax, grow, N_GROUPS), axis=0, keepdims=True)
    g_p = 1.0 / jnp.sum(jnp.exp(lg - gmax), axis=0, keepdims=True)
    sel = jnp.zeros((per_group, t), F32)
    for g in range(N_GROUPS):
        r0 = expert_row0 + g * per_group
        sel = sel + jnp.where(g_idx == g, logits[r0:r0 + per_group, :], 0.0)
    erow = lax.broadcasted_iota(jnp.int32, sel.shape, 0)
    m1 = jnp.max(sel, axis=0, keepdims=True)
    i1 = jnp.min(jnp.where(sel == m1, erow, per_group), axis=0, keepdims=True)
    sel2 = jnp.where(erow == i1, -jnp.inf, sel)
    m2 = jnp.max(sel2, axis=0, keepdims=True)
    i2 = jnp.min(jnp.where(sel2 == m2, erow, per_group), axis=0, keepdims=True)
    e21 = jnp.exp(m2 - m1)
    w1 = g_p / (1.0 + e21)
    w2 = g_p * e21 / (1.0 + e21)
    e1 = g_idx * per_group + i1
    e2 = g_idx * per_group + i2
    ids_ref[0:1, :] = e1
    ids_ref[1:2, :] = e2
    wts_ref[0:1, :] = w1
    wts_ref[1:2, :] = w2

    xrow = lax.broadcasted_iota(jnp.int32, (n_experts, t), 0)
    o1 = (xrow == e1).astype(F32)
    o2 = (xrow == e2).astype(F32)
    both = o1 + o2
    tr = lax.broadcasted_iota(jnp.int32, (t, t), 0)
    tc = lax.broadcasted_iota(jnp.int32, (t, t), 1)
    upper = jnp.where(tr < tc, 1.0, 0.0).astype(BF16)
    before = jnp.dot(both.astype(BF16), upper, preferred_element_type=F32) + carry_sc[...]
    rank_ref[0:1, :] = jnp.sum(o1 * before, axis=0, keepdims=True).astype(jnp.int32)
    rank_ref[1:2, :] = jnp.sum(o2 * before, axis=0, keepdims=True).astype(jnp.int32)
    carry_sc[...] = carry_sc[...] + jnp.sum(both, axis=1, keepdims=True)
    cnt_ref[...] = jnp.broadcast_to(carry_sc[...], cnt_ref.shape)


def _out_route(x, attn, lru_n, mod, attn_norm, w_out_a, w_out_l, norm2_g, w_r_t, b_r_col,
               *, n_experts, group_row0, expert_row0):
    b, s, d = x.shape
    t = min(ROUTE_TILE, s)
    nt = s // t
    n = b * s
    full = lambda a: pl.BlockSpec(a.shape, lambda i: (0,) * a.ndim)
    tok = lambda w: pl.BlockSpec((1, t, w), lambda i: (i // nt, i % nt, 0))
    kern = functools.partial(_route_kernel, n_experts=n_experts, group_row0=group_row0,
                             expert_row0=expert_row0)
    return pl.pallas_call(
        kern,
        grid=(b * nt,),
        in_specs=[tok(d), tok(attn.shape[-1]), tok(lru_n.shape[-1]),
                  pl.BlockSpec((1,) + mod.shape[1:], lambda i: (i // nt, 0, 0)),
                  full(attn_norm), full(w_out_a), full(w_out_l), full(norm2_g), full(w_r_t),
                  full(b_r_col)],
        out_specs=[tok(d), tok(d),
                   pl.BlockSpec((TOP_K, t), lambda i: (0, i)),
                   pl.BlockSpec((TOP_K, t), lambda i: (0, i)),
                   pl.BlockSpec((TOP_K, t), lambda i: (0, i)),
                   pl.BlockSpec((n_experts, LANES), lambda i: (0, 0))],
        out_shape=[jax.ShapeDtypeStruct((b, s, d), F32),
                   jax.ShapeDtypeStruct((b, s, d), F32),
                   jax.ShapeDtypeStruct((TOP_K, n), jnp.int32),
                   jax.ShapeDtypeStruct((TOP_K, n), jnp.int32),
                   jax.ShapeDtypeStruct((TOP_K, n), F32),
                   jax.ShapeDtypeStruct((n_experts, LANES), F32)],
        scratch_shapes=[pltpu.VMEM((n_experts, 1), F32)],
        compiler_params=_cparams("arbitrary"),
        name="out_route",
    )(x, attn, lru_n, mod, attn_norm, w_out_a, w_out_l, norm2_g, w_r_t, b_r_col)


def _row_copy(src, src_row, dst, dst_row, sem):
    return pltpu.make_async_copy(src.at[pl.ds(src_row, 1), :], dst.at[pl.ds(dst_row, 1), :], sem)


def _dispatch_kernel(cnt_ref, start_ref, nb_ref, dest_ref, h2_ref, xs_ref, zero_sc, sem,
                     *, tile, block, n_experts):
    i = pl.program_id(0)

    @pl.when(i == 0)
    def _():
        zero_sc[...] = jnp.zeros(zero_sc.shape, zero_sc.dtype)

        def tail_copy(j):
            return pltpu.make_async_copy(zero_sc, xs_ref.at[pl.ds(j * block, block), :], sem)

        def tail_issue(j, carry):
            tail_copy(j).start()
            return carry

        def tail_drain(j, carry):
            tail_copy(j).wait()
            return carry

        n_blocks = xs_ref.shape[0] // block
        lax.fori_loop(nb_ref[0], n_blocks, tail_issue, 0)
        lax.fori_loop(nb_ref[0], n_blocks, tail_drain, 0)

        def per_expert(e, carry):
            cnt = cnt_ref[e]
            first = start_ref[e] + cnt
            n_pad = (block - cnt % block) % block

            def issue(r, c2):
                _row_copy(zero_sc, 0, xs_ref, first + r, sem).start()
                return c2

            lax.fori_loop(0, n_pad, issue, 0)

            def drain(r, c2):
                _row_copy(zero_sc, 0, xs_ref, first, sem).wait()
                return c2

            lax.fori_loop(0, n_pad, drain, 0)
            return carry

        lax.fori_loop(0, n_experts, per_expert, 0)

    for tk in range(tile):
        for k in range(TOP_K):
            _row_copy(h2_ref, tk, xs_ref, dest_ref[k, tk], sem).start(priority=k % 2)

    def drain(tk, carry):
        for _ in range(TOP_K):
            _row_copy(h2_ref, 0, xs_ref, 0, sem).wait()
        return carry

    lax.fori_loop(0, tile, drain, 0, unroll=8)


def _dispatch(counts, starts, n_blocks_used, dest, h2, cap, *, block, n_experts):
    n, w = h2.shape
    t = min(DISPATCH_TILE, n)
    kern = functools.partial(_dispatch_kernel, tile=t, block=block, n_experts=n_experts)
    smem_tile = pl.BlockSpec((TOP_K, t), lambda i, c, s, nb: (0, i), memory_space=pltpu.SMEM)
    return pl.pallas_call(
        kern,
        grid_spec=pltpu.PrefetchScalarGridSpec(
            num_scalar_prefetch=3,
            grid=(n // t,),
            in_specs=[smem_tile, pl.BlockSpec((t, w), lambda i, c, s, nb: (i, 0))],
            out_specs=pl.BlockSpec(memory_space=pl.ANY),
            scratch_shapes=[pltpu.VMEM((block, w), h2.dtype), pltpu.SemaphoreType.DMA(())]),
        out_shape=jax.ShapeDtypeStruct((cap, w), h2.dtype),
        compiler_params=_cparams("arbitrary"),
        name="dispatch",
    )(counts, starts, n_blocks_used, dest, h2)


def _expert_kernel(be_ref, nb_ref, xs_ref, wgu_ref, wdn_ref, ys_ref, *, d_ff):
    @pl.when(pl.program_id(0) < nb_ref[0])
    def _():
        xb = xs_ref[...].astype(BF16)
        gu = jnp.dot(xb, wgu_ref[0], preferred_element_type=F32)
        gt = gu[:, :d_ff]
        hid = gt * jax.nn.sigmoid(gt) * gu[:, d_ff:]
        ys_ref[...] = jnp.dot(hid.astype(BF16), wdn_ref[0], preferred_element_type=F32)

    @pl.when(pl.program_id(0) >= nb_ref[0])
    def _():
        ys_ref[...] = jnp.zeros(ys_ref.shape, ys_ref.dtype)


def _experts(block_e, n_blocks_used, xs, w_gu, w_dn, *, block):
    cap, w = xs.shape
    d = w_dn.shape[2]
    d_ff = w_dn.shape[1]
    kern = functools.partial(_expert_kernel, d_ff=d_ff)
    live = lambda j, nb: jnp.minimum(j, nb[0] - 1)
    return pl.pallas_call(
        kern,
        grid_spec=pltpu.PrefetchScalarGridSpec(
            num_scalar_prefetch=2,
            grid=(cap // block,),
            in_specs=[pl.BlockSpec((block, w), lambda j, be, nb: (live(j, nb), 0)),
                      pl.BlockSpec((1, d, 2 * d_ff), lambda j, be, nb: (be[live(j, nb)], 0, 0)),
                      pl.BlockSpec((1, d_ff, d), lambda j, be, nb: (be[live(j, nb)], 0, 0))],
            out_specs=pl.BlockSpec((block, w), lambda j, be, nb: (j, 0))),
        out_shape=jax.ShapeDtypeStruct((cap, w), F32),
        compiler_params=_cparams("arbitrary"),
        name="experts",
    )(block_e, n_blocks_used, xs, w_gu, w_dn)


def _combine_kernel(dest_ref, dest_nx_ref, wts_ref, x1_ref, mod_ref, ys_ref, o_ref, buf, sem, *,
                    tile):
    i = pl.program_id(0)
    slot = i % 2

    def gather(d_ref, sl):
        for tk in range(tile):
            for k in range(TOP_K):
                _row_copy(ys_ref, d_ref[k, tk], buf.at[sl, k], tk, sem.at[sl]).start(priority=k % 2)

    @pl.when(i == 0)
    def _():
        gather(dest_ref, slot)

    @pl.when(i + 1 < pl.num_programs(0))
    def _():
        gather(dest_nx_ref, 1 - slot)

    for k in range(TOP_K):
        pltpu.make_async_copy(ys_ref.at[pl.ds(0, tile), :], buf.at[slot, k], sem.at[slot]).wait()

    d = o_ref.shape[-1]
    w0 = jnp.tile(_column(wts_ref[0:1, :]), (1, d // LANES))
    w1 = jnp.tile(_column(wts_ref[1:2, :]), (1, d // LANES))
    gate2 = mod_ref[0, 5:6, :]
    o_ref[0] = x1_ref[0] + gate2 * (w0 * buf[slot, 0] + w1 * buf[slot, 1])


def _combine(dest, wts, x1, mod, ys):
    b, s, d = x1.shape
    w = ys.shape[1]
    t = min(COMBINE_TILE, s)
    nt = s // t
    last = b * nt - 1
    kern = functools.partial(_combine_kernel, tile=t)
    smem_now = pl.BlockSpec((TOP_K, t), lambda i: (0, i), memory_space=pltpu.SMEM)
    smem_next = pl.BlockSpec((TOP_K, t), lambda i: (0, jnp.minimum(i + 1, last)),
                             memory_space=pltpu.SMEM)
    return pl.pallas_call(
        kern,
        grid=(b * nt,),
        in_specs=[smem_now, smem_next,
                  pl.BlockSpec((TOP_K, t), lambda i: (0, i)),
                  pl.BlockSpec((1, t, d), lambda i: (i // nt, i % nt, 0)),
                  pl.BlockSpec((1,) + mod.shape[1:], lambda i: (i // nt, 0, 0)),
                  pl.BlockSpec(memory_space=pl.ANY)],
        out_specs=pl.BlockSpec((1, t, d), lambda i: (i // nt, i % nt, 0)),
        out_shape=jax.ShapeDtypeStruct((b, s, d), F32),
        scratch_shapes=[pltpu.VMEM((2, TOP_K, t, w), F32), pltpu.SemaphoreType.DMA((2,))],
        compiler_params=_cparams("arbitrary"),
        name="combine",
    )(dest, dest, wts, x1, mod, ys)


def _layer(x, c, pos_row, w_ada, b_ada, norm1_g, w_in, q_a_norm, w_q_b, kv_a_norm, w_kv_b, q_norm,
           k_norm, conv_w, conv_b, lru_wa, lru_ba, lru_wx, lru_bx, lru_lambda, attn_out_norm,
           lru_out_norm, w_out, norm2_g, w_rg, b_rg, w_re, b_re, w_gate, w_up, w_down):
    b, s, d = x.shape
    n = b * s
    q_lora = q_a_norm.shape[0]
    kv_lora = kv_a_norm.shape[0]
    lru_w = lru_lambda.shape[0]
    attn_w = ATTN_HEADS * V_HEAD_DIM
    n_experts = w_gate.shape[0]
    row = lambda v: v.reshape(1, -1)

    mod = _adaln(c, w_ada, b_ada).reshape(b, 6, d)

    o1 = q_lora
    o2 = o1 + kv_lora
    o3 = o2 + QK_ROPE_DIM
    o4 = o3 + lru_w
    w_in_p = jnp.concatenate(
        [w_in[:, :o2], w_in[:, o3:o4], w_in[:, o4:], w_in[:, o2:o3],
         jnp.zeros((d, LANES - QK_ROPE_DIM), w_in.dtype)], axis=1).astype(BF16)
    pad_h = QK_PAD_DIM - QK_HEAD_DIM
    w_q_p = jnp.pad(w_q_b.reshape(q_lora, ATTN_HEADS, QK_HEAD_DIM),
                    ((0, 0), (0, 0), (0, pad_h))).reshape(q_lora, ATTN_HEADS * QK_PAD_DIM).astype(BF16)
    q_norm_p = row(jnp.pad(q_norm, (0, pad_h)))
    k_norm_nope = row(k_norm[:QK_NOPE_DIM])
    k_norm_rope_p = row(jnp.pad(k_norm[QK_NOPE_DIM:], (0, LANES - QK_ROPE_DIM)))
    inv = 1.0 / (ROPE_THETA ** (np.arange(0, QK_ROPE_DIM, 2, dtype=np.float32) / QK_ROPE_DIM))
    inv_col = jnp.asarray(inv.reshape(-1, 1))

    q, k, v, x_lru, gelu_g = _in_proj(
        x, pos_row, mod, row(norm1_g), w_in_p, row(q_a_norm), w_q_p, row(kv_a_norm),
        w_kv_b.astype(BF16), q_norm_p, k_norm_nope, k_norm_rope_p, inv_col,
        q_lora=q_lora, kv_lora=kv_lora, lru_w=lru_w)

    attn = _attention(q, k, v)

    w_ax = jnp.concatenate([lru_wa, lru_wx], axis=-1).astype(BF16)
    lru_n = _rg_lru(x_lru, gelu_g, pos_row, conv_w, row(conv_b), w_ax, row(lru_ba), row(lru_bx),
                    row(lru_lambda), row(lru_out_norm))

    group_row0, expert_row0 = 0, SUBLANES
    w_r_t = jnp.concatenate([w_rg.T, jnp.zeros((expert_row0 - N_GROUPS, d), w_rg.dtype), w_re.T],
                            axis=0).astype(BF16)
    b_r_col = jnp.concatenate([b_rg, jnp.zeros((expert_row0 - N_GROUPS,), b_rg.dtype),
                               b_re]).reshape(-1, 1)
    x1, h2, ids, ranks, wts, cnt = _out_route(
        x, attn, lru_n, mod, row(attn_out_norm), w_out[:attn_w].astype(BF16),
        w_out[attn_w:].astype(BF16), row(norm2_g), w_r_t, b_r_col,
        n_experts=n_experts, group_row0=group_row0, expert_row0=expert_row0)

    blk = EXPERT_BLOCK
    counts = cnt[:, 0].astype(jnp.int32)
    padded = (counts + blk - 1) // blk * blk
    pad_ends = jnp.cumsum(padded)
    starts = (pad_ends - padded).astype(jnp.int32)
    dest = ranks
    for e in range(n_experts):
        dest = dest + jnp.where(ids == e, starts[e], 0)
    cap = (n * TOP_K + n_experts * (blk - 1) + blk - 1) // blk * blk
    n_blocks = cap // blk
    block_first_row = jnp.arange(n_blocks, dtype=jnp.int32) * blk
    block_e = jnp.minimum(jnp.sum(pad_ends[None, :] <= block_first_row[:, None], axis=1),
                          n_experts - 1).astype(jnp.int32)
    n_used = (pad_ends[-1:] // blk).astype(jnp.int32)

    xs = _dispatch(counts, starts, n_used, dest, h2.reshape(n, d), cap,
                   block=blk, n_experts=n_experts)
    w_gu = jnp.concatenate([w_gate, w_up], axis=-1).astype(BF16)
    ys = _experts(block_e, n_used, xs, w_gu, w_down.astype(BF16), block=blk)
    return _combine(dest, wts, x1, mod, ys)


def kernel(x, c, positions, w_ada, b_ada, norm1_g, w_in, q_a_norm, w_q_b, kv_a_norm, w_kv_b, q_norm, k_norm, conv_w, conv_b, lru_wa, lru_ba, lru_wx, lru_bx, lru_lambda, attn_out_norm, lru_out_norm, w_out, norm2_g, w_router_group, b_router_group, w_router_expert, b_router_expert, w_gate, w_up, w_down):
    b, s, _ = x.shape
    pos_row = positions.reshape(b, 1, s)
    for l in range(w_ada.shape[0]):
        x = _layer(x, c, pos_row, w_ada[l], b_ada[l], norm1_g[l], w_in[l], q_a_norm[l], w_q_b[l],
                   kv_a_norm[l], w_kv_b[l], q_norm[l], k_norm[l], conv_w[l], conv_b[l], lru_wa[l],
                   lru_ba[l], lru_wx[l], lru_bx[l], lru_lambda[l], attn_out_norm[l], lru_out_norm[l],
                   w_out[l], norm2_g[l], w_router_group[l], b_router_group[l], w_router_expert[l],
                   b_router_expert[l], w_gate[l], w_up[l], w_down[l])
    return x
```

```python
import functools
import math

import numpy as np
import jax
import jax.numpy as jnp
from jax import lax
from jax.experimental import pallas as pl
from jax.experimental.pallas import tpu as pltpu

F32 = jnp.float32
BF16 = jnp.bfloat16

ATTN_HEADS = 4
QK_NOPE_DIM = 128
QK_ROPE_DIM = 64
QK_HEAD_DIM = QK_NOPE_DIM + QK_ROPE_DIM
V_HEAD_DIM = 128
ROPE_THETA = 10000.0
LRU_HEADS = 4
CONV_WIDTH = 4
LRU_C = 8.0
N_GROUPS = 4
TOP_K = 2
EPS = 1e-6

LANES = 128
SUBLANES = 8
QK_PAD_DIM = 256
V_PAD_DIM = 256
VMEM_LIMIT = 56 * 1024 * 1024

PROJ_TILE = 512
PROJ_CHUNK = 128
ATTN_TILE = 1024
ATTN_CHUNK = 128
LRU_TILE = 512
ROUTE_TILE = 512
DISPATCH_TILE = 512
COMBINE_TILE = 512
EXPERT_BLOCK = 256


def _cparams(*sem):
    return pltpu.CompilerParams(dimension_semantics=sem, vmem_limit_bytes=VMEM_LIMIT)


def _rms(x, n):
    return lax.rsqrt(jnp.sum(x * x, axis=-1, keepdims=True) * (1.0 / n) + EPS)


def _sigmoid(x):
    return 0.5 * jnp.tanh(0.5 * x) + 0.5


def _column(row, width=LANES):
    t = row.shape[-1]
    return jnp.broadcast_to(row, (width, t)).T


def _adaln_kernel(c_ref, w_ref, b_ref, o_ref):
    c = c_ref[...]
    sc = c * jax.nn.sigmoid(c)
    o_ref[...] = jnp.dot(sc.astype(BF16), w_ref[...].astype(BF16),
                         preferred_element_type=F32) + b_ref[...]


def _adaln(c, w_ada, b_ada):
    b, d = c.shape
    n_out = w_ada.shape[1]
    return pl.pallas_call(
        _adaln_kernel,
        grid=(n_out // d,),
        in_specs=[pl.BlockSpec((b, d), lambda j: (0, 0)),
                  pl.BlockSpec((d, d), lambda j: (0, j)),
                  pl.BlockSpec((1, d), lambda j: (0, j))],
        out_specs=pl.BlockSpec((b, d), lambda j: (0, j)),
        out_shape=jax.ShapeDtypeStruct((b, n_out), F32),
        compiler_params=_cparams("parallel"),
        name="adaln",
    )(c, w_ada, b_ada.reshape(1, n_out))


def _rope_tables(pos_row, inv_col):
    half = QK_ROPE_DIM // 2
    ang = inv_col * pos_row
    cos = jnp.cos(ang)
    sin = jnp.sin(ang)
    t = ang.shape[-1]
    zeros = lambda n: jnp.zeros((n, t), F32)
    c = jnp.concatenate([cos, cos, zeros(LANES - QK_ROPE_DIM)], axis=0)
    sa = jnp.concatenate([-sin, zeros(LANES - half)], axis=0)
    sb = jnp.concatenate([zeros(half), sin, zeros(LANES - QK_ROPE_DIM)], axis=0)
    return c.T, sa.T, sb.T


def _rope(x, c, sa, sb):
    return x * c + pltpu.roll(x, LANES - QK_ROPE_DIM // 2, 1) * sa + pltpu.roll(x, QK_ROPE_DIM // 2, 1) * sb


def _in_proj_kernel(x_ref, pos_ref, mod_ref, g1_ref, w_in_ref, qa_ref, wq_ref, kva_ref, wkv_ref,
                    qn_ref, kn_nope_ref, kn_rope_ref, inv_ref,
                    q_ref, k_ref, v_ref, xl_ref, gl_ref, *, q_lora, kv_lora, lru_w, chunk):
    t, d = x_ref.shape[1:]
    shift = mod_ref[0, 0:1, :]
    gain = g1_ref[...] * (1.0 + mod_ref[0, 1:2, :])
    c_all, sa_all, sb_all = _rope_tables(pos_ref[0].astype(F32), inv_ref[...])
    sm_scale = QK_HEAD_DIM ** -0.5 * math.log2(math.e)

    for r0 in range(0, t, chunk):
        rows = slice(r0, r0 + chunk)
        x = x_ref[0, rows, :]
        h = x * _rms(x, d) * gain + shift
        proj = jnp.dot(h.astype(BF16), w_in_ref[...], preferred_element_type=F32)
        o = 0
        q_lat = proj[:, o:o + q_lora]; o += q_lora
        kv_lat = proj[:, o:o + kv_lora]; o += kv_lora
        x_lru = proj[:, o:o + lru_w]; o += lru_w
        g_lru = proj[:, o:o + lru_w]; o += lru_w
        k_pe = proj[:, o:o + LANES]

        xl_ref[0, rows, :] = x_lru
        gl_ref[0, rows, :] = jax.nn.gelu(g_lru).astype(BF16)

        c, sa, sb = c_all[rows], sa_all[rows], sb_all[rows]
        qn = q_lat * _rms(q_lat, q_lora) * qa_ref[...]
        q_full = jnp.dot(qn.astype(BF16), wq_ref[...], preferred_element_type=F32)
        kvn = kv_lat * _rms(kv_lat, kv_lora) * kva_ref[...]
        kv_full = jnp.dot(kvn.astype(BF16), wkv_ref[...], preferred_element_type=F32)

        pe_ss = jnp.sum(k_pe * k_pe, axis=-1, keepdims=True)
        pe_rot = _rope(k_pe * kn_rope_ref[...], c, sa, sb)
        for hd in range(ATTN_HEADS):
            qh = q_full[:, hd * QK_PAD_DIM:(hd + 1) * QK_PAD_DIM]
            qh = qh * (_rms(qh, QK_HEAD_DIM) * sm_scale) * qn_ref[...]
            q_ref[0, hd, rows, 0:QK_NOPE_DIM] = qh[:, 0:QK_NOPE_DIM].astype(BF16)
            q_ref[0, hd, rows, QK_NOPE_DIM:QK_PAD_DIM] = _rope(qh[:, QK_NOPE_DIM:QK_PAD_DIM],
                                                               c, sa, sb).astype(BF16)
            base = hd * (QK_NOPE_DIM + V_HEAD_DIM)
            kn = kv_full[:, base:base + QK_NOPE_DIM]
            vv = kv_full[:, base + QK_NOPE_DIM:base + QK_NOPE_DIM + V_HEAD_DIM]
            r = lax.rsqrt((jnp.sum(kn * kn, axis=-1, keepdims=True) + pe_ss) * (1.0 / QK_HEAD_DIM) + EPS)
            k_ref[0, hd, rows, 0:QK_NOPE_DIM] = (kn * r * kn_nope_ref[...]).astype(BF16)
            k_ref[0, hd, rows, QK_NOPE_DIM:QK_PAD_DIM] = (pe_rot * r).astype(BF16)
            v_ref[0, hd, rows, :] = vv.astype(BF16)


def _in_proj(x, pos_row, mod, norm1_g, w_in_p, q_a_norm, w_q_p, kv_a_norm, w_kv, q_norm_p,
             k_norm_nope, k_norm_rope_p, inv_col, *, q_lora, kv_lora, lru_w):
    b, s, d = x.shape
    t = min(PROJ_TILE, s)
    full = lambda a: pl.BlockSpec(a.shape, lambda i, j: (0,) * a.ndim)
    kern = functools.partial(_in_proj_kernel, q_lora=q_lora, kv_lora=kv_lora, lru_w=lru_w,
                             chunk=min(PROJ_CHUNK, t))
    head_out = lambda w: pl.BlockSpec((1, ATTN_HEADS, t, w), lambda i, j: (i, 0, j, 0))
    return pl.pallas_call(
        kern,
        grid=(b, s // t),
        in_specs=[pl.BlockSpec((1, t, d), lambda i, j: (i, j, 0)),
                  pl.BlockSpec((1, 1, t), lambda i, j: (i, 0, j)),
                  pl.BlockSpec((1,) + mod.shape[1:], lambda i, j: (i, 0, 0)),
                  full(norm1_g), full(w_in_p), full(q_a_norm), full(w_q_p), full(kv_a_norm),
                  full(w_kv), full(q_norm_p), full(k_norm_nope), full(k_norm_rope_p), full(inv_col)],
        out_specs=[head_out(QK_PAD_DIM), head_out(QK_PAD_DIM), head_out(V_HEAD_DIM),
                   pl.BlockSpec((1, t, lru_w), lambda i, j: (i, j, 0)),
                   pl.BlockSpec((1, t, lru_w), lambda i, j: (i, j, 0))],
        out_shape=[jax.ShapeDtypeStruct((b, ATTN_HEADS, s, QK_PAD_DIM), BF16),
                   jax.ShapeDtypeStruct((b, ATTN_HEADS, s, QK_PAD_DIM), BF16),
                   jax.ShapeDtypeStruct((b, ATTN_HEADS, s, V_HEAD_DIM), BF16),
                   jax.ShapeDtypeStruct((b, s, lru_w), F32),
                   jax.ShapeDtypeStruct((b, s, lru_w), BF16)],
        compiler_params=_cparams("parallel", "parallel"),
        name="in_proj",
    )(x, pos_row, mod, norm1_g, w_in_p, q_a_norm, w_q_p, kv_a_norm, w_kv, q_norm_p,
      k_norm_nope, k_norm_rope_p, inv_col)


def _attn_kernel(q_ref, k_ref, v_ref, o_ref, m_sc, acc_sc, s_sc, v_sc, *, tile, chunk):
    qi = pl.program_id(2)

    @pl.when(qi == 0)
    def _():
        v_sc[:, 0:V_HEAD_DIM] = v_ref[0, 0]
        v_sc[:, V_HEAD_DIM:V_PAD_DIM] = jnp.ones((v_sc.shape[0], V_PAD_DIM - V_HEAD_DIM), BF16)

    m_sc[...] = jnp.full(m_sc.shape, -jnp.inf, F32)
    acc_sc[...] = jnp.zeros(acc_sc.shape, F32)

    def scores(j, slot):
        start = pl.multiple_of(j * tile, tile)
        k = k_ref[0, 0, pl.ds(start, tile), :]
        s_sc[slot] = lax.dot_general(q_ref[0, 0], k, (((1,), (1,)), ((), ())),
                                     preferred_element_type=F32)

    def accumulate(j, slot, masked):
        start = pl.multiple_of(j * tile, tile)
        for r in range(tile // chunk):
            rows = slice(r * chunk, (r + 1) * chunk)
            cols = -(-(r + 1) * chunk // LANES) * LANES if masked else tile
            s = s_sc[slot, rows, 0:cols]
            if masked:
                row = lax.broadcasted_iota(jnp.int32, s.shape, 0) + r * chunk
                col = lax.broadcasted_iota(jnp.int32, s.shape, 1)
                s = jnp.where(col <= row, s, -jnp.inf)
            v = v_sc[pl.ds(start, cols), :]
            m_old = m_sc[rows, :]
            m_new = jnp.maximum(m_old, jnp.max(s, axis=-1, keepdims=True))
            alpha = jnp.exp2(m_old - m_new)
            p = jnp.exp2(s - jnp.tile(m_new, (1, cols // LANES)))
            acc_sc[rows, :] = (jnp.tile(alpha, (1, V_PAD_DIM // LANES)) * acc_sc[rows, :]
                               + jnp.dot(p.astype(BF16), v, preferred_element_type=F32))
            m_sc[rows, :] = m_new

    scores(0, 0)

    def pair(p, carry):
        scores(2 * p + 1, 1)
        accumulate(2 * p, 0, False)
        scores(2 * p + 2, 0)
        accumulate(2 * p + 1, 1, False)
        return carry

    lax.fori_loop(0, qi // 2, pair, 0)

    @pl.when(qi % 2 == 0)
    def _():
        accumulate(qi, 0, True)

    @pl.when(qi % 2 == 1)
    def _():
        scores(qi, 1)
        accumulate(qi - 1, 0, False)
        accumulate(qi, 1, True)

    o_ref[0] = (acc_sc[:, 0:V_HEAD_DIM] / acc_sc[:, V_HEAD_DIM:2 * V_HEAD_DIM]).astype(o_ref.dtype)


def _attention(q, k, v):
    b, h, s, _ = q.shape
    t = min(ATTN_TILE, s)
    assert V_HEAD_DIM == LANES and V_PAD_DIM == 2 * V_HEAD_DIM
    kern = functools.partial(_attn_kernel, tile=t, chunk=min(ATTN_CHUNK, t))
    return pl.pallas_call(
        kern,
        grid=(b, h, s // t),
        in_specs=[pl.BlockSpec((1, 1, t, QK_PAD_DIM), lambda i, j, n: (i, j, n, 0)),
                  pl.BlockSpec((1, 1, s, QK_PAD_DIM), lambda i, j, n: (i, j, 0, 0)),
                  pl.BlockSpec((1, 1, s, V_HEAD_DIM), lambda i, j, n: (i, j, 0, 0))],
        out_specs=pl.BlockSpec((1, t, V_HEAD_DIM), lambda i, j, n: (i, n, j)),
        out_shape=jax.ShapeDtypeStruct((b, s, h * V_HEAD_DIM), BF16),
        scratch_shapes=[pltpu.VMEM((t, LANES), F32), pltpu.VMEM((t, V_PAD_DIM), F32),
                        pltpu.VMEM((2, t, t), F32), pltpu.VMEM((s, V_PAD_DIM), BF16)],
        compiler_params=_cparams("parallel", "parallel", "arbitrary"),
        name="attention",
    )(q, k, v)


def _lru_kernel(xl_ref, gl_ref, pos_ref, cw_ref, cb_ref, wax_ref, ba_ref, bx_ref, lam_ref, gn_ref,
                o_ref, ext_sc, a_sc, b_sc, h_sc, carry_sc):
    t, c = a_sc.shape
    head_w = c // LRU_HEADS

    @pl.when(pl.program_id(1) == 0)
    def _():
        ext_sc[0:SUBLANES, :] = jnp.zeros((SUBLANES, c), F32)
        carry_sc[...] = jnp.zeros(carry_sc.shape, F32)

    x = xl_ref[0]
    ext_sc[SUBLANES:SUBLANES + t, :] = x
    xc = cb_ref[...] + x * cw_ref[CONV_WIDTH - 1:CONV_WIDTH, :]
    for back in range(1, CONV_WIDTH):
        tap = CONV_WIDTH - 1 - back
        xc = xc + ext_sc[SUBLANES - back:SUBLANES - back + t, :] * cw_ref[tap:tap + 1, :]
    ext_sc[0:SUBLANES, :] = x[t - SUBLANES:t, :]

    reset = _column((pos_ref[0] == 0).astype(F32)) > 0.5
    lam = lam_ref[...]
    neg_sp = -LRU_C * (jnp.maximum(-lam, 0.0) + jnp.log1p(jnp.exp(-jnp.abs(lam))))
    for hd in range(LRU_HEADS):
        sl = slice(hd * head_w, (hd + 1) * head_w)
        xh = xc[:, sl]
        gates = jnp.dot(xh.astype(BF16), wax_ref[hd], preferred_element_type=F32)
        r = _sigmoid(gates[:, :head_w] + ba_ref[:, sl])
        i = _sigmoid(gates[:, head_w:] + bx_ref[:, sl])
        a = jnp.exp(r * neg_sp[:, sl])
        gap = jnp.maximum(1.0 - a * a, 0.0)
        mult = jnp.where(gap > 0.0, gap * lax.rsqrt(gap), 0.0)
        a = jnp.where(reset, 0.0, a)
        mult = jnp.where(reset, 1.0, mult)
        a_sc[:, sl] = a
        b_sc[:, sl] = mult * (i * xh)

    srow = lax.broadcasted_iota(jnp.int32, (SUBLANES, c), 0)

    def group(g, h_prev):
        start = pl.multiple_of(g * SUBLANES, SUBLANES)
        a = a_sc[pl.ds(start, SUBLANES), :]
        bb = b_sc[pl.ds(start, SUBLANES), :]
        sh = 1
        while sh < SUBLANES:
            keep = srow >= sh
            a_prev = jnp.where(keep, pltpu.roll(a, sh, 0), 1.0)
            b_prev = jnp.where(keep, pltpu.roll(bb, sh, 0), 0.0)
            bb = bb + a * b_prev
            a = a * a_prev
            sh *= 2
        hh = bb + a * h_prev
        h_sc[pl.ds(start, SUBLANES), :] = hh
        return jnp.broadcast_to(hh[SUBLANES - 1:SUBLANES, :], (SUBLANES, c))

    carry_sc[...] = lax.fori_loop(0, t // SUBLANES, group, carry_sc[...], unroll=4)

    y = h_sc[...] * gl_ref[0].astype(F32)
    o_ref[0] = (y * _rms(y, c) * gn_ref[...]).astype(o_ref.dtype)


def _rg_lru(x_lru, gelu_g, pos_row, conv_w, conv_b, w_ax, b_a, b_x, lam, out_norm):
    b, s, c = x_lru.shape
    t = min(LRU_TILE, s)
    full = lambda a: pl.BlockSpec(a.shape, lambda i, j: (0,) * a.ndim)
    return pl.pallas_call(
        _lru_kernel,
        grid=(b, s // t),
        in_specs=[pl.BlockSpec((1, t, c), lambda i, j: (i, j, 0)),
                  pl.BlockSpec((1, t, c), lambda i, j: (i, j, 0)),
                  pl.BlockSpec((1, 1, t), lambda i, j: (i, 0, j)),
                  full(conv_w), full(conv_b), full(w_ax), full(b_a), full(b_x), full(lam),
                  full(out_norm)],
        out_specs=pl.BlockSpec((1, t, c), lambda i, j: (i, j, 0)),
        out_shape=jax.ShapeDtypeStruct((b, s, c), BF16),
        scratch_shapes=[pltpu.VMEM((t + SUBLANES, c), F32), pltpu.VMEM((t, c), F32),
                        pltpu.VMEM((t, c), F32), pltpu.VMEM((t, c), F32),
                        pltpu.VMEM((SUBLANES, c), F32)],
        compiler_params=_cparams("parallel", "arbitrary"),
        name="rg_lru",
    )(x_lru, gelu_g, pos_row, conv_w, conv_b, w_ax, b_a, b_x, lam, out_norm)


def _route_kernel(x_ref, at_ref, lr_ref, mod_ref, an_ref, wa_ref, wl_ref, g2_ref, wr_ref, br_ref,
                  x1_ref, h2_ref, ids_ref, rank_ref, wts_ref, cnt_ref, carry_sc,
                  *, n_experts, group_row0, expert_row0):
    @pl.when(pl.program_id(0) == 0)
    def _():
        carry_sc[...] = jnp.zeros(carry_sc.shape, F32)

    x = x_ref[0]
    t, d = x.shape
    gate1 = mod_ref[0, 2:3, :]
    shift2 = mod_ref[0, 3:4, :]
    gain2 = g2_ref[...] * (1.0 + mod_ref[0, 4:5, :])
    at = at_ref[0].astype(F32)
    at_n = at * _rms(at, at.shape[-1]) * an_ref[...]
    mix = jnp.dot(at_n.astype(BF16), wa_ref[...], preferred_element_type=F32)
    mix = mix + jnp.dot(lr_ref[0], wl_ref[...], preferred_element_type=F32)
    x1 = x + gate1 * mix
    x1_ref[0] = x1
    h2 = x1 * _rms(x1, d) * gain2 + shift2
    h2_ref[0] = h2

    logits = lax.dot_general(wr_ref[...], h2.astype(BF16), (((1,), (1,)), ((), ())),
                             preferred_element_type=F32) + br_ref[...]
    per_group = n_experts // N_GROUPS
    lg = logits[group_row0:group_row0 + N_GROUPS, :]
    grow = lax.broadcasted_iota(jnp.int32, lg.shape, 0)
    gmax = jnp.max(lg, axis=0, keepdims=True)
    g_idx = jnp.min(jnp.where(lg == gmax, grow, N_GROUPS), axis=0, keepdims=True)
    g_p = 1.0 / jnp.sum(jnp.exp(lg - gmax), axis=0, keepdims=True)
    sel = jnp.zeros((per_group, t), F32)
    for g in range(N_GROUPS):
        r0 = expert_row0 + g * per_group
        sel = sel + jnp.where(g_idx == g, logits[r0:r0 + per_group, :], 0.0)
    erow = lax.broadcasted_iota(jnp.int32, sel.shape, 0)
    m1 = jnp.max(sel, axis=0, keepdims=True)
    i1 = jnp.min(jnp.where(sel == m1, erow, per_group), axis=0, keepdims=True)
    sel2 = jnp.where(erow == i1, -jnp.inf, sel)
    m2 = jnp.max(sel2, axis=0, keepdims=True)
    i2 = jnp.min(jnp.where(sel2 == m2, erow, per_group), axis=0, keepdims=True)
    e21 = jnp.exp(m2 - m1)
    w1 = g_p / (1.0 + e21)
    w2 = g_p * e21 / (1.0 + e21)
    e1 = g_idx * per_group + i1
    e2 = g_idx * per_group + i2
    ids_ref[0:1, :] = e1
    ids_ref[1:2, :] = e2
    wts_ref[0:1, :] = w1
    wts_ref[1:2, :] = w2

    xrow = lax.broadcasted_iota(jnp.int32, (n_experts, t), 0)
    o1 = (xrow == e1).astype(F32)
    o2 = (xrow == e2).astype(F32)
    both = o1 + o2
    tr = lax.broadcasted_iota(jnp.int32, (t, t), 0)
    tc = lax.broadcasted_iota(jnp.int32, (t, t), 1)
    upper = jnp.where(tr < tc, 1.0, 0.0).astype(BF16)
    before = jnp.dot(both.astype(BF16), upper, preferred_element_type=F32) + carry_sc[...]
    rank_ref[0:1, :] = jnp.sum(o1 * before, axis=0, keepdims=True).astype(jnp.int32)
    rank_ref[1:2, :] = jnp.sum(o2 * before, axis=0, keepdims=True).astype(jnp.int32)
    carry_sc[...] = carry_sc[...] + jnp.sum(both, axis=1, keepdims=True)
    cnt_ref[...] = jnp.broadcast_to(carry_sc[...], cnt_ref.shape)


def _out_route(x, attn, lru_n, mod, attn_norm, w_out_a, w_out_l, norm2_g, w_r_t, b_r_col,
               *, n_experts, group_row0, expert_row0):
    b, s, d = x.shape
    t = min(ROUTE_TILE, s)
    nt = s // t
    n = b * s
    full = lambda a: pl.BlockSpec(a.shape, lambda i: (0,) * a.ndim)
    tok = lambda w: pl.BlockSpec((1, t, w), lambda i: (i // nt, i % nt, 0))
    kern = functools.partial(_route_kernel, n_experts=n_experts, group_row0=group_row0,
                             expert_row0=expert_row0)
    return pl.pallas_call(
        kern,
        grid=(b * nt,),
        in_specs=[tok(d), tok(attn.shape[-1]), tok(lru_n.shape[-1]),
                  pl.BlockSpec((1,) + mod.shape[1:], lambda i: (i // nt, 0, 0)),
                  full(attn_norm), full(w_out_a), full(w_out_l), full(norm2_g), full(w_r_t),
                  full(b_r_col)],
        out_specs=[tok(d), tok(d),
                   pl.BlockSpec((TOP_K, t), lambda i: (0, i)),
                   pl.BlockSpec((TOP_K, t), lambda i: (0, i)),
                   pl.BlockSpec((TOP_K, t), lambda i: (0, i)),
                   pl.BlockSpec((n_experts, LANES), lambda i: (0, 0))],
        out_shape=[jax.ShapeDtypeStruct((b, s, d), F32),
                   jax.ShapeDtypeStruct((b, s, d), F32),
                   jax.ShapeDtypeStruct((TOP_K, n), jnp.int32),
                   jax.ShapeDtypeStruct((TOP_K, n), jnp.int32),
                   jax.ShapeDtypeStruct((TOP_K, n), F32),
                   jax.ShapeDtypeStruct((n_experts, LANES), F32)],
        scratch_shapes=[pltpu.VMEM((n_experts, 1), F32)],
        compiler_params=_cparams("arbitrary"),
        name="out_route",
    )(x, attn, lru_n, mod, attn_norm, w_out_a, w_out_l, norm2_g, w_r_t, b_r_col)


def _row_copy(src, src_row, dst, dst_row, sem):
    return pltpu.make_async_copy(src.at[pl.ds(src_row, 1), :], dst.at[pl.ds(dst_row, 1), :], sem)


def _dispatch_kernel(cnt_ref, start_ref, nb_ref, dest_ref, h2_ref, xs_ref, zero_sc, sem,
                     *, tile, block, n_experts):
    i = pl.program_id(0)

    @pl.when(i == 0)
    def _():
        zero_sc[...] = jnp.zeros(zero_sc.shape, zero_sc.dtype)

        def tail_copy(j):
            return pltpu.make_async_copy(zero_sc, xs_ref.at[pl.ds(j * block, block), :], sem)

        def tail_issue(j, carry):
            tail_copy(j).start()
            return carry

        def tail_drain(j, carry):
            tail_copy(j).wait()
            return carry

        n_blocks = xs_ref.shape[0] // block
        lax.fori_loop(nb_ref[0], n_blocks, tail_issue, 0)
        lax.fori_loop(nb_ref[0], n_blocks, tail_drain, 0)

        def per_expert(e, carry):
            cnt = cnt_ref[e]
            first = start_ref[e] + cnt
            n_pad = (block - cnt % block) % block

            def issue(r, c2):
                _row_copy(zero_sc, 0, xs_ref, first + r, sem).start()
                return c2

            lax.fori_loop(0, n_pad, issue, 0)

            def drain(r, c2):
                _row_copy(zero_sc, 0, xs_ref, first, sem).wait()
                return c2

            lax.fori_loop(0, n_pad, drain, 0)
            return carry

        lax.fori_loop(0, n_experts, per_expert, 0)

    for tk in range(tile):
        for k in range(TOP_K):
            _row_copy(h2_ref, tk, xs_ref, dest_ref[k, tk], sem).start(priority=k % 2)

    def drain(tk, carry):
        for _ in range(TOP_K):
            _row_copy(h2_ref, 0, xs_ref, 0, sem).wait()
        return carry

    lax.fori_loop(0, tile, drain, 0, unroll=8)


def _dispatch(counts, starts, n_blocks_used, dest, h2, cap, *, block, n_experts):
    n, w = h2.shape
    t = min(DISPATCH_TILE, n)
    kern = functools.partial(_dispatch_kernel, tile=t, block=block, n_experts=n_experts)
    smem_tile = pl.BlockSpec((TOP_K, t), lambda i, c, s, nb: (0, i), memory_space=pltpu.SMEM)
    return pl.pallas_call(
        kern,
        grid_spec=pltpu.PrefetchScalarGridSpec(
            num_scalar_prefetch=3,
            grid=(n // t,),
            in_specs=[smem_tile, pl.BlockSpec((t, w), lambda i, c, s, nb: (i, 0))],
            out_specs=pl.BlockSpec(memory_space=pl.ANY),
            scratch_shapes=[pltpu.VMEM((block, w), h2.dtype), pltpu.SemaphoreType.DMA(())]),
        out_shape=jax.ShapeDtypeStruct((cap, w), h2.dtype),
        compiler_params=_cparams("arbitrary"),
        name="dispatch",
    )(counts, starts, n_blocks_used, dest, h2)


def _expert_kernel(be_ref, nb_ref, xs_ref, wg_ref, wu_ref, wd_ref, ys_ref, wgu_sc, wdn_sc, *, d_ff):
    j = pl.program_id(0)
    live = jnp.minimum(j, nb_ref[0] - 1)

    @pl.when((j == 0) | (be_ref[live] != be_ref[jnp.maximum(live - 1, 0)]))
    def _():
        wgu_sc[:, 0:d_ff] = wg_ref[0].astype(BF16)
        wgu_sc[:, d_ff:2 * d_ff] = wu_ref[0].astype(BF16)
        wdn_sc[...] = wd_ref[0].astype(BF16)

    @pl.when(j < nb_ref[0])
    def _():
        xb = xs_ref[...].astype(BF16)
        gu = jnp.dot(xb, wgu_sc[...], preferred_element_type=F32)
        gt = gu[:, :d_ff]
        hid = gt * jax.nn.sigmoid(gt) * gu[:, d_ff:]
        ys_ref[...] = jnp.dot(hid.astype(BF16), wdn_sc[...], preferred_element_type=F32)

    @pl.when(pl.program_id(0) >= nb_ref[0])
    def _():
        ys_ref[...] = jnp.zeros(ys_ref.shape, ys_ref.dtype)


def _experts(block_e, n_blocks_used, xs, w_gate, w_up, w_down, *, block):
    cap, w = xs.shape
    d = w_down.shape[2]
    d_ff = w_down.shape[1]
    kern = functools.partial(_expert_kernel, d_ff=d_ff)
    live = lambda j, nb: jnp.minimum(j, nb[0] - 1)
    return pl.pallas_call(
        kern,
        grid_spec=pltpu.PrefetchScalarGridSpec(
            num_scalar_prefetch=2,
            grid=(cap // block,),
            in_specs=[pl.BlockSpec((block, w), lambda j, be, nb: (live(j, nb), 0)),
                      pl.BlockSpec((1, d, d_ff), lambda j, be, nb: (be[live(j, nb)], 0, 0)),
                      pl.BlockSpec((1, d, d_ff), lambda j, be, nb: (be[live(j, nb)], 0, 0)),
                      pl.BlockSpec((1, d_ff, d), lambda j, be, nb: (be[live(j, nb)], 0, 0))],
            out_specs=pl.BlockSpec((block, w), lambda j, be, nb: (j, 0)),
            scratch_shapes=[pltpu.VMEM((d, 2 * d_ff), BF16), pltpu.VMEM((d_ff, d), BF16)]),
        out_shape=jax.ShapeDtypeStruct((cap, w), F32),
        compiler_params=_cparams("arbitrary"),
        name="experts",
    )(block_e, n_blocks_used, xs, w_gate, w_up, w_down)


def _combine_kernel(dest_ref, dest_nx_ref, wts_ref, x1_ref, mod_ref, ys_ref, o_ref, buf, sem, *,
                    tile):
    i = pl.program_id(0)
    slot = i % 2

    def gather(d_ref, sl):
        for tk in range(tile):
            for k in range(TOP_K):
                _row_copy(ys_ref, d_ref[k, tk], buf.at[sl, k], tk, sem.at[sl]).start(priority=k % 2)

    @pl.when(i == 0)
    def _():
        gather(dest_ref, slot)

    @pl.when(i + 1 < pl.num_programs(0))
    def _():
        gather(dest_nx_ref, 1 - slot)

    for k in range(TOP_K):
        pltpu.make_async_copy(ys_ref.at[pl.ds(0, tile), :], buf.at[slot, k], sem.at[slot]).wait()

    d = o_ref.shape[-1]
    w0 = jnp.tile(_column(wts_ref[0:1, :]), (1, d // LANES))
    w1 = jnp.tile(_column(wts_ref[1:2, :]), (1, d // LANES))
    gate2 = mod_ref[0, 5:6, :]
    o_ref[0] = x1_ref[0] + gate2 * (w0 * buf[slot, 0] + w1 * buf[slot, 1])


def _combine(dest, wts, x1, mod, ys):
    b, s, d = x1.shape
    w = ys.shape[1]
    t = min(COMBINE_TILE, s)
    nt = s // t
    last = b * nt - 1
    kern = functools.partial(_combine_kernel, tile=t)
    smem_now = pl.BlockSpec((TOP_K, t), lambda i: (0, i), memory_space=pltpu.SMEM)
    smem_next = pl.BlockSpec((TOP_K, t), lambda i: (0, jnp.minimum(i + 1, last)),
                             memory_space=pltpu.SMEM)
    return pl.pallas_call(
        kern,
        grid=(b * nt,),
        in_specs=[smem_now, smem_next,
                  pl.BlockSpec((TOP_K, t), lambda i: (0, i)),
                  pl.BlockSpec((1, t, d), lambda i: (i // nt, i % nt, 0)),
                  pl.BlockSpec((1,) + mod.shape[1:], lambda i: (i // nt, 0, 0)),
                  pl.BlockSpec(memory_space=pl.ANY)],
        out_specs=pl.BlockSpec((1, t, d), lambda i: (i // nt, i % nt, 0)),
        out_shape=jax.ShapeDtypeStruct((b, s, d), F32),
        scratch_shapes=[pltpu.VMEM((2, TOP_K, t, w), F32), pltpu.SemaphoreType.DMA((2,))],
        compiler_params=_cparams("arbitrary"),
        name="combine",
    )(dest, dest, wts, x1, mod, ys)


def _layer(x, c, pos_row, w_ada, b_ada, norm1_g, w_in, q_a_norm, w_q_b, kv_a_norm, w_kv_b, q_norm,
           k_norm, conv_w, conv_b, lru_wa, lru_ba, lru_wx, lru_bx, lru_lambda, attn_out_norm,
           lru_out_norm, w_out, norm2_g, w_rg, b_rg, w_re, b_re, w_gate, w_up, w_down):
    b, s, d = x.shape
    n = b * s
    q_lora = q_a_norm.shape[0]
    kv_lora = kv_a_norm.shape[0]
    lru_w = lru_lambda.shape[0]
    attn_w = ATTN_HEADS * V_HEAD_DIM
    n_experts = w_gate.shape[0]
    row = lambda v: v.reshape(1, -1)

    mod = _adaln(c, w_ada, b_ada).reshape(b, 6, d)

    o1 = q_lora
    o2 = o1 + kv_lora
    o3 = o2 + QK_ROPE_DIM
    o4 = o3 + lru_w
    w_in_p = jnp.concatenate(
        [w_in[:, :o2], w_in[:, o3:o4], w_in[:, o4:], w_in[:, o2:o3],
         jnp.zeros((d, LANES - QK_ROPE_DIM), w_in.dtype)], axis=1).astype(BF16)
    pad_h = QK_PAD_DIM - QK_HEAD_DIM
    w_q_p = jnp.pad(w_q_b.reshape(q_lora, ATTN_HEADS, QK_HEAD_DIM),
                    ((0, 0), (0, 0), (0, pad_h))).reshape(q_lora, ATTN_HEADS * QK_PAD_DIM).astype(BF16)
    q_norm_p = row(jnp.pad(q_norm, (0, pad_h)))
    k_norm_nope = row(k_norm[:QK_NOPE_DIM])
    k_norm_rope_p = row(jnp.pad(k_norm[QK_NOPE_DIM:], (0, LANES - QK_ROPE_DIM)))
    inv = 1.0 / (ROPE_THETA ** (np.arange(0, QK_ROPE_DIM, 2, dtype=np.float32) / QK_ROPE_DIM))
    inv_col = jnp.asarray(inv.reshape(-1, 1))

    q, k, v, x_lru, gelu_g = _in_proj(
        x, pos_row, mod, row(norm1_g), w_in_p, row(q_a_norm), w_q_p, row(kv_a_norm),
        w_kv_b.astype(BF16), q_norm_p, k_norm_nope, k_norm_rope_p, inv_col,
        q_lora=q_lora, kv_lora=kv_lora, lru_w=lru_w)

    attn = _attention(q, k, v)

    w_ax = jnp.concatenate([lru_wa, lru_wx], axis=-1).astype(BF16)
    lru_n = _rg_lru(x_lru, gelu_g, pos_row, conv_w, row(conv_b), w_ax, row(lru_ba), row(lru_bx),
                    row(lru_lambda), row(lru_out_norm))

    group_row0, expert_row0 = 0, SUBLANES
    w_r_t = jnp.concatenate([w_rg.T, jnp.zeros((expert_row0 - N_GROUPS, d), w_rg.dtype), w_re.T],
                            axis=0).astype(BF16)
    b_r_col = jnp.concatenate([b_rg, jnp.zeros((expert_row0 - N_GROUPS,), b_rg.dtype),
                               b_re]).reshape(-1, 1)
    x1, h2, ids, ranks, wts, cnt = _out_route(
        x, attn, lru_n, mod, row(attn_out_norm), w_out[:attn_w].astype(BF16),
        w_out[attn_w:].astype(BF16), row(norm2_g), w_r_t, b_r_col,
        n_experts=n_experts, group_row0=group_row0, expert_row0=expert_row0)

    blk = EXPERT_BLOCK
    counts = cnt[:, 0].astype(jnp.int32)
    padded = (counts + blk - 1) // blk * blk
    pad_ends = jnp.cumsum(padded)
    starts = (pad_ends - padded).astype(jnp.int32)
    dest = ranks
    for e in range(n_experts):
        dest = dest + jnp.where(ids == e, starts[e], 0)
    cap = (n * TOP_K + n_experts * (blk - 1) + blk - 1) // blk * blk
    n_blocks = cap // blk
    block_first_row = jnp.arange(n_blocks, dtype=jnp.int32) * blk
    block_e = jnp.minimum(jnp.sum(pad_ends[None, :] <= block_first_row[:, None], axis=1),
                          n_experts - 1).astype(jnp.int32)
    n_used = (pad_ends[-1:] // blk).astype(jnp.int32)

    xs = _dispatch(counts, starts, n_used, dest, h2.reshape(n, d), cap,
                   block=blk, n_experts=n_experts)
    ys = _experts(block_e, n_used, xs, w_gate, w_up, w_down, block=blk)
    return _combine(dest, wts, x1, mod, ys)


def kernel(x, c, positions, w_ada, b_ada, norm1_g, w_in, q_a_norm, w_q_b, kv_a_norm, w_kv_b, q_norm, k_norm, conv_w, conv_b, lru_wa, lru_ba, lru_wx, lru_bx, lru_lambda, attn_out_norm, lru_out_norm, w_out, norm2_g, w_router_group, b_router_group, w_router_expert, b_router_expert, w_gate, w_up, w_down):
    b, s, _ = x.shape
    pos_row = positions.reshape(b, 1, s)
    for l in range(w_ada.shape[0]):
        x = _layer(x, c, pos_row, w_ada[l], b_ada[l], norm1_g[l], w_in[l], q_a_norm[l], w_q_b[l],
                   kv_a_norm[l], w_kv_b[l], q_norm[l], k_norm[l], conv_w[l], conv_b[l], lru_wa[l],
                   lru_ba[l], lru_wx[l], lru_bx[l], lru_lambda[l], attn_out_norm[l], lru_out_norm[l],
                   w_out[l], norm2_g[l], w_router_group[l], b_router_group[l], w_router_expert[l],
                   b_router_expert[l], w_gate[l], w_up[l], w_down[l])
    return x
```

```python
import functools
import math

import numpy as np
import jax
import jax.numpy as jnp
from jax import lax
from jax.experimental import pallas as pl
from jax.experimental.pallas import tpu as pltpu

F32 = jnp.float32
BF16 = jnp.bfloat16

ATTN_HEADS = 4
QK_NOPE_DIM = 128
QK_ROPE_DIM = 64
QK_HEAD_DIM = QK_NOPE_DIM + QK_ROPE_DIM
V_HEAD_DIM = 128
ROPE_THETA = 10000.0
LRU_HEADS = 4
CONV_WIDTH = 4
LRU_C = 8.0
N_GROUPS = 4
TOP_K = 2
EPS = 1e-6

LANES = 128
SUBLANES = 8
QK_PAD_DIM = 256
V_PAD_DIM = 256
VMEM_LIMIT = 56 * 1024 * 1024

PROJ_TILE = 512
PROJ_CHUNK = 128
ATTN_TILE = 1024
ATTN_CHUNK = 128
LRU_TILE = 512
ROUTE_TILE = 512
COMBINE_TILE = 512
EXPERT_BLOCK = 256


def _cparams(*sem):
    return pltpu.CompilerParams(dimension_semantics=sem, vmem_limit_bytes=VMEM_LIMIT)


def _rms(x, n):
    return lax.rsqrt(jnp.sum(x * x, axis=-1, keepdims=True) * (1.0 / n) + EPS)


def _sigmoid(x):
    return 0.5 * jnp.tanh(0.5 * x) + 0.5


def _column(row, width=LANES):
    t = row.shape[-1]
    return jnp.broadcast_to(row, (width, t)).T


def _adaln_kernel(c_ref, w_ref, b_ref, o_ref):
    c = c_ref[...]
    sc = c * jax.nn.sigmoid(c)
    o_ref[...] = jnp.dot(sc.astype(BF16), w_ref[...].astype(BF16),
                         preferred_element_type=F32) + b_ref[...]


def _adaln(c, w_ada, b_ada):
    b, d = c.shape
    n_out = w_ada.shape[1]
    return pl.pallas_call(
        _adaln_kernel,
        grid=(n_out // d,),
        in_specs=[pl.BlockSpec((b, d), lambda j: (0, 0)),
                  pl.BlockSpec((d, d), lambda j: (0, j)),
                  pl.BlockSpec((1, d), lambda j: (0, j))],
        out_specs=pl.BlockSpec((b, d), lambda j: (0, j)),
        out_shape=jax.ShapeDtypeStruct((b, n_out), F32),
        compiler_params=_cparams("parallel"),
        name="adaln",
    )(c, w_ada, b_ada.reshape(1, n_out))


def _rope_tables(pos_row, inv_col):
    half = QK_ROPE_DIM // 2
    ang = inv_col * pos_row
    cos = jnp.cos(ang)
    sin = jnp.sin(ang)
    t = ang.shape[-1]
    zeros = lambda n: jnp.zeros((n, t), F32)
    c = jnp.concatenate([cos, cos, zeros(LANES - QK_ROPE_DIM)], axis=0)
    sa = jnp.concatenate([-sin, zeros(LANES - half)], axis=0)
    sb = jnp.concatenate([zeros(half), sin, zeros(LANES - QK_ROPE_DIM)], axis=0)
    return c.T, sa.T, sb.T


def _rope(x, c, sa, sb):
    return x * c + pltpu.roll(x, LANES - QK_ROPE_DIM // 2, 1) * sa + pltpu.roll(x, QK_ROPE_DIM // 2, 1) * sb


def _in_proj_kernel(x_ref, pos_ref, mod_ref, g1_ref, w_in_ref, qa_ref, wq_ref, kva_ref, wkv_ref,
                    qn_ref, kn_nope_ref, kn_rope_ref, inv_ref,
                    q_ref, k_ref, v_ref, xl_ref, gl_ref, *, q_lora, kv_lora, lru_w, chunk):
    t, d = x_ref.shape[1:]
    shift = mod_ref[0, 0:1, :]
    gain = g1_ref[...] * (1.0 + mod_ref[0, 1:2, :])
    c_all, sa_all, sb_all = _rope_tables(pos_ref[0].astype(F32), inv_ref[...])
    sm_scale = QK_HEAD_DIM ** -0.5 * math.log2(math.e)

    for r0 in range(0, t, chunk):
        rows = slice(r0, r0 + chunk)
        x = x_ref[0, rows, :]
        h = x * _rms(x, d) * gain + shift
        proj = jnp.dot(h.astype(BF16), w_in_ref[...], preferred_element_type=F32)
        o = 0
        q_lat = proj[:, o:o + q_lora]; o += q_lora
        kv_lat = proj[:, o:o + kv_lora]; o += kv_lora
        x_lru = proj[:, o:o + lru_w]; o += lru_w
        g_lru = proj[:, o:o + lru_w]; o += lru_w
        k_pe = proj[:, o:o + LANES]

        xl_ref[0, rows, :] = x_lru
        gl_ref[0, rows, :] = jax.nn.gelu(g_lru).astype(BF16)

        c, sa, sb = c_all[rows], sa_all[rows], sb_all[rows]
        qn = q_lat * _rms(q_lat, q_lora) * qa_ref[...]
        q_full = jnp.dot(qn.astype(BF16), wq_ref[...], preferred_element_type=F32)
        kvn = kv_lat * _rms(kv_lat, kv_lora) * kva_ref[...]
        kv_full = jnp.dot(kvn.astype(BF16), wkv_ref[...], preferred_element_type=F32)

        pe_ss = jnp.sum(k_pe * k_pe, axis=-1, keepdims=True)
        pe_rot = _rope(k_pe * kn_rope_ref[...], c, sa, sb)
        for hd in range(ATTN_HEADS):
            qh = q_full[:, hd * QK_PAD_DIM:(hd + 1) * QK_PAD_DIM]
            qh = qh * (_rms(qh, QK_HEAD_DIM) * sm_scale) * qn_ref[...]
            q_ref[0, hd, rows, 0:QK_NOPE_DIM] = qh[:, 0:QK_NOPE_DIM].astype(BF16)
            q_ref[0, hd, rows, QK_NOPE_DIM:QK_PAD_DIM] = _rope(qh[:, QK_NOPE_DIM:QK_PAD_DIM],
                                                               c, sa, sb).astype(BF16)
            base = hd * (QK_NOPE_DIM + V_HEAD_DIM)
            kn = kv_full[:, base:base + QK_NOPE_DIM]
            vv = kv_full[:, base + QK_NOPE_DIM:base + QK_NOPE_DIM + V_HEAD_DIM]
            r = lax.rsqrt((jnp.sum(kn * kn, axis=-1, keepdims=True) + pe_ss) * (1.0 / QK_HEAD_DIM) + EPS)
            k_ref[0, hd, rows, 0:QK_NOPE_DIM] = (kn * r * kn_nope_ref[...]).astype(BF16)
            k_ref[0, hd, rows, QK_NOPE_DIM:QK_PAD_DIM] = (pe_rot * r).astype(BF16)
            v_ref[0, hd, rows, :] = vv.astype(BF16)


def _in_proj(x, pos_row, mod, norm1_g, w_in_p, q_a_norm, w_q_p, kv_a_norm, w_kv, q_norm_p,
             k_norm_nope, k_norm_rope_p, inv_col, *, q_lora, kv_lora, lru_w):
    b, s, d = x.shape
    t = min(PROJ_TILE, s)
    full = lambda a: pl.BlockSpec(a.shape, lambda i, j: (0,) * a.ndim)
    kern = functools.partial(_in_proj_kernel, q_lora=q_lora, kv_lora=kv_lora, lru_w=lru_w,
                             chunk=min(PROJ_CHUNK, t))
    head_out = lambda w: pl.BlockSpec((1, ATTN_HEADS, t, w), lambda i, j: (i, 0, j, 0))
    return pl.pallas_call(
        kern,
        grid=(b, s // t),
        in_specs=[pl.BlockSpec((1, t, d), lambda i, j: (i, j, 0)),
                  pl.BlockSpec((1, 1, t), lambda i, j: (i, 0, j)),
                  pl.BlockSpec((1,) + mod.shape[1:], lambda i, j: (i, 0, 0)),
                  full(norm1_g), full(w_in_p), full(q_a_norm), full(w_q_p), full(kv_a_norm),
                  full(w_kv), full(q_norm_p), full(k_norm_nope), full(k_norm_rope_p), full(inv_col)],
        out_specs=[head_out(QK_PAD_DIM), head_out(QK_PAD_DIM), head_out(V_HEAD_DIM),
                   pl.BlockSpec((1, t, lru_w), lambda i, j: (i, j, 0)),
                   pl.BlockSpec((1, t, lru_w), lambda i, j: (i, j, 0))],
        out_shape=[jax.ShapeDtypeStruct((b, ATTN_HEADS, s, QK_PAD_DIM), BF16),
                   jax.ShapeDtypeStruct((b, ATTN_HEADS, s, QK_PAD_DIM), BF16),
                   jax.ShapeDtypeStruct((b, ATTN_HEADS, s, V_HEAD_DIM), BF16),
                   jax.ShapeDtypeStruct((b, s, lru_w), F32),
                   jax.ShapeDtypeStruct((b, s, lru_w), BF16)],
        compiler_params=_cparams("parallel", "parallel"),
        name="in_proj",
    )(x, pos_row, mod, norm1_g, w_in_p, q_a_norm, w_q_p, kv_a_norm, w_kv, q_norm_p,
      k_norm_nope, k_norm_rope_p, inv_col)


def _attn_kernel(q_ref, k_ref, v_ref, o_ref, m_sc, acc_sc, s_sc, v_sc, *, tile, chunk):
    qi = pl.program_id(2)

    @pl.when(qi == 0)
    def _():
        v_sc[:, 0:V_HEAD_DIM] = v_ref[0, 0]
        v_sc[:, V_HEAD_DIM:V_PAD_DIM] = jnp.ones((v_sc.shape[0], V_PAD_DIM - V_HEAD_DIM), BF16)

    m_sc[...] = jnp.full(m_sc.shape, -jnp.inf, F32)
    acc_sc[...] = jnp.zeros(acc_sc.shape, F32)

    def scores(j, slot):
        start = pl.multiple_of(j * tile, tile)
        k = k_ref[0, 0, pl.ds(start, tile), :]
        s_sc[slot] = lax.dot_general(q_ref[0, 0], k, (((1,), (1,)), ((), ())),
                                     preferred_element_type=F32)

    def accumulate(j, slot, masked):
        start = pl.multiple_of(j * tile, tile)
        for r in range(tile // chunk):
            rows = slice(r * chunk, (r + 1) * chunk)
            cols = -(-(r + 1) * chunk // LANES) * LANES if masked else tile
            s = s_sc[slot, rows, 0:cols]
            if masked:
                row = lax.broadcasted_iota(jnp.int32, s.shape, 0) + r * chunk
                col = lax.broadcasted_iota(jnp.int32, s.shape, 1)
                s = jnp.where(col <= row, s, -jnp.inf)
            v = v_sc[pl.ds(start, cols), :]
            m_old = m_sc[rows, :]
            m_new = jnp.maximum(m_old, jnp.max(s, axis=-1, keepdims=True))
            alpha = jnp.exp2(m_old - m_new)
            p = jnp.exp2(s - jnp.tile(m_new, (1, cols // LANES)))
            acc_sc[rows, :] = (jnp.tile(alpha, (1, V_PAD_DIM // LANES)) * acc_sc[rows, :]
                               + jnp.dot(p.astype(BF16), v, preferred_element_type=F32))
            m_sc[rows, :] = m_new

    scores(0, 0)

    def pair(p, carry):
        scores(2 * p + 1, 1)
        accumulate(2 * p, 0, False)
        scores(2 * p + 2, 0)
        accumulate(2 * p + 1, 1, False)
        return carry

    lax.fori_loop(0, qi // 2, pair, 0)

    @pl.when(qi % 2 == 0)
    def _():
        accumulate(qi, 0, True)

    @pl.when(qi % 2 == 1)
    def _():
        scores(qi, 1)
        accumulate(qi - 1, 0, False)
        accumulate(qi, 1, True)

    o_ref[0] = (acc_sc[:, 0:V_HEAD_DIM] / acc_sc[:, V_HEAD_DIM:2 * V_HEAD_DIM]).astype(o_ref.dtype)


def _attention(q, k, v):
    b, h, s, _ = q.shape
    t = min(ATTN_TILE, s)
    assert V_HEAD_DIM == LANES and V_PAD_DIM == 2 * V_HEAD_DIM
    kern = functools.partial(_attn_kernel, tile=t, chunk=min(ATTN_CHUNK, t))
    return pl.pallas_call(
        kern,
        grid=(b, h, s // t),
        in_specs=[pl.BlockSpec((1, 1, t, QK_PAD_DIM), lambda i, j, n: (i, j, n, 0)),
                  pl.BlockSpec((1, 1, s, QK_PAD_DIM), lambda i, j, n: (i, j, 0, 0)),
                  pl.BlockSpec((1, 1, s, V_HEAD_DIM), lambda i, j, n: (i, j, 0, 0))],
        out_specs=pl.BlockSpec((1, t, V_HEAD_DIM), lambda i, j, n: (i, n, j)),
        out_shape=jax.ShapeDtypeStruct((b, s, h * V_HEAD_DIM), BF16),
        scratch_shapes=[pltpu.VMEM((t, LANES), F32), pltpu.VMEM((t, V_PAD_DIM), F32),
                        pltpu.VMEM((2, t, t), F32), pltpu.VMEM((s, V_PAD_DIM), BF16)],
        compiler_params=_cparams("parallel", "parallel", "arbitrary"),
        name="attention",
    )(q, k, v)


def _lru_kernel(xl_ref, gl_ref, pos_ref, cw_ref, cb_ref, wax_ref, ba_ref, bx_ref, lam_ref, gn_ref,
                o_ref, ext_sc, a_sc, b_sc, h_sc, carry_sc):
    t, c = a_sc.shape
    head_w = c // LRU_HEADS

    @pl.when(pl.program_id(1) == 0)
    def _():
        ext_sc[0:SUBLANES, :] = jnp.zeros((SUBLANES, c), F32)
        carry_sc[...] = jnp.zeros(carry_sc.shape, F32)

    x = xl_ref[0]
    ext_sc[SUBLANES:SUBLANES + t, :] = x
    xc = cb_ref[...] + x * cw_ref[CONV_WIDTH - 1:CONV_WIDTH, :]
    for back in range(1, CONV_WIDTH):
        tap = CONV_WIDTH - 1 - back
        xc = xc + ext_sc[SUBLANES - back:SUBLANES - back + t, :] * cw_ref[tap:tap + 1, :]
    ext_sc[0:SUBLANES, :] = x[t - SUBLANES:t, :]

    reset = _column((pos_ref[0] == 0).astype(F32)) > 0.5
    lam = lam_ref[...]
    neg_sp = -LRU_C * (jnp.maximum(-lam, 0.0) + jnp.log1p(jnp.exp(-jnp.abs(lam))))
    for hd in range(LRU_HEADS):
        sl = slice(hd * head_w, (hd + 1) * head_w)
        xh = xc[:, sl]
        gates = jnp.dot(xh.astype(BF16), wax_ref[hd], preferred_element_type=F32)
        r = _sigmoid(gates[:, :head_w] + ba_ref[:, sl])
        i = _sigmoid(gates[:, head_w:] + bx_ref[:, sl])
        a = jnp.exp(r * neg_sp[:, sl])
        gap = jnp.maximum(1.0 - a * a, 0.0)
        mult = jnp.where(gap > 0.0, gap * lax.rsqrt(gap), 0.0)
        a = jnp.where(reset, 0.0, a)
        mult = jnp.where(reset, 1.0, mult)
        a_sc[:, sl] = a
        b_sc[:, sl] = mult * (i * xh)

    srow = lax.broadcasted_iota(jnp.int32, (SUBLANES, c), 0)

    def group(g, h_prev):
        start = pl.multiple_of(g * SUBLANES, SUBLANES)
        a = a_sc[pl.ds(start, SUBLANES), :]
        bb = b_sc[pl.ds(start, SUBLANES), :]
        sh = 1
        while sh < SUBLANES:
            keep = srow >= sh
            a_prev = jnp.where(keep, pltpu.roll(a, sh, 0), 1.0)
            b_prev = jnp.where(keep, pltpu.roll(bb, sh, 0), 0.0)
            bb = bb + a * b_prev
            a = a * a_prev
            sh *= 2
        hh = bb + a * h_prev
        h_sc[pl.ds(start, SUBLANES), :] = hh
        return jnp.broadcast_to(hh[SUBLANES - 1:SUBLANES, :], (SUBLANES, c))

    carry_sc[...] = lax.fori_loop(0, t // SUBLANES, group, carry_sc[...], unroll=4)

    y = h_sc[...] * gl_ref[0].astype(F32)
    o_ref[0] = (y * _rms(y, c) * gn_ref[...]).astype(o_ref.dtype)


def _rg_lru(x_lru, gelu_g, pos_row, conv_w, conv_b, w_ax, b_a, b_x, lam, out_norm):
    b, s, c = x_lru.shape
    t = min(LRU_TILE, s)
    full = lambda a: pl.BlockSpec(a.shape, lambda i, j: (0,) * a.ndim)
    return pl.pallas_call(
        _lru_kernel,
        grid=(b, s // t),
        in_specs=[pl.BlockSpec((1, t, c), lambda i, j: (i, j, 0)),
                  pl.BlockSpec((1, t, c), lambda i, j: (i, j, 0)),
                  pl.BlockSpec((1, 1, t), lambda i, j: (i, 0, j)),
                  full(conv_w), full(conv_b), full(w_ax), full(b_a), full(b_x), full(lam),
                  full(out_norm)],
        out_specs=pl.BlockSpec((1, t, c), lambda i, j: (i, j, 0)),
        out_shape=jax.ShapeDtypeStruct((b, s, c), BF16),
        scratch_shapes=[pltpu.VMEM((t + SUBLANES, c), F32), pltpu.VMEM((t, c), F32),
                        pltpu.VMEM((t, c), F32), pltpu.VMEM((t, c), F32),
                        pltpu.VMEM((SUBLANES, c), F32)],
        compiler_params=_cparams("parallel", "arbitrary"),
        name="rg_lru",
    )(x_lru, gelu_g, pos_row, conv_w, conv_b, w_ax, b_a, b_x, lam, out_norm)


def _route_kernel(x_ref, at_ref, lr_ref, mod_ref, an_ref, wa_ref, wl_ref, g2_ref, wr_ref, br_ref,
                  x1_ref, h2_ref, ids_ref, rank_ref, wts_ref, cnt_ref, carry_sc,
                  *, n_experts, group_row0, expert_row0):
    @pl.when(pl.program_id(0) == 0)
    def _():
        carry_sc[...] = jnp.zeros(carry_sc.shape, F32)

    x = x_ref[0]
    t, d = x.shape
    gate1 = mod_ref[0, 2:3, :]
    shift2 = mod_ref[0, 3:4, :]
    gain2 = g2_ref[...] * (1.0 + mod_ref[0, 4:5, :])
    at = at_ref[0].astype(F32)
    at_n = at * _rms(at, at.shape[-1]) * an_ref[...]
    mix = jnp.dot(at_n.astype(BF16), wa_ref[...], preferred_element_type=F32)
    mix = mix + jnp.dot(lr_ref[0], wl_ref[...], preferred_element_type=F32)
    x1 = x + gate1 * mix
    x1_ref[0] = x1
    h2 = x1 * _rms(x1, d) * gain2 + shift2
    h2_ref[0] = h2

    logits = lax.dot_general(wr_ref[...], h2.astype(BF16), (((1,), (1,)), ((), ())),
                             preferred_element_type=F32) + br_ref[...]
    per_group = n_experts // N_GROUPS
    lg = logits[group_row0:group_row0 + N_GROUPS, :]
    grow = lax.broadcasted_iota(jnp.int32, lg.shape, 0)
    gmax = jnp.max(lg, axis=0, keepdims=True)
    g_idx = jnp.min(jnp.where(lg == gmax, grow, N_GROUPS), axis=0, keepdims=True)
    g_p = 1.0 / jnp.sum(jnp.exp(lg - gmax), axis=0, keepdims=True)
    sel = jnp.zeros((per_group, t), F32)
    for g in range(N_GROUPS):
        r0 = expert_row0 + g * per_group
        sel = sel + jnp.where(g_idx == g, logits[r0:r0 + per_group, :], 0.0)
    erow = lax.broadcasted_iota(jnp.int32, sel.shape, 0)
    m1 = jnp.max(sel, axis=0, keepdims=True)
    i1 = jnp.min(jnp.where(sel == m1, erow, per_group), axis=0, keepdims=True)
    sel2 = jnp.where(erow == i1, -jnp.inf, sel)
    m2 = jnp.max(sel2, axis=0, keepdims=True)
    i2 = jnp.min(jnp.where(sel2 == m2, erow, per_group), axis=0, keepdims=True)
    e21 = jnp.exp(m2 - m1)
    w1 = g_p / (1.0 + e21)
    w2 = g_p * e21 / (1.0 + e21)
    e1 = g_idx * per_group + i1
    e2 = g_idx * per_group + i2
    ids_ref[0:1, :] = e1
    ids_ref[1:2, :] = e2
    wts_ref[0:1, :] = w1
    wts_ref[1:2, :] = w2

    xrow = lax.broadcasted_iota(jnp.int32, (n_experts, t), 0)
    o1 = (xrow == e1).astype(F32)
    o2 = (xrow == e2).astype(F32)
    both = o1 + o2
    tr = lax.broadcasted_iota(jnp.int32, (t, t), 0)
    tc = lax.broadcasted_iota(jnp.int32, (t, t), 1)
    upper = jnp.where(tr < tc, 1.0, 0.0).astype(BF16)
    before = jnp.dot(both.astype(BF16), upper, preferred_element_type=F32) + carry_sc[...]
    rank_ref[0:1, :] = jnp.sum(o1 * before, axis=0, keepdims=True).astype(jnp.int32)
    rank_ref[1:2, :] = jnp.sum(o2 * before, axis=0, keepdims=True).astype(jnp.int32)
    carry_sc[...] = carry_sc[...] + jnp.sum(both, axis=1, keepdims=True)
    cnt_ref[...] = jnp.broadcast_to(carry_sc[...], cnt_ref.shape)


def _out_route(x, attn, lru_n, mod, attn_norm, w_out_a, w_out_l, norm2_g, w_r_t, b_r_col,
               *, n_experts, group_row0, expert_row0):
    b, s, d = x.shape
    t = min(ROUTE_TILE, s)
    nt = s // t
    n = b * s
    full = lambda a: pl.BlockSpec(a.shape, lambda i: (0,) * a.ndim)
    tok = lambda w: pl.BlockSpec((1, t, w), lambda i: (i // nt, i % nt, 0))
    kern = functools.partial(_route_kernel, n_experts=n_experts, group_row0=group_row0,
                             expert_row0=expert_row0)
    return pl.pallas_call(
        kern,
        grid=(b * nt,),
        in_specs=[tok(d), tok(attn.shape[-1]), tok(lru_n.shape[-1]),
                  pl.BlockSpec((1,) + mod.shape[1:], lambda i: (i // nt, 0, 0)),
                  full(attn_norm), full(w_out_a), full(w_out_l), full(norm2_g), full(w_r_t),
                  full(b_r_col)],
        out_specs=[tok(d), tok(d),
                   pl.BlockSpec((TOP_K, t), lambda i: (0, i)),
                   pl.BlockSpec((TOP_K, t), lambda i: (0, i)),
                   pl.BlockSpec((TOP_K, t), lambda i: (0, i)),
                   pl.BlockSpec((n_experts, LANES), lambda i: (0, 0))],
        out_shape=[jax.ShapeDtypeStruct((b, s, d), F32),
                   jax.ShapeDtypeStruct((b, s, d), F32),
                   jax.ShapeDtypeStruct((TOP_K, n), jnp.int32),
                   jax.ShapeDtypeStruct((TOP_K, n), jnp.int32),
                   jax.ShapeDtypeStruct((TOP_K, n), F32),
                   jax.ShapeDtypeStruct((n_experts, LANES), F32)],
        scratch_shapes=[pltpu.VMEM((n_experts, 1), F32)],
        compiler_params=_cparams("arbitrary"),
        name="out_route",
    )(x, attn, lru_n, mod, attn_norm, w_out_a, w_out_l, norm2_g, w_r_t, b_r_col)


def _row_copy(src, src_row, dst, dst_row, sem):
    return pltpu.make_async_copy(src.at[pl.ds(src_row, 1), :], dst.at[pl.ds(dst_row, 1), :], sem)


def _invert_kernel(dest_ref, cnt_ref, start_ref, end_ref, tok_ref, *, n_tokens, block):
    def clear(i, carry):
        tok_ref[i] = 0
        return carry

    def per_expert(e, carry):
        first = start_ref[e] + cnt_ref[e]
        n_pad = (block - cnt_ref[e] % block) % block
        lax.fori_loop(first, first + n_pad, clear, 0)
        return carry

    lax.fori_loop(0, cnt_ref.shape[0], per_expert, 0)
    lax.fori_loop(end_ref[0], tok_ref.shape[0], clear, 0)
    for k in range(TOP_K):
        def place(t, carry, k=k):
            tok_ref[dest_ref[k * n_tokens + t]] = t
            return carry

        lax.fori_loop(0, n_tokens, place, 0, unroll=8)


def _invert(dest_flat, counts, starts, rows_used, cap, n_tokens, *, block):
    smem = pl.BlockSpec(memory_space=pltpu.SMEM)
    return pl.pallas_call(
        functools.partial(_invert_kernel, n_tokens=n_tokens, block=block),
        in_specs=[smem, smem, smem, smem],
        out_specs=smem,
        out_shape=jax.ShapeDtypeStruct((cap,), jnp.int32),
        name="invert",
    )(dest_flat, counts, starts, rows_used)


def _expert_kernel(be_ref, nb_ref, tok_ref, tok_nx_ref, h2_ref, wg_ref, wu_ref, wd_ref, ys_ref,
                   xbuf, wgu_sc, wdn_sc, sem, *, d_ff, block):
    j = pl.program_id(0)
    n_live = nb_ref[0]
    live = jnp.minimum(j, n_live - 1)
    slot = j % 2

    def gather(t_ref, sl):
        for r in range(block):
            _row_copy(h2_ref, t_ref[r], xbuf.at[sl], r, sem.at[sl]).start(priority=r % 2)

    @pl.when(j == 0)
    def _():
        gather(tok_ref, slot)

    @pl.when(j + 1 < n_live)
    def _():
        gather(tok_nx_ref, 1 - slot)

    @pl.when((j == 0) | (be_ref[live] != be_ref[jnp.maximum(live - 1, 0)]))
    def _():
        wgu_sc[:, 0:d_ff] = wg_ref[0].astype(BF16)
        wgu_sc[:, d_ff:2 * d_ff] = wu_ref[0].astype(BF16)
        wdn_sc[...] = wd_ref[0].astype(BF16)

    @pl.when(j < n_live)
    def _():
        pltpu.make_async_copy(h2_ref.at[pl.ds(0, block), :], xbuf.at[slot], sem.at[slot]).wait()
        xb = xbuf[slot].astype(BF16)
        gu = jnp.dot(xb, wgu_sc[...], preferred_element_type=F32)
        gt = gu[:, :d_ff]
        hid = gt * jax.nn.sigmoid(gt) * gu[:, d_ff:]
        ys_ref[...] = jnp.dot(hid.astype(BF16), wdn_sc[...], preferred_element_type=F32)

    @pl.when(j >= n_live)
    def _():
        ys_ref[...] = jnp.zeros(ys_ref.shape, ys_ref.dtype)


def _experts(block_e, n_blocks_used, slot_tok, h2, w_gate, w_up, w_down, *, block):
    cap = slot_tok.shape[0]
    d = w_down.shape[2]
    d_ff = w_down.shape[1]
    n_blocks = cap // block
    kern = functools.partial(_expert_kernel, d_ff=d_ff, block=block)
    live = lambda j, nb: jnp.minimum(j, nb[0] - 1)
    nxt = lambda j, nb: jnp.minimum(j + 1, nb[0] - 1)
    return pl.pallas_call(
        kern,
        grid_spec=pltpu.PrefetchScalarGridSpec(
            num_scalar_prefetch=2,
            grid=(n_blocks,),
            in_specs=[pl.BlockSpec((block,), lambda j, be, nb: (live(j, nb),), memory_space=pltpu.SMEM),
                      pl.BlockSpec((block,), lambda j, be, nb: (nxt(j, nb),), memory_space=pltpu.SMEM),
                      pl.BlockSpec(memory_space=pl.ANY),
                      pl.BlockSpec((1, d, d_ff), lambda j, be, nb: (be[live(j, nb)], 0, 0)),
                      pl.BlockSpec((1, d, d_ff), lambda j, be, nb: (be[live(j, nb)], 0, 0)),
                      pl.BlockSpec((1, d_ff, d), lambda j, be, nb: (be[live(j, nb)], 0, 0))],
            out_specs=pl.BlockSpec((block, d), lambda j, be, nb: (j, 0)),
            scratch_shapes=[pltpu.VMEM((2, block, d), F32), pltpu.VMEM((d, 2 * d_ff), BF16),
                            pltpu.VMEM((d_ff, d), BF16), pltpu.SemaphoreType.DMA((2,))]),
        out_shape=jax.ShapeDtypeStruct((cap, d), F32),
        compiler_params=_cparams("arbitrary"),
        name="experts",
    )(block_e, n_blocks_used, slot_tok, slot_tok, h2, w_gate, w_up, w_down)


def _combine_kernel(dest_ref, dest_nx_ref, wts_ref, x1_ref, mod_ref, ys_ref, o_ref, buf, sem, *,
                    tile):
    i = pl.program_id(0)
    slot = i % 2

    def gather(d_ref, sl):
        for tk in range(tile):
            for k in range(TOP_K):
                _row_copy(ys_ref, d_ref[k, tk], buf.at[sl, k], tk, sem.at[sl]).start(priority=k % 2)

    @pl.when(i == 0)
    def _():
        gather(dest_ref, slot)

    @pl.when(i + 1 < pl.num_programs(0))
    def _():
        gather(dest_nx_ref, 1 - slot)

    for k in range(TOP_K):
        pltpu.make_async_copy(ys_ref.at[pl.ds(0, tile), :], buf.at[slot, k], sem.at[slot]).wait()

    d = o_ref.shape[-1]
    w0 = jnp.tile(_column(wts_ref[0:1, :]), (1, d // LANES))
    w1 = jnp.tile(_column(wts_ref[1:2, :]), (1, d // LANES))
    gate2 = mod_ref[0, 5:6, :]
    o_ref[0] = x1_ref[0] + gate2 * (w0 * buf[slot, 0] + w1 * buf[slot, 1])


def _combine(dest, wts, x1, mod, ys):
    b, s, d = x1.shape
    w = ys.shape[1]
    t = min(COMBINE_TILE, s)
    nt = s // t
    last = b * nt - 1
    kern = functools.partial(_combine_kernel, tile=t)
    smem_now = pl.BlockSpec((TOP_K, t), lambda i: (0, i), memory_space=pltpu.SMEM)
    smem_next = pl.BlockSpec((TOP_K, t), lambda i: (0, jnp.minimum(i + 1, last)),
                             memory_space=pltpu.SMEM)
    return pl.pallas_call(
        kern,
        grid=(b * nt,),
        in_specs=[smem_now, smem_next,
                  pl.BlockSpec((TOP_K, t), lambda i: (0, i)),
                  pl.BlockSpec((1, t, d), lambda i: (i // nt, i % nt, 0)),
                  pl.BlockSpec((1,) + mod.shape[1:], lambda i: (i // nt, 0, 0)),
                  pl.BlockSpec(memory_space=pl.ANY)],
        out_specs=pl.BlockSpec((1, t, d), lambda i: (i // nt, i % nt, 0)),
        out_shape=jax.ShapeDtypeStruct((b, s, d), F32),
        scratch_shapes=[pltpu.VMEM((2, TOP_K, t, w), F32), pltpu.SemaphoreType.DMA((2,))],
        compiler_params=_cparams("arbitrary"),
        name="combine",
    )(dest, dest, wts, x1, mod, ys)


def _layer(x, c, pos_row, w_ada, b_ada, norm1_g, w_in, q_a_norm, w_q_b, kv_a_norm, w_kv_b, q_norm,
           k_norm, conv_w, conv_b, lru_wa, lru_ba, lru_wx, lru_bx, lru_lambda, attn_out_norm,
           lru_out_norm, w_out, norm2_g, w_rg, b_rg, w_re, b_re, w_gate, w_up, w_down):
    b, s, d = x.shape
    n = b * s
    q_lora = q_a_norm.shape[0]
    kv_lora = kv_a_norm.shape[0]
    lru_w = lru_lambda.shape[0]
    attn_w = ATTN_HEADS * V_HEAD_DIM
    n_experts = w_gate.shape[0]
    row = lambda v: v.reshape(1, -1)

    mod = _adaln(c, w_ada, b_ada).reshape(b, 6, d)

    o1 = q_lora
    o2 = o1 + kv_lora
    o3 = o2 + QK_ROPE_DIM
    o4 = o3 + lru_w
    w_in_p = jnp.concatenate(
        [w_in[:, :o2], w_in[:, o3:o4], w_in[:, o4:], w_in[:, o2:o3],
         jnp.zeros((d, LANES - QK_ROPE_DIM), w_in.dtype)], axis=1).astype(BF16)
    pad_h = QK_PAD_DIM - QK_HEAD_DIM
    w_q_p = jnp.pad(w_q_b.reshape(q_lora, ATTN_HEADS, QK_HEAD_DIM),
                    ((0, 0), (0, 0), (0, pad_h))).reshape(q_lora, ATTN_HEADS * QK_PAD_DIM).astype(BF16)
    q_norm_p = row(jnp.pad(q_norm, (0, pad_h)))
    k_norm_nope = row(k_norm[:QK_NOPE_DIM])
    k_norm_rope_p = row(jnp.pad(k_norm[QK_NOPE_DIM:], (0, LANES - QK_ROPE_DIM)))
    inv = 1.0 / (ROPE_THETA ** (np.arange(0, QK_ROPE_DIM, 2, dtype=np.float32) / QK_ROPE_DIM))
    inv_col = jnp.asarray(inv.reshape(-1, 1))

    q, k, v, x_lru, gelu_g = _in_proj(
        x, pos_row, mod, row(norm1_g), w_in_p, row(q_a_norm), w_q_p, row(kv_a_norm),
        w_kv_b.astype(BF16), q_norm_p, k_norm_nope, k_norm_rope_p, inv_col,
        q_lora=q_lora, kv_lora=kv_lora, lru_w=lru_w)

    attn = _attention(q, k, v)

    w_ax = jnp.concatenate([lru_wa, lru_wx], axis=-1).astype(BF16)
    lru_n = _rg_lru(x_lru, gelu_g, pos_row, conv_w, row(conv_b), w_ax, row(lru_ba), row(lru_bx),
                    row(lru_lambda), row(lru_out_norm))

    group_row0, expert_row0 = 0, SUBLANES
    w_r_t = jnp.concatenate([w_rg.T, jnp.zeros((expert_row0 - N_GROUPS, d), w_rg.dtype), w_re.T],
                            axis=0).astype(BF16)
    b_r_col = jnp.concatenate([b_rg, jnp.zeros((expert_row0 - N_GROUPS,), b_rg.dtype),
                               b_re]).reshape(-1, 1)
    x1, h2, ids, ranks, wts, cnt = _out_route(
        x, attn, lru_n, mod, row(attn_out_norm), w_out[:attn_w].astype(BF16),
        w_out[attn_w:].astype(BF16), row(norm2_g), w_r_t, b_r_col,
        n_experts=n_experts, group_row0=group_row0, expert_row0=expert_row0)

    blk = EXPERT_BLOCK
    counts = cnt[:, 0].astype(jnp.int32)
    padded = (counts + blk - 1) // blk * blk
    pad_ends = jnp.cumsum(padded)
    starts = (pad_ends - padded).astype(jnp.int32)
    dest = ranks
    for e in range(n_experts):
        dest = dest + jnp.where(ids == e, starts[e], 0)
    cap = (n * TOP_K + n_experts * (blk - 1) + blk - 1) // blk * blk
    n_blocks = cap // blk
    block_first_row = jnp.arange(n_blocks, dtype=jnp.int32) * blk
    block_e = jnp.minimum(jnp.sum(pad_ends[None, :] <= block_first_row[:, None], axis=1),
                          n_experts - 1).astype(jnp.int32)
    n_used = (pad_ends[-1:] // blk).astype(jnp.int32)

    slot_tok = _invert(dest.reshape(-1), counts, starts, pad_ends[-1:].astype(jnp.int32), cap, n, block=blk)
    ys = _experts(block_e, n_used, slot_tok, h2.reshape(n, d), w_gate, w_up, w_down, block=blk)
    return _combine(dest, wts, x1, mod, ys)


def kernel(x, c, positions, w_ada, b_ada, norm1_g, w_in, q_a_norm, w_q_b, kv_a_norm, w_kv_b, q_norm, k_norm, conv_w, conv_b, lru_wa, lru_ba, lru_wx, lru_bx, lru_lambda, attn_out_norm, lru_out_norm, w_out, norm2_g, w_router_group, b_router_group, w_router_expert, b_router_expert, w_gate, w_up, w_down):
    b, s, _ = x.shape
    pos_row = positions.reshape(b, 1, s)
    for l in range(w_ada.shape[0]):
        x = _layer(x, c, pos_row, w_ada[l], b_ada[l], norm1_g[l], w_in[l], q_a_norm[l], w_q_b[l],
                   kv_a_norm[l], w_kv_b[l], q_norm[l], k_norm[l], conv_w[l], conv_b[l], lru_wa[l],
                   lru_ba[l], lru_wx[l], lru_bx[l], lru_lambda[l], attn_out_norm[l], lru_out_norm[l],
                   w_out[l], norm2_g[l], w_router_group[l], b_router_group[l], w_router_expert[l],
                   b_router_expert[l], w_gate[l], w_up[l], w_down[l])
    return x
```

```python
import functools
import math

import numpy as np
import jax
import jax.numpy as jnp
from jax import lax
from jax.experimental import pallas as pl
from jax.experimental.pallas import tpu as pltpu

F32 = jnp.float32
BF16 = jnp.bfloat16

ATTN_HEADS = 4
QK_NOPE_DIM = 128
QK_ROPE_DIM = 64
QK_HEAD_DIM = QK_NOPE_DIM + QK_ROPE_DIM
V_HEAD_DIM = 128
ROPE_THETA = 10000.0
LRU_HEADS = 4
CONV_WIDTH = 4
LRU_C = 8.0
N_GROUPS = 4
TOP_K = 2
EPS = 1e-6

LANES = 128
SUBLANES = 8
QK_PAD_DIM = 256
V_PAD_DIM = 256
VMEM_LIMIT = 56 * 1024 * 1024

PROJ_TILE = 512
PROJ_CHUNK = 128
ATTN_TILE = 1024
ATTN_CHUNK = 128
LRU_TILE = 512
ROUTE_TILE = 512
COMBINE_TILE = 512
EXPERT_BLOCK = 256


def _cparams(*sem):
    return pltpu.CompilerParams(dimension_semantics=sem, vmem_limit_bytes=VMEM_LIMIT)


def _rms(x, n):
    return lax.rsqrt(jnp.sum(x * x, axis=-1, keepdims=True) * (1.0 / n) + EPS)


def _sigmoid(x):
    return 0.5 * jnp.tanh(0.5 * x) + 0.5


def _column(row, width=LANES):
    t = row.shape[-1]
    return jnp.broadcast_to(row, (width, t)).T


def _adaln_kernel(c_ref, w_ref, b_ref, o_ref):
    c = c_ref[...]
    sc = c * jax.nn.sigmoid(c)
    o_ref[...] = jnp.dot(sc.astype(BF16), w_ref[...].astype(BF16),
                         preferred_element_type=F32) + b_ref[...]


def _adaln(c, w_ada, b_ada):
    b, d = c.shape
    n_out = w_ada.shape[1]
    return pl.pallas_call(
        _adaln_kernel,
        grid=(n_out // d,),
        in_specs=[pl.BlockSpec((b, d), lambda j: (0, 0)),
                  pl.BlockSpec((d, d), lambda j: (0, j)),
                  pl.BlockSpec((1, d), lambda j: (0, j))],
        out_specs=pl.BlockSpec((b, d), lambda j: (0, j)),
        out_shape=jax.ShapeDtypeStruct((b, n_out), F32),
        compiler_params=_cparams("parallel"),
        name="adaln",
    )(c, w_ada, b_ada.reshape(1, n_out))


def _rope_tables(pos_row, inv_col):
    half = QK_ROPE_DIM // 2
    ang = inv_col * pos_row
    cos = jnp.cos(ang)
    sin = jnp.sin(ang)
    t = ang.shape[-1]
    zeros = lambda n: jnp.zeros((n, t), F32)
    c = jnp.concatenate([cos, cos, zeros(LANES - QK_ROPE_DIM)], axis=0)
    sa = jnp.concatenate([-sin, zeros(LANES - half)], axis=0)
    sb = jnp.concatenate([zeros(half), sin, zeros(LANES - QK_ROPE_DIM)], axis=0)
    return c.T, sa.T, sb.T


def _rope(x, c, sa, sb):
    return x * c + pltpu.roll(x, LANES - QK_ROPE_DIM // 2, 1) * sa + pltpu.roll(x, QK_ROPE_DIM // 2, 1) * sb


def _in_proj_kernel(x_ref, pos_ref, mod_ref, g1_ref, w_in_ref, qa_ref, wq_ref, kva_ref, wkv_ref,
                    qn_ref, kn_nope_ref, kn_rope_ref, inv_ref,
                    q_ref, k_ref, v_ref, xl_ref, gl_ref, *, q_lora, kv_lora, lru_w, chunk):
    t, d = x_ref.shape[1:]
    shift = mod_ref[0, 0:1, :]
    gain = g1_ref[...] * (1.0 + mod_ref[0, 1:2, :])
    c_all, sa_all, sb_all = _rope_tables(pos_ref[0].astype(F32), inv_ref[...])
    sm_scale = QK_HEAD_DIM ** -0.5 * math.log2(math.e)

    for r0 in range(0, t, chunk):
        rows = slice(r0, r0 + chunk)
        x = x_ref[0, rows, :]
        h = x * _rms(x, d) * gain + shift
        proj = jnp.dot(h.astype(BF16), w_in_ref[...], preferred_element_type=F32)
        o = 0
        q_lat = proj[:, o:o + q_lora]; o += q_lora
        kv_lat = proj[:, o:o + kv_lora]; o += kv_lora
        x_lru = proj[:, o:o + lru_w]; o += lru_w
        g_lru = proj[:, o:o + lru_w]; o += lru_w
        k_pe = proj[:, o:o + LANES]

        xl_ref[0, rows, :] = x_lru
        gl_ref[0, rows, :] = jax.nn.gelu(g_lru).astype(BF16)

        c, sa, sb = c_all[rows], sa_all[rows], sb_all[rows]
        qn = q_lat * _rms(q_lat, q_lora) * qa_ref[...]
        q_full = jnp.dot(qn.astype(BF16), wq_ref[...], preferred_element_type=F32)
        kvn = kv_lat * _rms(kv_lat, kv_lora) * kva_ref[...]
        kv_full = jnp.dot(kvn.astype(BF16), wkv_ref[...], preferred_element_type=F32)

        pe_ss = jnp.sum(k_pe * k_pe, axis=-1, keepdims=True)
        pe_rot = _rope(k_pe * kn_rope_ref[...], c, sa, sb)
        for hd in range(ATTN_HEADS):
            qh = q_full[:, hd * QK_PAD_DIM:(hd + 1) * QK_PAD_DIM]
            qh = qh * (_rms(qh, QK_HEAD_DIM) * sm_scale) * qn_ref[...]
            q_ref[0, hd, rows, 0:QK_NOPE_DIM] = qh[:, 0:QK_NOPE_DIM].astype(BF16)
            q_ref[0, hd, rows, QK_NOPE_DIM:QK_PAD_DIM] = _rope(qh[:, QK_NOPE_DIM:QK_PAD_DIM],
                                                               c, sa, sb).astype(BF16)
            base = hd * (QK_NOPE_DIM + V_HEAD_DIM)
            kn = kv_full[:, base:base + QK_NOPE_DIM]
            vv = kv_full[:, base + QK_NOPE_DIM:base + QK_NOPE_DIM + V_HEAD_DIM]
            r = lax.rsqrt((jnp.sum(kn * kn, axis=-1, keepdims=True) + pe_ss) * (1.0 / QK_HEAD_DIM) + EPS)
            k_ref[0, hd, rows, 0:QK_NOPE_DIM] = (kn * r * kn_nope_ref[...]).astype(BF16)
            k_ref[0, hd, rows, QK_NOPE_DIM:QK_PAD_DIM] = (pe_rot * r).astype(BF16)
            v_ref[0, hd, rows, :] = vv.astype(BF16)


def _in_proj(x, pos_row, mod, norm1_g, w_in_p, q_a_norm, w_q_p, kv_a_norm, w_kv, q_norm_p,
             k_norm_nope, k_norm_rope_p, inv_col, *, q_lora, kv_lora, lru_w):
    b, s, d = x.shape
    t = min(PROJ_TILE, s)
    full = lambda a: pl.BlockSpec(a.shape, lambda i, j: (0,) * a.ndim)
    kern = functools.partial(_in_proj_kernel, q_lora=q_lora, kv_lora=kv_lora, lru_w=lru_w,
                             chunk=min(PROJ_CHUNK, t))
    head_out = lambda w: pl.BlockSpec((1, ATTN_HEADS, t, w), lambda i, j: (i, 0, j, 0))
    return pl.pallas_call(
        kern,
        grid=(b, s // t),
        in_specs=[pl.BlockSpec((1, t, d), lambda i, j: (i, j, 0)),
                  pl.BlockSpec((1, 1, t), lambda i, j: (i, 0, j)),
                  pl.BlockSpec((1,) + mod.shape[1:], lambda i, j: (i, 0, 0)),
                  full(norm1_g), full(w_in_p), full(q_a_norm), full(w_q_p), full(kv_a_norm),
                  full(w_kv), full(q_norm_p), full(k_norm_nope), full(k_norm_rope_p), full(inv_col)],
        out_specs=[head_out(QK_PAD_DIM), head_out(QK_PAD_DIM), head_out(V_HEAD_DIM),
                   pl.BlockSpec((1, t, lru_w), lambda i, j: (i, j, 0)),
                   pl.BlockSpec((1, t, lru_w), lambda i, j: (i, j, 0))],
        out_shape=[jax.ShapeDtypeStruct((b, ATTN_HEADS, s, QK_PAD_DIM), BF16),
                   jax.ShapeDtypeStruct((b, ATTN_HEADS, s, QK_PAD_DIM), BF16),
                   jax.ShapeDtypeStruct((b, ATTN_HEADS, s, V_HEAD_DIM), BF16),
                   jax.ShapeDtypeStruct((b, s, lru_w), F32),
                   jax.ShapeDtypeStruct((b, s, lru_w), BF16)],
        compiler_params=_cparams("parallel", "parallel"),
        name="in_proj",
    )(x, pos_row, mod, norm1_g, w_in_p, q_a_norm, w_q_p, kv_a_norm, w_kv, q_norm_p,
      k_norm_nope, k_norm_rope_p, inv_col)


def _attn_kernel(q_ref, k_ref, v_ref, o_ref, m_sc, acc_sc, s_sc, v_sc, *, tile, chunk):
    qi = pl.program_id(2)

    @pl.when(qi == 0)
    def _():
        v_sc[:, 0:V_HEAD_DIM] = v_ref[0, 0]
        v_sc[:, V_HEAD_DIM:V_PAD_DIM] = jnp.ones((v_sc.shape[0], V_PAD_DIM - V_HEAD_DIM), BF16)

    m_sc[...] = jnp.full(m_sc.shape, -jnp.inf, F32)
    acc_sc[...] = jnp.zeros(acc_sc.shape, F32)

    def scores(j, slot):
        start = pl.multiple_of(j * tile, tile)
        k = k_ref[0, 0, pl.ds(start, tile), :]
        s_sc[slot] = lax.dot_general(q_ref[0, 0], k, (((1,), (1,)), ((), ())),
                                     preferred_element_type=F32)

    def accumulate(j, slot, masked):
        start = pl.multiple_of(j * tile, tile)
        for r in range(tile // chunk):
            rows = slice(r * chunk, (r + 1) * chunk)
            cols = -(-(r + 1) * chunk // LANES) * LANES if masked else tile
            s = s_sc[slot, rows, 0:cols]
            if masked:
                row = lax.broadcasted_iota(jnp.int32, s.shape, 0) + r * chunk
                col = lax.broadcasted_iota(jnp.int32, s.shape, 1)
                s = jnp.where(col <= row, s, -jnp.inf)
            v = v_sc[pl.ds(start, cols), :]
            m_old = m_sc[rows, :]
            m_new = jnp.maximum(m_old, jnp.max(s, axis=-1, keepdims=True))
            alpha = jnp.exp2(m_old - m_new)
            p = jnp.exp2(s - jnp.tile(m_new, (1, cols // LANES)))
            acc_sc[rows, :] = (jnp.tile(alpha, (1, V_PAD_DIM // LANES)) * acc_sc[rows, :]
                               + jnp.dot(p.astype(BF16), v, preferred_element_type=F32))
            m_sc[rows, :] = m_new

    scores(0, 0)

    def pair(p, carry):
        scores(2 * p + 1, 1)
        accumulate(2 * p, 0, False)
        scores(2 * p + 2, 0)
        accumulate(2 * p + 1, 1, False)
        return carry

    lax.fori_loop(0, qi // 2, pair, 0)

    @pl.when(qi % 2 == 0)
    def _():
        accumulate(qi, 0, True)

    @pl.when(qi % 2 == 1)
    def _():
        scores(qi, 1)
        accumulate(qi - 1, 0, False)
        accumulate(qi, 1, True)

    o_ref[0] = (acc_sc[:, 0:V_HEAD_DIM] / acc_sc[:, V_HEAD_DIM:2 * V_HEAD_DIM]).astype(o_ref.dtype)


def _attention(q, k, v):
    b, h, s, _ = q.shape
    t = min(ATTN_TILE, s)
    assert V_HEAD_DIM == LANES and V_PAD_DIM == 2 * V_HEAD_DIM
    kern = functools.partial(_attn_kernel, tile=t, chunk=min(ATTN_CHUNK, t))
    return pl.pallas_call(
        kern,
        grid=(b, h, s // t),
        in_specs=[pl.BlockSpec((1, 1, t, QK_PAD_DIM), lambda i, j, n: (i, j, n, 0)),
                  pl.BlockSpec((1, 1, s, QK_PAD_DIM), lambda i, j, n: (i, j, 0, 0)),
                  pl.BlockSpec((1, 1, s, V_HEAD_DIM), lambda i, j, n: (i, j, 0, 0))],
        out_specs=pl.BlockSpec((1, t, V_HEAD_DIM), lambda i, j, n: (i, n, j)),
        out_shape=jax.ShapeDtypeStruct((b, s, h * V_HEAD_DIM), BF16),
        scratch_shapes=[pltpu.VMEM((t, LANES), F32), pltpu.VMEM((t, V_PAD_DIM), F32),
                        pltpu.VMEM((2, t, t), F32), pltpu.VMEM((s, V_PAD_DIM), BF16)],
        compiler_params=_cparams("parallel", "parallel", "arbitrary"),
        name="attention",
    )(q, k, v)


def _lru_kernel(xl_ref, gl_ref, pos_ref, cw_ref, cb_ref, wax_ref, ba_ref, bx_ref, lam_ref, gn_ref,
                o_ref, ext_sc, a_sc, b_sc, h_sc, carry_sc):
    t, c = a_sc.shape
    head_w = c // LRU_HEADS

    @pl.when(pl.program_id(1) == 0)
    def _():
        ext_sc[0:SUBLANES, :] = jnp.zeros((SUBLANES, c), F32)
        carry_sc[...] = jnp.zeros(carry_sc.shape, F32)

    x = xl_ref[0]
    ext_sc[SUBLANES:SUBLANES + t, :] = x
    xc = cb_ref[...] + x * cw_ref[CONV_WIDTH - 1:CONV_WIDTH, :]
    for back in range(1, CONV_WIDTH):
        tap = CONV_WIDTH - 1 - back
        xc = xc + ext_sc[SUBLANES - back:SUBLANES - back + t, :] * cw_ref[tap:tap + 1, :]
    ext_sc[0:SUBLANES, :] = x[t - SUBLANES:t, :]

    reset = _column((pos_ref[0] == 0).astype(F32)) > 0.5
    lam = lam_ref[...]
    neg_sp = -LRU_C * (jnp.maximum(-lam, 0.0) + jnp.log1p(jnp.exp(-jnp.abs(lam))))
    for hd in range(LRU_HEADS):
        sl = slice(hd * head_w, (hd + 1) * head_w)
        xh = xc[:, sl]
        gates = jnp.dot(xh.astype(BF16), wax_ref[hd], preferred_element_type=F32)
        r = _sigmoid(gates[:, :head_w] + ba_ref[:, sl])
        i = _sigmoid(gates[:, head_w:] + bx_ref[:, sl])
        a = jnp.exp(r * neg_sp[:, sl])
        gap = jnp.maximum(1.0 - a * a, 0.0)
        mult = jnp.where(gap > 0.0, gap * lax.rsqrt(gap), 0.0)
        a = jnp.where(reset, 0.0, a)
        mult = jnp.where(reset, 1.0, mult)
        a_sc[:, sl] = a
        b_sc[:, sl] = mult * (i * xh)

    srow = lax.broadcasted_iota(jnp.int32, (SUBLANES, c), 0)

    def group(g, h_prev):
        start = pl.multiple_of(g * SUBLANES, SUBLANES)
        a = a_sc[pl.ds(start, SUBLANES), :]
        bb = b_sc[pl.ds(start, SUBLANES), :]
        sh = 1
        while sh < SUBLANES:
            keep = srow >= sh
            a_prev = jnp.where(keep, pltpu.roll(a, sh, 0), 1.0)
            b_prev = jnp.where(keep, pltpu.roll(bb, sh, 0), 0.0)
            bb = bb + a * b_prev
            a = a * a_prev
            sh *= 2
        hh = bb + a * h_prev
        h_sc[pl.ds(start, SUBLANES), :] = hh
        return jnp.broadcast_to(hh[SUBLANES - 1:SUBLANES, :], (SUBLANES, c))

    carry_sc[...] = lax.fori_loop(0, t // SUBLANES, group, carry_sc[...], unroll=4)

    y = h_sc[...] * gl_ref[0].astype(F32)
    o_ref[0] = (y * _rms(y, c) * gn_ref[...]).astype(o_ref.dtype)


def _rg_lru(x_lru, gelu_g, pos_row, conv_w, conv_b, w_ax, b_a, b_x, lam, out_norm):
    b, s, c = x_lru.shape
    t = min(LRU_TILE, s)
    full = lambda a: pl.BlockSpec(a.shape, lambda i, j: (0,) * a.ndim)
    return pl.pallas_call(
        _lru_kernel,
        grid=(b, s // t),
        in_specs=[pl.BlockSpec((1, t, c), lambda i, j: (i, j, 0)),
                  pl.BlockSpec((1, t, c), lambda i, j: (i, j, 0)),
                  pl.BlockSpec((1, 1, t), lambda i, j: (i, 0, j)),
                  full(conv_w), full(conv_b), full(w_ax), full(b_a), full(b_x), full(lam),
                  full(out_norm)],
        out_specs=pl.BlockSpec((1, t, c), lambda i, j: (i, j, 0)),
        out_shape=jax.ShapeDtypeStruct((b, s, c), BF16),
        scratch_shapes=[pltpu.VMEM((t + SUBLANES, c), F32), pltpu.VMEM((t, c), F32),
                        pltpu.VMEM((t, c), F32), pltpu.VMEM((t, c), F32),
                        pltpu.VMEM((SUBLANES, c), F32)],
        compiler_params=_cparams("parallel", "arbitrary"),
        name="rg_lru",
    )(x_lru, gelu_g, pos_row, conv_w, conv_b, w_ax, b_a, b_x, lam, out_norm)


def _route_kernel(x_ref, at_ref, lr_ref, mod_ref, an_ref, wa_ref, wl_ref, g2_ref, wr_ref, br_ref,
                  x1_ref, h2_ref, ids_ref, rank_ref, wts_ref, cnt_ref, carry_sc,
                  *, n_experts, group_row0, expert_row0):
    @pl.when(pl.program_id(0) == 0)
    def _():
        carry_sc[...] = jnp.zeros(carry_sc.shape, F32)

    x = x_ref[0]
    t, d = x.shape
    gate1 = mod_ref[0, 2:3, :]
    shift2 = mod_ref[0, 3:4, :]
    gain2 = g2_ref[...] * (1.0 + mod_ref[0, 4:5, :])
    at = at_ref[0].astype(F32)
    at_n = at * _rms(at, at.shape[-1]) * an_ref[...]
    mix = jnp.dot(at_n.astype(BF16), wa_ref[...], preferred_element_type=F32)
    mix = mix + jnp.dot(lr_ref[0], wl_ref[...], preferred_element_type=F32)
    x1 = x + gate1 * mix
    x1_ref[0] = x1
    h2 = x1 * _rms(x1, d) * gain2 + shift2
    h2_ref[0] = h2

    logits = lax.dot_general(wr_ref[...], h2.astype(BF16), (((1,), (1,)), ((), ())),
                             preferred_element_type=F32) + br_ref[...]
    per_group = n_experts // N_GROUPS
    lg = logits[group_row0:group_row0 + N_GROUPS, :]
    grow = lax.broadcasted_iota(jnp.int32, lg.shape, 0)
    gmax = jnp.max(lg, axis=0, keepdims=True)
    g_idx = jnp.min(jnp.where(lg == gmax, grow, N_GROUPS), axis=0, keepdims=True)
    g_p = 1.0 / jnp.sum(jnp.exp(lg - gmax), axis=0, keepdims=True)
    sel = jnp.zeros((per_group, t), F32)
    for g in range(N_GROUPS):
        r0 = expert_row0 + g * per_group
        sel = sel + jnp.where(g_idx == g, logits[r0:r0 + per_group, :], 0.0)
    erow = lax.broadcasted_iota(jnp.int32, sel.shape, 0)
    m1 = jnp.max(sel, axis=0, keepdims=True)
    i1 = jnp.min(jnp.where(sel == m1, erow, per_group), axis=0, keepdims=True)
    sel2 = jnp.where(erow == i1, -jnp.inf, sel)
    m2 = jnp.max(sel2, axis=0, keepdims=True)
    i2 = jnp.min(jnp.where(sel2 == m2, erow, per_group), axis=0, keepdims=True)
    e21 = jnp.exp(m2 - m1)
    w1 = g_p / (1.0 + e21)
    w2 = g_p * e21 / (1.0 + e21)
    e1 = g_idx * per_group + i1
    e2 = g_idx * per_group + i2
    ids_ref[0:1, :] = e1
    ids_ref[1:2, :] = e2
    wts_ref[0:1, :] = w1
    wts_ref[1:2, :] = w2

    xrow = lax.broadcasted_iota(jnp.int32, (n_experts, t), 0)
    o1 = (xrow == e1).astype(F32)
    o2 = (xrow == e2).astype(F32)
    both = o1 + o2
    tr = lax.broadcasted_iota(jnp.int32, (t, t), 0)
    tc = lax.broadcasted_iota(jnp.int32, (t, t), 1)
    upper = jnp.where(tr < tc, 1.0, 0.0).astype(BF16)
    before = jnp.dot(both.astype(BF16), upper, preferred_element_type=F32) + carry_sc[...]
    rank_ref[0:1, :] = jnp.sum(o1 * before, axis=0, keepdims=True).astype(jnp.int32)
    rank_ref[1:2, :] = jnp.sum(o2 * before, axis=0, keepdims=True).astype(jnp.int32)
    carry_sc[...] = carry_sc[...] + jnp.sum(both, axis=1, keepdims=True)
    cnt_ref[...] = jnp.broadcast_to(carry_sc[...], cnt_ref.shape)


def _out_route(x, attn, lru_n, mod, attn_norm, w_out_a, w_out_l, norm2_g, w_r_t, b_r_col,
               *, n_experts, group_row0, expert_row0):
    b, s, d = x.shape
    t = min(ROUTE_TILE, s)
    nt = s // t
    n = b * s
    full = lambda a: pl.BlockSpec(a.shape, lambda i: (0,) * a.ndim)
    tok = lambda w: pl.BlockSpec((1, t, w), lambda i: (i // nt, i % nt, 0))
    kern = functools.partial(_route_kernel, n_experts=n_experts, group_row0=group_row0,
                             expert_row0=expert_row0)
    return pl.pallas_call(
        kern,
        grid=(b * nt,),
        in_specs=[tok(d), tok(attn.shape[-1]), tok(lru_n.shape[-1]),
                  pl.BlockSpec((1,) + mod.shape[1:], lambda i: (i // nt, 0, 0)),
                  full(attn_norm), full(w_out_a), full(w_out_l), full(norm2_g), full(w_r_t),
                  full(b_r_col)],
        out_specs=[tok(d), tok(d),
                   pl.BlockSpec((TOP_K, t), lambda i: (0, i)),
                   pl.BlockSpec((TOP_K, t), lambda i: (0, i)),
                   pl.BlockSpec((TOP_K, t), lambda i: (0, i)),
                   pl.BlockSpec((n_experts, LANES), lambda i: (0, 0))],
        out_shape=[jax.ShapeDtypeStruct((b, s, d), F32),
                   jax.ShapeDtypeStruct((b, s, d), F32),
                   jax.ShapeDtypeStruct((TOP_K, n), jnp.int32),
                   jax.ShapeDtypeStruct((TOP_K, n), jnp.int32),
                   jax.ShapeDtypeStruct((TOP_K, n), F32),
                   jax.ShapeDtypeStruct((n_experts, LANES), F32)],
        scratch_shapes=[pltpu.VMEM((n_experts, 1), F32)],
        compiler_params=_cparams("arbitrary"),
        name="out_route",
    )(x, attn, lru_n, mod, attn_norm, w_out_a, w_out_l, norm2_g, w_r_t, b_r_col)


def _row_copy(src, src_row, dst, dst_row, sem):
    return pltpu.make_async_copy(src.at[pl.ds(src_row, 1), :], dst.at[pl.ds(dst_row, 1), :], sem)


def _invert_kernel(dest_ref, cnt_ref, start_ref, end_ref, tok_ref, *, n_tokens, block):
    def clear(i, carry):
        tok_ref[i] = 0
        return carry

    def per_expert(e, carry):
        first = start_ref[e] + cnt_ref[e]
        n_pad = (block - cnt_ref[e] % block) % block
        lax.fori_loop(first, first + n_pad, clear, 0)
        return carry

    lax.fori_loop(0, cnt_ref.shape[0], per_expert, 0)
    lax.fori_loop(end_ref[0], tok_ref.shape[0], clear, 0)
    for k in range(TOP_K):
        def place(t, carry, k=k):
            tok_ref[dest_ref[k * n_tokens + t]] = t
            return carry

        lax.fori_loop(0, n_tokens, place, 0, unroll=8)


def _invert(dest_flat, counts, starts, rows_used, cap, n_tokens, *, block):
    smem = pl.BlockSpec(memory_space=pltpu.SMEM)
    return pl.pallas_call(
        functools.partial(_invert_kernel, n_tokens=n_tokens, block=block),
        in_specs=[smem, smem, smem, smem],
        out_specs=smem,
        out_shape=jax.ShapeDtypeStruct((cap,), jnp.int32),
        name="invert",
    )(dest_flat, counts, starts, rows_used)


def _expert_kernel(be_ref, nb_ref, tok_ref, tok_nx_ref, h2_ref, wg_ref, wu_ref, wd_ref, ys_ref,
                   xbuf, wgu_sc, wdn_sc, sem, *, d_ff, block):
    j = pl.program_id(0)
    n_live = nb_ref[0]
    live = jnp.minimum(j, n_live - 1)
    slot = j % 2

    def gather(t_ref, sl):
        for r in range(block):
            _row_copy(h2_ref, t_ref[r], xbuf.at[sl], r, sem.at[sl]).start(priority=1)

    @pl.when(j == 0)
    def _():
        gather(tok_ref, slot)

    @pl.when(j + 1 < n_live)
    def _():
        gather(tok_nx_ref, 1 - slot)

    @pl.when((j == 0) | (be_ref[live] != be_ref[jnp.maximum(live - 1, 0)]))
    def _():
        wgu_sc[:, 0:d_ff] = wg_ref[0].astype(BF16)
        wgu_sc[:, d_ff:2 * d_ff] = wu_ref[0].astype(BF16)
        wdn_sc[...] = wd_ref[0].astype(BF16)

    @pl.when(j < n_live)
    def _():
        pltpu.make_async_copy(h2_ref.at[pl.ds(0, block), :], xbuf.at[slot], sem.at[slot]).wait()
        xb = xbuf[slot].astype(BF16)
        gu = jnp.dot(xb, wgu_sc[...], preferred_element_type=F32)
        gt = gu[:, :d_ff]
        hid = gt * jax.nn.sigmoid(gt) * gu[:, d_ff:]
        ys_ref[...] = jnp.dot(hid.astype(BF16), wdn_sc[...], preferred_element_type=F32)

    @pl.when(j >= n_live)
    def _():
        ys_ref[...] = jnp.zeros(ys_ref.shape, ys_ref.dtype)


def _experts(block_e, n_blocks_used, slot_tok, h2, w_gate, w_up, w_down, *, block):
    cap = slot_tok.shape[0]
    d = w_down.shape[2]
    d_ff = w_down.shape[1]
    n_blocks = cap // block
    kern = functools.partial(_expert_kernel, d_ff=d_ff, block=block)
    live = lambda j, nb: jnp.minimum(j, nb[0] - 1)
    nxt = lambda j, nb: jnp.minimum(j + 1, nb[0] - 1)
    return pl.pallas_call(
        kern,
        grid_spec=pltpu.PrefetchScalarGridSpec(
            num_scalar_prefetch=2,
            grid=(n_blocks,),
            in_specs=[pl.BlockSpec((block,), lambda j, be, nb: (live(j, nb),), memory_space=pltpu.SMEM),
                      pl.BlockSpec((block,), lambda j, be, nb: (nxt(j, nb),), memory_space=pltpu.SMEM),
                      pl.BlockSpec(memory_space=pl.ANY),
                      pl.BlockSpec((1, d, d_ff), lambda j, be, nb: (be[live(j, nb)], 0, 0)),
                      pl.BlockSpec((1, d, d_ff), lambda j, be, nb: (be[live(j, nb)], 0, 0)),
                      pl.BlockSpec((1, d_ff, d), lambda j, be, nb: (be[live(j, nb)], 0, 0))],
            out_specs=pl.BlockSpec((block, d), lambda j, be, nb: (j, 0)),
            scratch_shapes=[pltpu.VMEM((2, block, d), F32), pltpu.VMEM((d, 2 * d_ff), BF16),
                            pltpu.VMEM((d_ff, d), BF16), pltpu.SemaphoreType.DMA((2,))]),
        out_shape=jax.ShapeDtypeStruct((cap, d), F32),
        compiler_params=_cparams("arbitrary"),
        name="experts",
    )(block_e, n_blocks_used, slot_tok, slot_tok, h2, w_gate, w_up, w_down)


def _combine_kernel(dest_ref, dest_nx_ref, wts_ref, x1_ref, mod_ref, ys_ref, o_ref, buf, sem, *,
                    tile):
    i = pl.program_id(0)
    slot = i % 2

    def gather(d_ref, sl):
        for tk in range(tile):
            for k in range(TOP_K):
                _row_copy(ys_ref, d_ref[k, tk], buf.at[sl, k], tk, sem.at[sl]).start(priority=k % 2)

    @pl.when(i == 0)
    def _():
        gather(dest_ref, slot)

    @pl.when(i + 1 < pl.num_programs(0))
    def _():
        gather(dest_nx_ref, 1 - slot)

    for k in range(TOP_K):
        pltpu.make_async_copy(ys_ref.at[pl.ds(0, tile), :], buf.at[slot, k], sem.at[slot]).wait()

    d = o_ref.shape[-1]
    w0 = jnp.tile(_column(wts_ref[0:1, :]), (1, d // LANES))
    w1 = jnp.tile(_column(wts_ref[1:2, :]), (1, d // LANES))
    gate2 = mod_ref[0, 5:6, :]
    o_ref[0] = x1_ref[0] + gate2 * (w0 * buf[slot, 0] + w1 * buf[slot, 1])


def _combine(dest, wts, x1, mod, ys):
    b, s, d = x1.shape
    w = ys.shape[1]
    t = min(COMBINE_TILE, s)
    nt = s // t
    last = b * nt - 1
    kern = functools.partial(_combine_kernel, tile=t)
    smem_now = pl.BlockSpec((TOP_K, t), lambda i: (0, i), memory_space=pltpu.SMEM)
    smem_next = pl.BlockSpec((TOP_K, t), lambda i: (0, jnp.minimum(i + 1, last)),
                             memory_space=pltpu.SMEM)
    return pl.pallas_call(
        kern,
        grid=(b * nt,),
        in_specs=[smem_now, smem_next,
                  pl.BlockSpec((TOP_K, t), lambda i: (0, i)),
                  pl.BlockSpec((1, t, d), lambda i: (i // nt, i % nt, 0)),
                  pl.BlockSpec((1,) + mod.shape[1:], lambda i: (i // nt, 0, 0)),
                  pl.BlockSpec(memory_space=pl.ANY)],
        out_specs=pl.BlockSpec((1, t, d), lambda i: (i // nt, i % nt, 0)),
        out_shape=jax.ShapeDtypeStruct((b, s, d), F32),
        scratch_shapes=[pltpu.VMEM((2, TOP_K, t, w), F32), pltpu.SemaphoreType.DMA((2,))],
        compiler_params=_cparams("arbitrary"),
        name="combine",
    )(dest, dest, wts, x1, mod, ys)


def _layer(x, c, pos_row, w_ada, b_ada, norm1_g, w_in, q_a_norm, w_q_b, kv_a_norm, w_kv_b, q_norm,
           k_norm, conv_w, conv_b, lru_wa, lru_ba, lru_wx, lru_bx, lru_lambda, attn_out_norm,
           lru_out_norm, w_out, norm2_g, w_rg, b_rg, w_re, b_re, w_gate, w_up, w_down):
    b, s, d = x.shape
    n = b * s
    q_lora = q_a_norm.shape[0]
    kv_lora = kv_a_norm.shape[0]
    lru_w = lru_lambda.shape[0]
    attn_w = ATTN_HEADS * V_HEAD_DIM
    n_experts = w_gate.shape[0]
    row = lambda v: v.reshape(1, -1)

    mod = _adaln(c, w_ada, b_ada).reshape(b, 6, d)

    o1 = q_lora
    o2 = o1 + kv_lora
    o3 = o2 + QK_ROPE_DIM
    o4 = o3 + lru_w
    w_in_p = jnp.concatenate(
        [w_in[:, :o2], w_in[:, o3:o4], w_in[:, o4:], w_in[:, o2:o3],
         jnp.zeros((d, LANES - QK_ROPE_DIM), w_in.dtype)], axis=1).astype(BF16)
    pad_h = QK_PAD_DIM - QK_HEAD_DIM
    w_q_p = jnp.pad(w_q_b.reshape(q_lora, ATTN_HEADS, QK_HEAD_DIM),
                    ((0, 0), (0, 0), (0, pad_h))).reshape(q_lora, ATTN_HEADS * QK_PAD_DIM).astype(BF16)
    q_norm_p = row(jnp.pad(q_norm, (0, pad_h)))
    k_norm_nope = row(k_norm[:QK_NOPE_DIM])
    k_norm_rope_p = row(jnp.pad(k_norm[QK_NOPE_DIM:], (0, LANES - QK_ROPE_DIM)))
    inv = 1.0 / (ROPE_THETA ** (np.arange(0, QK_ROPE_DIM, 2, dtype=np.float32) / QK_ROPE_DIM))
    inv_col = jnp.asarray(inv.reshape(-1, 1))

    q, k, v, x_lru, gelu_g = _in_proj(
        x, pos_row, mod, row(norm1_g), w_in_p, row(q_a_norm), w_q_p, row(kv_a_norm),
        w_kv_b.astype(BF16), q_norm_p, k_norm_nope, k_norm_rope_p, inv_col,
        q_lora=q_lora, kv_lora=kv_lora, lru_w=lru_w)

    attn = _attention(q, k, v)

    w_ax = jnp.concatenate([lru_wa, lru_wx], axis=-1).astype(BF16)
    lru_n = _rg_lru(x_lru, gelu_g, pos_row, conv_w, row(conv_b), w_ax, row(lru_ba), row(lru_bx),
                    row(lru_lambda), row(lru_out_norm))

    group_row0, expert_row0 = 0, SUBLANES
    w_r_t = jnp.concatenate([w_rg.T, jnp.zeros((expert_row0 - N_GROUPS, d), w_rg.dtype), w_re.T],
                            axis=0).astype(BF16)
    b_r_col = jnp.concatenate([b_rg, jnp.zeros((expert_row0 - N_GROUPS,), b_rg.dtype),
                               b_re]).reshape(-1, 1)
    x1, h2, ids, ranks, wts, cnt = _out_route(
        x, attn, lru_n, mod, row(attn_out_norm), w_out[:attn_w].astype(BF16),
        w_out[attn_w:].astype(BF16), row(norm2_g), w_r_t, b_r_col,
        n_experts=n_experts, group_row0=group_row0, expert_row0=expert_row0)

    blk = EXPERT_BLOCK
    counts = cnt[:, 0].astype(jnp.int32)
    padded = (counts + blk - 1) // blk * blk
    pad_ends = jnp.cumsum(padded)
    starts = (pad_ends - padded).astype(jnp.int32)
    dest = ranks
    for e in range(n_experts):
        dest = dest + jnp.where(ids == e, starts[e], 0)
    cap = (n * TOP_K + n_experts * (blk - 1) + blk - 1) // blk * blk
    n_blocks = cap // blk
    block_first_row = jnp.arange(n_blocks, dtype=jnp.int32) * blk
    block_e = jnp.minimum(jnp.sum(pad_ends[None, :] <= block_first_row[:, None], axis=1),
                          n_experts - 1).astype(jnp.int32)
    n_used = (pad_ends[-1:] // blk).astype(jnp.int32)

    slot_tok = _invert(dest.reshape(-1), counts, starts, pad_ends[-1:].astype(jnp.int32), cap, n, block=blk)
    ys = _experts(block_e, n_used, slot_tok, h2.reshape(n, d), w_gate, w_up, w_down, block=blk)
    return _combine(dest, wts, x1, mod, ys)


def kernel(x, c, positions, w_ada, b_ada, norm1_g, w_in, q_a_norm, w_q_b, kv_a_norm, w_kv_b, q_norm, k_norm, conv_w, conv_b, lru_wa, lru_ba, lru_wx, lru_bx, lru_lambda, attn_out_norm, lru_out_norm, w_out, norm2_g, w_router_group, b_router_group, w_router_expert, b_router_expert, w_gate, w_up, w_down):
    b, s, _ = x.shape
    pos_row = positions.reshape(b, 1, s)
    for l in range(w_ada.shape[0]):
        x = _layer(x, c, pos_row, w_ada[l], b_ada[l], norm1_g[l], w_in[l], q_a_norm[l], w_q_b[l],
                   kv_a_norm[l], w_kv_b[l], q_norm[l], k_norm[l], conv_w[l], conv_b[l], lru_wa[l],
                   lru_ba[l], lru_wx[l], lru_bx[l], lru_lambda[l], attn_out_norm[l], lru_out_norm[l],
                   w_out[l], norm2_g[l], w_router_group[l], b_router_group[l], w_router_expert[l],
                   b_router_expert[l], w_gate[l], w_up[l], w_down[l])
    return x
```

```python
import functools
import math

import numpy as np
import jax
import jax.numpy as jnp
from jax import lax
from jax.experimental import pallas as pl
from jax.experimental.pallas import tpu as pltpu

F32 = jnp.float32
BF16 = jnp.bfloat16

ATTN_HEADS = 4
QK_NOPE_DIM = 128
QK_ROPE_DIM = 64
QK_HEAD_DIM = QK_NOPE_DIM + QK_ROPE_DIM
V_HEAD_DIM = 128
ROPE_THETA = 10000.0
LRU_HEADS = 4
CONV_WIDTH = 4
LRU_C = 8.0
N_GROUPS = 4
TOP_K = 2
EPS = 1e-6

LANES = 128
SUBLANES = 8
QK_PAD_DIM = 256
V_PAD_DIM = 256
VMEM_LIMIT = 56 * 1024 * 1024

PROJ_TILE = 512
PROJ_CHUNK = 128
ATTN_TILE = 1024
ATTN_CHUNK = 128
LRU_TILE = 512
ROUTE_TILE = 512
DISPATCH_TILE = 512
COMBINE_TILE = 512
EXPERT_BLOCK = 256


def _cparams(*sem):
    return pltpu.CompilerParams(dimension_semantics=sem, vmem_limit_bytes=VMEM_LIMIT)


def _rms(x, n):
    return lax.rsqrt(jnp.sum(x * x, axis=-1, keepdims=True) * (1.0 / n) + EPS)


def _sigmoid(x):
    return 0.5 * jnp.tanh(0.5 * x) + 0.5


def _column(row, width=LANES):
    t = row.shape[-1]
    return jnp.broadcast_to(row, (width, t)).T


def _adaln_kernel(c_ref, w_ref, b_ref, o_ref):
    c = c_ref[...]
    sc = c * jax.nn.sigmoid(c)
    o_ref[...] = jnp.dot(sc.astype(BF16), w_ref[...].astype(BF16),
                         preferred_element_type=F32) + b_ref[...]


def _adaln(c, w_ada, b_ada):
    b, d = c.shape
    n_out = w_ada.shape[1]
    return pl.pallas_call(
        _adaln_kernel,
        grid=(n_out // d,),
        in_specs=[pl.BlockSpec((b, d), lambda j: (0, 0)),
                  pl.BlockSpec((d, d), lambda j: (0, j)),
                  pl.BlockSpec((1, d), lambda j: (0, j))],
        out_specs=pl.BlockSpec((b, d), lambda j: (0, j)),
        out_shape=jax.ShapeDtypeStruct((b, n_out), F32),
        compiler_params=_cparams("parallel"),
        name="adaln",
    )(c, w_ada, b_ada.reshape(1, n_out))


def _rope_tables(pos_row, inv_col):
    half = QK_ROPE_DIM // 2
    ang = inv_col * pos_row
    cos = jnp.cos(ang)
    sin = jnp.sin(ang)
    t = ang.shape[-1]
    zeros = lambda n: jnp.zeros((n, t), F32)
    c = jnp.concatenate([cos, cos, zeros(LANES - QK_ROPE_DIM)], axis=0)
    sa = jnp.concatenate([-sin, zeros(LANES - half)], axis=0)
    sb = jnp.concatenate([zeros(half), sin, zeros(LANES - QK_ROPE_DIM)], axis=0)
    return c.T, sa.T, sb.T


def _rope(x, c, sa, sb):
    return x * c + pltpu.roll(x, LANES - QK_ROPE_DIM // 2, 1) * sa + pltpu.roll(x, QK_ROPE_DIM // 2, 1) * sb


def _in_proj_kernel(x_ref, pos_ref, mod_ref, g1_ref, w_in_ref, qa_ref, wq_ref, kva_ref, wkv_ref,
                    qn_ref, kn_nope_ref, kn_rope_ref, inv_ref,
                    q_ref, k_ref, v_ref, xl_ref, gl_ref, *, q_lora, kv_lora, lru_w, chunk):
    t, d = x_ref.shape[1:]
    shift = mod_ref[0, 0:1, :]
    gain = g1_ref[...] * (1.0 + mod_ref[0, 1:2, :])
    c_all, sa_all, sb_all = _rope_tables(pos_ref[0].astype(F32), inv_ref[...])
    sm_scale = QK_HEAD_DIM ** -0.5 * math.log2(math.e)

    for r0 in range(0, t, chunk):
        rows = slice(r0, r0 + chunk)
        x = x_ref[0, rows, :]
        h = x * _rms(x, d) * gain + shift
        proj = jnp.dot(h.astype(BF16), w_in_ref[...], preferred_element_type=F32)
        o = 0
        q_lat = proj[:, o:o + q_lora]; o += q_lora
        kv_lat = proj[:, o:o + kv_lora]; o += kv_lora
        x_lru = proj[:, o:o + lru_w]; o += lru_w
        g_lru = proj[:, o:o + lru_w]; o += lru_w
        k_pe = proj[:, o:o + LANES]

        xl_ref[0, rows, :] = x_lru
        gl_ref[0, rows, :] = jax.nn.gelu(g_lru).astype(BF16)

        c, sa, sb = c_all[rows], sa_all[rows], sb_all[rows]
        qn = q_lat * _rms(q_lat, q_lora) * qa_ref[...]
        q_full = jnp.dot(qn.astype(BF16), wq_ref[...], preferred_element_type=F32)
        kvn = kv_lat * _rms(kv_lat, kv_lora) * kva_ref[...]
        kv_full = jnp.dot(kvn.astype(BF16), wkv_ref[...], preferred_element_type=F32)

        pe_ss = jnp.sum(k_pe * k_pe, axis=-1, keepdims=True)
        pe_rot = _rope(k_pe * kn_rope_ref[...], c, sa, sb)
        for hd in range(ATTN_HEADS):
            qh = q_full[:, hd * QK_PAD_DIM:(hd + 1) * QK_PAD_DIM]
            qh = qh * (_rms(qh, QK_HEAD_DIM) * sm_scale) * qn_ref[...]
            q_ref[0, hd, rows, 0:QK_NOPE_DIM] = qh[:, 0:QK_NOPE_DIM].astype(BF16)
            q_ref[0, hd, rows, QK_NOPE_DIM:QK_PAD_DIM] = _rope(qh[:, QK_NOPE_DIM:QK_PAD_DIM],
                                                               c, sa, sb).astype(BF16)
            base = hd * (QK_NOPE_DIM + V_HEAD_DIM)
            kn = kv_full[:, base:base + QK_NOPE_DIM]
            vv = kv_full[:, base + QK_NOPE_DIM:base + QK_NOPE_DIM + V_HEAD_DIM]
            r = lax.rsqrt((jnp.sum(kn * kn, axis=-1, keepdims=True) + pe_ss) * (1.0 / QK_HEAD_DIM) + EPS)
            k_ref[0, hd, rows, 0:QK_NOPE_DIM] = (kn * r * kn_nope_ref[...]).astype(BF16)
            k_ref[0, hd, rows, QK_NOPE_DIM:QK_PAD_DIM] = (pe_rot * r).astype(BF16)
            v_ref[0, hd, rows, :] = vv.astype(BF16)


def _in_proj(x, pos_row, mod, norm1_g, w_in_p, q_a_norm, w_q_p, kv_a_norm, w_kv, q_norm_p,
             k_norm_nope, k_norm_rope_p, inv_col, *, q_lora, kv_lora, lru_w):
    b, s, d = x.shape
    t = min(PROJ_TILE, s)
    full = lambda a: pl.BlockSpec(a.shape, lambda i, j: (0,) * a.ndim)
    kern = functools.partial(_in_proj_kernel, q_lora=q_lora, kv_lora=kv_lora, lru_w=lru_w,
                             chunk=min(PROJ_CHUNK, t))
    head_out = lambda w: pl.BlockSpec((1, ATTN_HEADS, t, w), lambda i, j: (i, 0, j, 0))
    return pl.pallas_call(
        kern,
        grid=(b, s // t),
        in_specs=[pl.BlockSpec((1, t, d), lambda i, j: (i, j, 0)),
                  pl.BlockSpec((1, 1, t), lambda i, j: (i, 0, j)),
                  pl.BlockSpec((1,) + mod.shape[1:], lambda i, j: (i, 0, 0)),
                  full(norm1_g), full(w_in_p), full(q_a_norm), full(w_q_p), full(kv_a_norm),
                  full(w_kv), full(q_norm_p), full(k_norm_nope), full(k_norm_rope_p), full(inv_col)],
        out_specs=[head_out(QK_PAD_DIM), head_out(QK_PAD_DIM), head_out(V_HEAD_DIM),
                   pl.BlockSpec((1, t, lru_w), lambda i, j: (i, j, 0)),
                   pl.BlockSpec((1, t, lru_w), lambda i, j: (i, j, 0))],
        out_shape=[jax.ShapeDtypeStruct((b, ATTN_HEADS, s, QK_PAD_DIM), BF16),
                   jax.ShapeDtypeStruct((b, ATTN_HEADS, s, QK_PAD_DIM), BF16),
                   jax.ShapeDtypeStruct((b, ATTN_HEADS, s, V_HEAD_DIM), BF16),
                   jax.ShapeDtypeStruct((b, s, lru_w), F32),
                   jax.ShapeDtypeStruct((b, s, lru_w), BF16)],
        compiler_params=_cparams("parallel", "parallel"),
        name="in_proj",
    )(x, pos_row, mod, norm1_g, w_in_p, q_a_norm, w_q_p, kv_a_norm, w_kv, q_norm_p,
      k_norm_nope, k_norm_rope_p, inv_col)


def _attn_kernel(q_ref, k_ref, v_ref, o_ref, m_sc, acc_sc, s_sc, v_sc, *, tile, chunk):
    qi = pl.program_id(2)

    @pl.when(qi == 0)
    def _():
        v_sc[:, 0:V_HEAD_DIM] = v_ref[0, 0]
        v_sc[:, V_HEAD_DIM:V_PAD_DIM] = jnp.ones((v_sc.shape[0], V_PAD_DIM - V_HEAD_DIM), BF16)

    m_sc[...] = jnp.full(m_sc.shape, -jnp.inf, F32)
    acc_sc[...] = jnp.zeros(acc_sc.shape, F32)

    def scores(j, slot):
        start = pl.multiple_of(j * tile, tile)
        k = k_ref[0, 0, pl.ds(start, tile), :]
        s_sc[slot] = lax.dot_general(q_ref[0, 0], k, (((1,), (1,)), ((), ())),
                                     preferred_element_type=F32)

    def accumulate(j, slot, masked):
        start = pl.multiple_of(j * tile, tile)
        for r in range(tile // chunk):
            rows = slice(r * chunk, (r + 1) * chunk)
            cols = -(-(r + 1) * chunk // LANES) * LANES if masked else tile
            s = s_sc[slot, rows, 0:cols]
            if masked:
                row = lax.broadcasted_iota(jnp.int32, s.shape, 0) + r * chunk
                col = lax.broadcasted_iota(jnp.int32, s.shape, 1)
                s = jnp.where(col <= row, s, -jnp.inf)
            v = v_sc[pl.ds(start, cols), :]
            m_old = m_sc[rows, :]
            m_new = jnp.maximum(m_old, jnp.max(s, axis=-1, keepdims=True))
            alpha = jnp.exp2(m_old - m_new)
            p = jnp.exp2(s - jnp.tile(m_new, (1, cols // LANES)))
            acc_sc[rows, :] = (jnp.tile(alpha, (1, V_PAD_DIM // LANES)) * acc_sc[rows, :]
                               + jnp.dot(p.astype(BF16), v, preferred_element_type=F32))
            m_sc[rows, :] = m_new

    scores(0, 0)

    def pair(p, carry):
        scores(2 * p + 1, 1)
        accumulate(2 * p, 0, False)
        scores(2 * p + 2, 0)
        accumulate(2 * p + 1, 1, False)
        return carry

    lax.fori_loop(0, qi // 2, pair, 0)

    @pl.when(qi % 2 == 0)
    def _():
        accumulate(qi, 0, True)

    @pl.when(qi % 2 == 1)
    def _():
        scores(qi, 1)
        accumulate(qi - 1, 0, False)
        accumulate(qi, 1, True)

    o_ref[0] = (acc_sc[:, 0:V_HEAD_DIM] / acc_sc[:, V_HEAD_DIM:2 * V_HEAD_DIM]).astype(o_ref.dtype)


def _attention(q, k, v):
    b, h, s, _ = q.shape
    t = min(ATTN_TILE, s)
    assert V_HEAD_DIM == LANES and V_PAD_DIM == 2 * V_HEAD_DIM
    kern = functools.partial(_attn_kernel, tile=t, chunk=min(ATTN_CHUNK, t))
    return pl.pallas_call(
        kern,
        grid=(b, h, s // t),
        in_specs=[pl.BlockSpec((1, 1, t, QK_PAD_DIM), lambda i, j, n: (i, j, n, 0)),
                  pl.BlockSpec((1, 1, s, QK_PAD_DIM), lambda i, j, n: (i, j, 0, 0)),
                  pl.BlockSpec((1, 1, s, V_HEAD_DIM), lambda i, j, n: (i, j, 0, 0))],
        out_specs=pl.BlockSpec((1, t, V_HEAD_DIM), lambda i, j, n: (i, n, j)),
        out_shape=jax.ShapeDtypeStruct((b, s, h * V_HEAD_DIM), BF16),
        scratch_shapes=[pltpu.VMEM((t, LANES), F32), pltpu.VMEM((t, V_PAD_DIM), F32),
                        pltpu.VMEM((2, t, t), F32), pltpu.VMEM((s, V_PAD_DIM), BF16)],
        compiler_params=_cparams("parallel", "parallel", "arbitrary"),
        name="attention",
    )(q, k, v)


def _lru_kernel(xl_ref, gl_ref, pos_ref, cw_ref, cb_ref, wax_ref, ba_ref, bx_ref, lam_ref, gn_ref,
                o_ref, ext_sc, a_sc, b_sc, h_sc, carry_sc):
    t, c = a_sc.shape
    head_w = c // LRU_HEADS

    @pl.when(pl.program_id(1) == 0)
    def _():
        ext_sc[0:SUBLANES, :] = jnp.zeros((SUBLANES, c), F32)
        carry_sc[...] = jnp.zeros(carry_sc.shape, F32)

    x = xl_ref[0]
    ext_sc[SUBLANES:SUBLANES + t, :] = x
    xc = cb_ref[...] + x * cw_ref[CONV_WIDTH - 1:CONV_WIDTH, :]
    for back in range(1, CONV_WIDTH):
        tap = CONV_WIDTH - 1 - back
        xc = xc + ext_sc[SUBLANES - back:SUBLANES - back + t, :] * cw_ref[tap:tap + 1, :]
    ext_sc[0:SUBLANES, :] = x[t - SUBLANES:t, :]

    reset = _column((pos_ref[0] == 0).astype(F32)) > 0.5
    lam = lam_ref[...]
    neg_sp = -LRU_C * (jnp.maximum(-lam, 0.0) + jnp.log1p(jnp.exp(-jnp.abs(lam))))
    for hd in range(LRU_HEADS):
        sl = slice(hd * head_w, (hd + 1) * head_w)
        xh = xc[:, sl]
        gates = jnp.dot(xh.astype(BF16), wax_ref[hd], preferred_element_type=F32)
        r = _sigmoid(gates[:, :head_w] + ba_ref[:, sl])
        i = _sigmoid(gates[:, head_w:] + bx_ref[:, sl])
        a = jnp.exp(r * neg_sp[:, sl])
        gap = jnp.maximum(1.0 - a * a, 0.0)
        mult = jnp.where(gap > 0.0, gap * lax.rsqrt(gap), 0.0)
        a = jnp.where(reset, 0.0, a)
        mult = jnp.where(reset, 1.0, mult)
        a_sc[:, sl] = a
        b_sc[:, sl] = mult * (i * xh)

    srow = lax.broadcasted_iota(jnp.int32, (SUBLANES, c), 0)

    def group(g, h_prev):
        start = pl.multiple_of(g * SUBLANES, SUBLANES)
        a = a_sc[pl.ds(start, SUBLANES), :]
        bb = b_sc[pl.ds(start, SUBLANES), :]
        sh = 1
        while sh < SUBLANES:
            keep = srow >= sh
            a_prev = jnp.where(keep, pltpu.roll(a, sh, 0), 1.0)
            b_prev = jnp.where(keep, pltpu.roll(bb, sh, 0), 0.0)
            bb = bb + a * b_prev
            a = a * a_prev
            sh *= 2
        hh = bb + a * h_prev
        h_sc[pl.ds(start, SUBLANES), :] = hh
        return jnp.broadcast_to(hh[SUBLANES - 1:SUBLANES, :], (SUBLANES, c))

    carry_sc[...] = lax.fori_loop(0, t // SUBLANES, group, carry_sc[...], unroll=4)

    y = h_sc[...] * gl_ref[0].astype(F32)
    o_ref[0] = (y * _rms(y, c) * gn_ref[...]).astype(o_ref.dtype)


def _rg_lru(x_lru, gelu_g, pos_row, conv_w, conv_b, w_ax, b_a, b_x, lam, out_norm):
    b, s, c = x_lru.shape
    t = min(LRU_TILE, s)
    full = lambda a: pl.BlockSpec(a.shape, lambda i, j: (0,) * a.ndim)
    return pl.pallas_call(
        _lru_kernel,
        grid=(b, s // t),
        in_specs=[pl.BlockSpec((1, t, c), lambda i, j: (i, j, 0)),
                  pl.BlockSpec((1, t, c), lambda i, j: (i, j, 0)),
                  pl.BlockSpec((1, 1, t), lambda i, j: (i, 0, j)),
                  full(conv_w), full(conv_b), full(w_ax), full(b_a), full(b_x), full(lam),
                  full(out_norm)],
        out_specs=pl.BlockSpec((1, t, c), lambda i, j: (i, j, 0)),
        out_shape=jax.ShapeDtypeStruct((b, s, c), BF16),
        scratch_shapes=[pltpu.VMEM((t + SUBLANES, c), F32), pltpu.VMEM((t, c), F32),
                        pltpu.VMEM((t, c), F32), pltpu.VMEM((t, c), F32),
                        pltpu.VMEM((SUBLANES, c), F32)],
        compiler_params=_cparams("parallel", "arbitrary"),
        name="rg_lru",
    )(x_lru, gelu_g, pos_row, conv_w, conv_b, w_ax, b_a, b_x, lam, out_norm)


def _route_kernel(x_ref, at_ref, lr_ref, mod_ref, an_ref, wa_ref, wl_ref, g2_ref, wr_ref, br_ref,
                  x1_ref, h2_ref, ids_ref, rank_ref, wts_ref, cnt_ref, carry_sc,
                  *, n_experts, group_row0, expert_row0):
    @pl.when(pl.program_id(0) == 0)
    def _():
        carry_sc[...] = jnp.zeros(carry_sc.shape, F32)

    x = x_ref[0]
    t, d = x.shape
    gate1 = mod_ref[0, 2:3, :]
    shift2 = mod_ref[0, 3:4, :]
    gain2 = g2_ref[...] * (1.0 + mod_ref[0, 4:5, :])
    at = at_ref[0].astype(F32)
    at_n = at * _rms(at, at.shape[-1]) * an_ref[...]
    mix = jnp.dot(at_n.astype(BF16), wa_ref[...], preferred_element_type=F32)
    mix = mix + jnp.dot(lr_ref[0], wl_ref[...], preferred_element_type=F32)
    x1 = x + gate1 * mix
    x1_ref[0] = x1
    h2 = x1 * _rms(x1, d) * gain2 + shift2
    h2_ref[0] = h2

    logits = lax.dot_general(wr_ref[...], h2.astype(BF16), (((1,), (1,)), ((), ())),
                             preferred_element_type=F32) + br_ref[...]
    per_group = n_experts // N_GROUPS
    lg = logits[group_row0:group_row0 + N_GROUPS, :]
    grow = lax.broadcasted_iota(jnp.int32, lg.shape, 0)
    gmax = jnp.max(lg, axis=0, keepdims=True)
    g_idx = jnp.min(jnp.where(lg == gmax, grow, N_GROUPS), axis=0, keepdims=True)
    g_p = 1.0 / jnp.sum(jnp.exp(lg - gmax), axis=0, keepdims=True)
    sel = jnp.zeros((per_group, t), F32)
    for g in range(N_GROUPS):
        r0 = expert_row0 + g * per_group
        sel = sel + jnp.where(g_idx == g, logits[r0:r0 + per_group, :], 0.0)
    erow = lax.broadcasted_iota(jnp.int32, sel.shape, 0)
    m1 = jnp.max(sel, axis=0, keepdims=True)
    i1 = jnp.min(jnp.where(sel == m1, erow, per_group), axis=0, keepdims=True)
    sel2 = jnp.where(erow == i1, -jnp.inf, sel)
    m2 = jnp.max(sel2, axis=0, keepdims=True)
    i2 = jnp.min(jnp.where(sel2 == m2, erow, per_group), axis=0, keepdims=True)
    e21 = jnp.exp(m2 - m1)
    w1 = g_p / (1.0 + e21)
    w2 = g_p * e21 / (1.0 + e21)
    e1 = g_idx * per_group + i1
    e2 = g_idx * per_group + i2
    ids_ref[0:1, :] = e1
    ids_ref[1:2, :] = e2
    wts_ref[0:1, :] = w1
    wts_ref[1:2, :] = w2

    xrow = lax.broadcasted_iota(jnp.int32, (n_experts, t), 0)
    o1 = (xrow == e1).astype(F32)
    o2 = (xrow == e2).astype(F32)
    both = o1 + o2
    tr = lax.broadcasted_iota(jnp.int32, (t, t), 0)
    tc = lax.broadcasted_iota(jnp.int32, (t, t), 1)
    upper = jnp.where(tr < tc, 1.0, 0.0).astype(BF16)
    before = jnp.dot(both.astype(BF16), upper, preferred_element_type=F32) + carry_sc[...]
    rank_ref[0:1, :] = jnp.sum(o1 * before, axis=0, keepdims=True).astype(jnp.int32)
    rank_ref[1:2, :] = jnp.sum(o2 * before, axis=0, keepdims=True).astype(jnp.int32)
    carry_sc[...] = carry_sc[...] + jnp.sum(both, axis=1, keepdims=True)
    cnt_ref[...] = jnp.broadcast_to(carry_sc[...], cnt_ref.shape)


def _out_route(x, attn, lru_n, mod, attn_norm, w_out_a, w_out_l, norm2_g, w_r_t, b_r_col,
               *, n_experts, group_row0, expert_row0):
    b, s, d = x.shape
    t = min(ROUTE_TILE, s)
    nt = s // t
    n = b * s
    full = lambda a: pl.BlockSpec(a.shape, lambda i: (0,) * a.ndim)
    tok = lambda w: pl.BlockSpec((1, t, w), lambda i: (i // nt, i % nt, 0))
    kern = functools.partial(_route_kernel, n_experts=n_experts, group_row0=group_row0,
                             expert_row0=expert_row0)
    return pl.pallas_call(
        kern,
        grid=(b * nt,),
        in_specs=[tok(d), tok(attn.shape[-1]), tok(lru_n.shape[-1]),
                  pl.BlockSpec((1,) + mod.shape[1:], lambda i: (i // nt, 0, 0)),
                  full(attn_norm), full(w_out_a), full(w_out_l), full(norm2_g), full(w_r_t),
                  full(b_r_col)],
        out_specs=[tok(d), tok(d),
                   pl.BlockSpec((TOP_K, t), lambda i: (0, i)),
                   pl.BlockSpec((TOP_K, t), lambda i: (0, i)),
                   pl.BlockSpec((TOP_K, t), lambda i: (0, i)),
                   pl.BlockSpec((n_experts, LANES), lambda i: (0, 0))],
        out_shape=[jax.ShapeDtypeStruct((b, s, d), F32),
                   jax.ShapeDtypeStruct((b, s, d), F32),
                   jax.ShapeDtypeStruct((TOP_K, n), jnp.int32),
                   jax.ShapeDtypeStruct((TOP_K, n), jnp.int32),
                   jax.ShapeDtypeStruct((TOP_K, n), F32),
                   jax.ShapeDtypeStruct((n_experts, LANES), F32)],
        scratch_shapes=[pltpu.VMEM((n_experts, 1), F32)],
        compiler_params=_cparams("arbitrary"),
        name="out_route",
    )(x, attn, lru_n, mod, attn_norm, w_out_a, w_out_l, norm2_g, w_r_t, b_r_col)


def _row_copy(src, src_row, dst, dst_row, sem):
    return pltpu.make_async_copy(src.at[pl.ds(src_row, 1), :], dst.at[pl.ds(dst_row, 1), :], sem)


def _dispatch_kernel(cnt_ref, start_ref, nb_ref, dest_ref, h2_ref, xs_ref, hbuf, zero_sc, ld_sem,
                     st_sem, sem, *, tile, block, n_experts):
    i = pl.program_id(0)

    @pl.when(i == 0)
    def _():
        zero_sc[...] = jnp.zeros(zero_sc.shape, zero_sc.dtype)

        def tail_copy(j):
            return pltpu.make_async_copy(zero_sc, xs_ref.at[pl.ds(j * block, block), :], sem)

        def tail_issue(j, carry):
            tail_copy(j).start()
            return carry

        def tail_drain(j, carry):
            tail_copy(j).wait()
            return carry

        n_blocks = xs_ref.shape[0] // block
        lax.fori_loop(nb_ref[0], n_blocks, tail_issue, 0)
        lax.fori_loop(nb_ref[0], n_blocks, tail_drain, 0)

        def per_expert(e, carry):
            cnt = cnt_ref[e]
            first = start_ref[e] + cnt
            n_pad = (block - cnt % block) % block

            def issue(r, c2):
                _row_copy(zero_sc, 0, xs_ref, first + r, sem).start()
                return c2

            lax.fori_loop(0, n_pad, issue, 0)

            def drain(r, c2):
                _row_copy(zero_sc, 0, xs_ref, first, sem).wait()
                return c2

            lax.fori_loop(0, n_pad, drain, 0)
            return carry

        lax.fori_loop(0, n_experts, per_expert, 0)

    n_steps = pl.num_programs(0)

    def load(tile_idx, buf):
        return pltpu.make_async_copy(h2_ref.at[pl.ds(tile_idx * tile, tile), :], hbuf.at[buf],
                                     ld_sem.at[buf])

    def scatter(buf, lo, hi):
        for tk in range(lo, hi):
            for k in range(TOP_K):
                _row_copy(hbuf.at[buf], tk, xs_ref, dest_ref[k, tk], st_sem.at[buf]).start(priority=k % 2)

    def scatter_wait(buf):
        for _ in range(TOP_K):
            pltpu.make_async_copy(hbuf.at[buf], xs_ref.at[pl.ds(0, tile), :], st_sem.at[buf]).wait()

    @pl.when(i == 0)
    def _():
        load(0, 0).start()

    for buf in range(2):
        @pl.when(i % 2 == buf)
        def _(buf=buf):
            load(i, buf).wait()
            scatter(buf, 0, tile // 2)

            @pl.when(i >= 1)
            def _():
                scatter_wait(1 - buf)

            @pl.when(i + 1 < n_steps)
            def _():
                load(i + 1, 1 - buf).start()

            scatter(buf, tile // 2, tile)

            @pl.when(i == n_steps - 1)
            def _():
                scatter_wait(buf)


def _dispatch(counts, starts, n_blocks_used, dest, h2, cap, *, block, n_experts):
    n, w = h2.shape
    t = min(DISPATCH_TILE, n)
    kern = functools.partial(_dispatch_kernel, tile=t, block=block, n_experts=n_experts)
    smem_tile = pl.BlockSpec((TOP_K, t), lambda i, c, s, nb: (0, i), memory_space=pltpu.SMEM)
    return pl.pallas_call(
        kern,
        grid_spec=pltpu.PrefetchScalarGridSpec(
            num_scalar_prefetch=3,
            grid=(n // t,),
            in_specs=[smem_tile, pl.BlockSpec(memory_space=pl.ANY)],
            out_specs=pl.BlockSpec(memory_space=pl.ANY),
            scratch_shapes=[pltpu.VMEM((2, t, w), h2.dtype), pltpu.VMEM((block, w), h2.dtype),
                            pltpu.SemaphoreType.DMA((2,)), pltpu.SemaphoreType.DMA((2,)),
                            pltpu.SemaphoreType.DMA(())]),
        out_shape=jax.ShapeDtypeStruct((cap, w), h2.dtype),
        compiler_params=_cparams("arbitrary"),
        name="dispatch",
    )(counts, starts, n_blocks_used, dest, h2)


def _expert_kernel(be_ref, nb_ref, xs_ref, wg_ref, wu_ref, wd_ref, ys_ref, wgu_sc, wdn_sc, *, d_ff):
    j = pl.program_id(0)
    live = jnp.minimum(j, nb_ref[0] - 1)

    @pl.when((j == 0) | (be_ref[live] != be_ref[jnp.maximum(live - 1, 0)]))
    def _():
        wgu_sc[:, 0:d_ff] = wg_ref[0].astype(BF16)
        wgu_sc[:, d_ff:2 * d_ff] = wu_ref[0].astype(BF16)
        wdn_sc[...] = wd_ref[0].astype(BF16)

    @pl.when(j < nb_ref[0])
    def _():
        xb = xs_ref[...].astype(BF16)
        gu = jnp.dot(xb, wgu_sc[...], preferred_element_type=F32)
        gt = gu[:, :d_ff]
        hid = gt * jax.nn.sigmoid(gt) * gu[:, d_ff:]
        ys_ref[...] = jnp.dot(hid.astype(BF16), wdn_sc[...], preferred_element_type=F32)

    @pl.when(pl.program_id(0) >= nb_ref[0])
    def _():
        ys_ref[...] = jnp.zeros(ys_ref.shape, ys_ref.dtype)


def _experts(block_e, n_blocks_used, xs, w_gate, w_up, w_down, *, block):
    cap, w = xs.shape
    d = w_down.shape[2]
    d_ff = w_down.shape[1]
    kern = functools.partial(_expert_kernel, d_ff=d_ff)
    live = lambda j, nb: jnp.minimum(j, nb[0] - 1)
    return pl.pallas_call(
        kern,
        grid_spec=pltpu.PrefetchScalarGridSpec(
            num_scalar_prefetch=2,
            grid=(cap // block,),
            in_specs=[pl.BlockSpec((block, w), lambda j, be, nb: (live(j, nb), 0)),
                      pl.BlockSpec((1, d, d_ff), lambda j, be, nb: (be[live(j, nb)], 0, 0)),
                      pl.BlockSpec((1, d, d_ff), lambda j, be, nb: (be[live(j, nb)], 0, 0)),
                      pl.BlockSpec((1, d_ff, d), lambda j, be, nb: (be[live(j, nb)], 0, 0))],
            out_specs=pl.BlockSpec((block, w), lambda j, be, nb: (j, 0)),
            scratch_shapes=[pltpu.VMEM((d, 2 * d_ff), BF16), pltpu.VMEM((d_ff, d), BF16)]),
        out_shape=jax.ShapeDtypeStruct((cap, w), F32),
        compiler_params=_cparams("arbitrary"),
        name="experts",
    )(block_e, n_blocks_used, xs, w_gate, w_up, w_down)


def _combine_kernel(dest_ref, dest_nx_ref, wts_ref, x1_ref, mod_ref, ys_ref, o_ref, buf, sem, *,
                    tile):
    i = pl.program_id(0)
    slot = i % 2

    def gather(d_ref, sl):
        for tk in range(tile):
            for k in range(TOP_K):
                _row_copy(ys_ref, d_ref[k, tk], buf.at[sl, k], tk, sem.at[sl]).start(priority=k % 2)

    @pl.when(i == 0)
    def _():
        gather(dest_ref, slot)

    @pl.when(i + 1 < pl.num_programs(0))
    def _():
        gather(dest_nx_ref, 1 - slot)

    for k in range(TOP_K):
        pltpu.make_async_copy(ys_ref.at[pl.ds(0, tile), :], buf.at[slot, k], sem.at[slot]).wait()

    d = o_ref.shape[-1]
    w0 = jnp.tile(_column(wts_ref[0:1, :]), (1, d // LANES))
    w1 = jnp.tile(_column(wts_ref[1:2, :]), (1, d // LANES))
    gate2 = mod_ref[0, 5:6, :]
    o_ref[0] = x1_ref[0] + gate2 * (w0 * buf[slot, 0] + w1 * buf[slot, 1])


def _combine(dest, wts, x1, mod, ys):
    b, s, d = x1.shape
    w = ys.shape[1]
    t = min(COMBINE_TILE, s)
    nt = s // t
    last = b * nt - 1
    kern = functools.partial(_combine_kernel, tile=t)
    smem_now = pl.BlockSpec((TOP_K, t), lambda i: (0, i), memory_space=pltpu.SMEM)
    smem_next = pl.BlockSpec((TOP_K, t), lambda i: (0, jnp.minimum(i + 1, last)),
                             memory_space=pltpu.SMEM)
    return pl.pallas_call(
        kern,
        grid=(b * nt,),
        in_specs=[smem_now, smem_next,
                  pl.BlockSpec((TOP_K, t), lambda i: (0, i)),
                  pl.BlockSpec((1, t, d), lambda i: (i // nt, i % nt, 0)),
                  pl.BlockSpec((1,) + mod.shape[1:], lambda i: (i // nt, 0, 0)),
                  pl.BlockSpec(memory_space=pl.ANY)],
        out_specs=pl.BlockSpec((1, t, d), lambda i: (i // nt, i % nt, 0)),
        out_shape=jax.ShapeDtypeStruct((b, s, d), F32),
        scratch_shapes=[pltpu.VMEM((2, TOP_K, t, w), F32), pltpu.SemaphoreType.DMA((2,))],
        compiler_params=_cparams("arbitrary"),
        name="combine",
    )(dest, dest, wts, x1, mod, ys)


def _layer(x, c, pos_row, w_ada, b_ada, norm1_g, w_in, q_a_norm, w_q_b, kv_a_norm, w_kv_b, q_norm,
           k_norm, conv_w, conv_b, lru_wa, lru_ba, lru_wx, lru_bx, lru_lambda, attn_out_norm,
           lru_out_norm, w_out, norm2_g, w_rg, b_rg, w_re, b_re, w_gate, w_up, w_down):
    b, s, d = x.shape
    n = b * s
    q_lora = q_a_norm.shape[0]
    kv_lora = kv_a_norm.shape[0]
    lru_w = lru_lambda.shape[0]
    attn_w = ATTN_HEADS * V_HEAD_DIM
    n_experts = w_gate.shape[0]
    row = lambda v: v.reshape(1, -1)

    mod = _adaln(c, w_ada, b_ada).reshape(b, 6, d)

    o1 = q_lora
    o2 = o1 + kv_lora
    o3 = o2 + QK_ROPE_DIM
    o4 = o3 + lru_w
    w_in_p = jnp.concatenate(
        [w_in[:, :o2], w_in[:, o3:o4], w_in[:, o4:], w_in[:, o2:o3],
         jnp.zeros((d, LANES - QK_ROPE_DIM), w_in.dtype)], axis=1).astype(BF16)
    pad_h = QK_PAD_DIM - QK_HEAD_DIM
    w_q_p = jnp.pad(w_q_b.reshape(q_lora, ATTN_HEADS, QK_HEAD_DIM),
                    ((0, 0), (0, 0), (0, pad_h))).reshape(q_lora, ATTN_HEADS * QK_PAD_DIM).astype(BF16)
    q_norm_p = row(jnp.pad(q_norm, (0, pad_h)))
    k_norm_nope = row(k_norm[:QK_NOPE_DIM])
    k_norm_rope_p = row(jnp.pad(k_norm[QK_NOPE_DIM:], (0, LANES - QK_ROPE_DIM)))
    inv = 1.0 / (ROPE_THETA ** (np.arange(0, QK_ROPE_DIM, 2, dtype=np.float32) / QK_ROPE_DIM))
    inv_col = jnp.asarray(inv.reshape(-1, 1))

    q, k, v, x_lru, gelu_g = _in_proj(
        x, pos_row, mod, row(norm1_g), w_in_p, row(q_a_norm), w_q_p, row(kv_a_norm),
        w_kv_b.astype(BF16), q_norm_p, k_norm_nope, k_norm_rope_p, inv_col,
        q_lora=q_lora, kv_lora=kv_lora, lru_w=lru_w)

    attn = _attention(q, k, v)

    w_ax = jnp.concatenate([lru_wa, lru_wx], axis=-1).astype(BF16)
    lru_n = _rg_lru(x_lru, gelu_g, pos_row, conv_w, row(conv_b), w_ax, row(lru_ba), row(lru_bx),
                    row(lru_lambda), row(lru_out_norm))

    group_row0, expert_row0 = 0, SUBLANES
    w_r_t = jnp.concatenate([w_rg.T, jnp.zeros((expert_row0 - N_GROUPS, d), w_rg.dtype), w_re.T],
                            axis=0).astype(BF16)
    b_r_col = jnp.concatenate([b_rg, jnp.zeros((expert_row0 - N_GROUPS,), b_rg.dtype),
                               b_re]).reshape(-1, 1)
    x1, h2, ids, ranks, wts, cnt = _out_route(
        x, attn, lru_n, mod, row(attn_out_norm), w_out[:attn_w].astype(BF16),
        w_out[attn_w:].astype(BF16), row(norm2_g), w_r_t, b_r_col,
        n_experts=n_experts, group_row0=group_row0, expert_row0=expert_row0)

    blk = EXPERT_BLOCK
    counts = cnt[:, 0].astype(jnp.int32)
    padded = (counts + blk - 1) // blk * blk
    pad_ends = jnp.cumsum(padded)
    starts = (pad_ends - padded).astype(jnp.int32)
    dest = ranks
    for e in range(n_experts):
        dest = dest + jnp.where(ids == e, starts[e], 0)
    cap = (n * TOP_K + n_experts * (blk - 1) + blk - 1) // blk * blk
    n_blocks = cap // blk
    block_first_row = jnp.arange(n_blocks, dtype=jnp.int32) * blk
    block_e = jnp.minimum(jnp.sum(pad_ends[None, :] <= block_first_row[:, None], axis=1),
                          n_experts - 1).astype(jnp.int32)
    n_used = (pad_ends[-1:] // blk).astype(jnp.int32)

    xs = _dispatch(counts, starts, n_used, dest, h2.reshape(n, d), cap,
                   block=blk, n_experts=n_experts)
    ys = _experts(block_e, n_used, xs, w_gate, w_up, w_down, block=blk)
    return _combine(dest, wts, x1, mod, ys)


def kernel(x, c, positions, w_ada, b_ada, norm1_g, w_in, q_a_norm, w_q_b, kv_a_norm, w_kv_b, q_norm, k_norm, conv_w, conv_b, lru_wa, lru_ba, lru_wx, lru_bx, lru_lambda, attn_out_norm, lru_out_norm, w_out, norm2_g, w_router_group, b_router_group, w_router_expert, b_router_expert, w_gate, w_up, w_down):
    b, s, _ = x.shape
    pos_row = positions.reshape(b, 1, s)
    for l in range(w_ada.shape[0]):
        x = _layer(x, c, pos_row, w_ada[l], b_ada[l], norm1_g[l], w_in[l], q_a_norm[l], w_q_b[l],
                   kv_a_norm[l], w_kv_b[l], q_norm[l], k_norm[l], conv_w[l], conv_b[l], lru_wa[l],
                   lru_ba[l], lru_wx[l], lru_bx[l], lru_lambda[l], attn_out_norm[l], lru_out_norm[l],
                   w_out[l], norm2_g[l], w_router_group[l], b_router_group[l], w_router_expert[l],
                   b_router_expert[l], w_gate[l], w_up[l], w_down[l])
    return x
```

```python
import functools
import math

import numpy as np
import jax
import jax.numpy as jnp
from jax import lax
from jax.experimental import pallas as pl
from jax.experimental.pallas import tpu as pltpu

F32 = jnp.float32
BF16 = jnp.bfloat16

ATTN_HEADS = 4
QK_NOPE_DIM = 128
QK_ROPE_DIM = 64
QK_HEAD_DIM = QK_NOPE_DIM + QK_ROPE_DIM
V_HEAD_DIM = 128
ROPE_THETA = 10000.0
LRU_HEADS = 4
CONV_WIDTH = 4
LRU_C = 8.0
N_GROUPS = 4
TOP_K = 2
EPS = 1e-6

LANES = 128
SUBLANES = 8
QK_PAD_DIM = 256
V_PAD_DIM = 256
VMEM_LIMIT = 56 * 1024 * 1024

PROJ_TILE = 512
PROJ_CHUNK = 128
ATTN_TILE = 1024
ATTN_CHUNK = 256
LRU_TILE = 512
ROUTE_TILE = 512
DISPATCH_TILE = 512
COMBINE_TILE = 512
EXPERT_BLOCK = 256


def _cparams(*sem):
    return pltpu.CompilerParams(dimension_semantics=sem, vmem_limit_bytes=VMEM_LIMIT)


def _rms(x, n):
    return lax.rsqrt(jnp.sum(x * x, axis=-1, keepdims=True) * (1.0 / n) + EPS)


def _sigmoid(x):
    return 0.5 * jnp.tanh(0.5 * x) + 0.5


def _column(row, width=LANES):
    t = row.shape[-1]
    return jnp.broadcast_to(row, (width, t)).T


def _adaln_kernel(c_ref, w_ref, b_ref, o_ref):
    c = c_ref[...]
    sc = c * jax.nn.sigmoid(c)
    o_ref[...] = jnp.dot(sc.astype(BF16), w_ref[...].astype(BF16),
                         preferred_element_type=F32) + b_ref[...]


def _adaln(c, w_ada, b_ada):
    b, d = c.shape
    n_out = w_ada.shape[1]
    return pl.pallas_call(
        _adaln_kernel,
        grid=(n_out // d,),
        in_specs=[pl.BlockSpec((b, d), lambda j: (0, 0)),
                  pl.BlockSpec((d, d), lambda j: (0, j)),
                  pl.BlockSpec((1, d), lambda j: (0, j))],
        out_specs=pl.BlockSpec((b, d), lambda j: (0, j)),
        out_shape=jax.ShapeDtypeStruct((b, n_out), F32),
        compiler_params=_cparams("parallel"),
        name="adaln",
    )(c, w_ada, b_ada.reshape(1, n_out))


def _rope_tables(pos_row, inv_col):
    half = QK_ROPE_DIM // 2
    ang = inv_col * pos_row
    cos = jnp.cos(ang)
    sin = jnp.sin(ang)
    t = ang.shape[-1]
    zeros = lambda n: jnp.zeros((n, t), F32)
    c = jnp.concatenate([cos, cos, zeros(LANES - QK_ROPE_DIM)], axis=0)
    sa = jnp.concatenate([-sin, zeros(LANES - half)], axis=0)
    sb = jnp.concatenate([zeros(half), sin, zeros(LANES - QK_ROPE_DIM)], axis=0)
    return c.T, sa.T, sb.T


def _rope(x, c, sa, sb):
    return x * c + pltpu.roll(x, LANES - QK_ROPE_DIM // 2, 1) * sa + pltpu.roll(x, QK_ROPE_DIM // 2, 1) * sb


def _in_proj_kernel(x_ref, pos_ref, mod_ref, g1_ref, w_in_ref, qa_ref, wq_ref, kva_ref, wkv_ref,
                    qn_ref, kn_nope_ref, kn_rope_ref, inv_ref,
                    q_ref, k_ref, v_ref, xl_ref, gl_ref, *, q_lora, kv_lora, lru_w, chunk):
    t, d = x_ref.shape[1:]
    shift = mod_ref[0, 0:1, :]
    gain = g1_ref[...] * (1.0 + mod_ref[0, 1:2, :])
    c_all, sa_all, sb_all = _rope_tables(pos_ref[0].astype(F32), inv_ref[...])
    sm_scale = QK_HEAD_DIM ** -0.5 * math.log2(math.e)

    for r0 in range(0, t, chunk):
        rows = slice(r0, r0 + chunk)
        x = x_ref[0, rows, :]
        h = x * _rms(x, d) * gain + shift
        proj = jnp.dot(h.astype(BF16), w_in_ref[...], preferred_element_type=F32)
        o = 0
        q_lat = proj[:, o:o + q_lora]; o += q_lora
        kv_lat = proj[:, o:o + kv_lora]; o += kv_lora
        x_lru = proj[:, o:o + lru_w]; o += lru_w
        g_lru = proj[:, o:o + lru_w]; o += lru_w
        k_pe = proj[:, o:o + LANES]

        xl_ref[0, rows, :] = x_lru
        gl_ref[0, rows, :] = jax.nn.gelu(g_lru).astype(BF16)

        c, sa, sb = c_all[rows], sa_all[rows], sb_all[rows]
        qn = q_lat * _rms(q_lat, q_lora) * qa_ref[...]
        q_full = jnp.dot(qn.astype(BF16), wq_ref[...], preferred_element_type=F32)
        kvn = kv_lat * _rms(kv_lat, kv_lora) * kva_ref[...]
        kv_full = jnp.dot(kvn.astype(BF16), wkv_ref[...], preferred_element_type=F32)

        pe_ss = jnp.sum(k_pe * k_pe, axis=-1, keepdims=True)
        pe_rot = _rope(k_pe * kn_rope_ref[...], c, sa, sb)
        for hd in range(ATTN_HEADS):
            qh = q_full[:, hd * QK_PAD_DIM:(hd + 1) * QK_PAD_DIM]
            qh = qh * (_rms(qh, QK_HEAD_DIM) * sm_scale) * qn_ref[...]
            q_ref[0, hd, rows, 0:QK_NOPE_DIM] = qh[:, 0:QK_NOPE_DIM].astype(BF16)
            q_ref[0, hd, rows, QK_NOPE_DIM:QK_PAD_DIM] = _rope(qh[:, QK_NOPE_DIM:QK_PAD_DIM],
                                                               c, sa, sb).astype(BF16)
            base = hd * (QK_NOPE_DIM + V_HEAD_DIM)
            kn = kv_full[:, base:base + QK_NOPE_DIM]
            vv = kv_full[:, base + QK_NOPE_DIM:base + QK_NOPE_DIM + V_HEAD_DIM]
            r = lax.rsqrt((jnp.sum(kn * kn, axis=-1, keepdims=True) + pe_ss) * (1.0 / QK_HEAD_DIM) + EPS)
            k_ref[0, hd, rows, 0:QK_NOPE_DIM] = (kn * r * kn_nope_ref[...]).astype(BF16)
            k_ref[0, hd, rows, QK_NOPE_DIM:QK_PAD_DIM] = (pe_rot * r).astype(BF16)
            v_ref[0, hd, rows, :] = vv.astype(BF16)


def _in_proj(x, pos_row, mod, norm1_g, w_in_p, q_a_norm, w_q_p, kv_a_norm, w_kv, q_norm_p,
             k_norm_nope, k_norm_rope_p, inv_col, *, q_lora, kv_lora, lru_w):
    b, s, d = x.shape
    t = min(PROJ_TILE, s)
    full = lambda a: pl.BlockSpec(a.shape, lambda i, j: (0,) * a.ndim)
    kern = functools.partial(_in_proj_kernel, q_lora=q_lora, kv_lora=kv_lora, lru_w=lru_w,
                             chunk=min(PROJ_CHUNK, t))
    head_out = lambda w: pl.BlockSpec((1, ATTN_HEADS, t, w), lambda i, j: (i, 0, j, 0))
    return pl.pallas_call(
        kern,
        grid=(b, s // t),
        in_specs=[pl.BlockSpec((1, t, d), lambda i, j: (i, j, 0)),
                  pl.BlockSpec((1, 1, t), lambda i, j: (i, 0, j)),
                  pl.BlockSpec((1,) + mod.shape[1:], lambda i, j: (i, 0, 0)),
                  full(norm1_g), full(w_in_p), full(q_a_norm), full(w_q_p), full(kv_a_norm),
                  full(w_kv), full(q_norm_p), full(k_norm_nope), full(k_norm_rope_p), full(inv_col)],
        out_specs=[head_out(QK_PAD_DIM), head_out(QK_PAD_DIM), head_out(V_HEAD_DIM),
                   pl.BlockSpec((1, t, lru_w), lambda i, j: (i, j, 0)),
                   pl.BlockSpec((1, t, lru_w), lambda i, j: (i, j, 0))],
        out_shape=[jax.ShapeDtypeStruct((b, ATTN_HEADS, s, QK_PAD_DIM), BF16),
                   jax.ShapeDtypeStruct((b, ATTN_HEADS, s, QK_PAD_DIM), BF16),
                   jax.ShapeDtypeStruct((b, ATTN_HEADS, s, V_HEAD_DIM), BF16),
                   jax.ShapeDtypeStruct((b, s, lru_w), F32),
                   jax.ShapeDtypeStruct((b, s, lru_w), BF16)],
        compiler_params=_cparams("parallel", "parallel"),
        name="in_proj",
    )(x, pos_row, mod, norm1_g, w_in_p, q_a_norm, w_q_p, kv_a_norm, w_kv, q_norm_p,
      k_norm_nope, k_norm_rope_p, inv_col)


def _attn_kernel(q_ref, k_ref, v_ref, o_ref, m_sc, acc_sc, s_sc, v_sc, *, tile, chunk):
    qi = pl.program_id(2)

    @pl.when(qi == 0)
    def _():
        v_sc[:, 0:V_HEAD_DIM] = v_ref[0, 0]
        v_sc[:, V_HEAD_DIM:V_PAD_DIM] = jnp.ones((v_sc.shape[0], V_PAD_DIM - V_HEAD_DIM), BF16)

    m_sc[...] = jnp.full(m_sc.shape, -jnp.inf, F32)
    acc_sc[...] = jnp.zeros(acc_sc.shape, F32)

    def scores(j, slot):
        start = pl.multiple_of(j * tile, tile)
        k = k_ref[0, 0, pl.ds(start, tile), :]
        s_sc[slot] = lax.dot_general(q_ref[0, 0], k, (((1,), (1,)), ((), ())),
                                     preferred_element_type=F32)

    def accumulate(j, slot, masked):
        start = pl.multiple_of(j * tile, tile)
        for r in range(tile // chunk):
            rows = slice(r * chunk, (r + 1) * chunk)
            cols = -(-(r + 1) * chunk // LANES) * LANES if masked else tile
            s = s_sc[slot, rows, 0:cols]
            if masked:
                row = lax.broadcasted_iota(jnp.int32, s.shape, 0) + r * chunk
                col = lax.broadcasted_iota(jnp.int32, s.shape, 1)
                s = jnp.where(col <= row, s, -jnp.inf)
            v = v_sc[pl.ds(start, cols), :]
            m_old = m_sc[rows, :]
            m_new = jnp.maximum(m_old, jnp.max(s, axis=-1, keepdims=True))
            alpha = jnp.exp2(m_old - m_new)
            p = jnp.exp2(s - jnp.tile(m_new, (1, cols // LANES)))
            acc_sc[rows, :] = (jnp.tile(alpha, (1, V_PAD_DIM // LANES)) * acc_sc[rows, :]
                               + jnp.dot(p.astype(BF16), v, preferred_element_type=F32))
            m_sc[rows, :] = m_new

    scores(0, 0)

    def pair(p, carry):
        scores(2 * p + 1, 1)
        accumulate(2 * p, 0, False)
        scores(2 * p + 2, 0)
        accumulate(2 * p + 1, 1, False)
        return carry

    lax.fori_loop(0, qi // 2, pair, 0)

    @pl.when(qi % 2 == 0)
    def _():
        accumulate(qi, 0, True)

    @pl.when(qi % 2 == 1)
    def _():
        scores(qi, 1)
        accumulate(qi - 1, 0, False)
        accumulate(qi, 1, True)

    o_ref[0] = (acc_sc[:, 0:V_HEAD_DIM] / acc_sc[:, V_HEAD_DIM:2 * V_HEAD_DIM]).astype(o_ref.dtype)


def _attention(q, k, v):
    b, h, s, _ = q.shape
    t = min(ATTN_TILE, s)
    assert V_HEAD_DIM == LANES and V_PAD_DIM == 2 * V_HEAD_DIM
    kern = functools.partial(_attn_kernel, tile=t, chunk=min(ATTN_CHUNK, t))
    return pl.pallas_call(
        kern,
        grid=(b, h, s // t),
        in_specs=[pl.BlockSpec((1, 1, t, QK_PAD_DIM), lambda i, j, n: (i, j, n, 0)),
                  pl.BlockSpec((1, 1, s, QK_PAD_DIM), lambda i, j, n: (i, j, 0, 0)),
                  pl.BlockSpec((1, 1, s, V_HEAD_DIM), lambda i, j, n: (i, j, 0, 0))],
        out_specs=pl.BlockSpec((1, t, V_HEAD_DIM), lambda i, j, n: (i, n, j)),
        out_shape=jax.ShapeDtypeStruct((b, s, h * V_HEAD_DIM), BF16),
        scratch_shapes=[pltpu.VMEM((t, LANES), F32), pltpu.VMEM((t, V_PAD_DIM), F32),
                        pltpu.VMEM((2, t, t), F32), pltpu.VMEM((s, V_PAD_DIM), BF16)],
        compiler_params=_cparams("parallel", "parallel", "arbitrary"),
        name="attention",
    )(q, k, v)


def _lru_kernel(xl_ref, gl_ref, pos_ref, cw_ref, cb_ref, wax_ref, ba_ref, bx_ref, lam_ref, gn_ref,
                o_ref, ext_sc, a_sc, b_sc, h_sc, carry_sc):
    t, c = a_sc.shape
    head_w = c // LRU_HEADS

    @pl.when(pl.program_id(1) == 0)
    def _():
        ext_sc[0:SUBLANES, :] = jnp.zeros((SUBLANES, c), F32)
        carry_sc[...] = jnp.zeros(carry_sc.shape, F32)

    x = xl_ref[0]
    ext_sc[SUBLANES:SUBLANES + t, :] = x
    xc = cb_ref[...] + x * cw_ref[CONV_WIDTH - 1:CONV_WIDTH, :]
    for back in range(1, CONV_WIDTH):
        tap = CONV_WIDTH - 1 - back
        xc = xc + ext_sc[SUBLANES - back:SUBLANES - back + t, :] * cw_ref[tap:tap + 1, :]
    ext_sc[0:SUBLANES, :] = x[t - SUBLANES:t, :]

    reset = _column((pos_ref[0] == 0).astype(F32)) > 0.5
    lam = lam_ref[...]
    neg_sp = -LRU_C * (jnp.maximum(-lam, 0.0) + jnp.log1p(jnp.exp(-jnp.abs(lam))))
    for hd in range(LRU_HEADS):
        sl = slice(hd * head_w, (hd + 1) * head_w)
        xh = xc[:, sl]
        gates = jnp.dot(xh.astype(BF16), wax_ref[hd], preferred_element_type=F32)
        r = _sigmoid(gates[:, :head_w] + ba_ref[:, sl])
        i = _sigmoid(gates[:, head_w:] + bx_ref[:, sl])
        a = jnp.exp(r * neg_sp[:, sl])
        gap = jnp.maximum(1.0 - a * a, 0.0)
        mult = jnp.where(gap > 0.0, gap * lax.rsqrt(gap), 0.0)
        a = jnp.where(reset, 0.0, a)
        mult = jnp.where(reset, 1.0, mult)
        a_sc[:, sl] = a
        b_sc[:, sl] = mult * (i * xh)

    srow = lax.broadcasted_iota(jnp.int32, (SUBLANES, c), 0)

    def group(g, h_prev):
        start = pl.multiple_of(g * SUBLANES, SUBLANES)
        a = a_sc[pl.ds(start, SUBLANES), :]
        bb = b_sc[pl.ds(start, SUBLANES), :]
        sh = 1
        while sh < SUBLANES:
            keep = srow >= sh
            a_prev = jnp.where(keep, pltpu.roll(a, sh, 0), 1.0)
            b_prev = jnp.where(keep, pltpu.roll(bb, sh, 0), 0.0)
            bb = bb + a * b_prev
            a = a * a_prev
            sh *= 2
        hh = bb + a * h_prev
        h_sc[pl.ds(start, SUBLANES), :] = hh
        return jnp.broadcast_to(hh[SUBLANES - 1:SUBLANES, :], (SUBLANES, c))

    carry_sc[...] = lax.fori_loop(0, t // SUBLANES, group, carry_sc[...], unroll=4)

    y = h_sc[...] * gl_ref[0].astype(F32)
    o_ref[0] = (y * _rms(y, c) * gn_ref[...]).astype(o_ref.dtype)


def _rg_lru(x_lru, gelu_g, pos_row, conv_w, conv_b, w_ax, b_a, b_x, lam, out_norm):
    b, s, c = x_lru.shape
    t = min(LRU_TILE, s)
    full = lambda a: pl.BlockSpec(a.shape, lambda i, j: (0,) * a.ndim)
    return pl.pallas_call(
        _lru_kernel,
        grid=(b, s // t),
        in_specs=[pl.BlockSpec((1, t, c), lambda i, j: (i, j, 0)),
                  pl.BlockSpec((1, t, c), lambda i, j: (i, j, 0)),
                  pl.BlockSpec((1, 1, t), lambda i, j: (i, 0, j)),
                  full(conv_w), full(conv_b), full(w_ax), full(b_a), full(b_x), full(lam),
                  full(out_norm)],
        out_specs=pl.BlockSpec((1, t, c), lambda i, j: (i, j, 0)),
        out_shape=jax.ShapeDtypeStruct((b, s, c), BF16),
        scratch_shapes=[pltpu.VMEM((t + SUBLANES, c), F32), pltpu.VMEM((t, c), F32),
                        pltpu.VMEM((t, c), F32), pltpu.VMEM((t, c), F32),
                        pltpu.VMEM((SUBLANES, c), F32)],
        compiler_params=_cparams("parallel", "arbitrary"),
        name="rg_lru",
    )(x_lru, gelu_g, pos_row, conv_w, conv_b, w_ax, b_a, b_x, lam, out_norm)


def _route_kernel(x_ref, at_ref, lr_ref, mod_ref, an_ref, wa_ref, wl_ref, g2_ref, wr_ref, br_ref,
                  x1_ref, ids_ref, rank_ref, wts_ref, cnt_ref, carry_sc,
                  *, n_experts, group_row0, expert_row0):
    @pl.when(pl.program_id(0) == 0)
    def _():
        carry_sc[...] = jnp.zeros(carry_sc.shape, F32)

    x = x_ref[0]
    t, d = x.shape
    gate1 = mod_ref[0, 2:3, :]
    shift2 = mod_ref[0, 3:4, :]
    gain2 = g2_ref[...] * (1.0 + mod_ref[0, 4:5, :])
    at = at_ref[0].astype(F32)
    at_n = at * _rms(at, at.shape[-1]) * an_ref[...]
    mix = jnp.dot(at_n.astype(BF16), wa_ref[...], preferred_element_type=F32)
    mix = mix + jnp.dot(lr_ref[0], wl_ref[...], preferred_element_type=F32)
    x1 = x + gate1 * mix
    x1_ref[0] = x1
    h2 = x1 * _rms(x1, d) * gain2 + shift2

    logits = lax.dot_general(wr_ref[...], h2.astype(BF16), (((1,), (1,)), ((), ())),
                             preferred_element_type=F32) + br_ref[...]
    per_group = n_experts // N_GROUPS
    lg = logits[group_row0:group_row0 + N_GROUPS, :]
    grow = lax.broadcasted_iota(jnp.int32, lg.shape, 0)
    gmax = jnp.max(lg, axis=0, keepdims=True)
    g_idx = jnp.min(jnp.where(lg == gmax, grow, N_GROUPS), axis=0, keepdims=True)
    g_p = 1.0 / jnp.sum(jnp.exp(lg - gmax), axis=0, keepdims=True)
    sel = jnp.zeros((per_group, t), F32)
    for g in range(N_GROUPS):
        r0 = expert_row0 + g * per_group
        sel = sel + jnp.where(g_idx == g, logits[r0:r0 + per_group, :], 0.0)
    erow = lax.broadcasted_iota(jnp.int32, sel.shape, 0)
    m1 = jnp.max(sel, axis=0, keepdims=True)
    i1 = jnp.min(jnp.where(sel == m1, erow, per_group), axis=0, keepdims=True)
    sel2 = jnp.where(erow == i1, -jnp.inf, sel)
    m2 = jnp.max(sel2, axis=0, keepdims=True)
    i2 = jnp.min(jnp.where(sel2 == m2, erow, per_group), axis=0, keepdims=True)
    e21 = jnp.exp(m2 - m1)
    w1 = g_p / (1.0 + e21)
    w2 = g_p * e21 / (1.0 + e21)
    e1 = g_idx * per_group + i1
    e2 = g_idx * per_group + i2
    ids_ref[0:1, :] = e1
    ids_ref[1:2, :] = e2
    wts_ref[0:1, :] = w1
    wts_ref[1:2, :] = w2

    xrow = lax.broadcasted_iota(jnp.int32, (n_experts, t), 0)
    o1 = (xrow == e1).astype(F32)
    o2 = (xrow == e2).astype(F32)
    both = o1 + o2
    tr = lax.broadcasted_iota(jnp.int32, (t, t), 0)
    tc = lax.broadcasted_iota(jnp.int32, (t, t), 1)
    upper = jnp.where(tr < tc, 1.0, 0.0).astype(BF16)
    before = jnp.dot(both.astype(BF16), upper, preferred_element_type=F32) + carry_sc[...]
    rank_ref[0:1, :] = jnp.sum(o1 * before, axis=0, keepdims=True).astype(jnp.int32)
    rank_ref[1:2, :] = jnp.sum(o2 * before, axis=0, keepdims=True).astype(jnp.int32)
    carry_sc[...] = carry_sc[...] + jnp.sum(both, axis=1, keepdims=True)
    cnt_ref[...] = jnp.broadcast_to(carry_sc[...], cnt_ref.shape)


def _out_route(x, attn, lru_n, mod, attn_norm, w_out_a, w_out_l, norm2_g, w_r_t, b_r_col,
               *, n_experts, group_row0, expert_row0):
    b, s, d = x.shape
    t = min(ROUTE_TILE, s)
    nt = s // t
    n = b * s
    full = lambda a: pl.BlockSpec(a.shape, lambda i: (0,) * a.ndim)
    tok = lambda w: pl.BlockSpec((1, t, w), lambda i: (i // nt, i % nt, 0))
    kern = functools.partial(_route_kernel, n_experts=n_experts, group_row0=group_row0,
                             expert_row0=expert_row0)
    return pl.pallas_call(
        kern,
        grid=(b * nt,),
        in_specs=[tok(d), tok(attn.shape[-1]), tok(lru_n.shape[-1]),
                  pl.BlockSpec((1,) + mod.shape[1:], lambda i: (i // nt, 0, 0)),
                  full(attn_norm), full(w_out_a), full(w_out_l), full(norm2_g), full(w_r_t),
                  full(b_r_col)],
        out_specs=[tok(d),
                   pl.BlockSpec((TOP_K, t), lambda i: (0, i)),
                   pl.BlockSpec((TOP_K, t), lambda i: (0, i)),
                   pl.BlockSpec((TOP_K, t), lambda i: (0, i)),
                   pl.BlockSpec((n_experts, LANES), lambda i: (0, 0))],
        out_shape=[jax.ShapeDtypeStruct((b, s, d), F32),
                   jax.ShapeDtypeStruct((TOP_K, n), jnp.int32),
                   jax.ShapeDtypeStruct((TOP_K, n), jnp.int32),
                   jax.ShapeDtypeStruct((TOP_K, n), F32),
                   jax.ShapeDtypeStruct((n_experts, LANES), F32)],
        scratch_shapes=[pltpu.VMEM((n_experts, 1), F32)],
        compiler_params=_cparams("arbitrary"),
        name="out_route",
    )(x, attn, lru_n, mod, attn_norm, w_out_a, w_out_l, norm2_g, w_r_t, b_r_col)


def _row_copy(src, src_row, dst, dst_row, sem):
    return pltpu.make_async_copy(src.at[pl.ds(src_row, 1), :], dst.at[pl.ds(dst_row, 1), :], sem)


def _dispatch_kernel(cnt_ref, start_ref, nb_ref, dest_ref, x1_ref, mod_ref, g2_ref, xs_ref, h2_sc,
                     zero_sc, sem, *, tile, block, n_experts):
    i = pl.program_id(0)

    x1 = x1_ref[0]
    gain2 = g2_ref[...] * (1.0 + mod_ref[0, 4:5, :])
    h2_sc[...] = x1 * _rms(x1, x1.shape[-1]) * gain2 + mod_ref[0, 3:4, :]

    @pl.when(i == 0)
    def _():
        zero_sc[...] = jnp.zeros(zero_sc.shape, zero_sc.dtype)

        def tail_copy(j):
            return pltpu.make_async_copy(zero_sc, xs_ref.at[pl.ds(j * block, block), :], sem)

        def tail_issue(j, carry):
            tail_copy(j).start()
            return carry

        def tail_drain(j, carry):
            tail_copy(j).wait()
            return carry

        n_blocks = xs_ref.shape[0] // block
        lax.fori_loop(nb_ref[0], n_blocks, tail_issue, 0)
        lax.fori_loop(nb_ref[0], n_blocks, tail_drain, 0)

        def per_expert(e, carry):
            cnt = cnt_ref[e]
            first = start_ref[e] + cnt
            n_pad = (block - cnt % block) % block

            def issue(r, c2):
                _row_copy(zero_sc, 0, xs_ref, first + r, sem).start()
                return c2

            lax.fori_loop(0, n_pad, issue, 0)

            def drain(r, c2):
                _row_copy(zero_sc, 0, xs_ref, first, sem).wait()
                return c2

            lax.fori_loop(0, n_pad, drain, 0)
            return carry

        lax.fori_loop(0, n_experts, per_expert, 0)

    for tk in range(tile):
        for k in range(TOP_K):
            _row_copy(h2_sc, tk, xs_ref, dest_ref[k, tk], sem).start(priority=k % 2)

    def drain(tk, carry):
        for _ in range(TOP_K):
            _row_copy(h2_sc, 0, xs_ref, 0, sem).wait()
        return carry

    lax.fori_loop(0, tile, drain, 0, unroll=8)


def _dispatch(counts, starts, n_blocks_used, dest, x1, mod, norm2_g, cap, *, block, n_experts):
    b, s, w = x1.shape
    t = min(DISPATCH_TILE, s)
    nt = s // t
    kern = functools.partial(_dispatch_kernel, tile=t, block=block, n_experts=n_experts)
    smem_tile = pl.BlockSpec((TOP_K, t), lambda i, c, st, nb: (0, i), memory_space=pltpu.SMEM)
    return pl.pallas_call(
        kern,
        grid_spec=pltpu.PrefetchScalarGridSpec(
            num_scalar_prefetch=3,
            grid=(b * nt,),
            in_specs=[smem_tile,
                      pl.BlockSpec((1, t, w), lambda i, c, st, nb: (i // nt, i % nt, 0)),
                      pl.BlockSpec((1,) + mod.shape[1:], lambda i, c, st, nb: (i // nt, 0, 0)),
                      pl.BlockSpec(norm2_g.shape, lambda i, c, st, nb: (0, 0))],
            out_specs=pl.BlockSpec(memory_space=pl.ANY),
            scratch_shapes=[pltpu.VMEM((t, w), F32), pltpu.VMEM((block, w), F32),
                            pltpu.SemaphoreType.DMA(())]),
        out_shape=jax.ShapeDtypeStruct((cap, w), F32),
        compiler_params=_cparams("arbitrary"),
        name="dispatch",
    )(counts, starts, n_blocks_used, dest, x1, mod, norm2_g)


def _expert_kernel(be_ref, nb_ref, xs_ref, wg_ref, wu_ref, wd_ref, ys_ref, wgu_sc, wdn_sc, *, d_ff):
    j = pl.program_id(0)
    live = jnp.minimum(j, nb_ref[0] - 1)

    @pl.when((j == 0) | (be_ref[live] != be_ref[jnp.maximum(live - 1, 0)]))
    def _():
        wgu_sc[:, 0:d_ff] = wg_ref[0].astype(BF16)
        wgu_sc[:, d_ff:2 * d_ff] = wu_ref[0].astype(BF16)
        wdn_sc[...] = wd_ref[0].astype(BF16)

    @pl.when(j < nb_ref[0])
    def _():
        xb = xs_ref[...].astype(BF16)
        gu = jnp.dot(xb, wgu_sc[...], preferred_element_type=F32)
        gt = gu[:, :d_ff]
        hid = gt * jax.nn.sigmoid(gt) * gu[:, d_ff:]
        ys_ref[...] = jnp.dot(hid.astype(BF16), wdn_sc[...], preferred_element_type=F32)

    @pl.when(pl.program_id(0) >= nb_ref[0])
    def _():
        ys_ref[...] = jnp.zeros(ys_ref.shape, ys_ref.dtype)


def _experts(block_e, n_blocks_used, xs, w_gate, w_up, w_down, *, block):
    cap, w = xs.shape
    d = w_down.shape[2]
    d_ff = w_down.shape[1]
    kern = functools.partial(_expert_kernel, d_ff=d_ff)
    live = lambda j, nb: jnp.minimum(j, nb[0] - 1)
    return pl.pallas_call(
        kern,
        grid_spec=pltpu.PrefetchScalarGridSpec(
            num_scalar_prefetch=2,
            grid=(cap // block,),
            in_specs=[pl.BlockSpec((block, w), lambda j, be, nb: (live(j, nb), 0)),
                      pl.BlockSpec((1, d, d_ff), lambda j, be, nb: (be[live(j, nb)], 0, 0)),
                      pl.BlockSpec((1, d, d_ff), lambda j, be, nb: (be[live(j, nb)], 0, 0)),
                      pl.BlockSpec((1, d_ff, d), lambda j, be, nb: (be[live(j, nb)], 0, 0))],
            out_specs=pl.BlockSpec((block, w), lambda j, be, nb: (j, 0)),
            scratch_shapes=[pltpu.VMEM((d, 2 * d_ff), BF16), pltpu.VMEM((d_ff, d), BF16)]),
        out_shape=jax.ShapeDtypeStruct((cap, w), F32),
        compiler_params=_cparams("arbitrary"),
        name="experts",
    )(block_e, n_blocks_used, xs, w_gate, w_up, w_down)


def _combine_kernel(dest_ref, dest_nx_ref, wts_ref, x1_ref, mod_ref, ys_ref, o_ref, buf, sem, *,
                    tile):
    i = pl.program_id(0)
    slot = i % 2

    def gather(d_ref, sl):
        for tk in range(tile):
            for k in range(TOP_K):
                _row_copy(ys_ref, d_ref[k, tk], buf.at[sl, k], tk, sem.at[sl]).start(priority=k % 2)

    @pl.when(i == 0)
    def _():
        gather(dest_ref, slot)

    @pl.when(i + 1 < pl.num_programs(0))
    def _():
        gather(dest_nx_ref, 1 - slot)

    for k in range(TOP_K):
        pltpu.make_async_copy(ys_ref.at[pl.ds(0, tile), :], buf.at[slot, k], sem.at[slot]).wait()

    d = o_ref.shape[-1]
    w0 = jnp.tile(_column(wts_ref[0:1, :]), (1, d // LANES))
    w1 = jnp.tile(_column(wts_ref[1:2, :]), (1, d // LANES))
    gate2 = mod_ref[0, 5:6, :]
    o_ref[0] = x1_ref[0] + gate2 * (w0 * buf[slot, 0] + w1 * buf[slot, 1])


def _combine(dest, wts, x1, mod, ys):
    b, s, d = x1.shape
    w = ys.shape[1]
    t = min(COMBINE_TILE, s)
    nt = s // t
    last = b * nt - 1
    kern = functools.partial(_combine_kernel, tile=t)
    smem_now = pl.BlockSpec((TOP_K, t), lambda i: (0, i), memory_space=pltpu.SMEM)
    smem_next = pl.BlockSpec((TOP_K, t), lambda i: (0, jnp.minimum(i + 1, last)),
                             memory_space=pltpu.SMEM)
    return pl.pallas_call(
        kern,
        grid=(b * nt,),
        in_specs=[smem_now, smem_next,
                  pl.BlockSpec((TOP_K, t), lambda i: (0, i)),
                  pl.BlockSpec((1, t, d), lambda i: (i // nt, i % nt, 0)),
                  pl.BlockSpec((1,) + mod.shape[1:], lambda i: (i // nt, 0, 0)),
                  pl.BlockSpec(memory_space=pl.ANY)],
        out_specs=pl.BlockSpec((1, t, d), lambda i: (i // nt, i % nt, 0)),
        out_shape=jax.ShapeDtypeStruct((b, s, d), F32),
        scratch_shapes=[pltpu.VMEM((2, TOP_K, t, w), F32), pltpu.SemaphoreType.DMA((2,))],
        compiler_params=_cparams("arbitrary"),
        name="combine",
    )(dest, dest, wts, x1, mod, ys)


def _layer(x, c, pos_row, w_ada, b_ada, norm1_g, w_in, q_a_norm, w_q_b, kv_a_norm, w_kv_b, q_norm,
           k_norm, conv_w, conv_b, lru_wa, lru_ba, lru_wx, lru_bx, lru_lambda, attn_out_norm,
           lru_out_norm, w_out, norm2_g, w_rg, b_rg, w_re, b_re, w_gate, w_up, w_down):
    b, s, d = x.shape
    n = b * s
    q_lora = q_a_norm.shape[0]
    kv_lora = kv_a_norm.shape[0]
    lru_w = lru_lambda.shape[0]
    attn_w = ATTN_HEADS * V_HEAD_DIM
    n_experts = w_gate.shape[0]
    row = lambda v: v.reshape(1, -1)

    mod = _adaln(c, w_ada, b_ada).reshape(b, 6, d)

    o1 = q_lora
    o2 = o1 + kv_lora
    o3 = o2 + QK_ROPE_DIM
    o4 = o3 + lru_w
    w_in_p = jnp.concatenate(
        [w_in[:, :o2], w_in[:, o3:o4], w_in[:, o4:], w_in[:, o2:o3],
         jnp.zeros((d, LANES - QK_ROPE_DIM), w_in.dtype)], axis=1).astype(BF16)
    pad_h = QK_PAD_DIM - QK_HEAD_DIM
    w_q_p = jnp.pad(w_q_b.reshape(q_lora, ATTN_HEADS, QK_HEAD_DIM),
                    ((0, 0), (0, 0), (0, pad_h))).reshape(q_lora, ATTN_HEADS * QK_PAD_DIM).astype(BF16)
    q_norm_p = row(jnp.pad(q_norm, (0, pad_h)))
    k_norm_nope = row(k_norm[:QK_NOPE_DIM])
    k_norm_rope_p = row(jnp.pad(k_norm[QK_NOPE_DIM:], (0, LANES - QK_ROPE_DIM)))
    inv = 1.0 / (ROPE_THETA ** (np.arange(0, QK_ROPE_DIM, 2, dtype=np.float32) / QK_ROPE_DIM))
    inv_col = jnp.asarray(inv.reshape(-1, 1))

    q, k, v, x_lru, gelu_g = _in_proj(
        x, pos_row, mod, row(norm1_g), w_in_p, row(q_a_norm), w_q_p, row(kv_a_norm),
        w_kv_b.astype(BF16), q_norm_p, k_norm_nope, k_norm_rope_p, inv_col,
        q_lora=q_lora, kv_lora=kv_lora, lru_w=lru_w)

    attn = _attention(q, k, v)

    w_ax = jnp.concatenate([lru_wa, lru_wx], axis=-1).astype(BF16)
    lru_n = _rg_lru(x_lru, gelu_g, pos_row, conv_w, row(conv_b), w_ax, row(lru_ba), row(lru_bx),
                    row(lru_lambda), row(lru_out_norm))

    group_row0, expert_row0 = 0, SUBLANES
    w_r_t = jnp.concatenate([w_rg.T, jnp.zeros((expert_row0 - N_GROUPS, d), w_rg.dtype), w_re.T],
                            axis=0).astype(BF16)
    b_r_col = jnp.concatenate([b_rg, jnp.zeros((expert_row0 - N_GROUPS,), b_rg.dtype),
                               b_re]).reshape(-1, 1)
    x1, ids, ranks, wts, cnt = _out_route(
        x, attn, lru_n, mod, row(attn_out_norm), w_out[:attn_w].astype(BF16),
        w_out[attn_w:].astype(BF16), row(norm2_g), w_r_t, b_r_col,
        n_experts=n_experts, group_row0=group_row0, expert_row0=expert_row0)

    blk = EXPERT_BLOCK
    counts = cnt[:, 0].astype(jnp.int32)
    padded = (counts + blk - 1) // blk * blk
    pad_ends = jnp.cumsum(padded)
    starts = (pad_ends - padded).astype(jnp.int32)
    ids_dense = ids.reshape(-1, LANES)
    seg_start = jnp.zeros_like(ids_dense)
    for e in range(n_experts):
        seg_start = seg_start + jnp.where(ids_dense == e, starts[e], 0)
    dest = seg_start.reshape(ids.shape) + ranks
    cap = (n * TOP_K + n_experts * (blk - 1) + blk - 1) // blk * blk
    n_blocks = cap // blk
    block_first_row = jnp.arange(n_blocks, dtype=jnp.int32) * blk
    block_e = jnp.minimum(jnp.sum(pad_ends[None, :] <= block_first_row[:, None], axis=1),
                          n_experts - 1).astype(jnp.int32)
    n_used = (pad_ends[-1:] // blk).astype(jnp.int32)

    xs = _dispatch(counts, starts, n_used, dest, x1, mod, row(norm2_g), cap,
                   block=blk, n_experts=n_experts)
    ys = _experts(block_e, n_used, xs, w_gate, w_up, w_down, block=blk)
    return _combine(dest, wts, x1, mod, ys)


def kernel(x, c, positions, w_ada, b_ada, norm1_g, w_in, q_a_norm, w_q_b, kv_a_norm, w_kv_b, q_norm, k_norm, conv_w, conv_b, lru_wa, lru_ba, lru_wx, lru_bx, lru_lambda, attn_out_norm, lru_out_norm, w_out, norm2_g, w_router_group, b_router_group, w_router_expert, b_router_expert, w_gate, w_up, w_down):
    b, s, _ = x.shape
    pos_row = positions.reshape(b, 1, s)
    for l in range(w_ada.shape[0]):
        x = _layer(x, c, pos_row, w_ada[l], b_ada[l], norm1_g[l], w_in[l], q_a_norm[l], w_q_b[l],
                   kv_a_norm[l], w_kv_b[l], q_norm[l], k_norm[l], conv_w[l], conv_b[l], lru_wa[l],
                   lru_ba[l], lru_wx[l], lru_bx[l], lru_lambda[l], attn_out_norm[l], lru_out_norm[l],
                   w_out[l], norm2_g[l], w_router_group[l], b_router_group[l], w_router_expert[l],
                   b_router_expert[l], w_gate[l], w_up[l], w_down[l])
    return x
```

```python
import functools
import math

import numpy as np
import jax
import jax.numpy as jnp
from jax import lax
from jax.experimental import pallas as pl
from jax.experimental.pallas import tpu as pltpu

F32 = jnp.float32
BF16 = jnp.bfloat16

ATTN_HEADS = 4
QK_NOPE_DIM = 128
QK_ROPE_DIM = 64
QK_HEAD_DIM = QK_NOPE_DIM + QK_ROPE_DIM
V_HEAD_DIM = 128
ROPE_THETA = 10000.0
LRU_HEADS = 4
CONV_WIDTH = 4
LRU_C = 8.0
N_GROUPS = 4
TOP_K = 2
EPS = 1e-6

LANES = 128
SUBLANES = 8
QK_PAD_DIM = 256
V_PAD_DIM = 256
VMEM_LIMIT = 56 * 1024 * 1024

PROJ_TILE = 512
PROJ_CHUNK = 128
ATTN_TILE = 1024
ATTN_CHUNK = 256
LRU_TILE = 512
ROUTE_TILE = 512
DISPATCH_TILE = 512
COMBINE_TILE = 512
EXPERT_BLOCK = 512


def _cparams(*sem):
    return pltpu.CompilerParams(dimension_semantics=sem, vmem_limit_bytes=VMEM_LIMIT)


def _rms(x, n):
    return lax.rsqrt(jnp.sum(x * x, axis=-1, keepdims=True) * (1.0 / n) + EPS)


def _sigmoid(x):
    return 0.5 * jnp.tanh(0.5 * x) + 0.5


def _column(row, width=LANES):
    t = row.shape[-1]
    return jnp.broadcast_to(row, (width, t)).T


def _adaln_kernel(c_ref, w_ref, b_ref, o_ref):
    c = c_ref[...]
    sc = c * jax.nn.sigmoid(c)
    o_ref[...] = jnp.dot(sc.astype(BF16), w_ref[...].astype(BF16),
                         preferred_element_type=F32) + b_ref[...]


def _adaln(c, w_ada, b_ada):
    b, d = c.shape
    n_out = w_ada.shape[1]
    return pl.pallas_call(
        _adaln_kernel,
        grid=(n_out // d,),
        in_specs=[pl.BlockSpec((b, d), lambda j: (0, 0)),
                  pl.BlockSpec((d, d), lambda j: (0, j)),
                  pl.BlockSpec((1, d), lambda j: (0, j))],
        out_specs=pl.BlockSpec((b, d), lambda j: (0, j)),
        out_shape=jax.ShapeDtypeStruct((b, n_out), F32),
        compiler_params=_cparams("parallel"),
        name="adaln",
    )(c, w_ada, b_ada.reshape(1, n_out))


def _rope_tables(pos_row, inv_col):
    half = QK_ROPE_DIM // 2
    ang = inv_col * pos_row
    cos = jnp.cos(ang)
    sin = jnp.sin(ang)
    t = ang.shape[-1]
    zeros = lambda n: jnp.zeros((n, t), F32)
    c = jnp.concatenate([cos, cos, zeros(LANES - QK_ROPE_DIM)], axis=0)
    sa = jnp.concatenate([-sin, zeros(LANES - half)], axis=0)
    sb = jnp.concatenate([zeros(half), sin, zeros(LANES - QK_ROPE_DIM)], axis=0)
    return c.T, sa.T, sb.T


def _rope(x, c, sa, sb):
    return x * c + pltpu.roll(x, LANES - QK_ROPE_DIM // 2, 1) * sa + pltpu.roll(x, QK_ROPE_DIM // 2, 1) * sb


def _in_proj_kernel(x_ref, pos_ref, mod_ref, g1_ref, w_in_ref, qa_ref, wq_ref, kva_ref, wkv_ref,
                    qn_ref, kn_nope_ref, kn_rope_ref, inv_ref,
                    q_ref, k_ref, v_ref, xl_ref, gl_ref, *, q_lora, kv_lora, lru_w, chunk):
    t, d = x_ref.shape[1:]
    shift = mod_ref[0, 0:1, :]
    gain = g1_ref[...] * (1.0 + mod_ref[0, 1:2, :])
    c_all, sa_all, sb_all = _rope_tables(pos_ref[0].astype(F32), inv_ref[...])
    sm_scale = QK_HEAD_DIM ** -0.5 * math.log2(math.e)

    for r0 in range(0, t, chunk):
        rows = slice(r0, r0 + chunk)
        x = x_ref[0, rows, :]
        h = x * _rms(x, d) * gain + shift
        proj = jnp.dot(h.astype(BF16), w_in_ref[...], preferred_element_type=F32)
        o = 0
        q_lat = proj[:, o:o + q_lora]; o += q_lora
        kv_lat = proj[:, o:o + kv_lora]; o += kv_lora
        x_lru = proj[:, o:o + lru_w]; o += lru_w
        g_lru = proj[:, o:o + lru_w]; o += lru_w
        k_pe = proj[:, o:o + LANES]

        xl_ref[0, rows, :] = x_lru
        gl_ref[0, rows, :] = jax.nn.gelu(g_lru).astype(BF16)

        c, sa, sb = c_all[rows], sa_all[rows], sb_all[rows]
        qn = q_lat * _rms(q_lat, q_lora) * qa_ref[...]
        q_full = jnp.dot(qn.astype(BF16), wq_ref[...], preferred_element_type=F32)
        kvn = kv_lat * _rms(kv_lat, kv_lora) * kva_ref[...]
        kv_full = jnp.dot(kvn.astype(BF16), wkv_ref[...], preferred_element_type=F32)

        pe_ss = jnp.sum(k_pe * k_pe, axis=-1, keepdims=True)
        pe_rot = _rope(k_pe * kn_rope_ref[...], c, sa, sb)
        for hd in range(ATTN_HEADS):
            qh = q_full[:, hd * QK_PAD_DIM:(hd + 1) * QK_PAD_DIM]
            qh = qh * (_rms(qh, QK_HEAD_DIM) * sm_scale) * qn_ref[...]
            q_ref[0, hd, rows, 0:QK_NOPE_DIM] = qh[:, 0:QK_NOPE_DIM].astype(BF16)
            q_ref[0, hd, rows, QK_NOPE_DIM:QK_PAD_DIM] = _rope(qh[:, QK_NOPE_DIM:QK_PAD_DIM],
                                                               c, sa, sb).astype(BF16)
            base = hd * (QK_NOPE_DIM + V_HEAD_DIM)
            kn = kv_full[:, base:base + QK_NOPE_DIM]
            vv = kv_full[:, base + QK_NOPE_DIM:base + QK_NOPE_DIM + V_HEAD_DIM]
            r = lax.rsqrt((jnp.sum(kn * kn, axis=-1, keepdims=True) + pe_ss) * (1.0 / QK_HEAD_DIM) + EPS)
            k_ref[0, hd, rows, 0:QK_NOPE_DIM] = (kn * r * kn_nope_ref[...]).astype(BF16)
            k_ref[0, hd, rows, QK_NOPE_DIM:QK_PAD_DIM] = (pe_rot * r).astype(BF16)
            v_ref[0, hd, rows, :] = vv.astype(BF16)


def _in_proj(x, pos_row, mod, norm1_g, w_in_p, q_a_norm, w_q_p, kv_a_norm, w_kv, q_norm_p,
             k_norm_nope, k_norm_rope_p, inv_col, *, q_lora, kv_lora, lru_w):
    b, s, d = x.shape
    t = min(PROJ_TILE, s)
    full = lambda a: pl.BlockSpec(a.shape, lambda i, j: (0,) * a.ndim)
    kern = functools.partial(_in_proj_kernel, q_lora=q_lora, kv_lora=kv_lora, lru_w=lru_w,
                             chunk=min(PROJ_CHUNK, t))
    head_out = lambda w: pl.BlockSpec((1, ATTN_HEADS, t, w), lambda i, j: (i, 0, j, 0))
    return pl.pallas_call(
        kern,
        grid=(b, s // t),
        in_specs=[pl.BlockSpec((1, t, d), lambda i, j: (i, j, 0)),
                  pl.BlockSpec((1, 1, t), lambda i, j: (i, 0, j)),
                  pl.BlockSpec((1,) + mod.shape[1:], lambda i, j: (i, 0, 0)),
                  full(norm1_g), full(w_in_p), full(q_a_norm), full(w_q_p), full(kv_a_norm),
                  full(w_kv), full(q_norm_p), full(k_norm_nope), full(k_norm_rope_p), full(inv_col)],
        out_specs=[head_out(QK_PAD_DIM), head_out(QK_PAD_DIM), head_out(V_HEAD_DIM),
                   pl.BlockSpec((1, t, lru_w), lambda i, j: (i, j, 0)),
                   pl.BlockSpec((1, t, lru_w), lambda i, j: (i, j, 0))],
        out_shape=[jax.ShapeDtypeStruct((b, ATTN_HEADS, s, QK_PAD_DIM), BF16),
                   jax.ShapeDtypeStruct((b, ATTN_HEADS, s, QK_PAD_DIM), BF16),
                   jax.ShapeDtypeStruct((b, ATTN_HEADS, s, V_HEAD_DIM), BF16),
                   jax.ShapeDtypeStruct((b, s, lru_w), F32),
                   jax.ShapeDtypeStruct((b, s, lru_w), BF16)],
        compiler_params=_cparams("parallel", "parallel"),
        name="in_proj",
    )(x, pos_row, mod, norm1_g, w_in_p, q_a_norm, w_q_p, kv_a_norm, w_kv, q_norm_p,
      k_norm_nope, k_norm_rope_p, inv_col)


def _attn_kernel(q_ref, k_ref, v_ref, o_ref, m_sc, acc_sc, s_sc, v_sc, *, tile, chunk):
    qi = pl.program_id(2)

    @pl.when(qi == 0)
    def _():
        v_sc[:, 0:V_HEAD_DIM] = v_ref[0, 0]
        v_sc[:, V_HEAD_DIM:V_PAD_DIM] = jnp.ones((v_sc.shape[0], V_PAD_DIM - V_HEAD_DIM), BF16)

    m_sc[...] = jnp.full(m_sc.shape, -jnp.inf, F32)
    acc_sc[...] = jnp.zeros(acc_sc.shape, F32)

    def scores(j, slot):
        start = pl.multiple_of(j * tile, tile)
        k = k_ref[0, 0, pl.ds(start, tile), :]
        s_sc[slot] = lax.dot_general(q_ref[0, 0], k, (((1,), (1,)), ((), ())),
                                     preferred_element_type=F32)

    def accumulate(j, slot, masked):
        start = pl.multiple_of(j * tile, tile)
        for r in range(tile // chunk):
            rows = slice(r * chunk, (r + 1) * chunk)
            cols = -(-(r + 1) * chunk // LANES) * LANES if masked else tile
            s = s_sc[slot, rows, 0:cols]
            if masked:
                row = lax.broadcasted_iota(jnp.int32, s.shape, 0) + r * chunk
                col = lax.broadcasted_iota(jnp.int32, s.shape, 1)
                s = jnp.where(col <= row, s, -jnp.inf)
            v = v_sc[pl.ds(start, cols), :]
            m_old = m_sc[rows, :]
            m_new = jnp.maximum(m_old, jnp.max(s, axis=-1, keepdims=True))
            alpha = jnp.exp2(m_old - m_new)
            p = jnp.exp2(s - jnp.tile(m_new, (1, cols // LANES)))
            acc_sc[rows, :] = (jnp.tile(alpha, (1, V_PAD_DIM // LANES)) * acc_sc[rows, :]
                               + jnp.dot(p.astype(BF16), v, preferred_element_type=F32))
            m_sc[rows, :] = m_new

    scores(0, 0)

    def pair(p, carry):
        scores(2 * p + 1, 1)
        accumulate(2 * p, 0, False)
        scores(2 * p + 2, 0)
        accumulate(2 * p + 1, 1, False)
        return carry

    lax.fori_loop(0, qi // 2, pair, 0)

    @pl.when(qi % 2 == 0)
    def _():
        accumulate(qi, 0, True)

    @pl.when(qi % 2 == 1)
    def _():
        scores(qi, 1)
        accumulate(qi - 1, 0, False)
        accumulate(qi, 1, True)

    o_ref[0] = (acc_sc[:, 0:V_HEAD_DIM] / acc_sc[:, V_HEAD_DIM:2 * V_HEAD_DIM]).astype(o_ref.dtype)


def _attention(q, k, v):
    b, h, s, _ = q.shape
    t = min(ATTN_TILE, s)
    assert V_HEAD_DIM == LANES and V_PAD_DIM == 2 * V_HEAD_DIM
    kern = functools.partial(_attn_kernel, tile=t, chunk=min(ATTN_CHUNK, t))
    return pl.pallas_call(
        kern,
        grid=(b, h, s // t),
        in_specs=[pl.BlockSpec((1, 1, t, QK_PAD_DIM), lambda i, j, n: (i, j, n, 0)),
                  pl.BlockSpec((1, 1, s, QK_PAD_DIM), lambda i, j, n: (i, j, 0, 0)),
                  pl.BlockSpec((1, 1, s, V_HEAD_DIM), lambda i, j, n: (i, j, 0, 0))],
        out_specs=pl.BlockSpec((1, t, V_HEAD_DIM), lambda i, j, n: (i, n, j)),
        out_shape=jax.ShapeDtypeStruct((b, s, h * V_HEAD_DIM), BF16),
        scratch_shapes=[pltpu.VMEM((t, LANES), F32), pltpu.VMEM((t, V_PAD_DIM), F32),
                        pltpu.VMEM((2, t, t), F32), pltpu.VMEM((s, V_PAD_DIM), BF16)],
        compiler_params=_cparams("parallel", "parallel", "arbitrary"),
        name="attention",
    )(q, k, v)


def _lru_kernel(xl_ref, gl_ref, pos_ref, cw_ref, cb_ref, wax_ref, ba_ref, bx_ref, lam_ref, gn_ref,
                o_ref, ext_sc, a_sc, b_sc, h_sc, carry_sc):
    t, c = a_sc.shape
    head_w = c // LRU_HEADS

    @pl.when(pl.program_id(1) == 0)
    def _():
        ext_sc[0:SUBLANES, :] = jnp.zeros((SUBLANES, c), F32)
        carry_sc[...] = jnp.zeros(carry_sc.shape, F32)

    x = xl_ref[0]
    ext_sc[SUBLANES:SUBLANES + t, :] = x
    xc = cb_ref[...] + x * cw_ref[CONV_WIDTH - 1:CONV_WIDTH, :]
    for back in range(1, CONV_WIDTH):
        tap = CONV_WIDTH - 1 - back
        xc = xc + ext_sc[SUBLANES - back:SUBLANES - back + t, :] * cw_ref[tap:tap + 1, :]
    ext_sc[0:SUBLANES, :] = x[t - SUBLANES:t, :]

    reset = _column((pos_ref[0] == 0).astype(F32)) > 0.5
    lam = lam_ref[...]
    neg_sp = -LRU_C * (jnp.maximum(-lam, 0.0) + jnp.log1p(jnp.exp(-jnp.abs(lam))))
    for hd in range(LRU_HEADS):
        sl = slice(hd * head_w, (hd + 1) * head_w)
        xh = xc[:, sl]
        gates = jnp.dot(xh.astype(BF16), wax_ref[hd], preferred_element_type=F32)
        r = _sigmoid(gates[:, :head_w] + ba_ref[:, sl])
        i = _sigmoid(gates[:, head_w:] + bx_ref[:, sl])
        a = jnp.exp(r * neg_sp[:, sl])
        gap = jnp.maximum(1.0 - a * a, 0.0)
        mult = jnp.where(gap > 0.0, gap * lax.rsqrt(gap), 0.0)
        a = jnp.where(reset, 0.0, a)
        mult = jnp.where(reset, 1.0, mult)
        a_sc[:, sl] = a
        b_sc[:, sl] = mult * (i * xh)

    srow = lax.broadcasted_iota(jnp.int32, (SUBLANES, c), 0)

    def group(g, h_prev):
        start = pl.multiple_of(g * SUBLANES, SUBLANES)
        a = a_sc[pl.ds(start, SUBLANES), :]
        bb = b_sc[pl.ds(start, SUBLANES), :]
        sh = 1
        while sh < SUBLANES:
            keep = srow >= sh
            a_prev = jnp.where(keep, pltpu.roll(a, sh, 0), 1.0)
            b_prev = jnp.where(keep, pltpu.roll(bb, sh, 0), 0.0)
            bb = bb + a * b_prev
            a = a * a_prev
            sh *= 2
        hh = bb + a * h_prev
        h_sc[pl.ds(start, SUBLANES), :] = hh
        return jnp.broadcast_to(hh[SUBLANES - 1:SUBLANES, :], (SUBLANES, c))

    carry_sc[...] = lax.fori_loop(0, t // SUBLANES, group, carry_sc[...], unroll=4)

    y = h_sc[...] * gl_ref[0].astype(F32)
    o_ref[0] = (y * _rms(y, c) * gn_ref[...]).astype(o_ref.dtype)


def _rg_lru(x_lru, gelu_g, pos_row, conv_w, conv_b, w_ax, b_a, b_x, lam, out_norm):
    b, s, c = x_lru.shape
    t = min(LRU_TILE, s)
    full = lambda a: pl.BlockSpec(a.shape, lambda i, j: (0,) * a.ndim)
    return pl.pallas_call(
        _lru_kernel,
        grid=(b, s // t),
        in_specs=[pl.BlockSpec((1, t, c), lambda i, j: (i, j, 0)),
                  pl.BlockSpec((1, t, c), lambda i, j: (i, j, 0)),
                  pl.BlockSpec((1, 1, t), lambda i, j: (i, 0, j)),
                  full(conv_w), full(conv_b), full(w_ax), full(b_a), full(b_x), full(lam),
                  full(out_norm)],
        out_specs=pl.BlockSpec((1, t, c), lambda i, j: (i, j, 0)),
        out_shape=jax.ShapeDtypeStruct((b, s, c), BF16),
        scratch_shapes=[pltpu.VMEM((t + SUBLANES, c), F32), pltpu.VMEM((t, c), F32),
                        pltpu.VMEM((t, c), F32), pltpu.VMEM((t, c), F32),
                        pltpu.VMEM((SUBLANES, c), F32)],
        compiler_params=_cparams("parallel", "arbitrary"),
        name="rg_lru",
    )(x_lru, gelu_g, pos_row, conv_w, conv_b, w_ax, b_a, b_x, lam, out_norm)


def _route_kernel(x_ref, at_ref, lr_ref, mod_ref, an_ref, wa_ref, wl_ref, g2_ref, wr_ref, br_ref,
                  x1_ref, ids_ref, rank_ref, wts_ref, cnt_ref, carry_sc,
                  *, n_experts, group_row0, expert_row0):
    @pl.when(pl.program_id(0) == 0)
    def _():
        carry_sc[...] = jnp.zeros(carry_sc.shape, F32)

    x = x_ref[0]
    t, d = x.shape
    gate1 = mod_ref[0, 2:3, :]
    shift2 = mod_ref[0, 3:4, :]
    gain2 = g2_ref[...] * (1.0 + mod_ref[0, 4:5, :])
    at = at_ref[0].astype(F32)
    at_n = at * _rms(at, at.shape[-1]) * an_ref[...]
    mix = jnp.dot(at_n.astype(BF16), wa_ref[...], preferred_element_type=F32)
    mix = mix + jnp.dot(lr_ref[0], wl_ref[...], preferred_element_type=F32)
    x1 = x + gate1 * mix
    x1_ref[0] = x1
    h2 = x1 * _rms(x1, d) * gain2 + shift2

    logits = lax.dot_general(wr_ref[...], h2.astype(BF16), (((1,), (1,)), ((), ())),
                             preferred_element_type=F32) + br_ref[...]
    per_group = n_experts // N_GROUPS
    lg = logits[group_row0:group_row0 + N_GROUPS, :]
    grow = lax.broadcasted_iota(jnp.int32, lg.shape, 0)
    gmax = jnp.max(lg, axis=0, keepdims=True)
    g_idx = jnp.min(jnp.where(lg == gmax, grow, N_GROUPS), axis=0, keepdims=True)
    g_p = 1.0 / jnp.sum(jnp.exp(lg - gmax), axis=0, keepdims=True)
    sel = jnp.zeros((per_group, t), F32)
    for g in range(N_GROUPS):
        r0 = expert_row0 + g * per_group
        sel = sel + jnp.where(g_idx == g, logits[r0:r0 + per_group, :], 0.0)
    erow = lax.broadcasted_iota(jnp.int32, sel.shape, 0)
    m1 = jnp.max(sel, axis=0, keepdims=True)
    i1 = jnp.min(jnp.where(sel == m1, erow, per_group), axis=0, keepdims=True)
    sel2 = jnp.where(erow == i1, -jnp.inf, sel)
    m2 = jnp.max(sel2, axis=0, keepdims=True)
    i2 = jnp.min(jnp.where(sel2 == m2, erow, per_group), axis=0, keepdims=True)
    e21 = jnp.exp(m2 - m1)
    w1 = g_p / (1.0 + e21)
    w2 = g_p * e21 / (1.0 + e21)
    e1 = g_idx * per_group + i1
    e2 = g_idx * per_group + i2
    ids_ref[0:1, :] = e1
    ids_ref[1:2, :] = e2
    wts_ref[0:1, :] = w1
    wts_ref[1:2, :] = w2

    xrow = lax.broadcasted_iota(jnp.int32, (n_experts, t), 0)
    o1 = (xrow == e1).astype(F32)
    o2 = (xrow == e2).astype(F32)
    both = o1 + o2
    tr = lax.broadcasted_iota(jnp.int32, (t, t), 0)
    tc = lax.broadcasted_iota(jnp.int32, (t, t), 1)
    upper = jnp.where(tr < tc, 1.0, 0.0).astype(BF16)
    before = jnp.dot(both.astype(BF16), upper, preferred_element_type=F32) + carry_sc[...]
    rank_ref[0:1, :] = jnp.sum(o1 * before, axis=0, keepdims=True).astype(jnp.int32)
    rank_ref[1:2, :] = jnp.sum(o2 * before, axis=0, keepdims=True).astype(jnp.int32)
    carry_sc[...] = carry_sc[...] + jnp.sum(both, axis=1, keepdims=True)
    cnt_ref[...] = jnp.broadcast_to(carry_sc[...], cnt_ref.shape)


def _out_route(x, attn, lru_n, mod, attn_norm, w_out_a, w_out_l, norm2_g, w_r_t, b_r_col,
               *, n_experts, group_row0, expert_row0):
    b, s, d = x.shape
    t = min(ROUTE_TILE, s)
    nt = s // t
    n = b * s
    full = lambda a: pl.BlockSpec(a.shape, lambda i: (0,) * a.ndim)
    tok = lambda w: pl.BlockSpec((1, t, w), lambda i: (i // nt, i % nt, 0))
    kern = functools.partial(_route_kernel, n_experts=n_experts, group_row0=group_row0,
                             expert_row0=expert_row0)
    return pl.pallas_call(
        kern,
        grid=(b * nt,),
        in_specs=[tok(d), tok(attn.shape[-1]), tok(lru_n.shape[-1]),
                  pl.BlockSpec((1,) + mod.shape[1:], lambda i: (i // nt, 0, 0)),
                  full(attn_norm), full(w_out_a), full(w_out_l), full(norm2_g), full(w_r_t),
                  full(b_r_col)],
        out_specs=[tok(d),
                   pl.BlockSpec((TOP_K, t), lambda i: (0, i)),
                   pl.BlockSpec((TOP_K, t), lambda i: (0, i)),
                   pl.BlockSpec((TOP_K, t), lambda i: (0, i)),
                   pl.BlockSpec((n_experts, LANES), lambda i: (0, 0))],
        out_shape=[jax.ShapeDtypeStruct((b, s, d), F32),
                   jax.ShapeDtypeStruct((TOP_K, n), jnp.int32),
                   jax.ShapeDtypeStruct((TOP_K, n), jnp.int32),
                   jax.ShapeDtypeStruct((TOP_K, n), F32),
                   jax.ShapeDtypeStruct((n_experts, LANES), F32)],
        scratch_shapes=[pltpu.VMEM((n_experts, 1), F32)],
        compiler_params=_cparams("arbitrary"),
        name="out_route",
    )(x, attn, lru_n, mod, attn_norm, w_out_a, w_out_l, norm2_g, w_r_t, b_r_col)


def _row_copy(src, src_row, dst, dst_row, sem):
    return pltpu.make_async_copy(src.at[pl.ds(src_row, 1), :], dst.at[pl.ds(dst_row, 1), :], sem)


def _dispatch_kernel(cnt_ref, start_ref, nb_ref, dest_ref, x1_ref, mod_ref, g2_ref, xs_ref, h2_sc,
                     zero_sc, sem, *, tile, block, n_experts):
    i = pl.program_id(0)

    x1 = x1_ref[0]
    gain2 = g2_ref[...] * (1.0 + mod_ref[0, 4:5, :])
    h2_sc[...] = x1 * _rms(x1, x1.shape[-1]) * gain2 + mod_ref[0, 3:4, :]

    @pl.when(i == 0)
    def _():
        zero_sc[...] = jnp.zeros(zero_sc.shape, zero_sc.dtype)

        def tail_copy(j):
            return pltpu.make_async_copy(zero_sc, xs_ref.at[pl.ds(j * block, block), :], sem)

        def tail_issue(j, carry):
            tail_copy(j).start()
            return carry

        def tail_drain(j, carry):
            tail_copy(j).wait()
            return carry

        n_blocks = xs_ref.shape[0] // block
        lax.fori_loop(nb_ref[0], n_blocks, tail_issue, 0)
        lax.fori_loop(nb_ref[0], n_blocks, tail_drain, 0)

        def per_expert(e, carry):
            cnt = cnt_ref[e]
            first = start_ref[e] + cnt
            n_pad = (block - cnt % block) % block

            def issue(r, c2):
                _row_copy(zero_sc, 0, xs_ref, first + r, sem).start()
                return c2

            lax.fori_loop(0, n_pad, issue, 0)

            def drain(r, c2):
                _row_copy(zero_sc, 0, xs_ref, first, sem).wait()
                return c2

            lax.fori_loop(0, n_pad, drain, 0)
            return carry

        lax.fori_loop(0, n_experts, per_expert, 0)

    for tk in range(tile):
        for k in range(TOP_K):
            _row_copy(h2_sc, tk, xs_ref, dest_ref[k, tk], sem).start(priority=k % 2)

    def drain(tk, carry):
        for _ in range(TOP_K):
            _row_copy(h2_sc, 0, xs_ref, 0, sem).wait()
        return carry

    lax.fori_loop(0, tile, drain, 0, unroll=8)


def _dispatch(counts, starts, n_blocks_used, dest, x1, mod, norm2_g, cap, *, block, n_experts):
    b, s, w = x1.shape
    t = min(DISPATCH_TILE, s)
    nt = s // t
    kern = functools.partial(_dispatch_kernel, tile=t, block=block, n_experts=n_experts)
    smem_tile = pl.BlockSpec((TOP_K, t), lambda i, c, st, nb: (0, i), memory_space=pltpu.SMEM)
    return pl.pallas_call(
        kern,
        grid_spec=pltpu.PrefetchScalarGridSpec(
            num_scalar_prefetch=3,
            grid=(b * nt,),
            in_specs=[smem_tile,
                      pl.BlockSpec((1, t, w), lambda i, c, st, nb: (i // nt, i % nt, 0)),
                      pl.BlockSpec((1,) + mod.shape[1:], lambda i, c, st, nb: (i // nt, 0, 0)),
                      pl.BlockSpec(norm2_g.shape, lambda i, c, st, nb: (0, 0))],
            out_specs=pl.BlockSpec(memory_space=pl.ANY),
            scratch_shapes=[pltpu.VMEM((t, w), F32), pltpu.VMEM((block, w), F32),
                            pltpu.SemaphoreType.DMA(())]),
        out_shape=jax.ShapeDtypeStruct((cap, w), F32),
        compiler_params=_cparams("arbitrary"),
        name="dispatch",
    )(counts, starts, n_blocks_used, dest, x1, mod, norm2_g)


def _expert_kernel(be_ref, nb_ref, xs_ref, wg_ref, wu_ref, wd_ref, ys_ref, wgu_sc, wdn_sc, *, d_ff):
    j = pl.program_id(0)
    live = jnp.minimum(j, nb_ref[0] - 1)

    @pl.when((j == 0) | (be_ref[live] != be_ref[jnp.maximum(live - 1, 0)]))
    def _():
        wgu_sc[:, 0:d_ff] = wg_ref[0].astype(BF16)
        wgu_sc[:, d_ff:2 * d_ff] = wu_ref[0].astype(BF16)
        wdn_sc[...] = wd_ref[0].astype(BF16)

    @pl.when(j < nb_ref[0])
    def _():
        xb = xs_ref[...].astype(BF16)
        gu = jnp.dot(xb, wgu_sc[...], preferred_element_type=F32)
        gt = gu[:, :d_ff]
        hid = gt * jax.nn.sigmoid(gt) * gu[:, d_ff:]
        ys_ref[...] = jnp.dot(hid.astype(BF16), wdn_sc[...], preferred_element_type=F32)

    @pl.when(pl.program_id(0) >= nb_ref[0])
    def _():
        ys_ref[...] = jnp.zeros(ys_ref.shape, ys_ref.dtype)


def _experts(block_e, n_blocks_used, xs, w_gate, w_up, w_down, *, block):
    cap, w = xs.shape
    d = w_down.shape[2]
    d_ff = w_down.shape[1]
    kern = functools.partial(_expert_kernel, d_ff=d_ff)
    live = lambda j, nb: jnp.minimum(j, nb[0] - 1)
    return pl.pallas_call(
        kern,
        grid_spec=pltpu.PrefetchScalarGridSpec(
            num_scalar_prefetch=2,
            grid=(cap // block,),
            in_specs=[pl.BlockSpec((block, w), lambda j, be, nb: (live(j, nb), 0)),
                      pl.BlockSpec((1, d, d_ff), lambda j, be, nb: (be[live(j, nb)], 0, 0)),
                      pl.BlockSpec((1, d, d_ff), lambda j, be, nb: (be[live(j, nb)], 0, 0)),
                      pl.BlockSpec((1, d_ff, d), lambda j, be, nb: (be[live(j, nb)], 0, 0))],
            out_specs=pl.BlockSpec((block, w), lambda j, be, nb: (j, 0)),
            scratch_shapes=[pltpu.VMEM((d, 2 * d_ff), BF16), pltpu.VMEM((d_ff, d), BF16)]),
        out_shape=jax.ShapeDtypeStruct((cap, w), F32),
        compiler_params=_cparams("arbitrary"),
        name="experts",
    )(block_e, n_blocks_used, xs, w_gate, w_up, w_down)


def _combine_kernel(dest_ref, dest_nx_ref, wts_ref, x1_ref, mod_ref, ys_ref, o_ref, buf, sem, *,
                    tile):
    i = pl.program_id(0)
    slot = i % 2

    def gather(d_ref, sl):
        for tk in range(tile):
            for k in range(TOP_K):
                _row_copy(ys_ref, d_ref[k, tk], buf.at[sl, k], tk, sem.at[sl]).start(priority=k % 2)

    @pl.when(i == 0)
    def _():
        gather(dest_ref, slot)

    @pl.when(i + 1 < pl.num_programs(0))
    def _():
        gather(dest_nx_ref, 1 - slot)

    for k in range(TOP_K):
        pltpu.make_async_copy(ys_ref.at[pl.ds(0, tile), :], buf.at[slot, k], sem.at[slot]).wait()

    d = o_ref.shape[-1]
    w0 = jnp.tile(_column(wts_ref[0:1, :]), (1, d // LANES))
    w1 = jnp.tile(_column(wts_ref[1:2, :]), (1, d // LANES))
    gate2 = mod_ref[0, 5:6, :]
    o_ref[0] = x1_ref[0] + gate2 * (w0 * buf[slot, 0] + w1 * buf[slot, 1])


def _combine(dest, wts, x1, mod, ys):
    b, s, d = x1.shape
    w = ys.shape[1]
    t = min(COMBINE_TILE, s)
    nt = s // t
    last = b * nt - 1
    kern = functools.partial(_combine_kernel, tile=t)
    smem_now = pl.BlockSpec((TOP_K, t), lambda i: (0, i), memory_space=pltpu.SMEM)
    smem_next = pl.BlockSpec((TOP_K, t), lambda i: (0, jnp.minimum(i + 1, last)),
                             memory_space=pltpu.SMEM)
    return pl.pallas_call(
        kern,
        grid=(b * nt,),
        in_specs=[smem_now, smem_next,
                  pl.BlockSpec((TOP_K, t), lambda i: (0, i)),
                  pl.BlockSpec((1, t, d), lambda i: (i // nt, i % nt, 0)),
                  pl.BlockSpec((1,) + mod.shape[1:], lambda i: (i // nt, 0, 0)),
                  pl.BlockSpec(memory_space=pl.ANY)],
        out_specs=pl.BlockSpec((1, t, d), lambda i: (i // nt, i % nt, 0)),
        out_shape=jax.ShapeDtypeStruct((b, s, d), F32),
        scratch_shapes=[pltpu.VMEM((2, TOP_K, t, w), F32), pltpu.SemaphoreType.DMA((2,))],
        compiler_params=_cparams("arbitrary"),
        name="combine",
    )(dest, dest, wts, x1, mod, ys)


def _layer(x, c, pos_row, w_ada, b_ada, norm1_g, w_in, q_a_norm, w_q_b, kv_a_norm, w_kv_b, q_norm,
           k_norm, conv_w, conv_b, lru_wa, lru_ba, lru_wx, lru_bx, lru_lambda, attn_out_norm,
           lru_out_norm, w_out, norm2_g, w_rg, b_rg, w_re, b_re, w_gate, w_up, w_down):
    b, s, d = x.shape
    n = b * s
    q_lora = q_a_norm.shape[0]
    kv_lora = kv_a_norm.shape[0]
    lru_w = lru_lambda.shape[0]
    attn_w = ATTN_HEADS * V_HEAD_DIM
    n_experts = w_gate.shape[0]
    row = lambda v: v.reshape(1, -1)

    mod = _adaln(c, w_ada, b_ada).reshape(b, 6, d)

    o1 = q_lora
    o2 = o1 + kv_lora
    o3 = o2 + QK_ROPE_DIM
    o4 = o3 + lru_w
    w_in_p = jnp.concatenate(
        [w_in[:, :o2], w_in[:, o3:o4], w_in[:, o4:], w_in[:, o2:o3],
         jnp.zeros((d, LANES - QK_ROPE_DIM), w_in.dtype)], axis=1).astype(BF16)
    pad_h = QK_PAD_DIM - QK_HEAD_DIM
    w_q_p = jnp.pad(w_q_b.reshape(q_lora, ATTN_HEADS, QK_HEAD_DIM),
                    ((0, 0), (0, 0), (0, pad_h))).reshape(q_lora, ATTN_HEADS * QK_PAD_DIM).astype(BF16)
    q_norm_p = row(jnp.pad(q_norm, (0, pad_h)))
    k_norm_nope = row(k_norm[:QK_NOPE_DIM])
    k_norm_rope_p = row(jnp.pad(k_norm[QK_NOPE_DIM:], (0, LANES - QK_ROPE_DIM)))
    inv = 1.0 / (ROPE_THETA ** (np.arange(0, QK_ROPE_DIM, 2, dtype=np.float32) / QK_ROPE_DIM))
    inv_col = jnp.asarray(inv.reshape(-1, 1))

    q, k, v, x_lru, gelu_g = _in_proj(
        x, pos_row, mod, row(norm1_g), w_in_p, row(q_a_norm), w_q_p, row(kv_a_norm),
        w_kv_b.astype(BF16), q_norm_p, k_norm_nope, k_norm_rope_p, inv_col,
        q_lora=q_lora, kv_lora=kv_lora, lru_w=lru_w)

    attn = _attention(q, k, v)

    w_ax = jnp.concatenate([lru_wa, lru_wx], axis=-1).astype(BF16)
    lru_n = _rg_lru(x_lru, gelu_g, pos_row, conv_w, row(conv_b), w_ax, row(lru_ba), row(lru_bx),
                    row(lru_lambda), row(lru_out_norm))

    group_row0, expert_row0 = 0, SUBLANES
    w_r_t = jnp.concatenate([w_rg.T, jnp.zeros((expert_row0 - N_GROUPS, d), w_rg.dtype), w_re.T],
                            axis=0).astype(BF16)
    b_r_col = jnp.concatenate([b_rg, jnp.zeros((expert_row0 - N_GROUPS,), b_rg.dtype),
                               b_re]).reshape(-1, 1)
    x1, ids, ranks, wts, cnt = _out_route(
        x, attn, lru_n, mod, row(attn_out_norm), w_out[:attn_w].astype(BF16),
        w_out[attn_w:].astype(BF16), row(norm2_g), w_r_t, b_r_col,
        n_experts=n_experts, group_row0=group_row0, expert_row0=expert_row0)

    blk = EXPERT_BLOCK
    counts = cnt[:, 0].astype(jnp.int32)
    padded = (counts + blk - 1) // blk * blk
    pad_ends = jnp.cumsum(padded)
    starts = (pad_ends - padded).astype(jnp.int32)
    ids_dense = ids.reshape(-1, LANES)
    seg_start = jnp.zeros_like(ids_dense)
    for e in range(n_experts):
        seg_start = seg_start + jnp.where(ids_dense == e, starts[e], 0)
    dest = seg_start.reshape(ids.shape) + ranks
    cap = (n * TOP_K + n_experts * (blk - 1) + blk - 1) // blk * blk
    n_blocks = cap // blk
    block_first_row = jnp.arange(n_blocks, dtype=jnp.int32) * blk
    block_e = jnp.minimum(jnp.sum(pad_ends[None, :] <= block_first_row[:, None], axis=1),
                          n_experts - 1).astype(jnp.int32)
    n_used = (pad_ends[-1:] // blk).astype(jnp.int32)

    xs = _dispatch(counts, starts, n_used, dest, x1, mod, row(norm2_g), cap,
                   block=blk, n_experts=n_experts)
    ys = _experts(block_e, n_used, xs, w_gate, w_up, w_down, block=blk)
    return _combine(dest, wts, x1, mod, ys)


def kernel(x, c, positions, w_ada, b_ada, norm1_g, w_in, q_a_norm, w_q_b, kv_a_norm, w_kv_b, q_norm, k_norm, conv_w, conv_b, lru_wa, lru_ba, lru_wx, lru_bx, lru_lambda, attn_out_norm, lru_out_norm, w_out, norm2_g, w_router_group, b_router_group, w_router_expert, b_router_expert, w_gate, w_up, w_down):
    b, s, _ = x.shape
    pos_row = positions.reshape(b, 1, s)
    for l in range(w_ada.shape[0]):
        x = _layer(x, c, pos_row, w_ada[l], b_ada[l], norm1_g[l], w_in[l], q_a_norm[l], w_q_b[l],
                   kv_a_norm[l], w_kv_b[l], q_norm[l], k_norm[l], conv_w[l], conv_b[l], lru_wa[l],
                   lru_ba[l], lru_wx[l], lru_bx[l], lru_lambda[l], attn_out_norm[l], lru_out_norm[l],
                   w_out[l], norm2_g[l], w_router_group[l], b_router_group[l], w_router_expert[l],
                   b_router_expert[l], w_gate[l], w_up[l], w_down[l])
    return x
```

```python
import functools
import math

import numpy as np
import jax
import jax.numpy as jnp
from jax import lax
from jax.experimental import pallas as pl
from jax.experimental.pallas import tpu as pltpu

F32 = jnp.float32
BF16 = jnp.bfloat16

ATTN_HEADS = 4
QK_NOPE_DIM = 128
QK_ROPE_DIM = 64
QK_HEAD_DIM = QK_NOPE_DIM + QK_ROPE_DIM
V_HEAD_DIM = 128
ROPE_THETA = 10000.0
LRU_HEADS = 4
CONV_WIDTH = 4
LRU_C = 8.0
N_GROUPS = 4
TOP_K = 2
EPS = 1e-6

LANES = 128
SUBLANES = 8
QK_PAD_DIM = 256
V_PAD_DIM = 256
VMEM_LIMIT = 56 * 1024 * 1024

PROJ_TILE = 512
PROJ_CHUNK = 128
ATTN_TILE = 1024
ATTN_CHUNK = 256
LRU_TILE = 512
ROUTE_TILE = 512
DISPATCH_TILE = 512
COMBINE_TILE = 512
EXPERT_BLOCK = 512


def _cparams(*sem):
    return pltpu.CompilerParams(dimension_semantics=sem, vmem_limit_bytes=VMEM_LIMIT)


def _rms(x, n):
    return lax.rsqrt(jnp.sum(x * x, axis=-1, keepdims=True) * (1.0 / n) + EPS)


def _sigmoid(x):
    return 0.5 * jnp.tanh(0.5 * x) + 0.5


def _column(row, width=LANES):
    t = row.shape[-1]
    return jnp.broadcast_to(row, (width, t)).T


def _adaln_kernel(c_ref, w_ref, b_ref, o_ref):
    c = c_ref[...]
    sc = c * jax.nn.sigmoid(c)
    o_ref[...] = jnp.dot(sc.astype(BF16), w_ref[...].astype(BF16),
                         preferred_element_type=F32) + b_ref[...]


def _adaln(c, w_ada, b_ada):
    b, d = c.shape
    n_out = w_ada.shape[1]
    return pl.pallas_call(
        _adaln_kernel,
        grid=(n_out // d,),
        in_specs=[pl.BlockSpec((b, d), lambda j: (0, 0)),
                  pl.BlockSpec((d, d), lambda j: (0, j)),
                  pl.BlockSpec((1, d), lambda j: (0, j))],
        out_specs=pl.BlockSpec((b, d), lambda j: (0, j)),
        out_shape=jax.ShapeDtypeStruct((b, n_out), F32),
        compiler_params=_cparams("parallel"),
        name="adaln",
    )(c, w_ada, b_ada.reshape(1, n_out))


def _rope_tables(pos_row, inv_col):
    half = QK_ROPE_DIM // 2
    ang = inv_col * pos_row
    cos = jnp.cos(ang)
    sin = jnp.sin(ang)
    t = ang.shape[-1]
    zeros = lambda n: jnp.zeros((n, t), F32)
    c = jnp.concatenate([cos, cos, zeros(LANES - QK_ROPE_DIM)], axis=0)
    sa = jnp.concatenate([-sin, zeros(LANES - half)], axis=0)
    sb = jnp.concatenate([zeros(half), sin, zeros(LANES - QK_ROPE_DIM)], axis=0)
    return c.T, sa.T, sb.T


def _rope(x, c, sa, sb):
    return x * c + pltpu.roll(x, LANES - QK_ROPE_DIM // 2, 1) * sa + pltpu.roll(x, QK_ROPE_DIM // 2, 1) * sb


def _in_proj_kernel(x_ref, pos_ref, mod_ref, g1_ref, w_in_ref, qa_ref, wq_ref, kva_ref, wkv_ref,
                    qn_ref, kn_nope_ref, kn_rope_ref, inv_ref,
                    q_ref, k_ref, v_ref, xl_ref, gl_ref, *, q_lora, kv_lora, lru_w, chunk):
    t, d = x_ref.shape[1:]
    shift = mod_ref[0, 0:1, :]
    gain = g1_ref[...] * (1.0 + mod_ref[0, 1:2, :])
    c_all, sa_all, sb_all = _rope_tables(pos_ref[0].astype(F32), inv_ref[...])
    sm_scale = QK_HEAD_DIM ** -0.5 * math.log2(math.e)

    for r0 in range(0, t, chunk):
        rows = slice(r0, r0 + chunk)
        x = x_ref[0, rows, :]
        h = x * _rms(x, d) * gain + shift
        proj = jnp.dot(h.astype(BF16), w_in_ref[...], preferred_element_type=F32)
        o = 0
        q_lat = proj[:, o:o + q_lora]; o += q_lora
        kv_lat = proj[:, o:o + kv_lora]; o += kv_lora
        x_lru = proj[:, o:o + lru_w]; o += lru_w
        g_lru = proj[:, o:o + lru_w]; o += lru_w
        k_pe = proj[:, o:o + LANES]

        xl_ref[0, rows, :] = x_lru
        gl_ref[0, rows, :] = jax.nn.gelu(g_lru).astype(BF16)

        c, sa, sb = c_all[rows], sa_all[rows], sb_all[rows]
        qn = q_lat * _rms(q_lat, q_lora) * qa_ref[...]
        q_full = jnp.dot(qn.astype(BF16), wq_ref[...], preferred_element_type=F32)
        kvn = kv_lat * _rms(kv_lat, kv_lora) * kva_ref[...]
        kv_full = jnp.dot(kvn.astype(BF16), wkv_ref[...], preferred_element_type=F32)

        pe_ss = jnp.sum(k_pe * k_pe, axis=-1, keepdims=True)
        pe_rot = _rope(k_pe * kn_rope_ref[...], c, sa, sb)
        for hd in range(ATTN_HEADS):
            qh = q_full[:, hd * QK_PAD_DIM:(hd + 1) * QK_PAD_DIM]
            qh = qh * (_rms(qh, QK_HEAD_DIM) * sm_scale) * qn_ref[...]
            q_ref[0, hd, rows, 0:QK_NOPE_DIM] = qh[:, 0:QK_NOPE_DIM].astype(BF16)
            q_ref[0, hd, rows, QK_NOPE_DIM:QK_PAD_DIM] = _rope(qh[:, QK_NOPE_DIM:QK_PAD_DIM],
                                                               c, sa, sb).astype(BF16)
            base = hd * (QK_NOPE_DIM + V_HEAD_DIM)
            kn = kv_full[:, base:base + QK_NOPE_DIM]
            vv = kv_full[:, base + QK_NOPE_DIM:base + QK_NOPE_DIM + V_HEAD_DIM]
            r = lax.rsqrt((jnp.sum(kn * kn, axis=-1, keepdims=True) + pe_ss) * (1.0 / QK_HEAD_DIM) + EPS)
            k_ref[0, hd, rows, 0:QK_NOPE_DIM] = (kn * r * kn_nope_ref[...]).astype(BF16)
            k_ref[0, hd, rows, QK_NOPE_DIM:QK_PAD_DIM] = (pe_rot * r).astype(BF16)
            v_ref[0, hd, rows, :] = vv.astype(BF16)


def _in_proj(x, pos_row, mod, norm1_g, w_in_p, q_a_norm, w_q_p, kv_a_norm, w_kv, q_norm_p,
             k_norm_nope, k_norm_rope_p, inv_col, *, q_lora, kv_lora, lru_w):
    b, s, d = x.shape
    t = min(PROJ_TILE, s)
    full = lambda a: pl.BlockSpec(a.shape, lambda i, j: (0,) * a.ndim)
    kern = functools.partial(_in_proj_kernel, q_lora=q_lora, kv_lora=kv_lora, lru_w=lru_w,
                             chunk=min(PROJ_CHUNK, t))
    head_out = lambda w: pl.BlockSpec((1, ATTN_HEADS, t, w), lambda i, j: (i, 0, j, 0))
    return pl.pallas_call(
        kern,
        grid=(b, s // t),
        in_specs=[pl.BlockSpec((1, t, d), lambda i, j: (i, j, 0)),
                  pl.BlockSpec((1, 1, t), lambda i, j: (i, 0, j)),
                  pl.BlockSpec((1,) + mod.shape[1:], lambda i, j: (i, 0, 0)),
                  full(norm1_g), full(w_in_p), full(q_a_norm), full(w_q_p), full(kv_a_norm),
                  full(w_kv), full(q_norm_p), full(k_norm_nope), full(k_norm_rope_p), full(inv_col)],
        out_specs=[head_out(QK_PAD_DIM), head_out(QK_PAD_DIM), head_out(V_HEAD_DIM),
                   pl.BlockSpec((1, t, lru_w), lambda i, j: (i, j, 0)),
                   pl.BlockSpec((1, t, lru_w), lambda i, j: (i, j, 0))],
        out_shape=[jax.ShapeDtypeStruct((b, ATTN_HEADS, s, QK_PAD_DIM), BF16),
                   jax.ShapeDtypeStruct((b, ATTN_HEADS, s, QK_PAD_DIM), BF16),
                   jax.ShapeDtypeStruct((b, ATTN_HEADS, s, V_HEAD_DIM), BF16),
                   jax.ShapeDtypeStruct((b, s, lru_w), F32),
                   jax.ShapeDtypeStruct((b, s, lru_w), BF16)],
        compiler_params=_cparams("parallel", "parallel"),
        name="in_proj",
    )(x, pos_row, mod, norm1_g, w_in_p, q_a_norm, w_q_p, kv_a_norm, w_kv, q_norm_p,
      k_norm_nope, k_norm_rope_p, inv_col)


def _attn_kernel(q_ref, k_ref, v_ref, o_ref, m_sc, acc_sc, s_sc, v_sc, *, tile, chunk):
    qi = pl.program_id(2)

    @pl.when(qi == 0)
    def _():
        v_sc[:, 0:V_HEAD_DIM] = v_ref[0, 0]
        v_sc[:, V_HEAD_DIM:V_PAD_DIM] = jnp.ones((v_sc.shape[0], V_PAD_DIM - V_HEAD_DIM), BF16)

    m_sc[...] = jnp.full(m_sc.shape, -jnp.inf, F32)
    acc_sc[...] = jnp.zeros(acc_sc.shape, F32)

    def scores(j, slot):
        start = pl.multiple_of(j * tile, tile)
        k = k_ref[0, 0, pl.ds(start, tile), :]
        s_sc[slot] = lax.dot_general(q_ref[0, 0], k, (((1,), (1,)), ((), ())),
                                     preferred_element_type=F32)

    def accumulate(j, slot, masked):
        start = pl.multiple_of(j * tile, tile)
        for r in range(tile // chunk):
            rows = slice(r * chunk, (r + 1) * chunk)
            cols = -(-(r + 1) * chunk // LANES) * LANES if masked else tile
            s = s_sc[slot, rows, 0:cols]
            if masked:
                row = lax.broadcasted_iota(jnp.int32, s.shape, 0) + r * chunk
                col = lax.broadcasted_iota(jnp.int32, s.shape, 1)
                s = jnp.where(col <= row, s, -jnp.inf)
            v = v_sc[pl.ds(start, cols), :]
            m_old = m_sc[rows, :]
            m_new = jnp.maximum(m_old, jnp.max(s, axis=-1, keepdims=True))
            alpha = jnp.exp2(m_old - m_new)
            p = jnp.exp2(s - jnp.tile(m_new, (1, cols // LANES)))
            acc_sc[rows, :] = (jnp.tile(alpha, (1, V_PAD_DIM // LANES)) * acc_sc[rows, :]
                               + jnp.dot(p.astype(BF16), v, preferred_element_type=F32))
            m_sc[rows, :] = m_new

    scores(0, 0)

    def pair(p, carry):
        scores(2 * p + 1, 1)
        accumulate(2 * p, 0, False)
        scores(2 * p + 2, 0)
        accumulate(2 * p + 1, 1, False)
        return carry

    lax.fori_loop(0, qi // 2, pair, 0)

    @pl.when(qi % 2 == 0)
    def _():
        accumulate(qi, 0, True)

    @pl.when(qi % 2 == 1)
    def _():
        scores(qi, 1)
        accumulate(qi - 1, 0, False)
        accumulate(qi, 1, True)

    o_ref[0] = (acc_sc[:, 0:V_HEAD_DIM] / acc_sc[:, V_HEAD_DIM:2 * V_HEAD_DIM]).astype(o_ref.dtype)


def _attention(q, k, v):
    b, h, s, _ = q.shape
    t = min(ATTN_TILE, s)
    assert V_HEAD_DIM == LANES and V_PAD_DIM == 2 * V_HEAD_DIM
    kern = functools.partial(_attn_kernel, tile=t, chunk=min(ATTN_CHUNK, t))
    return pl.pallas_call(
        kern,
        grid=(b, h, s // t),
        in_specs=[pl.BlockSpec((1, 1, t, QK_PAD_DIM), lambda i, j, n: (i, j, n, 0)),
                  pl.BlockSpec((1, 1, s, QK_PAD_DIM), lambda i, j, n: (i, j, 0, 0)),
                  pl.BlockSpec((1, 1, s, V_HEAD_DIM), lambda i, j, n: (i, j, 0, 0))],
        out_specs=pl.BlockSpec((1, t, V_HEAD_DIM), lambda i, j, n: (i, n, j)),
        out_shape=jax.ShapeDtypeStruct((b, s, h * V_HEAD_DIM), BF16),
        scratch_shapes=[pltpu.VMEM((t, LANES), F32), pltpu.VMEM((t, V_PAD_DIM), F32),
                        pltpu.VMEM((2, t, t), F32), pltpu.VMEM((s, V_PAD_DIM), BF16)],
        compiler_params=_cparams("parallel", "parallel", "arbitrary"),
        name="attention",
    )(q, k, v)


def _lru_kernel(xl_ref, gl_ref, pos_ref, cw_ref, cb_ref, wax_ref, ba_ref, bx_ref, lam_ref, gn_ref,
                o_ref, ext_sc, a_sc, b_sc, h_sc, carry_sc):
    t, c = a_sc.shape
    head_w = c // LRU_HEADS

    @pl.when(pl.program_id(1) == 0)
    def _():
        ext_sc[0:SUBLANES, :] = jnp.zeros((SUBLANES, c), F32)
        carry_sc[...] = jnp.zeros(carry_sc.shape, F32)

    x = xl_ref[0]
    ext_sc[SUBLANES:SUBLANES + t, :] = x
    xc = cb_ref[...] + x * cw_ref[CONV_WIDTH - 1:CONV_WIDTH, :]
    for back in range(1, CONV_WIDTH):
        tap = CONV_WIDTH - 1 - back
        xc = xc + ext_sc[SUBLANES - back:SUBLANES - back + t, :] * cw_ref[tap:tap + 1, :]
    ext_sc[0:SUBLANES, :] = x[t - SUBLANES:t, :]

    reset = _column((pos_ref[0] == 0).astype(F32)) > 0.5
    lam = lam_ref[...]
    neg_sp = -LRU_C * (jnp.maximum(-lam, 0.0) + jnp.log1p(jnp.exp(-jnp.abs(lam))))
    for hd in range(LRU_HEADS):
        sl = slice(hd * head_w, (hd + 1) * head_w)
        xh = xc[:, sl]
        gates = jnp.dot(xh.astype(BF16), wax_ref[hd], preferred_element_type=F32)
        r = _sigmoid(gates[:, :head_w] + ba_ref[:, sl])
        i = _sigmoid(gates[:, head_w:] + bx_ref[:, sl])
        a = jnp.exp(r * neg_sp[:, sl])
        gap = jnp.maximum(1.0 - a * a, 0.0)
        mult = jnp.where(gap > 0.0, gap * lax.rsqrt(gap), 0.0)
        a = jnp.where(reset, 0.0, a)
        mult = jnp.where(reset, 1.0, mult)
        a_sc[:, sl] = a
        b_sc[:, sl] = mult * (i * xh)

    srow = lax.broadcasted_iota(jnp.int32, (SUBLANES, c), 0)

    def group(g, h_prev):
        start = pl.multiple_of(g * SUBLANES, SUBLANES)
        a = a_sc[pl.ds(start, SUBLANES), :]
        bb = b_sc[pl.ds(start, SUBLANES), :]
        sh = 1
        while sh < SUBLANES:
            keep = srow >= sh
            a_prev = jnp.where(keep, pltpu.roll(a, sh, 0), 1.0)
            b_prev = jnp.where(keep, pltpu.roll(bb, sh, 0), 0.0)
            bb = bb + a * b_prev
            a = a * a_prev
            sh *= 2
        hh = bb + a * h_prev
        h_sc[pl.ds(start, SUBLANES), :] = hh
        return jnp.broadcast_to(hh[SUBLANES - 1:SUBLANES, :], (SUBLANES, c))

    carry_sc[...] = lax.fori_loop(0, t // SUBLANES, group, carry_sc[...], unroll=4)

    y = h_sc[...] * gl_ref[0].astype(F32)
    o_ref[0] = (y * _rms(y, c) * gn_ref[...]).astype(o_ref.dtype)


def _rg_lru(x_lru, gelu_g, pos_row, conv_w, conv_b, w_ax, b_a, b_x, lam, out_norm):
    b, s, c = x_lru.shape
    t = min(LRU_TILE, s)
    full = lambda a: pl.BlockSpec(a.shape, lambda i, j: (0,) * a.ndim)
    return pl.pallas_call(
        _lru_kernel,
        grid=(b, s // t),
        in_specs=[pl.BlockSpec((1, t, c), lambda i, j: (i, j, 0)),
                  pl.BlockSpec((1, t, c), lambda i, j: (i, j, 0)),
                  pl.BlockSpec((1, 1, t), lambda i, j: (i, 0, j)),
                  full(conv_w), full(conv_b), full(w_ax), full(b_a), full(b_x), full(lam),
                  full(out_norm)],
        out_specs=pl.BlockSpec((1, t, c), lambda i, j: (i, j, 0)),
        out_shape=jax.ShapeDtypeStruct((b, s, c), BF16),
        scratch_shapes=[pltpu.VMEM((t + SUBLANES, c), F32), pltpu.VMEM((t, c), F32),
                        pltpu.VMEM((t, c), F32), pltpu.VMEM((t, c), F32),
                        pltpu.VMEM((SUBLANES, c), F32)],
        compiler_params=_cparams("parallel", "arbitrary"),
        name="rg_lru",
    )(x_lru, gelu_g, pos_row, conv_w, conv_b, w_ax, b_a, b_x, lam, out_norm)


def _route_kernel(x_ref, at_ref, lr_ref, mod_ref, an_ref, wa_ref, wl_ref, g2_ref, wr_ref, br_ref,
                  x1_ref, ids_ref, rank_ref, wts_ref, cnt_ref, carry_sc,
                  *, n_experts, group_row0, expert_row0):
    @pl.when(pl.program_id(0) == 0)
    def _():
        carry_sc[...] = jnp.zeros(carry_sc.shape, F32)

    x = x_ref[0]
    t, d = x.shape
    gate1 = mod_ref[0, 2:3, :]
    shift2 = mod_ref[0, 3:4, :]
    gain2 = g2_ref[...] * (1.0 + mod_ref[0, 4:5, :])
    at = at_ref[0].astype(F32)
    at_n = at * _rms(at, at.shape[-1]) * an_ref[...]
    mix = jnp.dot(at_n.astype(BF16), wa_ref[...], preferred_element_type=F32)
    mix = mix + jnp.dot(lr_ref[0], wl_ref[...], preferred_element_type=F32)
    x1 = x + gate1 * mix
    x1_ref[0] = x1
    h2 = x1 * _rms(x1, d) * gain2 + shift2

    logits = lax.dot_general(wr_ref[...], h2.astype(BF16), (((1,), (1,)), ((), ())),
                             preferred_element_type=F32) + br_ref[...]
    per_group = n_experts // N_GROUPS
    lg = logits[group_row0:group_row0 + N_GROUPS, :]
    grow = lax.broadcasted_iota(jnp.int32, lg.shape, 0)
    gmax = jnp.max(lg, axis=0, keepdims=True)
    g_idx = jnp.min(jnp.where(lg == gmax, grow, N_GROUPS), axis=0, keepdims=True)
    g_p = 1.0 / jnp.sum(jnp.exp(lg - gmax), axis=0, keepdims=True)
    sel = jnp.zeros((per_group, t), F32)
    for g in range(N_GROUPS):
        r0 = expert_row0 + g * per_group
        sel = sel + jnp.where(g_idx == g, logits[r0:r0 + per_group, :], 0.0)
    erow = lax.broadcasted_iota(jnp.int32, sel.shape, 0)
    m1 = jnp.max(sel, axis=0, keepdims=True)
    i1 = jnp.min(jnp.where(sel == m1, erow, per_group), axis=0, keepdims=True)
    sel2 = jnp.where(erow == i1, -jnp.inf, sel)
    m2 = jnp.max(sel2, axis=0, keepdims=True)
    i2 = jnp.min(jnp.where(sel2 == m2, erow, per_group), axis=0, keepdims=True)
    e21 = jnp.exp(m2 - m1)
    w1 = g_p / (1.0 + e21)
    w2 = g_p * e21 / (1.0 + e21)
    e1 = g_idx * per_group + i1
    e2 = g_idx * per_group + i2
    ids_ref[0:1, :] = e1
    ids_ref[1:2, :] = e2
    wts_ref[0:1, :] = w1
    wts_ref[1:2, :] = w2

    xrow = lax.broadcasted_iota(jnp.int32, (n_experts, t), 0)
    o1 = (xrow == e1).astype(F32)
    o2 = (xrow == e2).astype(F32)
    both = o1 + o2
    tr = lax.broadcasted_iota(jnp.int32, (t, t), 0)
    tc = lax.broadcasted_iota(jnp.int32, (t, t), 1)
    upper = jnp.where(tr < tc, 1.0, 0.0).astype(BF16)
    before = jnp.dot(both.astype(BF16), upper, preferred_element_type=F32) + carry_sc[...]
    rank_ref[0:1, :] = jnp.sum(o1 * before, axis=0, keepdims=True).astype(jnp.int32)
    rank_ref[1:2, :] = jnp.sum(o2 * before, axis=0, keepdims=True).astype(jnp.int32)
    carry_sc[...] = carry_sc[...] + jnp.sum(both, axis=1, keepdims=True)
    cnt_ref[...] = jnp.broadcast_to(carry_sc[...], cnt_ref.shape)


def _out_route(x, attn, lru_n, mod, attn_norm, w_out_a, w_out_l, norm2_g, w_r_t, b_r_col,
               *, n_experts, group_row0, expert_row0):
    b, s, d = x.shape
    t = min(ROUTE_TILE, s)
    nt = s // t
    n = b * s
    full = lambda a: pl.BlockSpec(a.shape, lambda i: (0,) * a.ndim)
    tok = lambda w: pl.BlockSpec((1, t, w), lambda i: (i // nt, i % nt, 0))
    kern = functools.partial(_route_kernel, n_experts=n_experts, group_row0=group_row0,
                             expert_row0=expert_row0)
    return pl.pallas_call(
        kern,
        grid=(b * nt,),
        in_specs=[tok(d), tok(attn.shape[-1]), tok(lru_n.shape[-1]),
                  pl.BlockSpec((1,) + mod.shape[1:], lambda i: (i // nt, 0, 0)),
                  full(attn_norm), full(w_out_a), full(w_out_l), full(norm2_g), full(w_r_t),
                  full(b_r_col)],
        out_specs=[tok(d),
                   pl.BlockSpec((TOP_K, t), lambda i: (0, i)),
                   pl.BlockSpec((TOP_K, t), lambda i: (0, i)),
                   pl.BlockSpec((TOP_K, t), lambda i: (0, i)),
                   pl.BlockSpec((n_experts, LANES), lambda i: (0, 0))],
        out_shape=[jax.ShapeDtypeStruct((b, s, d), F32),
                   jax.ShapeDtypeStruct((TOP_K, n), jnp.int32),
                   jax.ShapeDtypeStruct((TOP_K, n), jnp.int32),
                   jax.ShapeDtypeStruct((TOP_K, n), F32),
                   jax.ShapeDtypeStruct((n_experts, LANES), F32)],
        scratch_shapes=[pltpu.VMEM((n_experts, 1), F32)],
        compiler_params=_cparams("arbitrary"),
        name="out_route",
    )(x, attn, lru_n, mod, attn_norm, w_out_a, w_out_l, norm2_g, w_r_t, b_r_col)


def _row_copy(src, src_row, dst, dst_row, sem):
    return pltpu.make_async_copy(src.at[pl.ds(src_row, 1), :], dst.at[pl.ds(dst_row, 1), :], sem)


def _dispatch_kernel(cnt_ref, start_ref, nb_ref, dest_ref, x1_ref, mod_ref, g2_ref, xs_ref, h2_sc,
                     zero_sc, sem, *, tile, block, n_experts):
    i = pl.program_id(0)

    x1 = x1_ref[0]
    gain2 = g2_ref[...] * (1.0 + mod_ref[0, 4:5, :])
    h2_sc[...] = x1 * _rms(x1, x1.shape[-1]) * gain2 + mod_ref[0, 3:4, :]

    @pl.when(i == 0)
    def _():
        zero_sc[...] = jnp.zeros(zero_sc.shape, zero_sc.dtype)

        def tail_copy(j):
            return pltpu.make_async_copy(zero_sc, xs_ref.at[pl.ds(j * block, block), :], sem)

        def tail_issue(j, carry):
            tail_copy(j).start()
            return carry

        def tail_drain(j, carry):
            tail_copy(j).wait()
            return carry

        n_blocks = xs_ref.shape[0] // block
        lax.fori_loop(nb_ref[0], n_blocks, tail_issue, 0)
        lax.fori_loop(nb_ref[0], n_blocks, tail_drain, 0)

        def pad_copies(e, act):
            cnt = cnt_ref[e]
            first = start_ref[e] + cnt
            n_pad = (block - cnt % block) % block
            n_single = (SUBLANES - first % SUBLANES) % SUBLANES
            for r in range(SUBLANES - 1):
                @pl.when(r < n_single)
                def _(r=r):
                    act(_row_copy(zero_sc, 0, xs_ref, first + r, sem))

            row = first + n_single
            rest = n_pad - n_single
            for bit in range(SUBLANES.bit_length() - 1, block.bit_length() - 1):
                size = 1 << bit
                take = (rest >> bit) & 1

                @pl.when(take == 1)
                def _(row=row, size=size):
                    act(pltpu.make_async_copy(
                        zero_sc.at[pl.ds(0, size), :],
                        xs_ref.at[pl.ds(pl.multiple_of(row, SUBLANES), size), :], sem))

                row = row + take * size

        def pad_issue(e, carry):
            pad_copies(e, lambda cp: cp.start())
            return carry

        def pad_drain(e, carry):
            pad_copies(e, lambda cp: cp.wait())
            return carry

        lax.fori_loop(0, n_experts, pad_issue, 0)
        lax.fori_loop(0, n_experts, pad_drain, 0)

    for tk in range(tile):
        for k in range(TOP_K):
            _row_copy(h2_sc, tk, xs_ref, dest_ref[k, tk], sem).start(priority=k % 2)

    def drain(tk, carry):
        for _ in range(TOP_K):
            _row_copy(h2_sc, 0, xs_ref, 0, sem).wait()
        return carry

    lax.fori_loop(0, tile, drain, 0, unroll=8)


def _dispatch(counts, starts, n_blocks_used, dest, x1, mod, norm2_g, cap, *, block, n_experts):
    b, s, w = x1.shape
    t = min(DISPATCH_TILE, s)
    nt = s // t
    kern = functools.partial(_dispatch_kernel, tile=t, block=block, n_experts=n_experts)
    smem_tile = pl.BlockSpec((TOP_K, t), lambda i, c, st, nb: (0, i), memory_space=pltpu.SMEM)
    return pl.pallas_call(
        kern,
        grid_spec=pltpu.PrefetchScalarGridSpec(
            num_scalar_prefetch=3,
            grid=(b * nt,),
            in_specs=[smem_tile,
                      pl.BlockSpec((1, t, w), lambda i, c, st, nb: (i // nt, i % nt, 0)),
                      pl.BlockSpec((1,) + mod.shape[1:], lambda i, c, st, nb: (i // nt, 0, 0)),
                      pl.BlockSpec(norm2_g.shape, lambda i, c, st, nb: (0, 0))],
            out_specs=pl.BlockSpec(memory_space=pl.ANY),
            scratch_shapes=[pltpu.VMEM((t, w), F32), pltpu.VMEM((block, w), F32),
                            pltpu.SemaphoreType.DMA(())]),
        out_shape=jax.ShapeDtypeStruct((cap, w), F32),
        compiler_params=_cparams("arbitrary"),
        name="dispatch",
    )(counts, starts, n_blocks_used, dest, x1, mod, norm2_g)


def _expert_kernel(be_ref, nb_ref, xs_ref, wg_ref, wu_ref, wd_ref, ys_ref, wgu_sc, wdn_sc, *, d_ff):
    j = pl.program_id(0)
    live = jnp.minimum(j, nb_ref[0] - 1)

    @pl.when((j == 0) | (be_ref[live] != be_ref[jnp.maximum(live - 1, 0)]))
    def _():
        wgu_sc[:, 0:d_ff] = wg_ref[0].astype(BF16)
        wgu_sc[:, d_ff:2 * d_ff] = wu_ref[0].astype(BF16)
        wdn_sc[...] = wd_ref[0].astype(BF16)

    @pl.when(j < nb_ref[0])
    def _():
        xb = xs_ref[...].astype(BF16)
        gu = jnp.dot(xb, wgu_sc[...], preferred_element_type=F32)
        gt = gu[:, :d_ff]
        hid = gt * jax.nn.sigmoid(gt) * gu[:, d_ff:]
        ys_ref[...] = jnp.dot(hid.astype(BF16), wdn_sc[...], preferred_element_type=F32)

    @pl.when(pl.program_id(0) >= nb_ref[0])
    def _():
        ys_ref[...] = jnp.zeros(ys_ref.shape, ys_ref.dtype)


def _experts(block_e, n_blocks_used, xs, w_gate, w_up, w_down, *, block):
    cap, w = xs.shape
    d = w_down.shape[2]
    d_ff = w_down.shape[1]
    kern = functools.partial(_expert_kernel, d_ff=d_ff)
    live = lambda j, nb: jnp.minimum(j, nb[0] - 1)
    return pl.pallas_call(
        kern,
        grid_spec=pltpu.PrefetchScalarGridSpec(
            num_scalar_prefetch=2,
            grid=(cap // block,),
            in_specs=[pl.BlockSpec((block, w), lambda j, be, nb: (live(j, nb), 0)),
                      pl.BlockSpec((1, d, d_ff), lambda j, be, nb: (be[live(j, nb)], 0, 0)),
                      pl.BlockSpec((1, d, d_ff), lambda j, be, nb: (be[live(j, nb)], 0, 0)),
                      pl.BlockSpec((1, d_ff, d), lambda j, be, nb: (be[live(j, nb)], 0, 0))],
            out_specs=pl.BlockSpec((block, w), lambda j, be, nb: (j, 0)),
            scratch_shapes=[pltpu.VMEM((d, 2 * d_ff), BF16), pltpu.VMEM((d_ff, d), BF16)]),
        out_shape=jax.ShapeDtypeStruct((cap, w), F32),
        compiler_params=_cparams("arbitrary"),
        name="experts",
    )(block_e, n_blocks_used, xs, w_gate, w_up, w_down)


def _combine_kernel(dest_ref, dest_nx_ref, wts_ref, x1_ref, mod_ref, ys_ref, o_ref, buf, sem, *,
                    tile):
    i = pl.program_id(0)
    slot = i % 2

    def gather(d_ref, sl):
        for tk in range(tile):
            for k in range(TOP_K):
                _row_copy(ys_ref, d_ref[k, tk], buf.at[sl, k], tk, sem.at[sl]).start(priority=k % 2)

    @pl.when(i == 0)
    def _():
        gather(dest_ref, slot)

    @pl.when(i + 1 < pl.num_programs(0))
    def _():
        gather(dest_nx_ref, 1 - slot)

    for k in range(TOP_K):
        pltpu.make_async_copy(ys_ref.at[pl.ds(0, tile), :], buf.at[slot, k], sem.at[slot]).wait()

    d = o_ref.shape[-1]
    w0 = jnp.tile(_column(wts_ref[0:1, :]), (1, d // LANES))
    w1 = jnp.tile(_column(wts_ref[1:2, :]), (1, d // LANES))
    gate2 = mod_ref[0, 5:6, :]
    o_ref[0] = x1_ref[0] + gate2 * (w0 * buf[slot, 0] + w1 * buf[slot, 1])


def _combine(dest, wts, x1, mod, ys):
    b, s, d = x1.shape
    w = ys.shape[1]
    t = min(COMBINE_TILE, s)
    nt = s // t
    last = b * nt - 1
    kern = functools.partial(_combine_kernel, tile=t)
    smem_now = pl.BlockSpec((TOP_K, t), lambda i: (0, i), memory_space=pltpu.SMEM)
    smem_next = pl.BlockSpec((TOP_K, t), lambda i: (0, jnp.minimum(i + 1, last)),
                             memory_space=pltpu.SMEM)
    return pl.pallas_call(
        kern,
        grid=(b * nt,),
        in_specs=[smem_now, smem_next,
                  pl.BlockSpec((TOP_K, t), lambda i: (0, i)),
                  pl.BlockSpec((1, t, d), lambda i: (i // nt, i % nt, 0)),
                  pl.BlockSpec((1,) + mod.shape[1:], lambda i: (i // nt, 0, 0)),
                  pl.BlockSpec(memory_space=pl.ANY)],
        out_specs=pl.BlockSpec((1, t, d), lambda i: (i // nt, i % nt, 0)),
        out_shape=jax.ShapeDtypeStruct((b, s, d), F32),
        scratch_shapes=[pltpu.VMEM((2, TOP_K, t, w), F32), pltpu.SemaphoreType.DMA((2,))],
        compiler_params=_cparams("arbitrary"),
        name="combine",
    )(dest, dest, wts, x1, mod, ys)


def _layer(x, c, pos_row, w_ada, b_ada, norm1_g, w_in, q_a_norm, w_q_b, kv_a_norm, w_kv_b, q_norm,
           k_norm, conv_w, conv_b, lru_wa, lru_ba, lru_wx, lru_bx, lru_lambda, attn_out_norm,
           lru_out_norm, w_out, norm2_g, w_rg, b_rg, w_re, b_re, w_gate, w_up, w_down):
    b, s, d = x.shape
    n = b * s
    q_lora = q_a_norm.shape[0]
    kv_lora = kv_a_norm.shape[0]
    lru_w = lru_lambda.shape[0]
    attn_w = ATTN_HEADS * V_HEAD_DIM
    n_experts = w_gate.shape[0]
    row = lambda v: v.reshape(1, -1)

    mod = _adaln(c, w_ada, b_ada).reshape(b, 6, d)

    o1 = q_lora
    o2 = o1 + kv_lora
    o3 = o2 + QK_ROPE_DIM
    o4 = o3 + lru_w
    w_in_p = jnp.concatenate(
        [w_in[:, :o2], w_in[:, o3:o4], w_in[:, o4:], w_in[:, o2:o3],
         jnp.zeros((d, LANES - QK_ROPE_DIM), w_in.dtype)], axis=1).astype(BF16)
    pad_h = QK_PAD_DIM - QK_HEAD_DIM
    w_q_p = jnp.pad(w_q_b.reshape(q_lora, ATTN_HEADS, QK_HEAD_DIM),
                    ((0, 0), (0, 0), (0, pad_h))).reshape(q_lora, ATTN_HEADS * QK_PAD_DIM).astype(BF16)
    q_norm_p = row(jnp.pad(q_norm, (0, pad_h)))
    k_norm_nope = row(k_norm[:QK_NOPE_DIM])
    k_norm_rope_p = row(jnp.pad(k_norm[QK_NOPE_DIM:], (0, LANES - QK_ROPE_DIM)))
    inv = 1.0 / (ROPE_THETA ** (np.arange(0, QK_ROPE_DIM, 2, dtype=np.float32) / QK_ROPE_DIM))
    inv_col = jnp.asarray(inv.reshape(-1, 1))

    q, k, v, x_lru, gelu_g = _in_proj(
        x, pos_row, mod, row(norm1_g), w_in_p, row(q_a_norm), w_q_p, row(kv_a_norm),
        w_kv_b.astype(BF16), q_norm_p, k_norm_nope, k_norm_rope_p, inv_col,
        q_lora=q_lora, kv_lora=kv_lora, lru_w=lru_w)

    attn = _attention(q, k, v)

    w_ax = jnp.concatenate([lru_wa, lru_wx], axis=-1).astype(BF16)
    lru_n = _rg_lru(x_lru, gelu_g, pos_row, conv_w, row(conv_b), w_ax, row(lru_ba), row(lru_bx),
                    row(lru_lambda), row(lru_out_norm))

    group_row0, expert_row0 = 0, SUBLANES
    w_r_t = jnp.concatenate([w_rg.T, jnp.zeros((expert_row0 - N_GROUPS, d), w_rg.dtype), w_re.T],
                            axis=0).astype(BF16)
    b_r_col = jnp.concatenate([b_rg, jnp.zeros((expert_row0 - N_GROUPS,), b_rg.dtype),
                               b_re]).reshape(-1, 1)
    x1, ids, ranks, wts, cnt = _out_route(
        x, attn, lru_n, mod, row(attn_out_norm), w_out[:attn_w].astype(BF16),
        w_out[attn_w:].astype(BF16), row(norm2_g), w_r_t, b_r_col,
        n_experts=n_experts, group_row0=group_row0, expert_row0=expert_row0)

    blk = EXPERT_BLOCK
    counts = cnt[:, 0].astype(jnp.int32)
    padded = (counts + blk - 1) // blk * blk
    pad_ends = jnp.cumsum(padded)
    starts = (pad_ends - padded).astype(jnp.int32)
    ids_dense = ids.reshape(-1, LANES)
    seg_start = jnp.zeros_like(ids_dense)
    for e in range(n_experts):
        seg_start = seg_start + jnp.where(ids_dense == e, starts[e], 0)
    dest = seg_start.reshape(ids.shape) + ranks
    cap = (n * TOP_K + n_experts * (blk - 1) + blk - 1) // blk * blk
    n_blocks = cap // blk
    block_first_row = jnp.arange(n_blocks, dtype=jnp.int32) * blk
    block_e = jnp.minimum(jnp.sum(pad_ends[None, :] <= block_first_row[:, None], axis=1),
                          n_experts - 1).astype(jnp.int32)
    n_used = (pad_ends[-1:] // blk).astype(jnp.int32)

    xs = _dispatch(counts, starts, n_used, dest, x1, mod, row(norm2_g), cap,
                   block=blk, n_experts=n_experts)
    ys = _experts(block_e, n_used, xs, w_gate, w_up, w_down, block=blk)
    return _combine(dest, wts, x1, mod, ys)


def kernel(x, c, positions, w_ada, b_ada, norm1_g, w_in, q_a_norm, w_q_b, kv_a_norm, w_kv_b, q_norm, k_norm, conv_w, conv_b, lru_wa, lru_ba, lru_wx, lru_bx, lru_lambda, attn_out_norm, lru_out_norm, w_out, norm2_g, w_router_group, b_router_group, w_router_expert, b_router_expert, w_gate, w_up, w_down):
    b, s, _ = x.shape
    pos_row = positions.reshape(b, 1, s)
    for l in range(w_ada.shape[0]):
        x = _layer(x, c, pos_row, w_ada[l], b_ada[l], norm1_g[l], w_in[l], q_a_norm[l], w_q_b[l],
                   kv_a_norm[l], w_kv_b[l], q_norm[l], k_norm[l], conv_w[l], conv_b[l], lru_wa[l],
                   lru_ba[l], lru_wx[l], lru_bx[l], lru_lambda[l], attn_out_norm[l], lru_out_norm[l],
                   w_out[l], norm2_g[l], w_router_group[l], b_router_group[l], w_router_expert[l],
                   b_router_expert[l], w_gate[l], w_up[l], w_down[l])
    return x
```

```python
import functools
import math

import numpy as np
import jax
import jax.numpy as jnp
from jax import lax
from jax.experimental import pallas as pl
from jax.experimental.pallas import tpu as pltpu

F32 = jnp.float32
BF16 = jnp.bfloat16

ATTN_HEADS = 4
QK_NOPE_DIM = 128
QK_ROPE_DIM = 64
QK_HEAD_DIM = QK_NOPE_DIM + QK_ROPE_DIM
V_HEAD_DIM = 128
ROPE_THETA = 10000.0
LRU_HEADS = 4
CONV_WIDTH = 4
LRU_C = 8.0
N_GROUPS = 4
TOP_K = 2
EPS = 1e-6

LANES = 128
SUBLANES = 8
QK_PAD_DIM = 256
V_PAD_DIM = 256
VMEM_LIMIT = 56 * 1024 * 1024

PROJ_TILE = 1024
PROJ_CHUNK = 128
ATTN_TILE = 1024
ATTN_CHUNK = 256
LRU_TILE = 1024
ROUTE_TILE = 1024
DISPATCH_TILE = 512
COMBINE_TILE = 512
EXPERT_BLOCK = 512


def _cparams(*sem):
    return pltpu.CompilerParams(dimension_semantics=sem, vmem_limit_bytes=VMEM_LIMIT)


def _rms(x, n):
    return lax.rsqrt(jnp.sum(x * x, axis=-1, keepdims=True) * (1.0 / n) + EPS)


def _sigmoid(x):
    return 0.5 * jnp.tanh(0.5 * x) + 0.5


def _column(row, width=LANES):
    t = row.shape[-1]
    return jnp.broadcast_to(row, (width, t)).T


def _adaln_kernel(c_ref, w_ref, b_ref, o_ref):
    c = c_ref[...]
    sc = c * jax.nn.sigmoid(c)
    o_ref[...] = jnp.dot(sc.astype(BF16), w_ref[...].astype(BF16),
                         preferred_element_type=F32) + b_ref[...]


def _adaln(c, w_ada, b_ada):
    b, d = c.shape
    n_out = w_ada.shape[1]
    return pl.pallas_call(
        _adaln_kernel,
        grid=(n_out // d,),
        in_specs=[pl.BlockSpec((b, d), lambda j: (0, 0)),
                  pl.BlockSpec((d, d), lambda j: (0, j)),
                  pl.BlockSpec((1, d), lambda j: (0, j))],
        out_specs=pl.BlockSpec((b, d), lambda j: (0, j)),
        out_shape=jax.ShapeDtypeStruct((b, n_out), F32),
        compiler_params=_cparams("parallel"),
        name="adaln",
    )(c, w_ada, b_ada.reshape(1, n_out))


def _rope_tables(pos_row, inv_col):
    half = QK_ROPE_DIM // 2
    ang = inv_col * pos_row
    cos = jnp.cos(ang)
    sin = jnp.sin(ang)
    t = ang.shape[-1]
    zeros = lambda n: jnp.zeros((n, t), F32)
    c = jnp.concatenate([cos, cos, zeros(LANES - QK_ROPE_DIM)], axis=0)
    sa = jnp.concatenate([-sin, zeros(LANES - half)], axis=0)
    sb = jnp.concatenate([zeros(half), sin, zeros(LANES - QK_ROPE_DIM)], axis=0)
    return c.T, sa.T, sb.T


def _rope(x, c, sa, sb):
    return x * c + pltpu.roll(x, LANES - QK_ROPE_DIM // 2, 1) * sa + pltpu.roll(x, QK_ROPE_DIM // 2, 1) * sb


def _in_proj_kernel(x_ref, pos_ref, mod_ref, g1_ref, w_in_ref, qa_ref, wq_ref, kva_ref, wkv_ref,
                    qn_ref, kn_nope_ref, kn_rope_ref, inv_ref,
                    q_ref, k_ref, v_ref, xl_ref, gl_ref, *, q_lora, kv_lora, lru_w, chunk):
    t, d = x_ref.shape[1:]
    shift = mod_ref[0, 0:1, :]
    gain = g1_ref[...] * (1.0 + mod_ref[0, 1:2, :])
    c_all, sa_all, sb_all = _rope_tables(pos_ref[0].astype(F32), inv_ref[...])
    sm_scale = QK_HEAD_DIM ** -0.5 * math.log2(math.e)

    for r0 in range(0, t, chunk):
        rows = slice(r0, r0 + chunk)
        x = x_ref[0, rows, :]
        h = x * _rms(x, d) * gain + shift
        proj = jnp.dot(h.astype(BF16), w_in_ref[...], preferred_element_type=F32)
        o = 0
        q_lat = proj[:, o:o + q_lora]; o += q_lora
        kv_lat = proj[:, o:o + kv_lora]; o += kv_lora
        x_lru = proj[:, o:o + lru_w]; o += lru_w
        g_lru = proj[:, o:o + lru_w]; o += lru_w
        k_pe = proj[:, o:o + LANES]

        xl_ref[0, rows, :] = x_lru
        gl_ref[0, rows, :] = jax.nn.gelu(g_lru).astype(BF16)

        c, sa, sb = c_all[rows], sa_all[rows], sb_all[rows]
        qn = q_lat * _rms(q_lat, q_lora) * qa_ref[...]
        q_full = jnp.dot(qn.astype(BF16), wq_ref[...], preferred_element_type=F32)
        kvn = kv_lat * _rms(kv_lat, kv_lora) * kva_ref[...]
        kv_full = jnp.dot(kvn.astype(BF16), wkv_ref[...], preferred_element_type=F32)

        pe_ss = jnp.sum(k_pe * k_pe, axis=-1, keepdims=True)
        pe_rot = _rope(k_pe * kn_rope_ref[...], c, sa, sb)
        for hd in range(ATTN_HEADS):
            qh = q_full[:, hd * QK_PAD_DIM:(hd + 1) * QK_PAD_DIM]
            qh = qh * (_rms(qh, QK_HEAD_DIM) * sm_scale) * qn_ref[...]
            q_ref[0, hd, rows, 0:QK_NOPE_DIM] = qh[:, 0:QK_NOPE_DIM].astype(BF16)
            q_ref[0, hd, rows, QK_NOPE_DIM:QK_PAD_DIM] = _rope(qh[:, QK_NOPE_DIM:QK_PAD_DIM],
                                                               c, sa, sb).astype(BF16)
            base = hd * (QK_NOPE_DIM + V_HEAD_DIM)
            kn = kv_full[:, base:base + QK_NOPE_DIM]
            vv = kv_full[:, base + QK_NOPE_DIM:base + QK_NOPE_DIM + V_HEAD_DIM]
            r = lax.rsqrt((jnp.sum(kn * kn, axis=-1, keepdims=True) + pe_ss) * (1.0 / QK_HEAD_DIM) + EPS)
            k_ref[0, hd, rows, 0:QK_NOPE_DIM] = (kn * r * kn_nope_ref[...]).astype(BF16)
            k_ref[0, hd, rows, QK_NOPE_DIM:QK_PAD_DIM] = (pe_rot * r).astype(BF16)
            v_ref[0, hd, rows, :] = vv.astype(BF16)


def _in_proj(x, pos_row, mod, norm1_g, w_in_p, q_a_norm, w_q_p, kv_a_norm, w_kv, q_norm_p,
             k_norm_nope, k_norm_rope_p, inv_col, *, q_lora, kv_lora, lru_w):
    b, s, d = x.shape
    t = min(PROJ_TILE, s)
    full = lambda a: pl.BlockSpec(a.shape, lambda i, j: (0,) * a.ndim)
    kern = functools.partial(_in_proj_kernel, q_lora=q_lora, kv_lora=kv_lora, lru_w=lru_w,
                             chunk=min(PROJ_CHUNK, t))
    head_out = lambda w: pl.BlockSpec((1, ATTN_HEADS, t, w), lambda i, j: (i, 0, j, 0))
    return pl.pallas_call(
        kern,
        grid=(b, s // t),
        in_specs=[pl.BlockSpec((1, t, d), lambda i, j: (i, j, 0)),
                  pl.BlockSpec((1, 1, t), lambda i, j: (i, 0, j)),
                  pl.BlockSpec((1,) + mod.shape[1:], lambda i, j: (i, 0, 0)),
                  full(norm1_g), full(w_in_p), full(q_a_norm), full(w_q_p), full(kv_a_norm),
                  full(w_kv), full(q_norm_p), full(k_norm_nope), full(k_norm_rope_p), full(inv_col)],
        out_specs=[head_out(QK_PAD_DIM), head_out(QK_PAD_DIM), head_out(V_HEAD_DIM),
                   pl.BlockSpec((1, t, lru_w), lambda i, j: (i, j, 0)),
                   pl.BlockSpec((1, t, lru_w), lambda i, j: (i, j, 0))],
        out_shape=[jax.ShapeDtypeStruct((b, ATTN_HEADS, s, QK_PAD_DIM), BF16),
                   jax.ShapeDtypeStruct((b, ATTN_HEADS, s, QK_PAD_DIM), BF16),
                   jax.ShapeDtypeStruct((b, ATTN_HEADS, s, V_HEAD_DIM), BF16),
                   jax.ShapeDtypeStruct((b, s, lru_w), F32),
                   jax.ShapeDtypeStruct((b, s, lru_w), BF16)],
        compiler_params=_cparams("parallel", "parallel"),
        name="in_proj",
    )(x, pos_row, mod, norm1_g, w_in_p, q_a_norm, w_q_p, kv_a_norm, w_kv, q_norm_p,
      k_norm_nope, k_norm_rope_p, inv_col)


def _attn_kernel(q_ref, k_ref, v_ref, o_ref, m_sc, acc_sc, s_sc, v_sc, *, tile, chunk):
    qi = pl.program_id(2)

    @pl.when(qi == 0)
    def _():
        v_sc[:, 0:V_HEAD_DIM] = v_ref[0, 0]
        v_sc[:, V_HEAD_DIM:V_PAD_DIM] = jnp.ones((v_sc.shape[0], V_PAD_DIM - V_HEAD_DIM), BF16)

    m_sc[...] = jnp.full(m_sc.shape, -jnp.inf, F32)
    acc_sc[...] = jnp.zeros(acc_sc.shape, F32)

    def scores(j, slot):
        start = pl.multiple_of(j * tile, tile)
        k = k_ref[0, 0, pl.ds(start, tile), :]
        s_sc[slot] = lax.dot_general(q_ref[0, 0], k, (((1,), (1,)), ((), ())),
                                     preferred_element_type=F32)

    def accumulate(j, slot, masked):
        start = pl.multiple_of(j * tile, tile)
        for r in range(tile // chunk):
            rows = slice(r * chunk, (r + 1) * chunk)
            cols = -(-(r + 1) * chunk // LANES) * LANES if masked else tile
            s = s_sc[slot, rows, 0:cols]
            if masked:
                row = lax.broadcasted_iota(jnp.int32, s.shape, 0) + r * chunk
                col = lax.broadcasted_iota(jnp.int32, s.shape, 1)
                s = jnp.where(col <= row, s, -jnp.inf)
            v = v_sc[pl.ds(start, cols), :]
            m_old = m_sc[rows, :]
            m_new = jnp.maximum(m_old, jnp.max(s, axis=-1, keepdims=True))
            alpha = jnp.exp2(m_old - m_new)
            p = jnp.exp2(s - jnp.tile(m_new, (1, cols // LANES)))
            acc_sc[rows, :] = (jnp.tile(alpha, (1, V_PAD_DIM // LANES)) * acc_sc[rows, :]
                               + jnp.dot(p.astype(BF16), v, preferred_element_type=F32))
            m_sc[rows, :] = m_new

    scores(0, 0)

    def pair(p, carry):
        scores(2 * p + 1, 1)
        accumulate(2 * p, 0, False)
        scores(2 * p + 2, 0)
        accumulate(2 * p + 1, 1, False)
        return carry

    lax.fori_loop(0, qi // 2, pair, 0)

    @pl.when(qi % 2 == 0)
    def _():
        accumulate(qi, 0, True)

    @pl.when(qi % 2 == 1)
    def _():
        scores(qi, 1)
        accumulate(qi - 1, 0, False)
        accumulate(qi, 1, True)

    o_ref[0] = (acc_sc[:, 0:V_HEAD_DIM] / acc_sc[:, V_HEAD_DIM:2 * V_HEAD_DIM]).astype(o_ref.dtype)


def _attention(q, k, v):
    b, h, s, _ = q.shape
    t = min(ATTN_TILE, s)
    assert V_HEAD_DIM == LANES and V_PAD_DIM == 2 * V_HEAD_DIM
    kern = functools.partial(_attn_kernel, tile=t, chunk=min(ATTN_CHUNK, t))
    return pl.pallas_call(
        kern,
        grid=(b, h, s // t),
        in_specs=[pl.BlockSpec((1, 1, t, QK_PAD_DIM), lambda i, j, n: (i, j, n, 0)),
                  pl.BlockSpec((1, 1, s, QK_PAD_DIM), lambda i, j, n: (i, j, 0, 0)),
                  pl.BlockSpec((1, 1, s, V_HEAD_DIM), lambda i, j, n: (i, j, 0, 0))],
        out_specs=pl.BlockSpec((1, t, V_HEAD_DIM), lambda i, j, n: (i, n, j)),
        out_shape=jax.ShapeDtypeStruct((b, s, h * V_HEAD_DIM), BF16),
        scratch_shapes=[pltpu.VMEM((t, LANES), F32), pltpu.VMEM((t, V_PAD_DIM), F32),
                        pltpu.VMEM((2, t, t), F32), pltpu.VMEM((s, V_PAD_DIM), BF16)],
        compiler_params=_cparams("parallel", "parallel", "arbitrary"),
        name="attention",
    )(q, k, v)


def _lru_kernel(xl_ref, gl_ref, pos_ref, cw_ref, cb_ref, wax_ref, ba_ref, bx_ref, lam_ref, gn_ref,
                o_ref, ext_sc, a_sc, b_sc, h_sc, carry_sc):
    t, c = a_sc.shape
    head_w = c // LRU_HEADS

    @pl.when(pl.program_id(1) == 0)
    def _():
        ext_sc[0:SUBLANES, :] = jnp.zeros((SUBLANES, c), F32)
        carry_sc[...] = jnp.zeros(carry_sc.shape, F32)

    x = xl_ref[0]
    ext_sc[SUBLANES:SUBLANES + t, :] = x
    xc = cb_ref[...] + x * cw_ref[CONV_WIDTH - 1:CONV_WIDTH, :]
    for back in range(1, CONV_WIDTH):
        tap = CONV_WIDTH - 1 - back
        xc = xc + ext_sc[SUBLANES - back:SUBLANES - back + t, :] * cw_ref[tap:tap + 1, :]
    ext_sc[0:SUBLANES, :] = x[t - SUBLANES:t, :]

    reset = _column((pos_ref[0] == 0).astype(F32)) > 0.5
    lam = lam_ref[...]
    neg_sp = -LRU_C * (jnp.maximum(-lam, 0.0) + jnp.log1p(jnp.exp(-jnp.abs(lam))))
    for hd in range(LRU_HEADS):
        sl = slice(hd * head_w, (hd + 1) * head_w)
        xh = xc[:, sl]
        gates = jnp.dot(xh.astype(BF16), wax_ref[hd], preferred_element_type=F32)
        r = _sigmoid(gates[:, :head_w] + ba_ref[:, sl])
        i = _sigmoid(gates[:, head_w:] + bx_ref[:, sl])
        a = jnp.exp(r * neg_sp[:, sl])
        gap = jnp.maximum(1.0 - a * a, 0.0)
        mult = jnp.where(gap > 0.0, gap * lax.rsqrt(gap), 0.0)
        a = jnp.where(reset, 0.0, a)
        mult = jnp.where(reset, 1.0, mult)
        a_sc[:, sl] = a
        b_sc[:, sl] = mult * (i * xh)

    srow = lax.broadcasted_iota(jnp.int32, (SUBLANES, c), 0)

    def group(g, h_prev):
        start = pl.multiple_of(g * SUBLANES, SUBLANES)
        a = a_sc[pl.ds(start, SUBLANES), :]
        bb = b_sc[pl.ds(start, SUBLANES), :]
        sh = 1
        while sh < SUBLANES:
            keep = srow >= sh
            a_prev = jnp.where(keep, pltpu.roll(a, sh, 0), 1.0)
            b_prev = jnp.where(keep, pltpu.roll(bb, sh, 0), 0.0)
            bb = bb + a * b_prev
            a = a * a_prev
            sh *= 2
        hh = bb + a * h_prev
        h_sc[pl.ds(start, SUBLANES), :] = hh
        return jnp.broadcast_to(hh[SUBLANES - 1:SUBLANES, :], (SUBLANES, c))

    carry_sc[...] = lax.fori_loop(0, t // SUBLANES, group, carry_sc[...], unroll=4)

    y = h_sc[...] * gl_ref[0].astype(F32)
    o_ref[0] = (y * _rms(y, c) * gn_ref[...]).astype(o_ref.dtype)


def _rg_lru(x_lru, gelu_g, pos_row, conv_w, conv_b, w_ax, b_a, b_x, lam, out_norm):
    b, s, c = x_lru.shape
    t = min(LRU_TILE, s)
    full = lambda a: pl.BlockSpec(a.shape, lambda i, j: (0,) * a.ndim)
    return pl.pallas_call(
        _lru_kernel,
        grid=(b, s // t),
        in_specs=[pl.BlockSpec((1, t, c), lambda i, j: (i, j, 0)),
                  pl.BlockSpec((1, t, c), lambda i, j: (i, j, 0)),
                  pl.BlockSpec((1, 1, t), lambda i, j: (i, 0, j)),
                  full(conv_w), full(conv_b), full(w_ax), full(b_a), full(b_x), full(lam),
                  full(out_norm)],
        out_specs=pl.BlockSpec((1, t, c), lambda i, j: (i, j, 0)),
        out_shape=jax.ShapeDtypeStruct((b, s, c), BF16),
        scratch_shapes=[pltpu.VMEM((t + SUBLANES, c), F32), pltpu.VMEM((t, c), F32),
                        pltpu.VMEM((t, c), F32), pltpu.VMEM((t, c), F32),
                        pltpu.VMEM((SUBLANES, c), F32)],
        compiler_params=_cparams("parallel", "arbitrary"),
        name="rg_lru",
    )(x_lru, gelu_g, pos_row, conv_w, conv_b, w_ax, b_a, b_x, lam, out_norm)


def _route_kernel(x_ref, at_ref, lr_ref, mod_ref, an_ref, wa_ref, wl_ref, g2_ref, wr_ref, br_ref,
                  x1_ref, ids_ref, rank_ref, wts_ref, cnt_ref, carry_sc,
                  *, n_experts, group_row0, expert_row0):
    @pl.when(pl.program_id(0) == 0)
    def _():
        carry_sc[...] = jnp.zeros(carry_sc.shape, F32)

    x = x_ref[0]
    t, d = x.shape
    gate1 = mod_ref[0, 2:3, :]
    shift2 = mod_ref[0, 3:4, :]
    gain2 = g2_ref[...] * (1.0 + mod_ref[0, 4:5, :])
    at = at_ref[0].astype(F32)
    at_n = at * _rms(at, at.shape[-1]) * an_ref[...]
    mix = jnp.dot(at_n.astype(BF16), wa_ref[...], preferred_element_type=F32)
    mix = mix + jnp.dot(lr_ref[0], wl_ref[...], preferred_element_type=F32)
    x1 = x + gate1 * mix
    x1_ref[0] = x1
    h2 = x1 * _rms(x1, d) * gain2 + shift2

    logits = lax.dot_general(wr_ref[...], h2.astype(BF16), (((1,), (1,)), ((), ())),
                             preferred_element_type=F32) + br_ref[...]
    per_group = n_experts // N_GROUPS
    lg = logits[group_row0:group_row0 + N_GROUPS, :]
    grow = lax.broadcasted_iota(jnp.int32, lg.shape, 0)
    gmax = jnp.max(lg, axis=0, keepdims=True)
    g_idx = jnp.min(jnp.where(lg == gmax, grow, N_GROUPS), axis=0, keepdims=True)
    g_p = 1.0 / jnp.sum(jnp.exp(lg - gmax), axis=0, keepdims=True)
    sel = jnp.zeros((per_group, t), F32)
    for g in range(N_GROUPS):
        r0 = expert_row0 + g * per_group
        sel = sel + jnp.where(g_idx == g, logits[r0:r0 + per_group, :], 0.0)
    erow = lax.broadcasted_iota(jnp.int32, sel.shape, 0)
    m1 = jnp.max(sel, axis=0, keepdims=True)
    i1 = jnp.min(jnp.where(sel == m1, erow, per_group), axis=0, keepdims=True)
    sel2 = jnp.where(erow == i1, -jnp.inf, sel)
    m2 = jnp.max(sel2, axis=0, keepdims=True)
    i2 = jnp.min(jnp.where(sel2 == m2, erow, per_group), axis=0, keepdims=True)
    e21 = jnp.exp(m2 - m1)
    w1 = g_p / (1.0 + e21)
    w2 = g_p * e21 / (1.0 + e21)
    e1 = g_idx * per_group + i1
    e2 = g_idx * per_group + i2
    ids_ref[0:1, :] = e1
    ids_ref[1:2, :] = e2
    wts_ref[0:1, :] = w1
    wts_ref[1:2, :] = w2

    xrow = lax.broadcasted_iota(jnp.int32, (n_experts, t), 0)
    o1 = (xrow == e1).astype(F32)
    o2 = (xrow == e2).astype(F32)
    both = o1 + o2
    tr = lax.broadcasted_iota(jnp.int32, (t, t), 0)
    tc = lax.broadcasted_iota(jnp.int32, (t, t), 1)
    upper = jnp.where(tr < tc, 1.0, 0.0).astype(BF16)
    before = jnp.dot(both.astype(BF16), upper, preferred_element_type=F32) + carry_sc[...]
    rank_ref[0:1, :] = jnp.sum(o1 * before, axis=0, keepdims=True).astype(jnp.int32)
    rank_ref[1:2, :] = jnp.sum(o2 * before, axis=0, keepdims=True).astype(jnp.int32)
    carry_sc[...] = carry_sc[...] + jnp.sum(both, axis=1, keepdims=True)
    cnt_ref[...] = jnp.broadcast_to(carry_sc[...], cnt_ref.shape)


def _out_route(x, attn, lru_n, mod, attn_norm, w_out_a, w_out_l, norm2_g, w_r_t, b_r_col,
               *, n_experts, group_row0, expert_row0):
    b, s, d = x.shape
    t = min(ROUTE_TILE, s)
    nt = s // t
    n = b * s
    full = lambda a: pl.BlockSpec(a.shape, lambda i: (0,) * a.ndim)
    tok = lambda w: pl.BlockSpec((1, t, w), lambda i: (i // nt, i % nt, 0))
    kern = functools.partial(_route_kernel, n_experts=n_experts, group_row0=group_row0,
                             expert_row0=expert_row0)
    return pl.pallas_call(
        kern,
        grid=(b * nt,),
        in_specs=[tok(d), tok(attn.shape[-1]), tok(lru_n.shape[-1]),
                  pl.BlockSpec((1,) + mod.shape[1:], lambda i: (i // nt, 0, 0)),
                  full(attn_norm), full(w_out_a), full(w_out_l), full(norm2_g), full(w_r_t),
                  full(b_r_col)],
        out_specs=[tok(d),
                   pl.BlockSpec((TOP_K, t), lambda i: (0, i)),
                   pl.BlockSpec((TOP_K, t), lambda i: (0, i)),
                   pl.BlockSpec((TOP_K, t), lambda i: (0, i)),
                   pl.BlockSpec((n_experts, LANES), lambda i: (0, 0))],
        out_shape=[jax.ShapeDtypeStruct((b, s, d), F32),
                   jax.ShapeDtypeStruct((TOP_K, n), jnp.int32),
                   jax.ShapeDtypeStruct((TOP_K, n), jnp.int32),
                   jax.ShapeDtypeStruct((TOP_K, n), F32),
                   jax.ShapeDtypeStruct((n_experts, LANES), F32)],
        scratch_shapes=[pltpu.VMEM((n_experts, 1), F32)],
        compiler_params=_cparams("arbitrary"),
        name="out_route",
    )(x, attn, lru_n, mod, attn_norm, w_out_a, w_out_l, norm2_g, w_r_t, b_r_col)


def _row_copy(src, src_row, dst, dst_row, sem):
    return pltpu.make_async_copy(src.at[pl.ds(src_row, 1), :], dst.at[pl.ds(dst_row, 1), :], sem)


def _dispatch_kernel(cnt_ref, start_ref, nb_ref, dest_ref, x1_ref, mod_ref, g2_ref, xs_ref, h2_sc,
                     zero_sc, sem, *, tile, block, n_experts):
    i = pl.program_id(0)

    x1 = x1_ref[0]
    gain2 = g2_ref[...] * (1.0 + mod_ref[0, 4:5, :])
    h2_sc[...] = x1 * _rms(x1, x1.shape[-1]) * gain2 + mod_ref[0, 3:4, :]

    @pl.when(i == 0)
    def _():
        zero_sc[...] = jnp.zeros(zero_sc.shape, zero_sc.dtype)

        def tail_copy(j):
            return pltpu.make_async_copy(zero_sc, xs_ref.at[pl.ds(j * block, block), :], sem)

        def tail_issue(j, carry):
            tail_copy(j).start()
            return carry

        def tail_drain(j, carry):
            tail_copy(j).wait()
            return carry

        n_blocks = xs_ref.shape[0] // block
        lax.fori_loop(nb_ref[0], n_blocks, tail_issue, 0)
        lax.fori_loop(nb_ref[0], n_blocks, tail_drain, 0)

        def pad_copies(e, act):
            cnt = cnt_ref[e]
            first = start_ref[e] + cnt
            n_pad = (block - cnt % block) % block
            n_single = (SUBLANES - first % SUBLANES) % SUBLANES
            for r in range(SUBLANES - 1):
                @pl.when(r < n_single)
                def _(r=r):
                    act(_row_copy(zero_sc, 0, xs_ref, first + r, sem))

            row = first + n_single
            rest = n_pad - n_single
            for bit in range(SUBLANES.bit_length() - 1, block.bit_length() - 1):
                size = 1 << bit
                take = (rest >> bit) & 1

                @pl.when(take == 1)
                def _(row=row, size=size):
                    act(pltpu.make_async_copy(
                        zero_sc.at[pl.ds(0, size), :],
                        xs_ref.at[pl.ds(pl.multiple_of(row, SUBLANES), size), :], sem))

                row = row + take * size

        def pad_issue(e, carry):
            pad_copies(e, lambda cp: cp.start())
            return carry

        def pad_drain(e, carry):
            pad_copies(e, lambda cp: cp.wait())
            return carry

        lax.fori_loop(0, n_experts, pad_issue, 0)
        lax.fori_loop(0, n_experts, pad_drain, 0)

    for tk in range(tile):
        for k in range(TOP_K):
            _row_copy(h2_sc, tk, xs_ref, dest_ref[k, tk], sem).start(priority=k % 2)

    def drain(tk, carry):
        for _ in range(TOP_K):
            _row_copy(h2_sc, 0, xs_ref, 0, sem).wait()
        return carry

    lax.fori_loop(0, tile, drain, 0, unroll=8)


def _dispatch(counts, starts, n_blocks_used, dest, x1, mod, norm2_g, cap, *, block, n_experts):
    b, s, w = x1.shape
    t = min(DISPATCH_TILE, s)
    nt = s // t
    kern = functools.partial(_dispatch_kernel, tile=t, block=block, n_experts=n_experts)
    smem_tile = pl.BlockSpec((TOP_K, t), lambda i, c, st, nb: (0, i), memory_space=pltpu.SMEM)
    return pl.pallas_call(
        kern,
        grid_spec=pltpu.PrefetchScalarGridSpec(
            num_scalar_prefetch=3,
            grid=(b * nt,),
            in_specs=[smem_tile,
                      pl.BlockSpec((1, t, w), lambda i, c, st, nb: (i // nt, i % nt, 0)),
                      pl.BlockSpec((1,) + mod.shape[1:], lambda i, c, st, nb: (i // nt, 0, 0)),
                      pl.BlockSpec(norm2_g.shape, lambda i, c, st, nb: (0, 0))],
            out_specs=pl.BlockSpec(memory_space=pl.ANY),
            scratch_shapes=[pltpu.VMEM((t, w), F32), pltpu.VMEM((block, w), F32),
                            pltpu.SemaphoreType.DMA(())]),
        out_shape=jax.ShapeDtypeStruct((cap, w), F32),
        compiler_params=_cparams("arbitrary"),
        name="dispatch",
    )(counts, starts, n_blocks_used, dest, x1, mod, norm2_g)


def _expert_kernel(be_ref, nb_ref, xs_ref, wg_ref, wu_ref, wd_ref, ys_ref, wgu_sc, wdn_sc, *, d_ff):
    j = pl.program_id(0)
    live = jnp.minimum(j, nb_ref[0] - 1)

    @pl.when((j == 0) | (be_ref[live] != be_ref[jnp.maximum(live - 1, 0)]))
    def _():
        wgu_sc[:, 0:d_ff] = wg_ref[0].astype(BF16)
        wgu_sc[:, d_ff:2 * d_ff] = wu_ref[0].astype(BF16)
        wdn_sc[...] = wd_ref[0].astype(BF16)

    @pl.when(j < nb_ref[0])
    def _():
        xb = xs_ref[...].astype(BF16)
        gu = jnp.dot(xb, wgu_sc[...], preferred_element_type=F32)
        gt = gu[:, :d_ff]
        hid = gt * jax.nn.sigmoid(gt) * gu[:, d_ff:]
        ys_ref[...] = jnp.dot(hid.astype(BF16), wdn_sc[...], preferred_element_type=F32)

    @pl.when(pl.program_id(0) >= nb_ref[0])
    def _():
        ys_ref[...] = jnp.zeros(ys_ref.shape, ys_ref.dtype)


def _experts(block_e, n_blocks_used, xs, w_gate, w_up, w_down, *, block):
    cap, w = xs.shape
    d = w_down.shape[2]
    d_ff = w_down.shape[1]
    kern = functools.partial(_expert_kernel, d_ff=d_ff)
    live = lambda j, nb: jnp.minimum(j, nb[0] - 1)
    return pl.pallas_call(
        kern,
        grid_spec=pltpu.PrefetchScalarGridSpec(
            num_scalar_prefetch=2,
            grid=(cap // block,),
            in_specs=[pl.BlockSpec((block, w), lambda j, be, nb: (live(j, nb), 0)),
                      pl.BlockSpec((1, d, d_ff), lambda j, be, nb: (be[live(j, nb)], 0, 0)),
                      pl.BlockSpec((1, d, d_ff), lambda j, be, nb: (be[live(j, nb)], 0, 0)),
                      pl.BlockSpec((1, d_ff, d), lambda j, be, nb: (be[live(j, nb)], 0, 0))],
            out_specs=pl.BlockSpec((block, w), lambda j, be, nb: (j, 0)),
            scratch_shapes=[pltpu.VMEM((d, 2 * d_ff), BF16), pltpu.VMEM((d_ff, d), BF16)]),
        out_shape=jax.ShapeDtypeStruct((cap, w), F32),
        compiler_params=_cparams("arbitrary"),
        name="experts",
    )(block_e, n_blocks_used, xs, w_gate, w_up, w_down)


def _combine_kernel(dest_ref, dest_nx_ref, wts_ref, x1_ref, mod_ref, ys_ref, o_ref, buf, sem, *,
                    tile):
    i = pl.program_id(0)
    slot = i % 2

    def gather(d_ref, sl):
        for tk in range(tile):
            for k in range(TOP_K):
                _row_copy(ys_ref, d_ref[k, tk], buf.at[sl, k], tk, sem.at[sl]).start(priority=k % 2)

    @pl.when(i == 0)
    def _():
        gather(dest_ref, slot)

    @pl.when(i + 1 < pl.num_programs(0))
    def _():
        gather(dest_nx_ref, 1 - slot)

    for k in range(TOP_K):
        pltpu.make_async_copy(ys_ref.at[pl.ds(0, tile), :], buf.at[slot, k], sem.at[slot]).wait()

    d = o_ref.shape[-1]
    w0 = jnp.tile(_column(wts_ref[0:1, :]), (1, d // LANES))
    w1 = jnp.tile(_column(wts_ref[1:2, :]), (1, d // LANES))
    gate2 = mod_ref[0, 5:6, :]
    o_ref[0] = x1_ref[0] + gate2 * (w0 * buf[slot, 0] + w1 * buf[slot, 1])


def _combine(dest, wts, x1, mod, ys):
    b, s, d = x1.shape
    w = ys.shape[1]
    t = min(COMBINE_TILE, s)
    nt = s // t
    last = b * nt - 1
    kern = functools.partial(_combine_kernel, tile=t)
    smem_now = pl.BlockSpec((TOP_K, t), lambda i: (0, i), memory_space=pltpu.SMEM)
    smem_next = pl.BlockSpec((TOP_K, t), lambda i: (0, jnp.minimum(i + 1, last)),
                             memory_space=pltpu.SMEM)
    return pl.pallas_call(
        kern,
        grid=(b * nt,),
        in_specs=[smem_now, smem_next,
                  pl.BlockSpec((TOP_K, t), lambda i: (0, i)),
                  pl.BlockSpec((1, t, d), lambda i: (i // nt, i % nt, 0)),
                  pl.BlockSpec((1,) + mod.shape[1:], lambda i: (i // nt, 0, 0)),
                  pl.BlockSpec(memory_space=pl.ANY)],
        out_specs=pl.BlockSpec((1, t, d), lambda i: (i // nt, i % nt, 0)),
        out_shape=jax.ShapeDtypeStruct((b, s, d), F32),
        scratch_shapes=[pltpu.VMEM((2, TOP_K, t, w), F32), pltpu.SemaphoreType.DMA((2,))],
        compiler_params=_cparams("arbitrary"),
        name="combine",
    )(dest, dest, wts, x1, mod, ys)


def _layer(x, c, pos_row, w_ada, b_ada, norm1_g, w_in, q_a_norm, w_q_b, kv_a_norm, w_kv_b, q_norm,
           k_norm, conv_w, conv_b, lru_wa, lru_ba, lru_wx, lru_bx, lru_lambda, attn_out_norm,
           lru_out_norm, w_out, norm2_g, w_rg, b_rg, w_re, b_re, w_gate, w_up, w_down):
    b, s, d = x.shape
    n = b * s
    q_lora = q_a_norm.shape[0]
    kv_lora = kv_a_norm.shape[0]
    lru_w = lru_lambda.shape[0]
    attn_w = ATTN_HEADS * V_HEAD_DIM
    n_experts = w_gate.shape[0]
    row = lambda v: v.reshape(1, -1)

    mod = _adaln(c, w_ada, b_ada).reshape(b, 6, d)

    o1 = q_lora
    o2 = o1 + kv_lora
    o3 = o2 + QK_ROPE_DIM
    o4 = o3 + lru_w
    w_in_p = jnp.concatenate(
        [w_in[:, :o2], w_in[:, o3:o4], w_in[:, o4:], w_in[:, o2:o3],
         jnp.zeros((d, LANES - QK_ROPE_DIM), w_in.dtype)], axis=1).astype(BF16)
    pad_h = QK_PAD_DIM - QK_HEAD_DIM
    w_q_p = jnp.pad(w_q_b.reshape(q_lora, ATTN_HEADS, QK_HEAD_DIM),
                    ((0, 0), (0, 0), (0, pad_h))).reshape(q_lora, ATTN_HEADS * QK_PAD_DIM).astype(BF16)
    q_norm_p = row(jnp.pad(q_norm, (0, pad_h)))
    k_norm_nope = row(k_norm[:QK_NOPE_DIM])
    k_norm_rope_p = row(jnp.pad(k_norm[QK_NOPE_DIM:], (0, LANES - QK_ROPE_DIM)))
    inv = 1.0 / (ROPE_THETA ** (np.arange(0, QK_ROPE_DIM, 2, dtype=np.float32) / QK_ROPE_DIM))
    inv_col = jnp.asarray(inv.reshape(-1, 1))

    q, k, v, x_lru, gelu_g = _in_proj(
        x, pos_row, mod, row(norm1_g), w_in_p, row(q_a_norm), w_q_p, row(kv_a_norm),
        w_kv_b.astype(BF16), q_norm_p, k_norm_nope, k_norm_rope_p, inv_col,
        q_lora=q_lora, kv_lora=kv_lora, lru_w=lru_w)

    attn = _attention(q, k, v)

    w_ax = jnp.concatenate([lru_wa, lru_wx], axis=-1).astype(BF16)
    lru_n = _rg_lru(x_lru, gelu_g, pos_row, conv_w, row(conv_b), w_ax, row(lru_ba), row(lru_bx),
                    row(lru_lambda), row(lru_out_norm))

    group_row0, expert_row0 = 0, SUBLANES
    w_r_t = jnp.concatenate([w_rg.T, jnp.zeros((expert_row0 - N_GROUPS, d), w_rg.dtype), w_re.T],
                            axis=0).astype(BF16)
    b_r_col = jnp.concatenate([b_rg, jnp.zeros((expert_row0 - N_GROUPS,), b_rg.dtype),
                               b_re]).reshape(-1, 1)
    x1, ids, ranks, wts, cnt = _out_route(
        x, attn, lru_n, mod, row(attn_out_norm), w_out[:attn_w].astype(BF16),
        w_out[attn_w:].astype(BF16), row(norm2_g), w_r_t, b_r_col,
        n_experts=n_experts, group_row0=group_row0, expert_row0=expert_row0)

    blk = EXPERT_BLOCK
    counts = cnt[:, 0].astype(jnp.int32)
    padded = (counts + blk - 1) // blk * blk
    pad_ends = jnp.cumsum(padded)
    starts = (pad_ends - padded).astype(jnp.int32)
    ids_dense = ids.reshape(-1, LANES)
    seg_start = jnp.zeros_like(ids_dense)
    for e in range(n_experts):
        seg_start = seg_start + jnp.where(ids_dense == e, starts[e], 0)
    dest = seg_start.reshape(ids.shape) + ranks
    cap = (n * TOP_K + n_experts * (blk - 1) + blk - 1) // blk * blk
    n_blocks = cap // blk
    block_first_row = jnp.arange(n_blocks, dtype=jnp.int32) * blk
    block_e = jnp.minimum(jnp.sum(pad_ends[None, :] <= block_first_row[:, None], axis=1),
                          n_experts - 1).astype(jnp.int32)
    n_used = (pad_ends[-1:] // blk).astype(jnp.int32)

    xs = _dispatch(counts, starts, n_used, dest, x1, mod, row(norm2_g), cap,
                   block=blk, n_experts=n_experts)
    ys = _experts(block_e, n_used, xs, w_gate, w_up, w_down, block=blk)
    return _combine(dest, wts, x1, mod, ys)


def kernel(x, c, positions, w_ada, b_ada, norm1_g, w_in, q_a_norm, w_q_b, kv_a_norm, w_kv_b, q_norm, k_norm, conv_w, conv_b, lru_wa, lru_ba, lru_wx, lru_bx, lru_lambda, attn_out_norm, lru_out_norm, w_out, norm2_g, w_router_group, b_router_group, w_router_expert, b_router_expert, w_gate, w_up, w_down):
    b, s, _ = x.shape
    pos_row = positions.reshape(b, 1, s)
    for l in range(w_ada.shape[0]):
        x = _layer(x, c, pos_row, w_ada[l], b_ada[l], norm1_g[l], w_in[l], q_a_norm[l], w_q_b[l],
                   kv_a_norm[l], w_kv_b[l], q_norm[l], k_norm[l], conv_w[l], conv_b[l], lru_wa[l],
                   lru_ba[l], lru_wx[l], lru_bx[l], lru_lambda[l], attn_out_norm[l], lru_out_norm[l],
                   w_out[l], norm2_g[l], w_router_group[l], b_router_group[l], w_router_expert[l],
                   b_router_expert[l], w_gate[l], w_up[l], w_down[l])
    return x
```

```python
import functools
import math

import numpy as np
import jax
import jax.numpy as jnp
from jax import lax
from jax.experimental import pallas as pl
from jax.experimental.pallas import tpu as pltpu

F32 = jnp.float32
BF16 = jnp.bfloat16

ATTN_HEADS = 4
QK_NOPE_DIM = 128
QK_ROPE_DIM = 64
QK_HEAD_DIM = QK_NOPE_DIM + QK_ROPE_DIM
V_HEAD_DIM = 128
ROPE_THETA = 10000.0
LRU_HEADS = 4
CONV_WIDTH = 4
LRU_C = 8.0
N_GROUPS = 4
TOP_K = 2
EPS = 1e-6

LANES = 128
SUBLANES = 8
QK_PAD_DIM = 256
V_PAD_DIM = 256
VMEM_LIMIT = 56 * 1024 * 1024

PROJ_TILE = 1024
PROJ_CHUNK = 128
ATTN_TILE = 1024
ATTN_CHUNK = 512
LRU_TILE = 1024
ROUTE_TILE = 1024
DISPATCH_TILE = 512
COMBINE_TILE = 512
EXPERT_BLOCK = 512


def _cparams(*sem):
    return pltpu.CompilerParams(dimension_semantics=sem, vmem_limit_bytes=VMEM_LIMIT)


def _rms(x, n):
    return lax.rsqrt(jnp.sum(x * x, axis=-1, keepdims=True) * (1.0 / n) + EPS)


def _sigmoid(x):
    return 0.5 * jnp.tanh(0.5 * x) + 0.5


def _column(row, width=LANES):
    t = row.shape[-1]
    return jnp.broadcast_to(row, (width, t)).T


def _adaln_kernel(c_ref, w_ref, b_ref, o_ref):
    c = c_ref[...]
    sc = c * jax.nn.sigmoid(c)
    o_ref[...] = jnp.dot(sc.astype(BF16), w_ref[...].astype(BF16),
                         preferred_element_type=F32) + b_ref[...]


def _adaln(c, w_ada, b_ada):
    b, d = c.shape
    n_out = w_ada.shape[1]
    return pl.pallas_call(
        _adaln_kernel,
        grid=(n_out // d,),
        in_specs=[pl.BlockSpec((b, d), lambda j: (0, 0)),
                  pl.BlockSpec((d, d), lambda j: (0, j)),
                  pl.BlockSpec((1, d), lambda j: (0, j))],
        out_specs=pl.BlockSpec((b, d), lambda j: (0, j)),
        out_shape=jax.ShapeDtypeStruct((b, n_out), F32),
        compiler_params=_cparams("parallel"),
        name="adaln",
    )(c, w_ada, b_ada.reshape(1, n_out))


def _rope_tables(pos_row, inv_col):
    half = QK_ROPE_DIM // 2
    ang = inv_col * pos_row
    cos = jnp.cos(ang)
    sin = jnp.sin(ang)
    t = ang.shape[-1]
    zeros = lambda n: jnp.zeros((n, t), F32)
    c = jnp.concatenate([cos, cos, zeros(LANES - QK_ROPE_DIM)], axis=0)
    sa = jnp.concatenate([-sin, zeros(LANES - half)], axis=0)
    sb = jnp.concatenate([zeros(half), sin, zeros(LANES - QK_ROPE_DIM)], axis=0)
    return c.T, sa.T, sb.T


def _rope(x, c, sa, sb):
    return x * c + pltpu.roll(x, LANES - QK_ROPE_DIM // 2, 1) * sa + pltpu.roll(x, QK_ROPE_DIM // 2, 1) * sb


def _in_proj_kernel(x_ref, pos_ref, mod_ref, g1_ref, w_in_ref, qa_ref, wq_ref, kva_ref, wkv_ref,
                    qn_ref, kn_nope_ref, kn_rope_ref, inv_ref,
                    q_ref, k_ref, v_ref, xl_ref, gl_ref, *, q_lora, kv_lora, lru_w, chunk):
    t, d = x_ref.shape[1:]
    shift = mod_ref[0, 0:1, :]
    gain = g1_ref[...] * (1.0 + mod_ref[0, 1:2, :])
    c_all, sa_all, sb_all = _rope_tables(pos_ref[0].astype(F32), inv_ref[...])
    sm_scale = QK_HEAD_DIM ** -0.5 * math.log2(math.e)

    for r0 in range(0, t, chunk):
        rows = slice(r0, r0 + chunk)
        x = x_ref[0, rows, :]
        h = x * _rms(x, d) * gain + shift
        proj = jnp.dot(h.astype(BF16), w_in_ref[...], preferred_element_type=F32)
        o = 0
        q_lat = proj[:, o:o + q_lora]; o += q_lora
        kv_lat = proj[:, o:o + kv_lora]; o += kv_lora
        x_lru = proj[:, o:o + lru_w]; o += lru_w
        g_lru = proj[:, o:o + lru_w]; o += lru_w
        k_pe = proj[:, o:o + LANES]

        xl_ref[0, rows, :] = x_lru
        gl_ref[0, rows, :] = jax.nn.gelu(g_lru).astype(BF16)

        c, sa, sb = c_all[rows], sa_all[rows], sb_all[rows]
        qn = q_lat * _rms(q_lat, q_lora) * qa_ref[...]
        q_full = jnp.dot(qn.astype(BF16), wq_ref[...], preferred_element_type=F32)
        kvn = kv_lat * _rms(kv_lat, kv_lora) * kva_ref[...]
        kv_full = jnp.dot(kvn.astype(BF16), wkv_ref[...], preferred_element_type=F32)

        pe_ss = jnp.sum(k_pe * k_pe, axis=-1, keepdims=True)
        pe_rot = _rope(k_pe * kn_rope_ref[...], c, sa, sb)
        for hd in range(ATTN_HEADS):
            qh = q_full[:, hd * QK_PAD_DIM:(hd + 1) * QK_PAD_DIM]
            qh = qh * (_rms(qh, QK_HEAD_DIM) * sm_scale) * qn_ref[...]
            q_ref[0, hd, rows, 0:QK_NOPE_DIM] = qh[:, 0:QK_NOPE_DIM].astype(BF16)
            q_ref[0, hd, rows, QK_NOPE_DIM:QK_PAD_DIM] = _rope(qh[:, QK_NOPE_DIM:QK_PAD_DIM],
                                                               c, sa, sb).astype(BF16)
            base = hd * (QK_NOPE_DIM + V_HEAD_DIM)
            kn = kv_full[:, base:base + QK_NOPE_DIM]
            vv = kv_full[:, base + QK_NOPE_DIM:base + QK_NOPE_DIM + V_HEAD_DIM]
            r = lax.rsqrt((jnp.sum(kn * kn, axis=-1, keepdims=True) + pe_ss) * (1.0 / QK_HEAD_DIM) + EPS)
            k_ref[0, hd, rows, 0:QK_NOPE_DIM] = (kn * r * kn_nope_ref[...]).astype(BF16)
            k_ref[0, hd, rows, QK_NOPE_DIM:QK_PAD_DIM] = (pe_rot * r).astype(BF16)
            v_ref[0, hd, rows, :] = vv.astype(BF16)


def _in_proj(x, pos_row, mod, norm1_g, w_in_p, q_a_norm, w_q_p, kv_a_norm, w_kv, q_norm_p,
             k_norm_nope, k_norm_rope_p, inv_col, *, q_lora, kv_lora, lru_w):
    b, s, d = x.shape
    t = min(PROJ_TILE, s)
    full = lambda a: pl.BlockSpec(a.shape, lambda i, j: (0,) * a.ndim)
    kern = functools.partial(_in_proj_kernel, q_lora=q_lora, kv_lora=kv_lora, lru_w=lru_w,
                             chunk=min(PROJ_CHUNK, t))
    head_out = lambda w: pl.BlockSpec((1, ATTN_HEADS, t, w), lambda i, j: (i, 0, j, 0))
    return pl.pallas_call(
        kern,
        grid=(b, s // t),
        in_specs=[pl.BlockSpec((1, t, d), lambda i, j: (i, j, 0)),
                  pl.BlockSpec((1, 1, t), lambda i, j: (i, 0, j)),
                  pl.BlockSpec((1,) + mod.shape[1:], lambda i, j: (i, 0, 0)),
                  full(norm1_g), full(w_in_p), full(q_a_norm), full(w_q_p), full(kv_a_norm),
                  full(w_kv), full(q_norm_p), full(k_norm_nope), full(k_norm_rope_p), full(inv_col)],
        out_specs=[head_out(QK_PAD_DIM), head_out(QK_PAD_DIM), head_out(V_HEAD_DIM),
                   pl.BlockSpec((1, t, lru_w), lambda i, j: (i, j, 0)),
                   pl.BlockSpec((1, t, lru_w), lambda i, j: (i, j, 0))],
        out_shape=[jax.ShapeDtypeStruct((b, ATTN_HEADS, s, QK_PAD_DIM), BF16),
                   jax.ShapeDtypeStruct((b, ATTN_HEADS, s, QK_PAD_DIM), BF16),
                   jax.ShapeDtypeStruct((b, ATTN_HEADS, s, V_HEAD_DIM), BF16),
                   jax.ShapeDtypeStruct((b, s, lru_w), F32),
                   jax.ShapeDtypeStruct((b, s, lru_w), BF16)],
        compiler_params=_cparams("parallel", "parallel"),
        name="in_proj",
    )(x, pos_row, mod, norm1_g, w_in_p, q_a_norm, w_q_p, kv_a_norm, w_kv, q_norm_p,
      k_norm_nope, k_norm_rope_p, inv_col)


def _attn_kernel(q_ref, k_ref, v_ref, o_ref, m_sc, acc_sc, s_sc, v_sc, *, tile, chunk):
    qi = pl.program_id(2)

    @pl.when(qi == 0)
    def _():
        v_sc[:, 0:V_HEAD_DIM] = v_ref[0, 0]
        v_sc[:, V_HEAD_DIM:V_PAD_DIM] = jnp.ones((v_sc.shape[0], V_PAD_DIM - V_HEAD_DIM), BF16)

    m_sc[...] = jnp.full(m_sc.shape, -jnp.inf, F32)
    acc_sc[...] = jnp.zeros(acc_sc.shape, F32)

    def scores(j, slot):
        start = pl.multiple_of(j * tile, tile)
        k = k_ref[0, 0, pl.ds(start, tile), :]
        s_sc[slot] = lax.dot_general(q_ref[0, 0], k, (((1,), (1,)), ((), ())),
                                     preferred_element_type=F32)

    def accumulate(j, slot, masked):
        start = pl.multiple_of(j * tile, tile)
        for r in range(tile // chunk):
            rows = slice(r * chunk, (r + 1) * chunk)
            cols = -(-(r + 1) * chunk // LANES) * LANES if masked else tile
            s = s_sc[slot, rows, 0:cols]
            if masked:
                row = lax.broadcasted_iota(jnp.int32, s.shape, 0) + r * chunk
                col = lax.broadcasted_iota(jnp.int32, s.shape, 1)
                s = jnp.where(col <= row, s, -jnp.inf)
            v = v_sc[pl.ds(start, cols), :]
            m_old = m_sc[rows, :]
            m_new = jnp.maximum(m_old, jnp.max(s, axis=-1, keepdims=True))
            alpha = jnp.exp2(m_old - m_new)
            p = jnp.exp2(s - jnp.tile(m_new, (1, cols // LANES)))
            acc_sc[rows, :] = (jnp.tile(alpha, (1, V_PAD_DIM // LANES)) * acc_sc[rows, :]
                               + jnp.dot(p.astype(BF16), v, preferred_element_type=F32))
            m_sc[rows, :] = m_new

    scores(0, 0)

    def pair(p, carry):
        scores(2 * p + 1, 1)
        accumulate(2 * p, 0, False)
        scores(2 * p + 2, 0)
        accumulate(2 * p + 1, 1, False)
        return carry

    lax.fori_loop(0, qi // 2, pair, 0)

    @pl.when(qi % 2 == 0)
    def _():
        accumulate(qi, 0, True)

    @pl.when(qi % 2 == 1)
    def _():
        scores(qi, 1)
        accumulate(qi - 1, 0, False)
        accumulate(qi, 1, True)

    o_ref[0] = (acc_sc[:, 0:V_HEAD_DIM] / acc_sc[:, V_HEAD_DIM:2 * V_HEAD_DIM]).astype(o_ref.dtype)


def _attention(q, k, v):
    b, h, s, _ = q.shape
    t = min(ATTN_TILE, s)
    assert V_HEAD_DIM == LANES and V_PAD_DIM == 2 * V_HEAD_DIM
    kern = functools.partial(_attn_kernel, tile=t, chunk=min(ATTN_CHUNK, t))
    return pl.pallas_call(
        kern,
        grid=(b, h, s // t),
        in_specs=[pl.BlockSpec((1, 1, t, QK_PAD_DIM), lambda i, j, n: (i, j, n, 0)),
                  pl.BlockSpec((1, 1, s, QK_PAD_DIM), lambda i, j, n: (i, j, 0, 0)),
                  pl.BlockSpec((1, 1, s, V_HEAD_DIM), lambda i, j, n: (i, j, 0, 0))],
        out_specs=pl.BlockSpec((1, t, V_HEAD_DIM), lambda i, j, n: (i, n, j)),
        out_shape=jax.ShapeDtypeStruct((b, s, h * V_HEAD_DIM), BF16),
        scratch_shapes=[pltpu.VMEM((t, LANES), F32), pltpu.VMEM((t, V_PAD_DIM), F32),
                        pltpu.VMEM((2, t, t), F32), pltpu.VMEM((s, V_PAD_DIM), BF16)],
        compiler_params=_cparams("parallel", "parallel", "arbitrary"),
        name="attention",
    )(q, k, v)


def _lru_kernel(xl_ref, gl_ref, pos_ref, cw_ref, cb_ref, wax_ref, ba_ref, bx_ref, lam_ref, gn_ref,
                o_ref, ext_sc, a_sc, b_sc, h_sc, carry_sc):
    t, c = a_sc.shape
    head_w = c // LRU_HEADS

    @pl.when(pl.program_id(1) == 0)
    def _():
        ext_sc[0:SUBLANES, :] = jnp.zeros((SUBLANES, c), F32)
        carry_sc[...] = jnp.zeros(carry_sc.shape, F32)

    x = xl_ref[0]
    ext_sc[SUBLANES:SUBLANES + t, :] = x
    xc = cb_ref[...] + x * cw_ref[CONV_WIDTH - 1:CONV_WIDTH, :]
    for back in range(1, CONV_WIDTH):
        tap = CONV_WIDTH - 1 - back
        xc = xc + ext_sc[SUBLANES - back:SUBLANES - back + t, :] * cw_ref[tap:tap + 1, :]
    ext_sc[0:SUBLANES, :] = x[t - SUBLANES:t, :]

    reset = _column((pos_ref[0] == 0).astype(F32)) > 0.5
    lam = lam_ref[...]
    neg_sp = -LRU_C * (jnp.maximum(-lam, 0.0) + jnp.log1p(jnp.exp(-jnp.abs(lam))))
    for hd in range(LRU_HEADS):
        sl = slice(hd * head_w, (hd + 1) * head_w)
        xh = xc[:, sl]
        gates = jnp.dot(xh.astype(BF16), wax_ref[hd], preferred_element_type=F32)
        r = _sigmoid(gates[:, :head_w] + ba_ref[:, sl])
        i = _sigmoid(gates[:, head_w:] + bx_ref[:, sl])
        a = jnp.exp(r * neg_sp[:, sl])
        gap = jnp.maximum(1.0 - a * a, 0.0)
        mult = jnp.where(gap > 0.0, gap * lax.rsqrt(gap), 0.0)
        a = jnp.where(reset, 0.0, a)
        mult = jnp.where(reset, 1.0, mult)
        a_sc[:, sl] = a
        b_sc[:, sl] = mult * (i * xh)

    srow = lax.broadcasted_iota(jnp.int32, (SUBLANES, c), 0)

    def group(g, h_prev):
        start = pl.multiple_of(g * SUBLANES, SUBLANES)
        a = a_sc[pl.ds(start, SUBLANES), :]
        bb = b_sc[pl.ds(start, SUBLANES), :]
        sh = 1
        while sh < SUBLANES:
            keep = srow >= sh
            a_prev = jnp.where(keep, pltpu.roll(a, sh, 0), 1.0)
            b_prev = jnp.where(keep, pltpu.roll(bb, sh, 0), 0.0)
            bb = bb + a * b_prev
            a = a * a_prev
            sh *= 2
        hh = bb + a * h_prev
        h_sc[pl.ds(start, SUBLANES), :] = hh
        return jnp.broadcast_to(hh[SUBLANES - 1:SUBLANES, :], (SUBLANES, c))

    carry_sc[...] = lax.fori_loop(0, t // SUBLANES, group, carry_sc[...], unroll=4)

    y = h_sc[...] * gl_ref[0].astype(F32)
    o_ref[0] = (y * _rms(y, c) * gn_ref[...]).astype(o_ref.dtype)


def _rg_lru(x_lru, gelu_g, pos_row, conv_w, conv_b, w_ax, b_a, b_x, lam, out_norm):
    b, s, c = x_lru.shape
    t = min(LRU_TILE, s)
    full = lambda a: pl.BlockSpec(a.shape, lambda i, j: (0,) * a.ndim)
    return pl.pallas_call(
        _lru_kernel,
        grid=(b, s // t),
        in_specs=[pl.BlockSpec((1, t, c), lambda i, j: (i, j, 0)),
                  pl.BlockSpec((1, t, c), lambda i, j: (i, j, 0)),
                  pl.BlockSpec((1, 1, t), lambda i, j: (i, 0, j)),
                  full(conv_w), full(conv_b), full(w_ax), full(b_a), full(b_x), full(lam),
                  full(out_norm)],
        out_specs=pl.BlockSpec((1, t, c), lambda i, j: (i, j, 0)),
        out_shape=jax.ShapeDtypeStruct((b, s, c), BF16),
        scratch_shapes=[pltpu.VMEM((t + SUBLANES, c), F32), pltpu.VMEM((t, c), F32),
                        pltpu.VMEM((t, c), F32), pltpu.VMEM((t, c), F32),
                        pltpu.VMEM((SUBLANES, c), F32)],
        compiler_params=_cparams("parallel", "arbitrary"),
        name="rg_lru",
    )(x_lru, gelu_g, pos_row, conv_w, conv_b, w_ax, b_a, b_x, lam, out_norm)


def _route_kernel(x_ref, at_ref, lr_ref, mod_ref, an_ref, wa_ref, wl_ref, g2_ref, wr_ref, br_ref,
                  x1_ref, ids_ref, rank_ref, wts_ref, cnt_ref, carry_sc,
                  *, n_experts, group_row0, expert_row0):
    @pl.when(pl.program_id(0) == 0)
    def _():
        carry_sc[...] = jnp.zeros(carry_sc.shape, F32)

    x = x_ref[0]
    t, d = x.shape
    gate1 = mod_ref[0, 2:3, :]
    shift2 = mod_ref[0, 3:4, :]
    gain2 = g2_ref[...] * (1.0 + mod_ref[0, 4:5, :])
    at = at_ref[0].astype(F32)
    at_n = at * _rms(at, at.shape[-1]) * an_ref[...]
    mix = jnp.dot(at_n.astype(BF16), wa_ref[...], preferred_element_type=F32)
    mix = mix + jnp.dot(lr_ref[0], wl_ref[...], preferred_element_type=F32)
    x1 = x + gate1 * mix
    x1_ref[0] = x1
    h2 = x1 * _rms(x1, d) * gain2 + shift2

    logits = lax.dot_general(wr_ref[...], h2.astype(BF16), (((1,), (1,)), ((), ())),
                             preferred_element_type=F32) + br_ref[...]
    per_group = n_experts // N_GROUPS
    lg = logits[group_row0:group_row0 + N_GROUPS, :]
    grow = lax.broadcasted_iota(jnp.int32, lg.shape, 0)
    gmax = jnp.max(lg, axis=0, keepdims=True)
    g_idx = jnp.min(jnp.where(lg == gmax, grow, N_GROUPS), axis=0, keepdims=True)
    g_p = 1.0 / jnp.sum(jnp.exp(lg - gmax), axis=0, keepdims=True)
    sel = jnp.zeros((per_group, t), F32)
    for g in range(N_GROUPS):
        r0 = expert_row0 + g * per_group
        sel = sel + jnp.where(g_idx == g, logits[r0:r0 + per_group, :], 0.0)
    erow = lax.broadcasted_iota(jnp.int32, sel.shape, 0)
    m1 = jnp.max(sel, axis=0, keepdims=True)
    i1 = jnp.min(jnp.where(sel == m1, erow, per_group), axis=0, keepdims=True)
    sel2 = jnp.where(erow == i1, -jnp.inf, sel)
    m2 = jnp.max(sel2, axis=0, keepdims=True)
    i2 = jnp.min(jnp.where(sel2 == m2, erow, per_group), axis=0, keepdims=True)
    e21 = jnp.exp(m2 - m1)
    w1 = g_p / (1.0 + e21)
    w2 = g_p * e21 / (1.0 + e21)
    e1 = g_idx * per_group + i1
    e2 = g_idx * per_group + i2
    ids_ref[0:1, :] = e1
    ids_ref[1:2, :] = e2
    wts_ref[0:1, :] = w1
    wts_ref[1:2, :] = w2

    xrow = lax.broadcasted_iota(jnp.int32, (n_experts, t), 0)
    o1 = (xrow == e1).astype(F32)
    o2 = (xrow == e2).astype(F32)
    both = o1 + o2
    tr = lax.broadcasted_iota(jnp.int32, (t, t), 0)
    tc = lax.broadcasted_iota(jnp.int32, (t, t), 1)
    upper = jnp.where(tr < tc, 1.0, 0.0).astype(BF16)
    before = jnp.dot(both.astype(BF16), upper, preferred_element_type=F32) + carry_sc[...]
    rank_ref[0:1, :] = jnp.sum(o1 * before, axis=0, keepdims=True).astype(jnp.int32)
    rank_ref[1:2, :] = jnp.sum(o2 * before, axis=0, keepdims=True).astype(jnp.int32)
    carry_sc[...] = carry_sc[...] + jnp.sum(both, axis=1, keepdims=True)
    cnt_ref[...] = jnp.broadcast_to(carry_sc[...], cnt_ref.shape)


def _out_route(x, attn, lru_n, mod, attn_norm, w_out_a, w_out_l, norm2_g, w_r_t, b_r_col,
               *, n_experts, group_row0, expert_row0):
    b, s, d = x.shape
    t = min(ROUTE_TILE, s)
    nt = s // t
    n = b * s
    full = lambda a: pl.BlockSpec(a.shape, lambda i: (0,) * a.ndim)
    tok = lambda w: pl.BlockSpec((1, t, w), lambda i: (i // nt, i % nt, 0))
    kern = functools.partial(_route_kernel, n_experts=n_experts, group_row0=group_row0,
                             expert_row0=expert_row0)
    return pl.pallas_call(
        kern,
        grid=(b * nt,),
        in_specs=[tok(d), tok(attn.shape[-1]), tok(lru_n.shape[-1]),
                  pl.BlockSpec((1,) + mod.shape[1:], lambda i: (i // nt, 0, 0)),
                  full(attn_norm), full(w_out_a), full(w_out_l), full(norm2_g), full(w_r_t),
                  full(b_r_col)],
        out_specs=[tok(d),
                   pl.BlockSpec((TOP_K, t), lambda i: (0, i)),
                   pl.BlockSpec((TOP_K, t), lambda i: (0, i)),
                   pl.BlockSpec((TOP_K, t), lambda i: (0, i)),
                   pl.BlockSpec((n_experts, LANES), lambda i: (0, 0))],
        out_shape=[jax.ShapeDtypeStruct((b, s, d), F32),
                   jax.ShapeDtypeStruct((TOP_K, n), jnp.int32),
                   jax.ShapeDtypeStruct((TOP_K, n), jnp.int32),
                   jax.ShapeDtypeStruct((TOP_K, n), F32),
                   jax.ShapeDtypeStruct((n_experts, LANES), F32)],
        scratch_shapes=[pltpu.VMEM((n_experts, 1), F32)],
        compiler_params=_cparams("arbitrary"),
        name="out_route",
    )(x, attn, lru_n, mod, attn_norm, w_out_a, w_out_l, norm2_g, w_r_t, b_r_col)


def _slots_kernel(start_ref, ids_ref, rank_ref, o_ref):
    ids = ids_ref[...]
    slot = rank_ref[...]
    for e in range(start_ref.shape[0]):
        slot = slot + jnp.where(ids == e, start_ref[e], 0)
    o_ref[...] = slot


def _slots(starts, ids, ranks):
    whole = pl.BlockSpec(ids.shape, lambda i, st: (0, 0))
    return pl.pallas_call(
        _slots_kernel,
        grid_spec=pltpu.PrefetchScalarGridSpec(num_scalar_prefetch=1, grid=(1,),
                                               in_specs=[whole, whole], out_specs=whole),
        out_shape=jax.ShapeDtypeStruct(ids.shape, jnp.int32),
        compiler_params=_cparams("arbitrary"),
        name="slots",
    )(starts, ids, ranks)


def _row_copy(src, src_row, dst, dst_row, sem):
    return pltpu.make_async_copy(src.at[pl.ds(src_row, 1), :], dst.at[pl.ds(dst_row, 1), :], sem)


def _dispatch_kernel(cnt_ref, start_ref, nb_ref, dest_ref, x1_ref, mod_ref, g2_ref, xs_ref, h2_sc,
                     zero_sc, sem, *, tile, block, n_experts):
    i = pl.program_id(0)

    x1 = x1_ref[0]
    gain2 = g2_ref[...] * (1.0 + mod_ref[0, 4:5, :])
    h2_sc[...] = x1 * _rms(x1, x1.shape[-1]) * gain2 + mod_ref[0, 3:4, :]

    @pl.when(i == 0)
    def _():
        zero_sc[...] = jnp.zeros(zero_sc.shape, zero_sc.dtype)

        def tail_copy(j):
            return pltpu.make_async_copy(zero_sc, xs_ref.at[pl.ds(j * block, block), :], sem)

        def tail_issue(j, carry):
            tail_copy(j).start()
            return carry

        def tail_drain(j, carry):
            tail_copy(j).wait()
            return carry

        n_blocks = xs_ref.shape[0] // block
        lax.fori_loop(nb_ref[0], n_blocks, tail_issue, 0)
        lax.fori_loop(nb_ref[0], n_blocks, tail_drain, 0)

        def pad_copies(e, act):
            cnt = cnt_ref[e]
            first = start_ref[e] + cnt
            n_pad = (block - cnt % block) % block
            n_single = (SUBLANES - first % SUBLANES) % SUBLANES
            for r in range(SUBLANES - 1):
                @pl.when(r < n_single)
                def _(r=r):
                    act(_row_copy(zero_sc, 0, xs_ref, first + r, sem))

            row = first + n_single
            rest = n_pad - n_single
            for bit in range(SUBLANES.bit_length() - 1, block.bit_length() - 1):
                size = 1 << bit
                take = (rest >> bit) & 1

                @pl.when(take == 1)
                def _(row=row, size=size):
                    act(pltpu.make_async_copy(
                        zero_sc.at[pl.ds(0, size), :],
                        xs_ref.at[pl.ds(pl.multiple_of(row, SUBLANES), size), :], sem))

                row = row + take * size

        def pad_issue(e, carry):
            pad_copies(e, lambda cp: cp.start())
            return carry

        def pad_drain(e, carry):
            pad_copies(e, lambda cp: cp.wait())
            return carry

        lax.fori_loop(0, n_experts, pad_issue, 0)
        lax.fori_loop(0, n_experts, pad_drain, 0)

    for tk in range(tile):
        for k in range(TOP_K):
            _row_copy(h2_sc, tk, xs_ref, dest_ref[k, tk], sem).start(priority=k % 2)

    def drain(tk, carry):
        for _ in range(TOP_K):
            _row_copy(h2_sc, 0, xs_ref, 0, sem).wait()
        return carry

    lax.fori_loop(0, tile, drain, 0, unroll=8)


def _dispatch(counts, starts, n_blocks_used, dest, x1, mod, norm2_g, cap, *, block, n_experts):
    b, s, w = x1.shape
    t = min(DISPATCH_TILE, s)
    nt = s // t
    kern = functools.partial(_dispatch_kernel, tile=t, block=block, n_experts=n_experts)
    smem_tile = pl.BlockSpec((TOP_K, t), lambda i, c, st, nb: (0, i), memory_space=pltpu.SMEM)
    return pl.pallas_call(
        kern,
        grid_spec=pltpu.PrefetchScalarGridSpec(
            num_scalar_prefetch=3,
            grid=(b * nt,),
            in_specs=[smem_tile,
                      pl.BlockSpec((1, t, w), lambda i, c, st, nb: (i // nt, i % nt, 0)),
                      pl.BlockSpec((1,) + mod.shape[1:], lambda i, c, st, nb: (i // nt, 0, 0)),
                      pl.BlockSpec(norm2_g.shape, lambda i, c, st, nb: (0, 0))],
            out_specs=pl.BlockSpec(memory_space=pl.ANY),
            scratch_shapes=[pltpu.VMEM((t, w), F32), pltpu.VMEM((block, w), F32),
                            pltpu.SemaphoreType.DMA(())]),
        out_shape=jax.ShapeDtypeStruct((cap, w), F32),
        compiler_params=_cparams("arbitrary"),
        name="dispatch",
    )(counts, starts, n_blocks_used, dest, x1, mod, norm2_g)


def _expert_kernel(be_ref, nb_ref, xs_ref, wg_ref, wu_ref, wd_ref, ys_ref, wgu_sc, wdn_sc, *, d_ff):
    j = pl.program_id(0)
    live = jnp.minimum(j, nb_ref[0] - 1)

    @pl.when((j == 0) | (be_ref[live] != be_ref[jnp.maximum(live - 1, 0)]))
    def _():
        wgu_sc[:, 0:d_ff] = wg_ref[0].astype(BF16)
        wgu_sc[:, d_ff:2 * d_ff] = wu_ref[0].astype(BF16)
        wdn_sc[...] = wd_ref[0].astype(BF16)

    @pl.when(j < nb_ref[0])
    def _():
        xb = xs_ref[...].astype(BF16)
        gu = jnp.dot(xb, wgu_sc[...], preferred_element_type=F32)
        gt = gu[:, :d_ff]
        hid = gt * jax.nn.sigmoid(gt) * gu[:, d_ff:]
        ys_ref[...] = jnp.dot(hid.astype(BF16), wdn_sc[...], preferred_element_type=F32)

    @pl.when(pl.program_id(0) >= nb_ref[0])
    def _():
        ys_ref[...] = jnp.zeros(ys_ref.shape, ys_ref.dtype)


def _experts(block_e, n_blocks_used, xs, w_gate, w_up, w_down, *, block):
    cap, w = xs.shape
    d = w_down.shape[2]
    d_ff = w_down.shape[1]
    kern = functools.partial(_expert_kernel, d_ff=d_ff)
    live = lambda j, nb: jnp.minimum(j, nb[0] - 1)
    return pl.pallas_call(
        kern,
        grid_spec=pltpu.PrefetchScalarGridSpec(
            num_scalar_prefetch=2,
            grid=(cap // block,),
            in_specs=[pl.BlockSpec((block, w), lambda j, be, nb: (live(j, nb), 0)),
                      pl.BlockSpec((1, d, d_ff), lambda j, be, nb: (be[live(j, nb)], 0, 0)),
                      pl.BlockSpec((1, d, d_ff), lambda j, be, nb: (be[live(j, nb)], 0, 0)),
                      pl.BlockSpec((1, d_ff, d), lambda j, be, nb: (be[live(j, nb)], 0, 0))],
            out_specs=pl.BlockSpec((block, w), lambda j, be, nb: (j, 0)),
            scratch_shapes=[pltpu.VMEM((d, 2 * d_ff), BF16), pltpu.VMEM((d_ff, d), BF16)]),
        out_shape=jax.ShapeDtypeStruct((cap, w), F32),
        compiler_params=_cparams("arbitrary"),
        name="experts",
    )(block_e, n_blocks_used, xs, w_gate, w_up, w_down)


def _combine_kernel(dest_ref, dest_nx_ref, wts_ref, x1_ref, mod_ref, ys_ref, o_ref, buf, sem, *,
                    tile):
    i = pl.program_id(0)
    slot = i % 2

    def gather(d_ref, sl):
        for tk in range(tile):
            for k in range(TOP_K):
                _row_copy(ys_ref, d_ref[k, tk], buf.at[sl, k], tk, sem.at[sl]).start(priority=k % 2)

    @pl.when(i == 0)
    def _():
        gather(dest_ref, slot)

    @pl.when(i + 1 < pl.num_programs(0))
    def _():
        gather(dest_nx_ref, 1 - slot)

    for k in range(TOP_K):
        pltpu.make_async_copy(ys_ref.at[pl.ds(0, tile), :], buf.at[slot, k], sem.at[slot]).wait()

    d = o_ref.shape[-1]
    w0 = jnp.tile(_column(wts_ref[0:1, :]), (1, d // LANES))
    w1 = jnp.tile(_column(wts_ref[1:2, :]), (1, d // LANES))
    gate2 = mod_ref[0, 5:6, :]
    o_ref[0] = x1_ref[0] + gate2 * (w0 * buf[slot, 0] + w1 * buf[slot, 1])


def _combine(dest, wts, x1, mod, ys):
    b, s, d = x1.shape
    w = ys.shape[1]
    t = min(COMBINE_TILE, s)
    nt = s // t
    last = b * nt - 1
    kern = functools.partial(_combine_kernel, tile=t)
    smem_now = pl.BlockSpec((TOP_K, t), lambda i: (0, i), memory_space=pltpu.SMEM)
    smem_next = pl.BlockSpec((TOP_K, t), lambda i: (0, jnp.minimum(i + 1, last)),
                             memory_space=pltpu.SMEM)
    return pl.pallas_call(
        kern,
        grid=(b * nt,),
        in_specs=[smem_now, smem_next,
                  pl.BlockSpec((TOP_K, t), lambda i: (0, i)),
                  pl.BlockSpec((1, t, d), lambda i: (i // nt, i % nt, 0)),
                  pl.BlockSpec((1,) + mod.shape[1:], lambda i: (i // nt, 0, 0)),
                  pl.BlockSpec(memory_space=pl.ANY)],
        out_specs=pl.BlockSpec((1, t, d), lambda i: (i // nt, i % nt, 0)),
        out_shape=jax.ShapeDtypeStruct((b, s, d), F32),
        scratch_shapes=[pltpu.VMEM((2, TOP_K, t, w), F32), pltpu.SemaphoreType.DMA((2,))],
        compiler_params=_cparams("arbitrary"),
        name="combine",
    )(dest, dest, wts, x1, mod, ys)


def _layer(x, c, pos_row, w_ada, b_ada, norm1_g, w_in, q_a_norm, w_q_b, kv_a_norm, w_kv_b, q_norm,
           k_norm, conv_w, conv_b, lru_wa, lru_ba, lru_wx, lru_bx, lru_lambda, attn_out_norm,
           lru_out_norm, w_out, norm2_g, w_rg, b_rg, w_re, b_re, w_gate, w_up, w_down):
    b, s, d = x.shape
    n = b * s
    q_lora = q_a_norm.shape[0]
    kv_lora = kv_a_norm.shape[0]
    lru_w = lru_lambda.shape[0]
    attn_w = ATTN_HEADS * V_HEAD_DIM
    n_experts = w_gate.shape[0]
    row = lambda v: v.reshape(1, -1)

    mod = _adaln(c, w_ada, b_ada).reshape(b, 6, d)

    o1 = q_lora
    o2 = o1 + kv_lora
    o3 = o2 + QK_ROPE_DIM
    o4 = o3 + lru_w
    w_in_p = jnp.concatenate(
        [w_in[:, :o2], w_in[:, o3:o4], w_in[:, o4:], w_in[:, o2:o3],
         jnp.zeros((d, LANES - QK_ROPE_DIM), w_in.dtype)], axis=1).astype(BF16)
    pad_h = QK_PAD_DIM - QK_HEAD_DIM
    w_q_p = jnp.pad(w_q_b.reshape(q_lora, ATTN_HEADS, QK_HEAD_DIM),
                    ((0, 0), (0, 0), (0, pad_h))).reshape(q_lora, ATTN_HEADS * QK_PAD_DIM).astype(BF16)
    q_norm_p = row(jnp.pad(q_norm, (0, pad_h)))
    k_norm_nope = row(k_norm[:QK_NOPE_DIM])
    k_norm_rope_p = row(jnp.pad(k_norm[QK_NOPE_DIM:], (0, LANES - QK_ROPE_DIM)))
    inv = 1.0 / (ROPE_THETA ** (np.arange(0, QK_ROPE_DIM, 2, dtype=np.float32) / QK_ROPE_DIM))
    inv_col = jnp.asarray(inv.reshape(-1, 1))

    q, k, v, x_lru, gelu_g = _in_proj(
        x, pos_row, mod, row(norm1_g), w_in_p, row(q_a_norm), w_q_p, row(kv_a_norm),
        w_kv_b.astype(BF16), q_norm_p, k_norm_nope, k_norm_rope_p, inv_col,
        q_lora=q_lora, kv_lora=kv_lora, lru_w=lru_w)

    attn = _attention(q, k, v)

    w_ax = jnp.concatenate([lru_wa, lru_wx], axis=-1).astype(BF16)
    lru_n = _rg_lru(x_lru, gelu_g, pos_row, conv_w, row(conv_b), w_ax, row(lru_ba), row(lru_bx),
                    row(lru_lambda), row(lru_out_norm))

    group_row0, expert_row0 = 0, SUBLANES
    w_r_t = jnp.concatenate([w_rg.T, jnp.zeros((expert_row0 - N_GROUPS, d), w_rg.dtype), w_re.T],
                            axis=0).astype(BF16)
    b_r_col = jnp.concatenate([b_rg, jnp.zeros((expert_row0 - N_GROUPS,), b_rg.dtype),
                               b_re]).reshape(-1, 1)
    x1, ids, ranks, wts, cnt = _out_route(
        x, attn, lru_n, mod, row(attn_out_norm), w_out[:attn_w].astype(BF16),
        w_out[attn_w:].astype(BF16), row(norm2_g), w_r_t, b_r_col,
        n_experts=n_experts, group_row0=group_row0, expert_row0=expert_row0)

    blk = EXPERT_BLOCK
    counts = cnt[:, 0].astype(jnp.int32)
    padded = (counts + blk - 1) // blk * blk
    pad_ends = jnp.cumsum(padded)
    starts = (pad_ends - padded).astype(jnp.int32)
    dest = _slots(starts, ids.reshape(-1, LANES), ranks.reshape(-1, LANES)).reshape(ids.shape)
    cap = (n * TOP_K + n_experts * (blk - 1) + blk - 1) // blk * blk
    n_blocks = cap // blk
    block_first_row = jnp.arange(n_blocks, dtype=jnp.int32) * blk
    block_e = jnp.minimum(jnp.sum(pad_ends[None, :] <= block_first_row[:, None], axis=1),
                          n_experts - 1).astype(jnp.int32)
    n_used = (pad_ends[-1:] // blk).astype(jnp.int32)

    xs = _dispatch(counts, starts, n_used, dest, x1, mod, row(norm2_g), cap,
                   block=blk, n_experts=n_experts)
    ys = _experts(block_e, n_used, xs, w_gate, w_up, w_down, block=blk)
    return _combine(dest, wts, x1, mod, ys)


def kernel(x, c, positions, w_ada, b_ada, norm1_g, w_in, q_a_norm, w_q_b, kv_a_norm, w_kv_b, q_norm, k_norm, conv_w, conv_b, lru_wa, lru_ba, lru_wx, lru_bx, lru_lambda, attn_out_norm, lru_out_norm, w_out, norm2_g, w_router_group, b_router_group, w_router_expert, b_router_expert, w_gate, w_up, w_down):
    b, s, _ = x.shape
    pos_row = positions.reshape(b, 1, s)
    for l in range(w_ada.shape[0]):
        x = _layer(x, c, pos_row, w_ada[l], b_ada[l], norm1_g[l], w_in[l], q_a_norm[l], w_q_b[l],
                   kv_a_norm[l], w_kv_b[l], q_norm[l], k_norm[l], conv_w[l], conv_b[l], lru_wa[l],
                   lru_ba[l], lru_wx[l], lru_bx[l], lru_lambda[l], attn_out_norm[l], lru_out_norm[l],
                   w_out[l], norm2_g[l], w_router_group[l], b_router_group[l], w_router_expert[l],
                   b_router_expert[l], w_gate[l], w_up[l], w_down[l])
    return x
```

```python
import functools
import math

import numpy as np
import jax
import jax.numpy as jnp
from jax import lax
from jax.experimental import pallas as pl
from jax.experimental.pallas import tpu as pltpu

F32 = jnp.float32
BF16 = jnp.bfloat16

ATTN_HEADS = 4
QK_NOPE_DIM = 128
QK_ROPE_DIM = 64
QK_HEAD_DIM = QK_NOPE_DIM + QK_ROPE_DIM
V_HEAD_DIM = 128
ROPE_THETA = 10000.0
LRU_HEADS = 4
CONV_WIDTH = 4
LRU_C = 8.0
N_GROUPS = 4
TOP_K = 2
EPS = 1e-6

LANES = 128
SUBLANES = 8
QK_PAD_DIM = 256
V_PAD_DIM = 256
VMEM_LIMIT = 56 * 1024 * 1024

PROJ_TILE = 1024
PROJ_CHUNK = 128
ATTN_TILE = 1024
ATTN_CHUNK = 256
LRU_TILE = 1024
ROUTE_TILE = 1024
DISPATCH_TILE = 512
COMBINE_TILE = 512
EXPERT_BLOCK = 512


def _cparams(*sem):
    return pltpu.CompilerParams(dimension_semantics=sem, vmem_limit_bytes=VMEM_LIMIT)


def _rms(x, n):
    return lax.rsqrt(jnp.sum(x * x, axis=-1, keepdims=True) * (1.0 / n) + EPS)


def _sigmoid(x):
    return 0.5 * jnp.tanh(0.5 * x) + 0.5


def _column(row, width=LANES):
    t = row.shape[-1]
    return jnp.broadcast_to(row, (width, t)).T


def _adaln_kernel(c_ref, w_ref, b_ref, o_ref):
    c = c_ref[...]
    sc = c * jax.nn.sigmoid(c)
    o_ref[...] = jnp.dot(sc.astype(BF16), w_ref[...].astype(BF16),
                         preferred_element_type=F32) + b_ref[...]


def _adaln(c, w_ada, b_ada):
    b, d = c.shape
    n_out = w_ada.shape[1]
    return pl.pallas_call(
        _adaln_kernel,
        grid=(n_out // d,),
        in_specs=[pl.BlockSpec((b, d), lambda j: (0, 0)),
                  pl.BlockSpec((d, d), lambda j: (0, j)),
                  pl.BlockSpec((1, d), lambda j: (0, j))],
        out_specs=pl.BlockSpec((b, d), lambda j: (0, j)),
        out_shape=jax.ShapeDtypeStruct((b, n_out), F32),
        compiler_params=_cparams("parallel"),
        name="adaln",
    )(c, w_ada, b_ada.reshape(1, n_out))


def _rope_tables(pos_row, inv_col):
    half = QK_ROPE_DIM // 2
    ang = inv_col * pos_row
    cos = jnp.cos(ang)
    sin = jnp.sin(ang)
    t = ang.shape[-1]
    zeros = lambda n: jnp.zeros((n, t), F32)
    c = jnp.concatenate([cos, cos, zeros(LANES - QK_ROPE_DIM)], axis=0)
    sa = jnp.concatenate([-sin, zeros(LANES - half)], axis=0)
    sb = jnp.concatenate([zeros(half), sin, zeros(LANES - QK_ROPE_DIM)], axis=0)
    return c.T, sa.T, sb.T


def _rope(x, c, sa, sb):
    return x * c + pltpu.roll(x, LANES - QK_ROPE_DIM // 2, 1) * sa + pltpu.roll(x, QK_ROPE_DIM // 2, 1) * sb


def _in_proj_kernel(x_ref, pos_ref, mod_ref, g1_ref, w_in_ref, qa_ref, wq_ref, kva_ref, wkv_ref,
                    qn_ref, kn_nope_ref, kn_rope_ref, inv_ref,
                    q_ref, k_ref, v_ref, xl_ref, gl_ref, *, q_lora, kv_lora, lru_w, chunk):
    t, d = x_ref.shape[1:]
    shift = mod_ref[0, 0:1, :]
    gain = g1_ref[...] * (1.0 + mod_ref[0, 1:2, :])
    c_all, sa_all, sb_all = _rope_tables(pos_ref[0].astype(F32), inv_ref[...])
    sm_scale = QK_HEAD_DIM ** -0.5 * math.log2(math.e)
    ones = jnp.ones((chunk, V_PAD_DIM - V_HEAD_DIM), BF16)

    for r0 in range(0, t, chunk):
        rows = slice(r0, r0 + chunk)
        x = x_ref[0, rows, :]
        h = x * _rms(x, d) * gain + shift
        proj = jnp.dot(h.astype(BF16), w_in_ref[...], preferred_element_type=F32)
        o = 0
        q_lat = proj[:, o:o + q_lora]; o += q_lora
        kv_lat = proj[:, o:o + kv_lora]; o += kv_lora
        x_lru = proj[:, o:o + lru_w]; o += lru_w
        g_lru = proj[:, o:o + lru_w]; o += lru_w
        k_pe = proj[:, o:o + LANES]

        xl_ref[0, rows, :] = x_lru
        gl_ref[0, rows, :] = jax.nn.gelu(g_lru).astype(BF16)

        c, sa, sb = c_all[rows], sa_all[rows], sb_all[rows]
        qn = q_lat * _rms(q_lat, q_lora) * qa_ref[...]
        q_full = jnp.dot(qn.astype(BF16), wq_ref[...], preferred_element_type=F32)
        kvn = kv_lat * _rms(kv_lat, kv_lora) * kva_ref[...]
        kv_full = jnp.dot(kvn.astype(BF16), wkv_ref[...], preferred_element_type=F32)

        pe_ss = jnp.sum(k_pe * k_pe, axis=-1, keepdims=True)
        pe_rot = _rope(k_pe * kn_rope_ref[...], c, sa, sb)
        for hd in range(ATTN_HEADS):
            qh = q_full[:, hd * QK_PAD_DIM:(hd + 1) * QK_PAD_DIM]
            qh = qh * (_rms(qh, QK_HEAD_DIM) * sm_scale) * qn_ref[...]
            q_ref[0, hd, rows, 0:QK_NOPE_DIM] = qh[:, 0:QK_NOPE_DIM].astype(BF16)
            q_ref[0, hd, rows, QK_NOPE_DIM:QK_PAD_DIM] = _rope(qh[:, QK_NOPE_DIM:QK_PAD_DIM],
                                                               c, sa, sb).astype(BF16)
            base = hd * (QK_NOPE_DIM + V_HEAD_DIM)
            kn = kv_full[:, base:base + QK_NOPE_DIM]
            vv = kv_full[:, base + QK_NOPE_DIM:base + QK_NOPE_DIM + V_HEAD_DIM]
            r = lax.rsqrt((jnp.sum(kn * kn, axis=-1, keepdims=True) + pe_ss) * (1.0 / QK_HEAD_DIM) + EPS)
            k_ref[0, hd, rows, 0:QK_NOPE_DIM] = (kn * r * kn_nope_ref[...]).astype(BF16)
            k_ref[0, hd, rows, QK_NOPE_DIM:QK_PAD_DIM] = (pe_rot * r).astype(BF16)
            v_ref[0, hd, rows, 0:V_HEAD_DIM] = vv.astype(BF16)
            v_ref[0, hd, rows, V_HEAD_DIM:V_PAD_DIM] = ones


def _in_proj(x, pos_row, mod, norm1_g, w_in_p, q_a_norm, w_q_p, kv_a_norm, w_kv, q_norm_p,
             k_norm_nope, k_norm_rope_p, inv_col, *, q_lora, kv_lora, lru_w):
    b, s, d = x.shape
    t = min(PROJ_TILE, s)
    full = lambda a: pl.BlockSpec(a.shape, lambda i, j: (0,) * a.ndim)
    kern = functools.partial(_in_proj_kernel, q_lora=q_lora, kv_lora=kv_lora, lru_w=lru_w,
                             chunk=min(PROJ_CHUNK, t))
    head_out = lambda w: pl.BlockSpec((1, ATTN_HEADS, t, w), lambda i, j: (i, 0, j, 0))
    return pl.pallas_call(
        kern,
        grid=(b, s // t),
        in_specs=[pl.BlockSpec((1, t, d), lambda i, j: (i, j, 0)),
                  pl.BlockSpec((1, 1, t), lambda i, j: (i, 0, j)),
                  pl.BlockSpec((1,) + mod.shape[1:], lambda i, j: (i, 0, 0)),
                  full(norm1_g), full(w_in_p), full(q_a_norm), full(w_q_p), full(kv_a_norm),
                  full(w_kv), full(q_norm_p), full(k_norm_nope), full(k_norm_rope_p), full(inv_col)],
        out_specs=[head_out(QK_PAD_DIM), head_out(QK_PAD_DIM), head_out(V_PAD_DIM),
                   pl.BlockSpec((1, t, lru_w), lambda i, j: (i, j, 0)),
                   pl.BlockSpec((1, t, lru_w), lambda i, j: (i, j, 0))],
        out_shape=[jax.ShapeDtypeStruct((b, ATTN_HEADS, s, QK_PAD_DIM), BF16),
                   jax.ShapeDtypeStruct((b, ATTN_HEADS, s, QK_PAD_DIM), BF16),
                   jax.ShapeDtypeStruct((b, ATTN_HEADS, s, V_PAD_DIM), BF16),
                   jax.ShapeDtypeStruct((b, s, lru_w), F32),
                   jax.ShapeDtypeStruct((b, s, lru_w), BF16)],
        compiler_params=_cparams("parallel", "parallel"),
        name="in_proj",
    )(x, pos_row, mod, norm1_g, w_in_p, q_a_norm, w_q_p, kv_a_norm, w_kv, q_norm_p,
      k_norm_nope, k_norm_rope_p, inv_col)


def _attn_kernel(q_ref, k_ref, v_ref, o_ref, m_sc, acc_sc, s_sc, *, tile, chunk):
    qi = pl.program_id(2)
    m_sc[...] = jnp.full(m_sc.shape, -jnp.inf, F32)
    acc_sc[...] = jnp.zeros(acc_sc.shape, F32)

    def scores(j, slot):
        start = pl.multiple_of(j * tile, tile)
        k = k_ref[0, 0, pl.ds(start, tile), :]
        s_sc[slot] = lax.dot_general(q_ref[0, 0], k, (((1,), (1,)), ((), ())),
                                     preferred_element_type=F32)

    def accumulate(j, slot, masked):
        start = pl.multiple_of(j * tile, tile)
        for r in range(tile // chunk):
            rows = slice(r * chunk, (r + 1) * chunk)
            cols = -(-(r + 1) * chunk // LANES) * LANES if masked else tile
            s = s_sc[slot, rows, 0:cols]
            if masked:
                row = lax.broadcasted_iota(jnp.int32, s.shape, 0) + r * chunk
                col = lax.broadcasted_iota(jnp.int32, s.shape, 1)
                s = jnp.where(col <= row, s, -jnp.inf)
            v = v_ref[0, 0, pl.ds(start, cols), :]
            m_old = m_sc[rows, :]
            m_new = jnp.maximum(m_old, jnp.max(s, axis=-1, keepdims=True))
            alpha = jnp.exp2(m_old - m_new)
            p = jnp.exp2(s - jnp.tile(m_new, (1, cols // LANES)))
            acc_sc[rows, :] = (jnp.tile(alpha, (1, V_PAD_DIM // LANES)) * acc_sc[rows, :]
                               + jnp.dot(p.astype(BF16), v, preferred_element_type=F32))
            m_sc[rows, :] = m_new

    scores(0, 0)

    def pair(p, carry):
        scores(2 * p + 1, 1)
        accumulate(2 * p, 0, False)
        scores(2 * p + 2, 0)
        accumulate(2 * p + 1, 1, False)
        return carry

    lax.fori_loop(0, qi // 2, pair, 0)

    @pl.when(qi % 2 == 0)
    def _():
        accumulate(qi, 0, True)

    @pl.when(qi % 2 == 1)
    def _():
        scores(qi, 1)
        accumulate(qi - 1, 0, False)
        accumulate(qi, 1, True)

    o_ref[0] = (acc_sc[:, 0:V_HEAD_DIM] / acc_sc[:, V_HEAD_DIM:2 * V_HEAD_DIM]).astype(o_ref.dtype)


def _attention(q, k, v):
    b, h, s, _ = q.shape
    t = min(ATTN_TILE, s)
    assert V_HEAD_DIM == LANES and V_PAD_DIM == 2 * V_HEAD_DIM
    kern = functools.partial(_attn_kernel, tile=t, chunk=min(ATTN_CHUNK, t))
    return pl.pallas_call(
        kern,
        grid=(b, h, s // t),
        in_specs=[pl.BlockSpec((1, 1, t, QK_PAD_DIM), lambda i, j, n: (i, j, n, 0)),
                  pl.BlockSpec((1, 1, s, QK_PAD_DIM), lambda i, j, n: (i, j, 0, 0)),
                  pl.BlockSpec((1, 1, s, V_PAD_DIM), lambda i, j, n: (i, j, 0, 0))],
        out_specs=pl.BlockSpec((1, t, V_HEAD_DIM), lambda i, j, n: (i, n, j)),
        out_shape=jax.ShapeDtypeStruct((b, s, h * V_HEAD_DIM), BF16),
        scratch_shapes=[pltpu.VMEM((t, LANES), F32), pltpu.VMEM((t, V_PAD_DIM), F32),
                        pltpu.VMEM((2, t, t), F32)],
        compiler_params=_cparams("parallel", "parallel", "arbitrary"),
        name="attention",
    )(q, k, v)


def _lru_kernel(xl_ref, gl_ref, pos_ref, cw_ref, cb_ref, wax_ref, ba_ref, bx_ref, lam_ref, gn_ref,
                o_ref, ext_sc, a_sc, b_sc, h_sc, carry_sc):
    t, c = a_sc.shape
    head_w = c // LRU_HEADS

    @pl.when(pl.program_id(1) == 0)
    def _():
        ext_sc[0:SUBLANES, :] = jnp.zeros((SUBLANES, c), F32)
        carry_sc[...] = jnp.zeros(carry_sc.shape, F32)

    x = xl_ref[0]
    ext_sc[SUBLANES:SUBLANES + t, :] = x
    xc = cb_ref[...] + x * cw_ref[CONV_WIDTH - 1:CONV_WIDTH, :]
    for back in range(1, CONV_WIDTH):
        tap = CONV_WIDTH - 1 - back
        xc = xc + ext_sc[SUBLANES - back:SUBLANES - back + t, :] * cw_ref[tap:tap + 1, :]
    ext_sc[0:SUBLANES, :] = x[t - SUBLANES:t, :]

    reset = _column((pos_ref[0] == 0).astype(F32)) > 0.5
    lam = lam_ref[...]
    neg_sp = -LRU_C * (jnp.maximum(-lam, 0.0) + jnp.log1p(jnp.exp(-jnp.abs(lam))))
    for hd in range(LRU_HEADS):
        sl = slice(hd * head_w, (hd + 1) * head_w)
        xh = xc[:, sl]
        gates = jnp.dot(xh.astype(BF16), wax_ref[hd], preferred_element_type=F32)
        r = _sigmoid(gates[:, :head_w] + ba_ref[:, sl])
        i = _sigmoid(gates[:, head_w:] + bx_ref[:, sl])
        a = jnp.exp(r * neg_sp[:, sl])
        gap = jnp.maximum(1.0 - a * a, 0.0)
        mult = jnp.where(gap > 0.0, gap * lax.rsqrt(gap), 0.0)
        a = jnp.where(reset, 0.0, a)
        mult = jnp.where(reset, 1.0, mult)
        a_sc[:, sl] = a
        b_sc[:, sl] = mult * (i * xh)

    srow = lax.broadcasted_iota(jnp.int32, (SUBLANES, c), 0)

    def group(g, h_prev):
        start = pl.multiple_of(g * SUBLANES, SUBLANES)
        a = a_sc[pl.ds(start, SUBLANES), :]
        bb = b_sc[pl.ds(start, SUBLANES), :]
        sh = 1
        while sh < SUBLANES:
            keep = srow >= sh
            a_prev = jnp.where(keep, pltpu.roll(a, sh, 0), 1.0)
            b_prev = jnp.where(keep, pltpu.roll(bb, sh, 0), 0.0)
            bb = bb + a * b_prev
            a = a * a_prev
            sh *= 2
        hh = bb + a * h_prev
        h_sc[pl.ds(start, SUBLANES), :] = hh
        return jnp.broadcast_to(hh[SUBLANES - 1:SUBLANES, :], (SUBLANES, c))

    carry_sc[...] = lax.fori_loop(0, t // SUBLANES, group, carry_sc[...], unroll=4)

    y = h_sc[...] * gl_ref[0].astype(F32)
    o_ref[0] = (y * _rms(y, c) * gn_ref[...]).astype(o_ref.dtype)


def _rg_lru(x_lru, gelu_g, pos_row, conv_w, conv_b, w_ax, b_a, b_x, lam, out_norm):
    b, s, c = x_lru.shape
    t = min(LRU_TILE, s)
    full = lambda a: pl.BlockSpec(a.shape, lambda i, j: (0,) * a.ndim)
    return pl.pallas_call(
        _lru_kernel,
        grid=(b, s // t),
        in_specs=[pl.BlockSpec((1, t, c), lambda i, j: (i, j, 0)),
                  pl.BlockSpec((1, t, c), lambda i, j: (i, j, 0)),
                  pl.BlockSpec((1, 1, t), lambda i, j: (i, 0, j)),
                  full(conv_w), full(conv_b), full(w_ax), full(b_a), full(b_x), full(lam),
                  full(out_norm)],
        out_specs=pl.BlockSpec((1, t, c), lambda i, j: (i, j, 0)),
        out_shape=jax.ShapeDtypeStruct((b, s, c), BF16),
        scratch_shapes=[pltpu.VMEM((t + SUBLANES, c), F32), pltpu.VMEM((t, c), F32),
                        pltpu.VMEM((t, c), F32), pltpu.VMEM((t, c), F32),
                        pltpu.VMEM((SUBLANES, c), F32)],
        compiler_params=_cparams("parallel", "arbitrary"),
        name="rg_lru",
    )(x_lru, gelu_g, pos_row, conv_w, conv_b, w_ax, b_a, b_x, lam, out_norm)


def _route_kernel(x_ref, at_ref, lr_ref, mod_ref, an_ref, wa_ref, wl_ref, g2_ref, wr_ref, br_ref,
                  x1_ref, ids_ref, rank_ref, wts_ref, cnt_ref, carry_sc,
                  *, n_experts, group_row0, expert_row0):
    @pl.when(pl.program_id(0) == 0)
    def _():
        carry_sc[...] = jnp.zeros(carry_sc.shape, F32)

    x = x_ref[0]
    t, d = x.shape
    gate1 = mod_ref[0, 2:3, :]
    shift2 = mod_ref[0, 3:4, :]
    gain2 = g2_ref[...] * (1.0 + mod_ref[0, 4:5, :])
    at = at_ref[0].astype(F32)
    at_n = at * _rms(at, at.shape[-1]) * an_ref[...]
    mix = jnp.dot(at_n.astype(BF16), wa_ref[...], preferred_element_type=F32)
    mix = mix + jnp.dot(lr_ref[0], wl_ref[...], preferred_element_type=F32)
    x1 = x + gate1 * mix
    x1_ref[0] = x1
    h2 = x1 * _rms(x1, d) * gain2 + shift2

    logits = lax.dot_general(wr_ref[...], h2.astype(BF16), (((1,), (1,)), ((), ())),
                             preferred_element_type=F32) + br_ref[...]
    per_group = n_experts // N_GROUPS
    lg = logits[group_row0:group_row0 + N_GROUPS, :]
    grow = lax.broadcasted_iota(jnp.int32, lg.shape, 0)
    gmax = jnp.max(lg, axis=0, keepdims=True)
    g_idx = jnp.min(jnp.where(lg == gmax, grow, N_GROUPS), axis=0, keepdims=True)
    g_p = 1.0 / jnp.sum(jnp.exp(lg - gmax), axis=0, keepdims=True)
    sel = jnp.zeros((per_group, t), F32)
    for g in range(N_GROUPS):
        r0 = expert_row0 + g * per_group
        sel = sel + jnp.where(g_idx == g, logits[r0:r0 + per_group, :], 0.0)
    erow = lax.broadcasted_iota(jnp.int32, sel.shape, 0)
    m1 = jnp.max(sel, axis=0, keepdims=True)
    i1 = jnp.min(jnp.where(sel == m1, erow, per_group), axis=0, keepdims=True)
    sel2 = jnp.where(erow == i1, -jnp.inf, sel)
    m2 = jnp.max(sel2, axis=0, keepdims=True)
    i2 = jnp.min(jnp.where(sel2 == m2, erow, per_group), axis=0, keepdims=True)
    e21 = jnp.exp(m2 - m1)
    w1 = g_p / (1.0 + e21)
    w2 = g_p * e21 / (1.0 + e21)
    e1 = g_idx * per_group + i1
    e2 = g_idx * per_group + i2
    ids_ref[0:1, :] = e1
    ids_ref[1:2, :] = e2
    wts_ref[0:1, :] = w1
    wts_ref[1:2, :] = w2

    xrow = lax.broadcasted_iota(jnp.int32, (n_experts, t), 0)
    o1 = (xrow == e1).astype(F32)
    o2 = (xrow == e2).astype(F32)
    both = o1 + o2
    tr = lax.broadcasted_iota(jnp.int32, (t, t), 0)
    tc = lax.broadcasted_iota(jnp.int32, (t, t), 1)
    upper = jnp.where(tr < tc, 1.0, 0.0).astype(BF16)
    before = jnp.dot(both.astype(BF16), upper, preferred_element_type=F32) + carry_sc[...]
    rank_ref[0:1, :] = jnp.sum(o1 * before, axis=0, keepdims=True).astype(jnp.int32)
    rank_ref[1:2, :] = jnp.sum(o2 * before, axis=0, keepdims=True).astype(jnp.int32)
    carry_sc[...] = carry_sc[...] + jnp.sum(both, axis=1, keepdims=True)
    cnt_ref[...] = jnp.broadcast_to(carry_sc[...], cnt_ref.shape)


def _out_route(x, attn, lru_n, mod, attn_norm, w_out_a, w_out_l, norm2_g, w_r_t, b_r_col,
               *, n_experts, group_row0, expert_row0):
    b, s, d = x.shape
    t = min(ROUTE_TILE, s)
    nt = s // t
    n = b * s
    full = lambda a: pl.BlockSpec(a.shape, lambda i: (0,) * a.ndim)
    tok = lambda w: pl.BlockSpec((1, t, w), lambda i: (i // nt, i % nt, 0))
    kern = functools.partial(_route_kernel, n_experts=n_experts, group_row0=group_row0,
                             expert_row0=expert_row0)
    return pl.pallas_call(
        kern,
        grid=(b * nt,),
        in_specs=[tok(d), tok(attn.shape[-1]), tok(lru_n.shape[-1]),
                  pl.BlockSpec((1,) + mod.shape[1:], lambda i: (i // nt, 0, 0)),
                  full(attn_norm), full(w_out_a), full(w_out_l), full(norm2_g), full(w_r_t),
                  full(b_r_col)],
        out_specs=[tok(d),
                   pl.BlockSpec((TOP_K, t), lambda i: (0, i)),
                   pl.BlockSpec((TOP_K, t), lambda i: (0, i)),
                   pl.BlockSpec((TOP_K, t), lambda i: (0, i)),
                   pl.BlockSpec((n_experts, LANES), lambda i: (0, 0))],
        out_shape=[jax.ShapeDtypeStruct((b, s, d), F32),
                   jax.ShapeDtypeStruct((TOP_K, n), jnp.int32),
                   jax.ShapeDtypeStruct((TOP_K, n), jnp.int32),
                   jax.ShapeDtypeStruct((TOP_K, n), F32),
                   jax.ShapeDtypeStruct((n_experts, LANES), F32)],
        scratch_shapes=[pltpu.VMEM((n_experts, 1), F32)],
        compiler_params=_cparams("arbitrary"),
        name="out_route",
    )(x, attn, lru_n, mod, attn_norm, w_out_a, w_out_l, norm2_g, w_r_t, b_r_col)


def _slots_kernel(start_ref, ids_ref, rank_ref, o_ref):
    ids = ids_ref[...]
    slot = rank_ref[...]
    for e in range(start_ref.shape[0]):
        slot = slot + jnp.where(ids == e, start_ref[e], 0)
    o_ref[...] = slot


def _slots(starts, ids, ranks):
    whole = pl.BlockSpec(ids.shape, lambda i, st: (0, 0))
    return pl.pallas_call(
        _slots_kernel,
        grid_spec=pltpu.PrefetchScalarGridSpec(num_scalar_prefetch=1, grid=(1,),
                                               in_specs=[whole, whole], out_specs=whole),
        out_shape=jax.ShapeDtypeStruct(ids.shape, jnp.int32),
        compiler_params=_cparams("arbitrary"),
        name="slots",
    )(starts, ids, ranks)


def _row_copy(src, src_row, dst, dst_row, sem):
    return pltpu.make_async_copy(src.at[pl.ds(src_row, 1), :], dst.at[pl.ds(dst_row, 1), :], sem)


def _dispatch_kernel(cnt_ref, start_ref, nb_ref, dest_ref, x1_ref, mod_ref, g2_ref, xs_ref, h2_sc,
                     zero_sc, sem, *, tile, block, n_experts):
    i = pl.program_id(0)

    x1 = x1_ref[0]
    gain2 = g2_ref[...] * (1.0 + mod_ref[0, 4:5, :])
    h2_sc[...] = x1 * _rms(x1, x1.shape[-1]) * gain2 + mod_ref[0, 3:4, :]

    @pl.when(i == 0)
    def _():
        zero_sc[...] = jnp.zeros(zero_sc.shape, zero_sc.dtype)

        def tail_copy(j):
            return pltpu.make_async_copy(zero_sc, xs_ref.at[pl.ds(j * block, block), :], sem)

        def tail_issue(j, carry):
            tail_copy(j).start()
            return carry

        def tail_drain(j, carry):
            tail_copy(j).wait()
            return carry

        n_blocks = xs_ref.shape[0] // block
        lax.fori_loop(nb_ref[0], n_blocks, tail_issue, 0)
        lax.fori_loop(nb_ref[0], n_blocks, tail_drain, 0)

        def pad_copies(e, act):
            cnt = cnt_ref[e]
            first = start_ref[e] + cnt
            n_pad = (block - cnt % block) % block
            n_single = (SUBLANES - first % SUBLANES) % SUBLANES
            for r in range(SUBLANES - 1):
                @pl.when(r < n_single)
                def _(r=r):
                    act(_row_copy(zero_sc, 0, xs_ref, first + r, sem))

            row = first + n_single
            rest = n_pad - n_single
            for bit in range(SUBLANES.bit_length() - 1, block.bit_length() - 1):
                size = 1 << bit
                take = (rest >> bit) & 1

                @pl.when(take == 1)
                def _(row=row, size=size):
                    act(pltpu.make_async_copy(
                        zero_sc.at[pl.ds(0, size), :],
                        xs_ref.at[pl.ds(pl.multiple_of(row, SUBLANES), size), :], sem))

                row = row + take * size

        def pad_issue(e, carry):
            pad_copies(e, lambda cp: cp.start())
            return carry

        def pad_drain(e, carry):
            pad_copies(e, lambda cp: cp.wait())
            return carry

        lax.fori_loop(0, n_experts, pad_issue, 0)
        lax.fori_loop(0, n_experts, pad_drain, 0)

    for tk in range(tile):
        for k in range(TOP_K):
            _row_copy(h2_sc, tk, xs_ref, dest_ref[k, tk], sem).start(priority=k % 2)

    def drain(tk, carry):
        for _ in range(TOP_K):
            _row_copy(h2_sc, 0, xs_ref, 0, sem).wait()
        return carry

    lax.fori_loop(0, tile, drain, 0, unroll=8)


def _dispatch(counts, starts, n_blocks_used, dest, x1, mod, norm2_g, cap, *, block, n_experts):
    b, s, w = x1.shape
    t = min(DISPATCH_TILE, s)
    nt = s // t
    kern = functools.partial(_dispatch_kernel, tile=t, block=block, n_experts=n_experts)
    smem_tile = pl.BlockSpec((TOP_K, t), lambda i, c, st, nb: (0, i), memory_space=pltpu.SMEM)
    return pl.pallas_call(
        kern,
        grid_spec=pltpu.PrefetchScalarGridSpec(
            num_scalar_prefetch=3,
            grid=(b * nt,),
            in_specs=[smem_tile,
                      pl.BlockSpec((1, t, w), lambda i, c, st, nb: (i // nt, i % nt, 0)),
                      pl.BlockSpec((1,) + mod.shape[1:], lambda i, c, st, nb: (i // nt, 0, 0)),
                      pl.BlockSpec(norm2_g.shape, lambda i, c, st, nb: (0, 0))],
            out_specs=pl.BlockSpec(memory_space=pl.ANY),
            scratch_shapes=[pltpu.VMEM((t, w), F32), pltpu.VMEM((block, w), F32),
                            pltpu.SemaphoreType.DMA(())]),
        out_shape=jax.ShapeDtypeStruct((cap, w), F32),
        compiler_params=_cparams("arbitrary"),
        name="dispatch",
    )(counts, starts, n_blocks_used, dest, x1, mod, norm2_g)


def _expert_kernel(be_ref, nb_ref, xs_ref, wg_ref, wu_ref, wd_ref, ys_ref, wgu_sc, wdn_sc, *, d_ff):
    j = pl.program_id(0)
    live = jnp.minimum(j, nb_ref[0] - 1)

    @pl.when((j == 0) | (be_ref[live] != be_ref[jnp.maximum(live - 1, 0)]))
    def _():
        wgu_sc[:, 0:d_ff] = wg_ref[0].astype(BF16)
        wgu_sc[:, d_ff:2 * d_ff] = wu_ref[0].astype(BF16)
        wdn_sc[...] = wd_ref[0].astype(BF16)

    @pl.when(j < nb_ref[0])
    def _():
        xb = xs_ref[...].astype(BF16)
        gu = jnp.dot(xb, wgu_sc[...], preferred_element_type=F32)
        gt = gu[:, :d_ff]
        hid = gt * jax.nn.sigmoid(gt) * gu[:, d_ff:]
        ys_ref[...] = jnp.dot(hid.astype(BF16), wdn_sc[...], preferred_element_type=F32)

    @pl.when(pl.program_id(0) >= nb_ref[0])
    def _():
        ys_ref[...] = jnp.zeros(ys_ref.shape, ys_ref.dtype)


def _experts(block_e, n_blocks_used, xs, w_gate, w_up, w_down, *, block):
    cap, w = xs.shape
    d = w_down.shape[2]
    d_ff = w_down.shape[1]
    kern = functools.partial(_expert_kernel, d_ff=d_ff)
    live = lambda j, nb: jnp.minimum(j, nb[0] - 1)
    return pl.pallas_call(
        kern,
        grid_spec=pltpu.PrefetchScalarGridSpec(
            num_scalar_prefetch=2,
            grid=(cap // block,),
            in_specs=[pl.BlockSpec((block, w), lambda j, be, nb: (live(j, nb), 0)),
                      pl.BlockSpec((1, d, d_ff), lambda j, be, nb: (be[live(j, nb)], 0, 0)),
                      pl.BlockSpec((1, d, d_ff), lambda j, be, nb: (be[live(j, nb)], 0, 0)),
                      pl.BlockSpec((1, d_ff, d), lambda j, be, nb: (be[live(j, nb)], 0, 0))],
            out_specs=pl.BlockSpec((block, w), lambda j, be, nb: (j, 0)),
            scratch_shapes=[pltpu.VMEM((d, 2 * d_ff), BF16), pltpu.VMEM((d_ff, d), BF16)]),
        out_shape=jax.ShapeDtypeStruct((cap, w), F32),
        compiler_params=_cparams("arbitrary"),
        name="experts",
    )(block_e, n_blocks_used, xs, w_gate, w_up, w_down)


def _combine_kernel(dest_ref, dest_nx_ref, wts_ref, x1_ref, mod_ref, ys_ref, o_ref, buf, sem, *,
                    tile):
    i = pl.program_id(0)
    slot = i % 2

    def gather(d_ref, sl):
        for tk in range(tile):
            for k in range(TOP_K):
                _row_copy(ys_ref, d_ref[k, tk], buf.at[sl, k], tk, sem.at[sl]).start(priority=k % 2)

    @pl.when(i == 0)
    def _():
        gather(dest_ref, slot)

    @pl.when(i + 1 < pl.num_programs(0))
    def _():
        gather(dest_nx_ref, 1 - slot)

    for k in range(TOP_K):
        pltpu.make_async_copy(ys_ref.at[pl.ds(0, tile), :], buf.at[slot, k], sem.at[slot]).wait()

    d = o_ref.shape[-1]
    w0 = jnp.tile(_column(wts_ref[0:1, :]), (1, d // LANES))
    w1 = jnp.tile(_column(wts_ref[1:2, :]), (1, d // LANES))
    gate2 = mod_ref[0, 5:6, :]
    o_ref[0] = x1_ref[0] + gate2 * (w0 * buf[slot, 0] + w1 * buf[slot, 1])


def _combine(dest, wts, x1, mod, ys):
    b, s, d = x1.shape
    w = ys.shape[1]
    t = min(COMBINE_TILE, s)
    nt = s // t
    last = b * nt - 1
    kern = functools.partial(_combine_kernel, tile=t)
    smem_now = pl.BlockSpec((TOP_K, t), lambda i: (0, i), memory_space=pltpu.SMEM)
    smem_next = pl.BlockSpec((TOP_K, t), lambda i: (0, jnp.minimum(i + 1, last)),
                             memory_space=pltpu.SMEM)
    return pl.pallas_call(
        kern,
        grid=(b * nt,),
        in_specs=[smem_now, smem_next,
                  pl.BlockSpec((TOP_K, t), lambda i: (0, i)),
                  pl.BlockSpec((1, t, d), lambda i: (i // nt, i % nt, 0)),
                  pl.BlockSpec((1,) + mod.shape[1:], lambda i: (i // nt, 0, 0)),
                  pl.BlockSpec(memory_space=pl.ANY)],
        out_specs=pl.BlockSpec((1, t, d), lambda i: (i // nt, i % nt, 0)),
        out_shape=jax.ShapeDtypeStruct((b, s, d), F32),
        scratch_shapes=[pltpu.VMEM((2, TOP_K, t, w), F32), pltpu.SemaphoreType.DMA((2,))],
        compiler_params=_cparams("arbitrary"),
        name="combine",
    )(dest, dest, wts, x1, mod, ys)


def _layer(x, c, pos_row, w_ada, b_ada, norm1_g, w_in, q_a_norm, w_q_b, kv_a_norm, w_kv_b, q_norm,
           k_norm, conv_w, conv_b, lru_wa, lru_ba, lru_wx, lru_bx, lru_lambda, attn_out_norm,
           lru_out_norm, w_out, norm2_g, w_rg, b_rg, w_re, b_re, w_gate, w_up, w_down):
    b, s, d = x.shape
    n = b * s
    q_lora = q_a_norm.shape[0]
    kv_lora = kv_a_norm.shape[0]
    lru_w = lru_lambda.shape[0]
    attn_w = ATTN_HEADS * V_HEAD_DIM
    n_experts = w_gate.shape[0]
    row = lambda v: v.reshape(1, -1)

    mod = _adaln(c, w_ada, b_ada).reshape(b, 6, d)

    o1 = q_lora
    o2 = o1 + kv_lora
    o3 = o2 + QK_ROPE_DIM
    o4 = o3 + lru_w
    w_in_p = jnp.concatenate(
        [w_in[:, :o2], w_in[:, o3:o4], w_in[:, o4:], w_in[:, o2:o3],
         jnp.zeros((d, LANES - QK_ROPE_DIM), w_in.dtype)], axis=1).astype(BF16)
    pad_h = QK_PAD_DIM - QK_HEAD_DIM
    w_q_p = jnp.pad(w_q_b.reshape(q_lora, ATTN_HEADS, QK_HEAD_DIM),
                    ((0, 0), (0, 0), (0, pad_h))).reshape(q_lora, ATTN_HEADS * QK_PAD_DIM).astype(BF16)
    q_norm_p = row(jnp.pad(q_norm, (0, pad_h)))
    k_norm_nope = row(k_norm[:QK_NOPE_DIM])
    k_norm_rope_p = row(jnp.pad(k_norm[QK_NOPE_DIM:], (0, LANES - QK_ROPE_DIM)))
    inv = 1.0 / (ROPE_THETA ** (np.arange(0, QK_ROPE_DIM, 2, dtype=np.float32) / QK_ROPE_DIM))
    inv_col = jnp.asarray(inv.reshape(-1, 1))

    q, k, v, x_lru, gelu_g = _in_proj(
        x, pos_row, mod, row(norm1_g), w_in_p, row(q_a_norm), w_q_p, row(kv_a_norm),
        w_kv_b.astype(BF16), q_norm_p, k_norm_nope, k_norm_rope_p, inv_col,
        q_lora=q_lora, kv_lora=kv_lora, lru_w=lru_w)

    attn = _attention(q, k, v)

    w_ax = jnp.concatenate([lru_wa, lru_wx], axis=-1).astype(BF16)
    lru_n = _rg_lru(x_lru, gelu_g, pos_row, conv_w, row(conv_b), w_ax, row(lru_ba), row(lru_bx),
                    row(lru_lambda), row(lru_out_norm))

    group_row0, expert_row0 = 0, SUBLANES
    w_r_t = jnp.concatenate([w_rg.T, jnp.zeros((expert_row0 - N_GROUPS, d), w_rg.dtype), w_re.T],
                            axis=0).astype(BF16)
    b_r_col = jnp.concatenate([b_rg, jnp.zeros((expert_row0 - N_GROUPS,), b_rg.dtype),
                               b_re]).reshape(-1, 1)
    x1, ids, ranks, wts, cnt = _out_route(
        x, attn, lru_n, mod, row(attn_out_norm), w_out[:attn_w].astype(BF16),
        w_out[attn_w:].astype(BF16), row(norm2_g), w_r_t, b_r_col,
        n_experts=n_experts, group_row0=group_row0, expert_row0=expert_row0)

    blk = EXPERT_BLOCK
    counts = cnt[:, 0].astype(jnp.int32)
    padded = (counts + blk - 1) // blk * blk
    pad_ends = jnp.cumsum(padded)
    starts = (pad_ends - padded).astype(jnp.int32)
    dest = _slots(starts, ids.reshape(-1, LANES), ranks.reshape(-1, LANES)).reshape(ids.shape)
    cap = (n * TOP_K + n_experts * (blk - 1) + blk - 1) // blk * blk
    n_blocks = cap // blk
    block_first_row = jnp.arange(n_blocks, dtype=jnp.int32) * blk
    block_e = jnp.minimum(jnp.sum(pad_ends[None, :] <= block_first_row[:, None], axis=1),
                          n_experts - 1).astype(jnp.int32)
    n_used = (pad_ends[-1:] // blk).astype(jnp.int32)

    xs = _dispatch(counts, starts, n_used, dest, x1, mod, row(norm2_g), cap,
                   block=blk, n_experts=n_experts)
    ys = _experts(block_e, n_used, xs, w_gate, w_up, w_down, block=blk)
    return _combine(dest, wts, x1, mod, ys)


def kernel(x, c, positions, w_ada, b_ada, norm1_g, w_in, q_a_norm, w_q_b, kv_a_norm, w_kv_b, q_norm, k_norm, conv_w, conv_b, lru_wa, lru_ba, lru_wx, lru_bx, lru_lambda, attn_out_norm, lru_out_norm, w_out, norm2_g, w_router_group, b_router_group, w_router_expert, b_router_expert, w_gate, w_up, w_down):
    b, s, _ = x.shape
    pos_row = positions.reshape(b, 1, s)
    for l in range(w_ada.shape[0]):
        x = _layer(x, c, pos_row, w_ada[l], b_ada[l], norm1_g[l], w_in[l], q_a_norm[l], w_q_b[l],
                   kv_a_norm[l], w_kv_b[l], q_norm[l], k_norm[l], conv_w[l], conv_b[l], lru_wa[l],
                   lru_ba[l], lru_wx[l], lru_bx[l], lru_lambda[l], attn_out_norm[l], lru_out_norm[l],
                   w_out[l], norm2_g[l], w_router_group[l], b_router_group[l], w_router_expert[l],
                   b_router_expert[l], w_gate[l], w_up[l], w_down[l])
    return x
```

```python
import functools
import math

import numpy as np
import jax
import jax.numpy as jnp
from jax import lax
from jax.experimental import pallas as pl
from jax.experimental.pallas import tpu as pltpu

F32 = jnp.float32
BF16 = jnp.bfloat16

ATTN_HEADS = 4
QK_NOPE_DIM = 128
QK_ROPE_DIM = 64
QK_HEAD_DIM = QK_NOPE_DIM + QK_ROPE_DIM
V_HEAD_DIM = 128
ROPE_THETA = 10000.0
LRU_HEADS = 4
CONV_WIDTH = 4
LRU_C = 8.0
N_GROUPS = 4
TOP_K = 2
EPS = 1e-6

LANES = 128
SUBLANES = 8
QK_PAD_DIM = 256
V_PAD_DIM = 256
VMEM_LIMIT = 56 * 1024 * 1024

PROJ_TILE = 1024
PROJ_CHUNK = 128
ATTN_TILE = 1024
ATTN_CHUNK = 256
LRU_TILE = 1024
ROUTE_TILE = 1024
DISPATCH_TILE = 512
COMBINE_TILE = 512
EXPERT_BLOCK = 512


def _cparams(*sem):
    return pltpu.CompilerParams(dimension_semantics=sem, vmem_limit_bytes=VMEM_LIMIT)


def _rms(x, n):
    return lax.rsqrt(jnp.sum(x * x, axis=-1, keepdims=True) * (1.0 / n) + EPS)


def _sigmoid(x):
    return 0.5 * jnp.tanh(0.5 * x) + 0.5


def _column(row, width=LANES):
    t = row.shape[-1]
    return jnp.broadcast_to(row, (width, t)).T


def _adaln_kernel(c_ref, w_ref, b_ref, o_ref):
    c = c_ref[...]
    sc = c * jax.nn.sigmoid(c)
    o_ref[...] = jnp.dot(sc.astype(BF16), w_ref[...].astype(BF16),
                         preferred_element_type=F32) + b_ref[...]


def _adaln(c, w_ada, b_ada):
    b, d = c.shape
    n_out = w_ada.shape[1]
    return pl.pallas_call(
        _adaln_kernel,
        grid=(n_out // d,),
        in_specs=[pl.BlockSpec((b, d), lambda j: (0, 0)),
                  pl.BlockSpec((d, d), lambda j: (0, j)),
                  pl.BlockSpec((1, d), lambda j: (0, j))],
        out_specs=pl.BlockSpec((b, d), lambda j: (0, j)),
        out_shape=jax.ShapeDtypeStruct((b, n_out), F32),
        compiler_params=_cparams("parallel"),
        name="adaln",
    )(c, w_ada, b_ada.reshape(1, n_out))


def _rope_tables(pos_row, inv_col):
    half = QK_ROPE_DIM // 2
    ang = inv_col * pos_row
    cos = jnp.cos(ang)
    sin = jnp.sin(ang)
    t = ang.shape[-1]
    zeros = lambda n: jnp.zeros((n, t), F32)
    c = jnp.concatenate([cos, cos, zeros(LANES - QK_ROPE_DIM)], axis=0)
    sa = jnp.concatenate([-sin, zeros(LANES - half)], axis=0)
    sb = jnp.concatenate([zeros(half), sin, zeros(LANES - QK_ROPE_DIM)], axis=0)
    return c.T, sa.T, sb.T


def _rope(x, c, sa, sb):
    return x * c + pltpu.roll(x, LANES - QK_ROPE_DIM // 2, 1) * sa + pltpu.roll(x, QK_ROPE_DIM // 2, 1) * sb


def _in_proj_kernel(x_ref, pos_ref, mod_ref, g1_ref, w_in_ref, qa_ref, wq_ref, kva_ref, wkv_ref,
                    qn_ref, kn_nope_ref, kn_rope_ref, inv_ref,
                    q_ref, k_ref, v_ref, xl_ref, gl_ref, *, q_lora, kv_lora, lru_w, chunk):
    t, d = x_ref.shape[1:]
    shift = mod_ref[0, 0:1, :]
    gain = g1_ref[...] * (1.0 + mod_ref[0, 1:2, :])
    c_all, sa_all, sb_all = _rope_tables(pos_ref[0].astype(F32), inv_ref[...])
    sm_scale = QK_HEAD_DIM ** -0.5 * math.log2(math.e)
    ones = jnp.ones((chunk, V_PAD_DIM - V_HEAD_DIM), BF16)

    for r0 in range(0, t, chunk):
        rows = slice(r0, r0 + chunk)
        x = x_ref[0, rows, :]
        h = x * _rms(x, d) * gain + shift
        proj = jnp.dot(h.astype(BF16), w_in_ref[...], preferred_element_type=F32)
        o = 0
        q_lat = proj[:, o:o + q_lora]; o += q_lora
        kv_lat = proj[:, o:o + kv_lora]; o += kv_lora
        x_lru = proj[:, o:o + lru_w]; o += lru_w
        g_lru = proj[:, o:o + lru_w]; o += lru_w
        k_pe = proj[:, o:o + LANES]

        xl_ref[0, rows, :] = x_lru
        gl_ref[0, rows, :] = jax.nn.gelu(g_lru).astype(BF16)

        c, sa, sb = c_all[rows], sa_all[rows], sb_all[rows]
        qn = q_lat * _rms(q_lat, q_lora) * qa_ref[...]
        q_full = jnp.dot(qn.astype(BF16), wq_ref[...], preferred_element_type=F32)
        kvn = kv_lat * _rms(kv_lat, kv_lora) * kva_ref[...]
        kv_full = jnp.dot(kvn.astype(BF16), wkv_ref[...], preferred_element_type=F32)

        pe_ss = jnp.sum(k_pe * k_pe, axis=-1, keepdims=True)
        pe_rot = _rope(k_pe * kn_rope_ref[...], c, sa, sb)
        for hd in range(ATTN_HEADS):
            qh = q_full[:, hd * QK_PAD_DIM:(hd + 1) * QK_PAD_DIM]
            qh = qh * (_rms(qh, QK_HEAD_DIM) * sm_scale) * qn_ref[...]
            q_ref[0, hd, rows, 0:QK_NOPE_DIM] = qh[:, 0:QK_NOPE_DIM].astype(BF16)
            q_ref[0, hd, rows, QK_NOPE_DIM:QK_PAD_DIM] = _rope(qh[:, QK_NOPE_DIM:QK_PAD_DIM],
                                                               c, sa, sb).astype(BF16)
            base = hd * (QK_NOPE_DIM + V_HEAD_DIM)
            kn = kv_full[:, base:base + QK_NOPE_DIM]
            vv = kv_full[:, base + QK_NOPE_DIM:base + QK_NOPE_DIM + V_HEAD_DIM]
            r = lax.rsqrt((jnp.sum(kn * kn, axis=-1, keepdims=True) + pe_ss) * (1.0 / QK_HEAD_DIM) + EPS)
            k_ref[0, hd, rows, 0:QK_NOPE_DIM] = (kn * r * kn_nope_ref[...]).astype(BF16)
            k_ref[0, hd, rows, QK_NOPE_DIM:QK_PAD_DIM] = (pe_rot * r).astype(BF16)
            v_ref[0, hd, rows, 0:V_HEAD_DIM] = vv.astype(BF16)
            v_ref[0, hd, rows, V_HEAD_DIM:V_PAD_DIM] = ones


def _in_proj(x, pos_row, mod, norm1_g, w_in_p, q_a_norm, w_q_p, kv_a_norm, w_kv, q_norm_p,
             k_norm_nope, k_norm_rope_p, inv_col, *, q_lora, kv_lora, lru_w):
    b, s, d = x.shape
    t = min(PROJ_TILE, s)
    full = lambda a: pl.BlockSpec(a.shape, lambda i, j: (0,) * a.ndim)
    kern = functools.partial(_in_proj_kernel, q_lora=q_lora, kv_lora=kv_lora, lru_w=lru_w,
                             chunk=min(PROJ_CHUNK, t))
    head_out = lambda w: pl.BlockSpec((1, ATTN_HEADS, t, w), lambda i, j: (i, 0, j, 0))
    return pl.pallas_call(
        kern,
        grid=(b, s // t),
        in_specs=[pl.BlockSpec((1, t, d), lambda i, j: (i, j, 0)),
                  pl.BlockSpec((1, 1, t), lambda i, j: (i, 0, j)),
                  pl.BlockSpec((1,) + mod.shape[1:], lambda i, j: (i, 0, 0)),
                  full(norm1_g), full(w_in_p), full(q_a_norm), full(w_q_p), full(kv_a_norm),
                  full(w_kv), full(q_norm_p), full(k_norm_nope), full(k_norm_rope_p), full(inv_col)],
        out_specs=[head_out(QK_PAD_DIM), head_out(QK_PAD_DIM), head_out(V_PAD_DIM),
                   pl.BlockSpec((1, t, lru_w), lambda i, j: (i, j, 0)),
                   pl.BlockSpec((1, t, lru_w), lambda i, j: (i, j, 0))],
        out_shape=[jax.ShapeDtypeStruct((b, ATTN_HEADS, s, QK_PAD_DIM), BF16),
                   jax.ShapeDtypeStruct((b, ATTN_HEADS, s, QK_PAD_DIM), BF16),
                   jax.ShapeDtypeStruct((b, ATTN_HEADS, s, V_PAD_DIM), BF16),
                   jax.ShapeDtypeStruct((b, s, lru_w), F32),
                   jax.ShapeDtypeStruct((b, s, lru_w), BF16)],
        compiler_params=_cparams("parallel", "parallel"),
        name="in_proj",
    )(x, pos_row, mod, norm1_g, w_in_p, q_a_norm, w_q_p, kv_a_norm, w_kv, q_norm_p,
      k_norm_nope, k_norm_rope_p, inv_col)


def _attn_kernel(q_ref, k_ref, v_ref, o_ref, m_sc, acc_sc, s_sc, *, tile, chunk):
    qi = pl.program_id(2)
    m_sc[...] = jnp.full(m_sc.shape, -jnp.inf, F32)
    acc_sc[...] = jnp.zeros(acc_sc.shape, F32)

    def scores(j, slot):
        start = pl.multiple_of(j * tile, tile)
        k = k_ref[0, 0, pl.ds(start, tile), :]
        s_sc[slot] = lax.dot_general(q_ref[0, 0], k, (((1,), (1,)), ((), ())),
                                     preferred_element_type=F32)

    def accumulate(j, slot, masked):
        start = pl.multiple_of(j * tile, tile)
        for r in range(tile // chunk):
            rows = slice(r * chunk, (r + 1) * chunk)
            cols = -(-(r + 1) * chunk // LANES) * LANES if masked else tile
            s = s_sc[slot, rows, 0:cols]
            if masked:
                row = lax.broadcasted_iota(jnp.int32, s.shape, 0) + r * chunk
                col = lax.broadcasted_iota(jnp.int32, s.shape, 1)
                s = jnp.where(col <= row, s, -jnp.inf)
            v = v_ref[0, 0, pl.ds(start, cols), :]
            m_old = m_sc[rows, :]
            m_new = jnp.maximum(m_old, jnp.max(s, axis=-1, keepdims=True))
            alpha = jnp.exp2(m_old - m_new)
            p = jnp.exp2(s - jnp.tile(m_new, (1, cols // LANES)))
            acc_sc[rows, :] = (jnp.tile(alpha, (1, V_PAD_DIM // LANES)) * acc_sc[rows, :]
                               + jnp.dot(p.astype(BF16), v, preferred_element_type=F32))
            m_sc[rows, :] = m_new

    scores(0, 0)

    def pair(p, carry):
        scores(2 * p + 1, 1)
        accumulate(2 * p, 0, False)
        scores(2 * p + 2, 0)
        accumulate(2 * p + 1, 1, False)
        return carry

    lax.fori_loop(0, qi // 2, pair, 0)

    @pl.when(qi % 2 == 0)
    def _():
        accumulate(qi, 0, True)

    @pl.when(qi % 2 == 1)
    def _():
        scores(qi, 1)
        accumulate(qi - 1, 0, False)
        accumulate(qi, 1, True)

    o_ref[0] = (acc_sc[:, 0:V_HEAD_DIM] / acc_sc[:, V_HEAD_DIM:2 * V_HEAD_DIM]).astype(o_ref.dtype)


def _attention(q, k, v):
    b, h, s, _ = q.shape
    t = min(ATTN_TILE, s)
    assert V_HEAD_DIM == LANES and V_PAD_DIM == 2 * V_HEAD_DIM
    kern = functools.partial(_attn_kernel, tile=t, chunk=min(ATTN_CHUNK, t))
    return pl.pallas_call(
        kern,
        grid=(b, h, s // t),
        in_specs=[pl.BlockSpec((1, 1, t, QK_PAD_DIM), lambda i, j, n: (i, j, n, 0)),
                  pl.BlockSpec((1, 1, s, QK_PAD_DIM), lambda i, j, n: (i, j, 0, 0)),
                  pl.BlockSpec((1, 1, s, V_PAD_DIM), lambda i, j, n: (i, j, 0, 0))],
        out_specs=pl.BlockSpec((1, t, V_HEAD_DIM), lambda i, j, n: (i, n, j)),
        out_shape=jax.ShapeDtypeStruct((b, s, h * V_HEAD_DIM), BF16),
        scratch_shapes=[pltpu.VMEM((t, LANES), F32), pltpu.VMEM((t, V_PAD_DIM), F32),
                        pltpu.VMEM((2, t, t), F32)],
        compiler_params=_cparams("parallel", "parallel", "arbitrary"),
        name="attention",
    )(q, k, v)


def _lru_kernel(xl_ref, gl_ref, pos_ref, cw_ref, cb_ref, wax_ref, ba_ref, bx_ref, lam_ref, gn_ref,
                o_ref, ext_sc, a_sc, b_sc, h_sc, carry_sc):
    t, c = a_sc.shape
    head_w = c // LRU_HEADS

    @pl.when(pl.program_id(1) == 0)
    def _():
        ext_sc[0:SUBLANES, :] = jnp.zeros((SUBLANES, c), F32)
        carry_sc[...] = jnp.zeros(carry_sc.shape, F32)

    x = xl_ref[0]
    ext_sc[SUBLANES:SUBLANES + t, :] = x
    xc = cb_ref[...] + x * cw_ref[CONV_WIDTH - 1:CONV_WIDTH, :]
    for back in range(1, CONV_WIDTH):
        tap = CONV_WIDTH - 1 - back
        xc = xc + ext_sc[SUBLANES - back:SUBLANES - back + t, :] * cw_ref[tap:tap + 1, :]
    ext_sc[0:SUBLANES, :] = x[t - SUBLANES:t, :]

    reset = _column((pos_ref[0] == 0).astype(F32)) > 0.5
    lam = lam_ref[...]
    neg_sp = -LRU_C * (jnp.maximum(-lam, 0.0) + jnp.log1p(jnp.exp(-jnp.abs(lam))))
    for hd in range(LRU_HEADS):
        sl = slice(hd * head_w, (hd + 1) * head_w)
        xh = xc[:, sl]
        gates = jnp.dot(xh.astype(BF16), wax_ref[hd], preferred_element_type=F32)
        r = _sigmoid(gates[:, :head_w] + ba_ref[:, sl])
        i = _sigmoid(gates[:, head_w:] + bx_ref[:, sl])
        a = jnp.exp(r * neg_sp[:, sl])
        gap = jnp.maximum(1.0 - a * a, 0.0)
        mult = jnp.where(gap > 0.0, gap * lax.rsqrt(gap), 0.0)
        a = jnp.where(reset, 0.0, a)
        mult = jnp.where(reset, 1.0, mult)
        a_sc[:, sl] = a
        b_sc[:, sl] = mult * (i * xh)

    srow = lax.broadcasted_iota(jnp.int32, (SUBLANES, c), 0)

    def group(g, h_prev):
        start = pl.multiple_of(g * SUBLANES, SUBLANES)
        a = a_sc[pl.ds(start, SUBLANES), :]
        bb = b_sc[pl.ds(start, SUBLANES), :]
        sh = 1
        while sh < SUBLANES:
            keep = srow >= sh
            a_prev = jnp.where(keep, pltpu.roll(a, sh, 0), 1.0)
            b_prev = jnp.where(keep, pltpu.roll(bb, sh, 0), 0.0)
            bb = bb + a * b_prev
            a = a * a_prev
            sh *= 2
        hh = bb + a * h_prev
        h_sc[pl.ds(start, SUBLANES), :] = hh
        return jnp.broadcast_to(hh[SUBLANES - 1:SUBLANES, :], (SUBLANES, c))

    carry_sc[...] = lax.fori_loop(0, t // SUBLANES, group, carry_sc[...], unroll=8)

    y = h_sc[...] * gl_ref[0].astype(F32)
    o_ref[0] = (y * _rms(y, c) * gn_ref[...]).astype(o_ref.dtype)


def _rg_lru(x_lru, gelu_g, pos_row, conv_w, conv_b, w_ax, b_a, b_x, lam, out_norm):
    b, s, c = x_lru.shape
    t = min(LRU_TILE, s)
    full = lambda a: pl.BlockSpec(a.shape, lambda i, j: (0,) * a.ndim)
    return pl.pallas_call(
        _lru_kernel,
        grid=(b, s // t),
        in_specs=[pl.BlockSpec((1, t, c), lambda i, j: (i, j, 0)),
                  pl.BlockSpec((1, t, c), lambda i, j: (i, j, 0)),
                  pl.BlockSpec((1, 1, t), lambda i, j: (i, 0, j)),
                  full(conv_w), full(conv_b), full(w_ax), full(b_a), full(b_x), full(lam),
                  full(out_norm)],
        out_specs=pl.BlockSpec((1, t, c), lambda i, j: (i, j, 0)),
        out_shape=jax.ShapeDtypeStruct((b, s, c), BF16),
        scratch_shapes=[pltpu.VMEM((t + SUBLANES, c), F32), pltpu.VMEM((t, c), F32),
                        pltpu.VMEM((t, c), F32), pltpu.VMEM((t, c), F32),
                        pltpu.VMEM((SUBLANES, c), F32)],
        compiler_params=_cparams("parallel", "arbitrary"),
        name="rg_lru",
    )(x_lru, gelu_g, pos_row, conv_w, conv_b, w_ax, b_a, b_x, lam, out_norm)


def _route_kernel(x_ref, at_ref, lr_ref, mod_ref, an_ref, wa_ref, wl_ref, g2_ref, wr_ref, br_ref,
                  x1_ref, ids_ref, rank_ref, wts_ref, cnt_ref, carry_sc,
                  *, n_experts, group_row0, expert_row0):
    @pl.when(pl.program_id(0) == 0)
    def _():
        carry_sc[...] = jnp.zeros(carry_sc.shape, F32)

    x = x_ref[0]
    t, d = x.shape
    gate1 = mod_ref[0, 2:3, :]
    shift2 = mod_ref[0, 3:4, :]
    gain2 = g2_ref[...] * (1.0 + mod_ref[0, 4:5, :])
    at = at_ref[0].astype(F32)
    at_n = at * _rms(at, at.shape[-1]) * an_ref[...]
    mix = jnp.dot(at_n.astype(BF16), wa_ref[...], preferred_element_type=F32)
    mix = mix + jnp.dot(lr_ref[0], wl_ref[...], preferred_element_type=F32)
    x1 = x + gate1 * mix
    x1_ref[0] = x1
    h2 = x1 * _rms(x1, d) * gain2 + shift2

    logits = lax.dot_general(wr_ref[...], h2.astype(BF16), (((1,), (1,)), ((), ())),
                             preferred_element_type=F32) + br_ref[...]
    per_group = n_experts // N_GROUPS
    lg = logits[group_row0:group_row0 + N_GROUPS, :]
    grow = lax.broadcasted_iota(jnp.int32, lg.shape, 0)
    gmax = jnp.max(lg, axis=0, keepdims=True)
    g_idx = jnp.min(jnp.where(lg == gmax, grow, N_GROUPS), axis=0, keepdims=True)
    g_p = 1.0 / jnp.sum(jnp.exp(lg - gmax), axis=0, keepdims=True)
    sel = jnp.zeros((per_group, t), F32)
    for g in range(N_GROUPS):
        r0 = expert_row0 + g * per_group
        sel = sel + jnp.where(g_idx == g, logits[r0:r0 + per_group, :], 0.0)
    erow = lax.broadcasted_iota(jnp.int32, sel.shape, 0)
    m1 = jnp.max(sel, axis=0, keepdims=True)
    i1 = jnp.min(jnp.where(sel == m1, erow, per_group), axis=0, keepdims=True)
    sel2 = jnp.where(erow == i1, -jnp.inf, sel)
    m2 = jnp.max(sel2, axis=0, keepdims=True)
    i2 = jnp.min(jnp.where(sel2 == m2, erow, per_group), axis=0, keepdims=True)
    e21 = jnp.exp(m2 - m1)
    w1 = g_p / (1.0 + e21)
    w2 = g_p * e21 / (1.0 + e21)
    e1 = g_idx * per_group + i1
    e2 = g_idx * per_group + i2
    ids_ref[0:1, :] = e1
    ids_ref[1:2, :] = e2
    wts_ref[0:1, :] = w1
    wts_ref[1:2, :] = w2

    xrow = lax.broadcasted_iota(jnp.int32, (n_experts, t), 0)
    o1 = (xrow == e1).astype(F32)
    o2 = (xrow == e2).astype(F32)
    both = o1 + o2
    tr = lax.broadcasted_iota(jnp.int32, (t, t), 0)
    tc = lax.broadcasted_iota(jnp.int32, (t, t), 1)
    upper = jnp.where(tr < tc, 1.0, 0.0).astype(BF16)
    before = jnp.dot(both.astype(BF16), upper, preferred_element_type=F32) + carry_sc[...]
    rank_ref[0:1, :] = jnp.sum(o1 * before, axis=0, keepdims=True).astype(jnp.int32)
    rank_ref[1:2, :] = jnp.sum(o2 * before, axis=0, keepdims=True).astype(jnp.int32)
    carry_sc[...] = carry_sc[...] + jnp.sum(both, axis=1, keepdims=True)
    cnt_ref[...] = jnp.broadcast_to(carry_sc[...], cnt_ref.shape)


def _out_route(x, attn, lru_n, mod, attn_norm, w_out_a, w_out_l, norm2_g, w_r_t, b_r_col,
               *, n_experts, group_row0, expert_row0):
    b, s, d = x.shape
    t = min(ROUTE_TILE, s)
    nt = s // t
    n = b * s
    full = lambda a: pl.BlockSpec(a.shape, lambda i: (0,) * a.ndim)
    tok = lambda w: pl.BlockSpec((1, t, w), lambda i: (i // nt, i % nt, 0))
    kern = functools.partial(_route_kernel, n_experts=n_experts, group_row0=group_row0,
                             expert_row0=expert_row0)
    return pl.pallas_call(
        kern,
        grid=(b * nt,),
        in_specs=[tok(d), tok(attn.shape[-1]), tok(lru_n.shape[-1]),
                  pl.BlockSpec((1,) + mod.shape[1:], lambda i: (i // nt, 0, 0)),
                  full(attn_norm), full(w_out_a), full(w_out_l), full(norm2_g), full(w_r_t),
                  full(b_r_col)],
        out_specs=[tok(d),
                   pl.BlockSpec((TOP_K, t), lambda i: (0, i)),
                   pl.BlockSpec((TOP_K, t), lambda i: (0, i)),
                   pl.BlockSpec((TOP_K, t), lambda i: (0, i)),
                   pl.BlockSpec((n_experts, LANES), lambda i: (0, 0))],
        out_shape=[jax.ShapeDtypeStruct((b, s, d), F32),
                   jax.ShapeDtypeStruct((TOP_K, n), jnp.int32),
                   jax.ShapeDtypeStruct((TOP_K, n), jnp.int32),
                   jax.ShapeDtypeStruct((TOP_K, n), F32),
                   jax.ShapeDtypeStruct((n_experts, LANES), F32)],
        scratch_shapes=[pltpu.VMEM((n_experts, 1), F32)],
        compiler_params=_cparams("arbitrary"),
        name="out_route",
    )(x, attn, lru_n, mod, attn_norm, w_out_a, w_out_l, norm2_g, w_r_t, b_r_col)


def _slots_kernel(start_ref, ids_ref, rank_ref, o_ref):
    ids = ids_ref[...]
    slot = rank_ref[...]
    for e in range(start_ref.shape[0]):
        slot = slot + jnp.where(ids == e, start_ref[e], 0)
    o_ref[...] = slot


def _slots(starts, ids, ranks):
    whole = pl.BlockSpec(ids.shape, lambda i, st: (0, 0))
    return pl.pallas_call(
        _slots_kernel,
        grid_spec=pltpu.PrefetchScalarGridSpec(num_scalar_prefetch=1, grid=(1,),
                                               in_specs=[whole, whole], out_specs=whole),
        out_shape=jax.ShapeDtypeStruct(ids.shape, jnp.int32),
        compiler_params=_cparams("arbitrary"),
        name="slots",
    )(starts, ids, ranks)


def _row_copy(src, src_row, dst, dst_row, sem):
    return pltpu.make_async_copy(src.at[pl.ds(src_row, 1), :], dst.at[pl.ds(dst_row, 1), :], sem)


def _dispatch_kernel(cnt_ref, start_ref, nb_ref, dest_ref, x1_ref, mod_ref, g2_ref, xs_ref, h2_sc,
                     zero_sc, sem, *, tile, block, n_experts):
    i = pl.program_id(0)

    x1 = x1_ref[0]
    gain2 = g2_ref[...] * (1.0 + mod_ref[0, 4:5, :])
    h2_sc[...] = x1 * _rms(x1, x1.shape[-1]) * gain2 + mod_ref[0, 3:4, :]

    @pl.when(i == 0)
    def _():
        zero_sc[...] = jnp.zeros(zero_sc.shape, zero_sc.dtype)

        def tail_copy(j):
            return pltpu.make_async_copy(zero_sc, xs_ref.at[pl.ds(j * block, block), :], sem)

        def tail_issue(j, carry):
            tail_copy(j).start()
            return carry

        def tail_drain(j, carry):
            tail_copy(j).wait()
            return carry

        n_blocks = xs_ref.shape[0] // block
        lax.fori_loop(nb_ref[0], n_blocks, tail_issue, 0)
        lax.fori_loop(nb_ref[0], n_blocks, tail_drain, 0)

        def pad_copies(e, act):
            cnt = cnt_ref[e]
            first = start_ref[e] + cnt
            n_pad = (block - cnt % block) % block
            n_single = (SUBLANES - first % SUBLANES) % SUBLANES
            for r in range(SUBLANES - 1):
                @pl.when(r < n_single)
                def _(r=r):
                    act(_row_copy(zero_sc, 0, xs_ref, first + r, sem))

            row = first + n_single
            rest = n_pad - n_single
            for bit in range(SUBLANES.bit_length() - 1, block.bit_length() - 1):
                size = 1 << bit
                take = (rest >> bit) & 1

                @pl.when(take == 1)
                def _(row=row, size=size):
                    act(pltpu.make_async_copy(
                        zero_sc.at[pl.ds(0, size), :],
                        xs_ref.at[pl.ds(pl.multiple_of(row, SUBLANES), size), :], sem))

                row = row + take * size

        def pad_issue(e, carry):
            pad_copies(e, lambda cp: cp.start())
            return carry

        def pad_drain(e, carry):
            pad_copies(e, lambda cp: cp.wait())
            return carry

        lax.fori_loop(0, n_experts, pad_issue, 0)
        lax.fori_loop(0, n_experts, pad_drain, 0)

    for tk in range(tile):
        for k in range(TOP_K):
            _row_copy(h2_sc, tk, xs_ref, dest_ref[k, tk], sem).start(priority=k % 2)

    def drain(tk, carry):
        for _ in range(TOP_K):
            _row_copy(h2_sc, 0, xs_ref, 0, sem).wait()
        return carry

    lax.fori_loop(0, tile, drain, 0, unroll=8)


def _dispatch(counts, starts, n_blocks_used, dest, x1, mod, norm2_g, cap, *, block, n_experts):
    b, s, w = x1.shape
    t = min(DISPATCH_TILE, s)
    nt = s // t
    kern = functools.partial(_dispatch_kernel, tile=t, block=block, n_experts=n_experts)
    smem_tile = pl.BlockSpec((TOP_K, t), lambda i, c, st, nb: (0, i), memory_space=pltpu.SMEM)
    return pl.pallas_call(
        kern,
        grid_spec=pltpu.PrefetchScalarGridSpec(
            num_scalar_prefetch=3,
            grid=(b * nt,),
            in_specs=[smem_tile,
                      pl.BlockSpec((1, t, w), lambda i, c, st, nb: (i // nt, i % nt, 0)),
                      pl.BlockSpec((1,) + mod.shape[1:], lambda i, c, st, nb: (i // nt, 0, 0)),
                      pl.BlockSpec(norm2_g.shape, lambda i, c, st, nb: (0, 0))],
            out_specs=pl.BlockSpec(memory_space=pl.ANY),
            scratch_shapes=[pltpu.VMEM((t, w), F32), pltpu.VMEM((block, w), F32),
                            pltpu.SemaphoreType.DMA(())]),
        out_shape=jax.ShapeDtypeStruct((cap, w), F32),
        compiler_params=_cparams("arbitrary"),
        name="dispatch",
    )(counts, starts, n_blocks_used, dest, x1, mod, norm2_g)


def _expert_kernel(be_ref, nb_ref, xs_ref, wg_ref, wu_ref, wd_ref, ys_ref, wgu_sc, wdn_sc, *, d_ff):
    j = pl.program_id(0)
    live = jnp.minimum(j, nb_ref[0] - 1)

    @pl.when((j == 0) | (be_ref[live] != be_ref[jnp.maximum(live - 1, 0)]))
    def _():
        wgu_sc[:, 0:d_ff] = wg_ref[0].astype(BF16)
        wgu_sc[:, d_ff:2 * d_ff] = wu_ref[0].astype(BF16)
        wdn_sc[...] = wd_ref[0].astype(BF16)

    @pl.when(j < nb_ref[0])
    def _():
        xb = xs_ref[...].astype(BF16)
        gu = jnp.dot(xb, wgu_sc[...], preferred_element_type=F32)
        gt = gu[:, :d_ff]
        hid = gt * jax.nn.sigmoid(gt) * gu[:, d_ff:]
        ys_ref[...] = jnp.dot(hid.astype(BF16), wdn_sc[...], preferred_element_type=F32)

    @pl.when(pl.program_id(0) >= nb_ref[0])
    def _():
        ys_ref[...] = jnp.zeros(ys_ref.shape, ys_ref.dtype)


def _experts(block_e, n_blocks_used, xs, w_gate, w_up, w_down, *, block):
    cap, w = xs.shape
    d = w_down.shape[2]
    d_ff = w_down.shape[1]
    kern = functools.partial(_expert_kernel, d_ff=d_ff)
    live = lambda j, nb: jnp.minimum(j, nb[0] - 1)
    return pl.pallas_call(
        kern,
        grid_spec=pltpu.PrefetchScalarGridSpec(
            num_scalar_prefetch=2,
            grid=(cap // block,),
            in_specs=[pl.BlockSpec((block, w), lambda j, be, nb: (live(j, nb), 0)),
                      pl.BlockSpec((1, d, d_ff), lambda j, be, nb: (be[live(j, nb)], 0, 0)),
                      pl.BlockSpec((1, d, d_ff), lambda j, be, nb: (be[live(j, nb)], 0, 0)),
                      pl.BlockSpec((1, d_ff, d), lambda j, be, nb: (be[live(j, nb)], 0, 0))],
            out_specs=pl.BlockSpec((block, w), lambda j, be, nb: (j, 0)),
            scratch_shapes=[pltpu.VMEM((d, 2 * d_ff), BF16), pltpu.VMEM((d_ff, d), BF16)]),
        out_shape=jax.ShapeDtypeStruct((cap, w), F32),
        compiler_params=_cparams("arbitrary"),
        name="experts",
    )(block_e, n_blocks_used, xs, w_gate, w_up, w_down)


def _combine_kernel(dest_ref, dest_nx_ref, wts_ref, x1_ref, mod_ref, ys_ref, o_ref, buf, sem, *,
                    tile):
    i = pl.program_id(0)
    slot = i % 2

    def gather(d_ref, sl):
        for tk in range(tile):
            for k in range(TOP_K):
                _row_copy(ys_ref, d_ref[k, tk], buf.at[sl, k], tk, sem.at[sl]).start(priority=k % 2)

    @pl.when(i == 0)
    def _():
        gather(dest_ref, slot)

    @pl.when(i + 1 < pl.num_programs(0))
    def _():
        gather(dest_nx_ref, 1 - slot)

    for k in range(TOP_K):
        pltpu.make_async_copy(ys_ref.at[pl.ds(0, tile), :], buf.at[slot, k], sem.at[slot]).wait()

    d = o_ref.shape[-1]
    w0 = jnp.tile(_column(wts_ref[0:1, :]), (1, d // LANES))
    w1 = jnp.tile(_column(wts_ref[1:2, :]), (1, d // LANES))
    gate2 = mod_ref[0, 5:6, :]
    o_ref[0] = x1_ref[0] + gate2 * (w0 * buf[slot, 0] + w1 * buf[slot, 1])


def _combine(dest, wts, x1, mod, ys):
    b, s, d = x1.shape
    w = ys.shape[1]
    t = min(COMBINE_TILE, s)
    nt = s // t
    last = b * nt - 1
    kern = functools.partial(_combine_kernel, tile=t)
    smem_now = pl.BlockSpec((TOP_K, t), lambda i: (0, i), memory_space=pltpu.SMEM)
    smem_next = pl.BlockSpec((TOP_K, t), lambda i: (0, jnp.minimum(i + 1, last)),
                             memory_space=pltpu.SMEM)
    return pl.pallas_call(
        kern,
        grid=(b * nt,),
        in_specs=[smem_now, smem_next,
                  pl.BlockSpec((TOP_K, t), lambda i: (0, i)),
                  pl.BlockSpec((1, t, d), lambda i: (i // nt, i % nt, 0)),
                  pl.BlockSpec((1,) + mod.shape[1:], lambda i: (i // nt, 0, 0)),
                  pl.BlockSpec(memory_space=pl.ANY)],
        out_specs=pl.BlockSpec((1, t, d), lambda i: (i // nt, i % nt, 0)),
        out_shape=jax.ShapeDtypeStruct((b, s, d), F32),
        scratch_shapes=[pltpu.VMEM((2, TOP_K, t, w), F32), pltpu.SemaphoreType.DMA((2,))],
        compiler_params=_cparams("arbitrary"),
        name="combine",
    )(dest, dest, wts, x1, mod, ys)


def _layer(x, c, pos_row, w_ada, b_ada, norm1_g, w_in, q_a_norm, w_q_b, kv_a_norm, w_kv_b, q_norm,
           k_norm, conv_w, conv_b, lru_wa, lru_ba, lru_wx, lru_bx, lru_lambda, attn_out_norm,
           lru_out_norm, w_out, norm2_g, w_rg, b_rg, w_re, b_re, w_gate, w_up, w_down):
    b, s, d = x.shape
    n = b * s
    q_lora = q_a_norm.shape[0]
    kv_lora = kv_a_norm.shape[0]
    lru_w = lru_lambda.shape[0]
    attn_w = ATTN_HEADS * V_HEAD_DIM
    n_experts = w_gate.shape[0]
    row = lambda v: v.reshape(1, -1)

    mod = _adaln(c, w_ada, b_ada).reshape(b, 6, d)

    o1 = q_lora
    o2 = o1 + kv_lora
    o3 = o2 + QK_ROPE_DIM
    o4 = o3 + lru_w
    w_in_p = jnp.concatenate(
        [w_in[:, :o2], w_in[:, o3:o4], w_in[:, o4:], w_in[:, o2:o3],
         jnp.zeros((d, LANES - QK_ROPE_DIM), w_in.dtype)], axis=1).astype(BF16)
    pad_h = QK_PAD_DIM - QK_HEAD_DIM
    w_q_p = jnp.pad(w_q_b.reshape(q_lora, ATTN_HEADS, QK_HEAD_DIM),
                    ((0, 0), (0, 0), (0, pad_h))).reshape(q_lora, ATTN_HEADS * QK_PAD_DIM).astype(BF16)
    q_norm_p = row(jnp.pad(q_norm, (0, pad_h)))
    k_norm_nope = row(k_norm[:QK_NOPE_DIM])
    k_norm_rope_p = row(jnp.pad(k_norm[QK_NOPE_DIM:], (0, LANES - QK_ROPE_DIM)))
    inv = 1.0 / (ROPE_THETA ** (np.arange(0, QK_ROPE_DIM, 2, dtype=np.float32) / QK_ROPE_DIM))
    inv_col = jnp.asarray(inv.reshape(-1, 1))

    q, k, v, x_lru, gelu_g = _in_proj(
        x, pos_row, mod, row(norm1_g), w_in_p, row(q_a_norm), w_q_p, row(kv_a_norm),
        w_kv_b.astype(BF16), q_norm_p, k_norm_nope, k_norm_rope_p, inv_col,
        q_lora=q_lora, kv_lora=kv_lora, lru_w=lru_w)

    attn = _attention(q, k, v)

    w_ax = jnp.concatenate([lru_wa, lru_wx], axis=-1).astype(BF16)
    lru_n = _rg_lru(x_lru, gelu_g, pos_row, conv_w, row(conv_b), w_ax, row(lru_ba), row(lru_bx),
                    row(lru_lambda), row(lru_out_norm))

    group_row0, expert_row0 = 0, SUBLANES
    w_r_t = jnp.concatenate([w_rg.T, jnp.zeros((expert_row0 - N_GROUPS, d), w_rg.dtype), w_re.T],
                            axis=0).astype(BF16)
    b_r_col = jnp.concatenate([b_rg, jnp.zeros((expert_row0 - N_GROUPS,), b_rg.dtype),
                               b_re]).reshape(-1, 1)
    x1, ids, ranks, wts, cnt = _out_route(
        x, attn, lru_n, mod, row(attn_out_norm), w_out[:attn_w].astype(BF16),
        w_out[attn_w:].astype(BF16), row(norm2_g), w_r_t, b_r_col,
        n_experts=n_experts, group_row0=group_row0, expert_row0=expert_row0)

    blk = EXPERT_BLOCK
    counts = cnt[:, 0].astype(jnp.int32)
    padded = (counts + blk - 1) // blk * blk
    pad_ends = jnp.cumsum(padded)
    starts = (pad_ends - padded).astype(jnp.int32)
    dest = _slots(starts, ids.reshape(-1, LANES), ranks.reshape(-1, LANES)).reshape(ids.shape)
    cap = (n * TOP_K + n_experts * (blk - 1) + blk - 1) // blk * blk
    n_blocks = cap // blk
    block_first_row = jnp.arange(n_blocks, dtype=jnp.int32) * blk
    block_e = jnp.minimum(jnp.sum(pad_ends[None, :] <= block_first_row[:, None], axis=1),
                          n_experts - 1).astype(jnp.int32)
    n_used = (pad_ends[-1:] // blk).astype(jnp.int32)

    xs = _dispatch(counts, starts, n_used, dest, x1, mod, row(norm2_g), cap,
                   block=blk, n_experts=n_experts)
    ys = _experts(block_e, n_used, xs, w_gate, w_up, w_down, block=blk)
    return _combine(dest, wts, x1, mod, ys)


def kernel(x, c, positions, w_ada, b_ada, norm1_g, w_in, q_a_norm, w_q_b, kv_a_norm, w_kv_b, q_norm, k_norm, conv_w, conv_b, lru_wa, lru_ba, lru_wx, lru_bx, lru_lambda, attn_out_norm, lru_out_norm, w_out, norm2_g, w_router_group, b_router_group, w_router_expert, b_router_expert, w_gate, w_up, w_down):
    b, s, _ = x.shape
    pos_row = positions.reshape(b, 1, s)
    for l in range(w_ada.shape[0]):
        x = _layer(x, c, pos_row, w_ada[l], b_ada[l], norm1_g[l], w_in[l], q_a_norm[l], w_q_b[l],
                   kv_a_norm[l], w_kv_b[l], q_norm[l], k_norm[l], conv_w[l], conv_b[l], lru_wa[l],
                   lru_ba[l], lru_wx[l], lru_bx[l], lru_lambda[l], attn_out_norm[l], lru_out_norm[l],
                   w_out[l], norm2_g[l], w_router_group[l], b_router_group[l], w_router_expert[l],
                   b_router_expert[l], w_gate[l], w_up[l], w_down[l])
    return x
```

```python
import functools
import math

import numpy as np
import jax
import jax.numpy as jnp
from jax import lax
from jax.experimental import pallas as pl
from jax.experimental.pallas import tpu as pltpu

F32 = jnp.float32
BF16 = jnp.bfloat16

ATTN_HEADS = 4
QK_NOPE_DIM = 128
QK_ROPE_DIM = 64
QK_HEAD_DIM = QK_NOPE_DIM + QK_ROPE_DIM
V_HEAD_DIM = 128
ROPE_THETA = 10000.0
LRU_HEADS = 4
CONV_WIDTH = 4
LRU_C = 8.0
N_GROUPS = 4
TOP_K = 2
EPS = 1e-6

LANES = 128
SUBLANES = 8
QK_PAD_DIM = 256
V_PAD_DIM = 256
VMEM_LIMIT = 56 * 1024 * 1024

PROJ_TILE = 1024
PROJ_CHUNK = 128
ATTN_TILE = 1024
ATTN_CHUNK = 256
LRU_TILE = 1024
ROUTE_TILE = 1024
DISPATCH_TILE = 512
COMBINE_TILE = 512
EXPERT_BLOCK = 512
EXPERT_DEPTH = 3


def _cparams(*sem):
    return pltpu.CompilerParams(dimension_semantics=sem, vmem_limit_bytes=VMEM_LIMIT)


def _rms(x, n):
    return lax.rsqrt(jnp.sum(x * x, axis=-1, keepdims=True) * (1.0 / n) + EPS)


def _sigmoid(x):
    return 0.5 * jnp.tanh(0.5 * x) + 0.5


def _column(row, width=LANES):
    t = row.shape[-1]
    return jnp.broadcast_to(row, (width, t)).T


def _adaln_kernel(c_ref, w_ref, b_ref, o_ref):
    c = c_ref[...]
    sc = c * jax.nn.sigmoid(c)
    o_ref[...] = jnp.dot(sc.astype(BF16), w_ref[...].astype(BF16),
                         preferred_element_type=F32) + b_ref[...]


def _adaln(c, w_ada, b_ada):
    b, d = c.shape
    n_out = w_ada.shape[1]
    return pl.pallas_call(
        _adaln_kernel,
        grid=(n_out // d,),
        in_specs=[pl.BlockSpec((b, d), lambda j: (0, 0)),
                  pl.BlockSpec((d, d), lambda j: (0, j)),
                  pl.BlockSpec((1, d), lambda j: (0, j))],
        out_specs=pl.BlockSpec((b, d), lambda j: (0, j)),
        out_shape=jax.ShapeDtypeStruct((b, n_out), F32),
        compiler_params=_cparams("parallel"),
        name="adaln",
    )(c, w_ada, b_ada.reshape(1, n_out))


def _rope_tables(pos_row, inv_col):
    half = QK_ROPE_DIM // 2
    ang = inv_col * pos_row
    cos = jnp.cos(ang)
    sin = jnp.sin(ang)
    t = ang.shape[-1]
    zeros = lambda n: jnp.zeros((n, t), F32)
    c = jnp.concatenate([cos, cos, zeros(LANES - QK_ROPE_DIM)], axis=0)
    sa = jnp.concatenate([-sin, zeros(LANES - half)], axis=0)
    sb = jnp.concatenate([zeros(half), sin, zeros(LANES - QK_ROPE_DIM)], axis=0)
    return c.T, sa.T, sb.T


def _rope(x, c, sa, sb):
    return x * c + pltpu.roll(x, LANES - QK_ROPE_DIM // 2, 1) * sa + pltpu.roll(x, QK_ROPE_DIM // 2, 1) * sb


def _in_proj_kernel(x_ref, pos_ref, mod_ref, g1_ref, w_in_ref, qa_ref, wq_ref, kva_ref, wkv_ref,
                    qn_ref, kn_nope_ref, kn_rope_ref, inv_ref,
                    q_ref, k_ref, v_ref, xl_ref, gl_ref, *, q_lora, kv_lora, lru_w, chunk):
    t, d = x_ref.shape[1:]
    shift = mod_ref[0, 0:1, :]
    gain = g1_ref[...] * (1.0 + mod_ref[0, 1:2, :])
    c_all, sa_all, sb_all = _rope_tables(pos_ref[0].astype(F32), inv_ref[...])
    sm_scale = QK_HEAD_DIM ** -0.5 * math.log2(math.e)
    ones = jnp.ones((chunk, V_PAD_DIM - V_HEAD_DIM), BF16)

    for r0 in range(0, t, chunk):
        rows = slice(r0, r0 + chunk)
        x = x_ref[0, rows, :]
        h = x * _rms(x, d) * gain + shift
        proj = jnp.dot(h.astype(BF16), w_in_ref[...], preferred_element_type=F32)
        o = 0
        q_lat = proj[:, o:o + q_lora]; o += q_lora
        kv_lat = proj[:, o:o + kv_lora]; o += kv_lora
        x_lru = proj[:, o:o + lru_w]; o += lru_w
        g_lru = proj[:, o:o + lru_w]; o += lru_w
        k_pe = proj[:, o:o + LANES]

        xl_ref[0, rows, :] = x_lru
        gl_ref[0, rows, :] = jax.nn.gelu(g_lru).astype(BF16)

        c, sa, sb = c_all[rows], sa_all[rows], sb_all[rows]
        qn = q_lat * _rms(q_lat, q_lora) * qa_ref[...]
        q_full = jnp.dot(qn.astype(BF16), wq_ref[...], preferred_element_type=F32)
        kvn = kv_lat * _rms(kv_lat, kv_lora) * kva_ref[...]
        kv_full = jnp.dot(kvn.astype(BF16), wkv_ref[...], preferred_element_type=F32)

        pe_ss = jnp.sum(k_pe * k_pe, axis=-1, keepdims=True)
        pe_rot = _rope(k_pe * kn_rope_ref[...], c, sa, sb)
        for hd in range(ATTN_HEADS):
            qh = q_full[:, hd * QK_PAD_DIM:(hd + 1) * QK_PAD_DIM]
            qh = qh * (_rms(qh, QK_HEAD_DIM) * sm_scale) * qn_ref[...]
            q_ref[0, hd, rows, 0:QK_NOPE_DIM] = qh[:, 0:QK_NOPE_DIM].astype(BF16)
            q_ref[0, hd, rows, QK_NOPE_DIM:QK_PAD_DIM] = _rope(qh[:, QK_NOPE_DIM:QK_PAD_DIM],
                                                               c, sa, sb).astype(BF16)
            base = hd * (QK_NOPE_DIM + V_HEAD_DIM)
            kn = kv_full[:, base:base + QK_NOPE_DIM]
            vv = kv_full[:, base + QK_NOPE_DIM:base + QK_NOPE_DIM + V_HEAD_DIM]
            r = lax.rsqrt((jnp.sum(kn * kn, axis=-1, keepdims=True) + pe_ss) * (1.0 / QK_HEAD_DIM) + EPS)
            k_ref[0, hd, rows, 0:QK_NOPE_DIM] = (kn * r * kn_nope_ref[...]).astype(BF16)
            k_ref[0, hd, rows, QK_NOPE_DIM:QK_PAD_DIM] = (pe_rot * r).astype(BF16)
            v_ref[0, hd, rows, 0:V_HEAD_DIM] = vv.astype(BF16)
            v_ref[0, hd, rows, V_HEAD_DIM:V_PAD_DIM] = ones


def _in_proj(x, pos_row, mod, norm1_g, w_in_p, q_a_norm, w_q_p, kv_a_norm, w_kv, q_norm_p,
             k_norm_nope, k_norm_rope_p, inv_col, *, q_lora, kv_lora, lru_w):
    b, s, d = x.shape
    t = min(PROJ_TILE, s)
    full = lambda a: pl.BlockSpec(a.shape, lambda i, j: (0,) * a.ndim)
    kern = functools.partial(_in_proj_kernel, q_lora=q_lora, kv_lora=kv_lora, lru_w=lru_w,
                             chunk=min(PROJ_CHUNK, t))
    head_out = lambda w: pl.BlockSpec((1, ATTN_HEADS, t, w), lambda i, j: (i, 0, j, 0))
    return pl.pallas_call(
        kern,
        grid=(b, s // t),
        in_specs=[pl.BlockSpec((1, t, d), lambda i, j: (i, j, 0)),
                  pl.BlockSpec((1, 1, t), lambda i, j: (i, 0, j)),
                  pl.BlockSpec((1,) + mod.shape[1:], lambda i, j: (i, 0, 0)),
                  full(norm1_g), full(w_in_p), full(q_a_norm), full(w_q_p), full(kv_a_norm),
                  full(w_kv), full(q_norm_p), full(k_norm_nope), full(k_norm_rope_p), full(inv_col)],
        out_specs=[head_out(QK_PAD_DIM), head_out(QK_PAD_DIM), head_out(V_PAD_DIM),
                   pl.BlockSpec((1, t, lru_w), lambda i, j: (i, j, 0)),
                   pl.BlockSpec((1, t, lru_w), lambda i, j: (i, j, 0))],
        out_shape=[jax.ShapeDtypeStruct((b, ATTN_HEADS, s, QK_PAD_DIM), BF16),
                   jax.ShapeDtypeStruct((b, ATTN_HEADS, s, QK_PAD_DIM), BF16),
                   jax.ShapeDtypeStruct((b, ATTN_HEADS, s, V_PAD_DIM), BF16),
                   jax.ShapeDtypeStruct((b, s, lru_w), F32),
                   jax.ShapeDtypeStruct((b, s, lru_w), BF16)],
        compiler_params=_cparams("parallel", "parallel"),
        name="in_proj",
    )(x, pos_row, mod, norm1_g, w_in_p, q_a_norm, w_q_p, kv_a_norm, w_kv, q_norm_p,
      k_norm_nope, k_norm_rope_p, inv_col)


def _attn_kernel(q_ref, k_ref, v_ref, o_ref, m_sc, acc_sc, s_sc, *, tile, chunk):
    qi = pl.program_id(2)
    m_sc[...] = jnp.full(m_sc.shape, -jnp.inf, F32)
    acc_sc[...] = jnp.zeros(acc_sc.shape, F32)

    def scores(j, slot):
        start = pl.multiple_of(j * tile, tile)
        k = k_ref[0, 0, pl.ds(start, tile), :]
        s_sc[slot] = lax.dot_general(q_ref[0, 0], k, (((1,), (1,)), ((), ())),
                                     preferred_element_type=F32)

    def accumulate(j, slot, masked):
        start = pl.multiple_of(j * tile, tile)
        for r in range(tile // chunk):
            rows = slice(r * chunk, (r + 1) * chunk)
            cols = -(-(r + 1) * chunk // LANES) * LANES if masked else tile
            s = s_sc[slot, rows, 0:cols]
            if masked:
                row = lax.broadcasted_iota(jnp.int32, s.shape, 0) + r * chunk
                col = lax.broadcasted_iota(jnp.int32, s.shape, 1)
                s = jnp.where(col <= row, s, -jnp.inf)
            v = v_ref[0, 0, pl.ds(start, cols), :]
            m_old = m_sc[rows, :]
            m_new = jnp.maximum(m_old, jnp.max(s, axis=-1, keepdims=True))
            alpha = jnp.exp2(m_old - m_new)
            p = jnp.exp2(s - jnp.tile(m_new, (1, cols // LANES)))
            acc_sc[rows, :] = (jnp.tile(alpha, (1, V_PAD_DIM // LANES)) * acc_sc[rows, :]
                               + jnp.dot(p.astype(BF16), v, preferred_element_type=F32))
            m_sc[rows, :] = m_new

    scores(0, 0)

    def pair(p, carry):
        scores(2 * p + 1, 1)
        accumulate(2 * p, 0, False)
        scores(2 * p + 2, 0)
        accumulate(2 * p + 1, 1, False)
        return carry

    lax.fori_loop(0, qi // 2, pair, 0)

    @pl.when(qi % 2 == 0)
    def _():
        accumulate(qi, 0, True)

    @pl.when(qi % 2 == 1)
    def _():
        scores(qi, 1)
        accumulate(qi - 1, 0, False)
        accumulate(qi, 1, True)

    o_ref[0] = (acc_sc[:, 0:V_HEAD_DIM] / acc_sc[:, V_HEAD_DIM:2 * V_HEAD_DIM]).astype(o_ref.dtype)


def _attention(q, k, v):
    b, h, s, _ = q.shape
    t = min(ATTN_TILE, s)
    assert V_HEAD_DIM == LANES and V_PAD_DIM == 2 * V_HEAD_DIM
    kern = functools.partial(_attn_kernel, tile=t, chunk=min(ATTN_CHUNK, t))
    return pl.pallas_call(
        kern,
        grid=(b, h, s // t),
        in_specs=[pl.BlockSpec((1, 1, t, QK_PAD_DIM), lambda i, j, n: (i, j, n, 0)),
                  pl.BlockSpec((1, 1, s, QK_PAD_DIM), lambda i, j, n: (i, j, 0, 0)),
                  pl.BlockSpec((1, 1, s, V_PAD_DIM), lambda i, j, n: (i, j, 0, 0))],
        out_specs=pl.BlockSpec((1, t, V_HEAD_DIM), lambda i, j, n: (i, n, j)),
        out_shape=jax.ShapeDtypeStruct((b, s, h * V_HEAD_DIM), BF16),
        scratch_shapes=[pltpu.VMEM((t, LANES), F32), pltpu.VMEM((t, V_PAD_DIM), F32),
                        pltpu.VMEM((2, t, t), F32)],
        compiler_params=_cparams("parallel", "parallel", "arbitrary"),
        name="attention",
    )(q, k, v)


def _lru_kernel(xl_ref, gl_ref, pos_ref, cw_ref, cb_ref, wax_ref, ba_ref, bx_ref, lam_ref, gn_ref,
                o_ref, ext_sc, a_sc, b_sc, h_sc, carry_sc):
    t, c = a_sc.shape
    head_w = c // LRU_HEADS

    @pl.when(pl.program_id(1) == 0)
    def _():
        ext_sc[0:SUBLANES, :] = jnp.zeros((SUBLANES, c), F32)
        carry_sc[...] = jnp.zeros(carry_sc.shape, F32)

    x = xl_ref[0]
    ext_sc[SUBLANES:SUBLANES + t, :] = x
    xc = cb_ref[...] + x * cw_ref[CONV_WIDTH - 1:CONV_WIDTH, :]
    for back in range(1, CONV_WIDTH):
        tap = CONV_WIDTH - 1 - back
        xc = xc + ext_sc[SUBLANES - back:SUBLANES - back + t, :] * cw_ref[tap:tap + 1, :]
    ext_sc[0:SUBLANES, :] = x[t - SUBLANES:t, :]

    reset = _column((pos_ref[0] == 0).astype(F32)) > 0.5
    lam = lam_ref[...]
    neg_sp = -LRU_C * (jnp.maximum(-lam, 0.0) + jnp.log1p(jnp.exp(-jnp.abs(lam))))
    for hd in range(LRU_HEADS):
        sl = slice(hd * head_w, (hd + 1) * head_w)
        xh = xc[:, sl]
        gates = jnp.dot(xh.astype(BF16), wax_ref[hd], preferred_element_type=F32)
        r = _sigmoid(gates[:, :head_w] + ba_ref[:, sl])
        i = _sigmoid(gates[:, head_w:] + bx_ref[:, sl])
        a = jnp.exp(r * neg_sp[:, sl])
        gap = jnp.maximum(1.0 - a * a, 0.0)
        mult = jnp.where(gap > 0.0, gap * lax.rsqrt(gap), 0.0)
        a = jnp.where(reset, 0.0, a)
        mult = jnp.where(reset, 1.0, mult)
        a_sc[:, sl] = a
        b_sc[:, sl] = mult * (i * xh)

    srow = lax.broadcasted_iota(jnp.int32, (SUBLANES, c), 0)

    def group(g, h_prev):
        start = pl.multiple_of(g * SUBLANES, SUBLANES)
        a = a_sc[pl.ds(start, SUBLANES), :]
        bb = b_sc[pl.ds(start, SUBLANES), :]
        sh = 1
        while sh < SUBLANES:
            keep = srow >= sh
            a_prev = jnp.where(keep, pltpu.roll(a, sh, 0), 1.0)
            b_prev = jnp.where(keep, pltpu.roll(bb, sh, 0), 0.0)
            bb = bb + a * b_prev
            a = a * a_prev
            sh *= 2
        hh = bb + a * h_prev
        h_sc[pl.ds(start, SUBLANES), :] = hh
        return jnp.broadcast_to(hh[SUBLANES - 1:SUBLANES, :], (SUBLANES, c))

    carry_sc[...] = lax.fori_loop(0, t // SUBLANES, group, carry_sc[...], unroll=8)

    y = h_sc[...] * gl_ref[0].astype(F32)
    o_ref[0] = (y * _rms(y, c) * gn_ref[...]).astype(o_ref.dtype)


def _rg_lru(x_lru, gelu_g, pos_row, conv_w, conv_b, w_ax, b_a, b_x, lam, out_norm):
    b, s, c = x_lru.shape
    t = min(LRU_TILE, s)
    full = lambda a: pl.BlockSpec(a.shape, lambda i, j: (0,) * a.ndim)
    return pl.pallas_call(
        _lru_kernel,
        grid=(b, s // t),
        in_specs=[pl.BlockSpec((1, t, c), lambda i, j: (i, j, 0)),
                  pl.BlockSpec((1, t, c), lambda i, j: (i, j, 0)),
                  pl.BlockSpec((1, 1, t), lambda i, j: (i, 0, j)),
                  full(conv_w), full(conv_b), full(w_ax), full(b_a), full(b_x), full(lam),
                  full(out_norm)],
        out_specs=pl.BlockSpec((1, t, c), lambda i, j: (i, j, 0)),
        out_shape=jax.ShapeDtypeStruct((b, s, c), BF16),
        scratch_shapes=[pltpu.VMEM((t + SUBLANES, c), F32), pltpu.VMEM((t, c), F32),
                        pltpu.VMEM((t, c), F32), pltpu.VMEM((t, c), F32),
                        pltpu.VMEM((SUBLANES, c), F32)],
        compiler_params=_cparams("parallel", "arbitrary"),
        name="rg_lru",
    )(x_lru, gelu_g, pos_row, conv_w, conv_b, w_ax, b_a, b_x, lam, out_norm)


def _route_kernel(x_ref, at_ref, lr_ref, mod_ref, an_ref, wa_ref, wl_ref, g2_ref, wr_ref, br_ref,
                  x1_ref, ids_ref, rank_ref, wts_ref, cnt_ref, carry_sc,
                  *, n_experts, group_row0, expert_row0):
    @pl.when(pl.program_id(0) == 0)
    def _():
        carry_sc[...] = jnp.zeros(carry_sc.shape, F32)

    x = x_ref[0]
    t, d = x.shape
    gate1 = mod_ref[0, 2:3, :]
    shift2 = mod_ref[0, 3:4, :]
    gain2 = g2_ref[...] * (1.0 + mod_ref[0, 4:5, :])
    at = at_ref[0].astype(F32)
    at_n = at * _rms(at, at.shape[-1]) * an_ref[...]
    mix = jnp.dot(at_n.astype(BF16), wa_ref[...], preferred_element_type=F32)
    mix = mix + jnp.dot(lr_ref[0], wl_ref[...], preferred_element_type=F32)
    x1 = x + gate1 * mix
    x1_ref[0] = x1
    h2 = x1 * _rms(x1, d) * gain2 + shift2

    logits = lax.dot_general(wr_ref[...], h2.astype(BF16), (((1,), (1,)), ((), ())),
                             preferred_element_type=F32) + br_ref[...]
    per_group = n_experts // N_GROUPS
    lg = logits[group_row0:group_row0 + N_GROUPS, :]
    grow = lax.broadcasted_iota(jnp.int32, lg.shape, 0)
    gmax = jnp.max(lg, axis=0, keepdims=True)
    g_idx = jnp.min(jnp.where(lg == gmax, grow, N_GROUPS), axis=0, keepdims=True)
    g_p = 1.0 / jnp.sum(jnp.exp(lg - gmax), axis=0, keepdims=True)
    sel = jnp.zeros((per_group, t), F32)
    for g in range(N_GROUPS):
        r0 = expert_row0 + g * per_group
        sel = sel + jnp.where(g_idx == g, logits[r0:r0 + per_group, :], 0.0)
    erow = lax.broadcasted_iota(jnp.int32, sel.shape, 0)
    m1 = jnp.max(sel, axis=0, keepdims=True)
    i1 = jnp.min(jnp.where(sel == m1, erow, per_group), axis=0, keepdims=True)
    sel2 = jnp.where(erow == i1, -jnp.inf, sel)
    m2 = jnp.max(sel2, axis=0, keepdims=True)
    i2 = jnp.min(jnp.where(sel2 == m2, erow, per_group), axis=0, keepdims=True)
    e21 = jnp.exp(m2 - m1)
    w1 = g_p / (1.0 + e21)
    w2 = g_p * e21 / (1.0 + e21)
    e1 = g_idx * per_group + i1
    e2 = g_idx * per_group + i2
    ids_ref[0:1, :] = e1
    ids_ref[1:2, :] = e2
    wts_ref[0:1, :] = w1
    wts_ref[1:2, :] = w2

    xrow = lax.broadcasted_iota(jnp.int32, (n_experts, t), 0)
    o1 = (xrow == e1).astype(F32)
    o2 = (xrow == e2).astype(F32)
    both = o1 + o2
    tr = lax.broadcasted_iota(jnp.int32, (t, t), 0)
    tc = lax.broadcasted_iota(jnp.int32, (t, t), 1)
    upper = jnp.where(tr < tc, 1.0, 0.0).astype(BF16)
    before = jnp.dot(both.astype(BF16), upper, preferred_element_type=F32) + carry_sc[...]
    rank_ref[0:1, :] = jnp.sum(o1 * before, axis=0, keepdims=True).astype(jnp.int32)
    rank_ref[1:2, :] = jnp.sum(o2 * before, axis=0, keepdims=True).astype(jnp.int32)
    carry_sc[...] = carry_sc[...] + jnp.sum(both, axis=1, keepdims=True)
    cnt_ref[...] = jnp.broadcast_to(carry_sc[...], cnt_ref.shape)


def _out_route(x, attn, lru_n, mod, attn_norm, w_out_a, w_out_l, norm2_g, w_r_t, b_r_col,
               *, n_experts, group_row0, expert_row0):
    b, s, d = x.shape
    t = min(ROUTE_TILE, s)
    nt = s // t
    n = b * s
    full = lambda a: pl.BlockSpec(a.shape, lambda i: (0,) * a.ndim)
    tok = lambda w: pl.BlockSpec((1, t, w), lambda i: (i // nt, i % nt, 0))
    kern = functools.partial(_route_kernel, n_experts=n_experts, group_row0=group_row0,
                             expert_row0=expert_row0)
    return pl.pallas_call(
        kern,
        grid=(b * nt,),
        in_specs=[tok(d), tok(attn.shape[-1]), tok(lru_n.shape[-1]),
                  pl.BlockSpec((1,) + mod.shape[1:], lambda i: (i // nt, 0, 0)),
                  full(attn_norm), full(w_out_a), full(w_out_l), full(norm2_g), full(w_r_t),
                  full(b_r_col)],
        out_specs=[tok(d),
                   pl.BlockSpec((TOP_K, t), lambda i: (0, i)),
                   pl.BlockSpec((TOP_K, t), lambda i: (0, i)),
                   pl.BlockSpec((TOP_K, t), lambda i: (0, i)),
                   pl.BlockSpec((n_experts, LANES), lambda i: (0, 0))],
        out_shape=[jax.ShapeDtypeStruct((b, s, d), F32),
                   jax.ShapeDtypeStruct((TOP_K, n), jnp.int32),
                   jax.ShapeDtypeStruct((TOP_K, n), jnp.int32),
                   jax.ShapeDtypeStruct((TOP_K, n), F32),
                   jax.ShapeDtypeStruct((n_experts, LANES), F32)],
        scratch_shapes=[pltpu.VMEM((n_experts, 1), F32)],
        compiler_params=_cparams("arbitrary"),
        name="out_route",
    )(x, attn, lru_n, mod, attn_norm, w_out_a, w_out_l, norm2_g, w_r_t, b_r_col)


def _slots_kernel(start_ref, ids_ref, rank_ref, o_ref):
    ids = ids_ref[...]
    slot = rank_ref[...]
    for e in range(start_ref.shape[0]):
        slot = slot + jnp.where(ids == e, start_ref[e], 0)
    o_ref[...] = slot


def _slots(starts, ids, ranks):
    whole = pl.BlockSpec(ids.shape, lambda i, st: (0, 0))
    return pl.pallas_call(
        _slots_kernel,
        grid_spec=pltpu.PrefetchScalarGridSpec(num_scalar_prefetch=1, grid=(1,),
                                               in_specs=[whole, whole], out_specs=whole),
        out_shape=jax.ShapeDtypeStruct(ids.shape, jnp.int32),
        compiler_params=_cparams("arbitrary"),
        name="slots",
    )(starts, ids, ranks)


def _row_copy(src, src_row, dst, dst_row, sem):
    return pltpu.make_async_copy(src.at[pl.ds(src_row, 1), :], dst.at[pl.ds(dst_row, 1), :], sem)


def _dispatch_kernel(cnt_ref, start_ref, nb_ref, dest_ref, x1_ref, mod_ref, g2_ref, xs_ref, h2_sc,
                     zero_sc, sem, *, tile, block, n_experts):
    i = pl.program_id(0)

    x1 = x1_ref[0]
    gain2 = g2_ref[...] * (1.0 + mod_ref[0, 4:5, :])
    h2_sc[...] = x1 * _rms(x1, x1.shape[-1]) * gain2 + mod_ref[0, 3:4, :]

    @pl.when(i == 0)
    def _():
        zero_sc[...] = jnp.zeros(zero_sc.shape, zero_sc.dtype)

        def tail_copy(j):
            return pltpu.make_async_copy(zero_sc, xs_ref.at[pl.ds(j * block, block), :], sem)

        def tail_issue(j, carry):
            tail_copy(j).start()
            return carry

        def tail_drain(j, carry):
            tail_copy(j).wait()
            return carry

        n_blocks = xs_ref.shape[0] // block
        lax.fori_loop(nb_ref[0], n_blocks, tail_issue, 0)
        lax.fori_loop(nb_ref[0], n_blocks, tail_drain, 0)

        def pad_copies(e, act):
            cnt = cnt_ref[e]
            first = start_ref[e] + cnt
            n_pad = (block - cnt % block) % block
            n_single = (SUBLANES - first % SUBLANES) % SUBLANES
            for r in range(SUBLANES - 1):
                @pl.when(r < n_single)
                def _(r=r):
                    act(_row_copy(zero_sc, 0, xs_ref, first + r, sem))

            row = first + n_single
            rest = n_pad - n_single
            for bit in range(SUBLANES.bit_length() - 1, block.bit_length() - 1):
                size = 1 << bit
                take = (rest >> bit) & 1

                @pl.when(take == 1)
                def _(row=row, size=size):
                    act(pltpu.make_async_copy(
                        zero_sc.at[pl.ds(0, size), :],
                        xs_ref.at[pl.ds(pl.multiple_of(row, SUBLANES), size), :], sem))

                row = row + take * size

        def pad_issue(e, carry):
            pad_copies(e, lambda cp: cp.start())
            return carry

        def pad_drain(e, carry):
            pad_copies(e, lambda cp: cp.wait())
            return carry

        lax.fori_loop(0, n_experts, pad_issue, 0)
        lax.fori_loop(0, n_experts, pad_drain, 0)

    for tk in range(tile):
        for k in range(TOP_K):
            _row_copy(h2_sc, tk, xs_ref, dest_ref[k, tk], sem).start(priority=k % 2)

    def drain(tk, carry):
        for _ in range(TOP_K):
            _row_copy(h2_sc, 0, xs_ref, 0, sem).wait()
        return carry

    lax.fori_loop(0, tile, drain, 0, unroll=8)


def _dispatch(counts, starts, n_blocks_used, dest, x1, mod, norm2_g, cap, *, block, n_experts):
    b, s, w = x1.shape
    t = min(DISPATCH_TILE, s)
    nt = s // t
    kern = functools.partial(_dispatch_kernel, tile=t, block=block, n_experts=n_experts)
    smem_tile = pl.BlockSpec((TOP_K, t), lambda i, c, st, nb: (0, i), memory_space=pltpu.SMEM)
    return pl.pallas_call(
        kern,
        grid_spec=pltpu.PrefetchScalarGridSpec(
            num_scalar_prefetch=3,
            grid=(b * nt,),
            in_specs=[smem_tile,
                      pl.BlockSpec((1, t, w), lambda i, c, st, nb: (i // nt, i % nt, 0)),
                      pl.BlockSpec((1,) + mod.shape[1:], lambda i, c, st, nb: (i // nt, 0, 0)),
                      pl.BlockSpec(norm2_g.shape, lambda i, c, st, nb: (0, 0))],
            out_specs=pl.BlockSpec(memory_space=pl.ANY),
            scratch_shapes=[pltpu.VMEM((t, w), F32), pltpu.VMEM((block, w), F32),
                            pltpu.SemaphoreType.DMA(())]),
        out_shape=jax.ShapeDtypeStruct((cap, w), F32),
        compiler_params=_cparams("arbitrary"),
        name="dispatch",
    )(counts, starts, n_blocks_used, dest, x1, mod, norm2_g)


def _expert_kernel(be_ref, nb_ref, xs_ref, wg_ref, wu_ref, wd_ref, ys_ref, xbuf, wgu_sc, wdn_sc, sem,
                   *, d_ff, block):
    j = pl.program_id(0)
    n_live = nb_ref[0]
    live = jnp.minimum(j, n_live - 1)

    def fetch(blk, slot):
        rows = pl.ds(pl.multiple_of(blk * block, block), block)
        return pltpu.make_async_copy(xs_ref.at[rows, :], xbuf.at[slot], sem.at[slot])

    @pl.when(j == 0)
    def _():
        for ahead in range(EXPERT_DEPTH - 1):
            @pl.when(ahead < n_live)
            def _(ahead=ahead):
                fetch(ahead, ahead).start()

    @pl.when(j + EXPERT_DEPTH - 1 < n_live)
    def _():
        fetch(j + EXPERT_DEPTH - 1, (j + EXPERT_DEPTH - 1) % EXPERT_DEPTH).start()

    @pl.when((j == 0) | (be_ref[live] != be_ref[jnp.maximum(live - 1, 0)]))
    def _():
        wgu_sc[:, 0:d_ff] = wg_ref[0].astype(BF16)
        wgu_sc[:, d_ff:2 * d_ff] = wu_ref[0].astype(BF16)
        wdn_sc[...] = wd_ref[0].astype(BF16)

    @pl.when(j < n_live)
    def _():
        slot = j % EXPERT_DEPTH
        fetch(j, slot).wait()
        xb = xbuf[slot].astype(BF16)
        gu = jnp.dot(xb, wgu_sc[...], preferred_element_type=F32)
        gt = gu[:, :d_ff]
        hid = gt * jax.nn.sigmoid(gt) * gu[:, d_ff:]
        ys_ref[...] = jnp.dot(hid.astype(BF16), wdn_sc[...], preferred_element_type=F32)

    @pl.when(pl.program_id(0) >= nb_ref[0])
    def _():
        ys_ref[...] = jnp.zeros(ys_ref.shape, ys_ref.dtype)


def _experts(block_e, n_blocks_used, xs, w_gate, w_up, w_down, *, block):
    cap, w = xs.shape
    d = w_down.shape[2]
    d_ff = w_down.shape[1]
    kern = functools.partial(_expert_kernel, d_ff=d_ff, block=block)
    live = lambda j, nb: jnp.minimum(j, nb[0] - 1)
    return pl.pallas_call(
        kern,
        grid_spec=pltpu.PrefetchScalarGridSpec(
            num_scalar_prefetch=2,
            grid=(cap // block,),
            in_specs=[pl.BlockSpec(memory_space=pl.ANY),
                      pl.BlockSpec((1, d, d_ff), lambda j, be, nb: (be[live(j, nb)], 0, 0)),
                      pl.BlockSpec((1, d, d_ff), lambda j, be, nb: (be[live(j, nb)], 0, 0)),
                      pl.BlockSpec((1, d_ff, d), lambda j, be, nb: (be[live(j, nb)], 0, 0))],
            out_specs=pl.BlockSpec((block, w), lambda j, be, nb: (j, 0)),
            scratch_shapes=[pltpu.VMEM((EXPERT_DEPTH, block, w), F32),
                            pltpu.VMEM((d, 2 * d_ff), BF16), pltpu.VMEM((d_ff, d), BF16),
                            pltpu.SemaphoreType.DMA((EXPERT_DEPTH,))]),
        out_shape=jax.ShapeDtypeStruct((cap, w), F32),
        compiler_params=_cparams("arbitrary"),
        name="experts",
    )(block_e, n_blocks_used, xs, w_gate, w_up, w_down)


def _combine_kernel(dest_ref, dest_nx_ref, wts_ref, x1_ref, mod_ref, ys_ref, o_ref, buf, sem, *,
                    tile):
    i = pl.program_id(0)
    slot = i % 2

    def gather(d_ref, sl):
        for tk in range(tile):
            for k in range(TOP_K):
                _row_copy(ys_ref, d_ref[k, tk], buf.at[sl, k], tk, sem.at[sl]).start(priority=k % 2)

    @pl.when(i == 0)
    def _():
        gather(dest_ref, slot)

    @pl.when(i + 1 < pl.num_programs(0))
    def _():
        gather(dest_nx_ref, 1 - slot)

    for k in range(TOP_K):
        pltpu.make_async_copy(ys_ref.at[pl.ds(0, tile), :], buf.at[slot, k], sem.at[slot]).wait()

    d = o_ref.shape[-1]
    w0 = jnp.tile(_column(wts_ref[0:1, :]), (1, d // LANES))
    w1 = jnp.tile(_column(wts_ref[1:2, :]), (1, d // LANES))
    gate2 = mod_ref[0, 5:6, :]
    o_ref[0] = x1_ref[0] + gate2 * (w0 * buf[slot, 0] + w1 * buf[slot, 1])


def _combine(dest, wts, x1, mod, ys):
    b, s, d = x1.shape
    w = ys.shape[1]
    t = min(COMBINE_TILE, s)
    nt = s // t
    last = b * nt - 1
    kern = functools.partial(_combine_kernel, tile=t)
    smem_now = pl.BlockSpec((TOP_K, t), lambda i: (0, i), memory_space=pltpu.SMEM)
    smem_next = pl.BlockSpec((TOP_K, t), lambda i: (0, jnp.minimum(i + 1, last)),
                             memory_space=pltpu.SMEM)
    return pl.pallas_call(
        kern,
        grid=(b * nt,),
        in_specs=[smem_now, smem_next,
                  pl.BlockSpec((TOP_K, t), lambda i: (0, i)),
                  pl.BlockSpec((1, t, d), lambda i: (i // nt, i % nt, 0)),
                  pl.BlockSpec((1,) + mod.shape[1:], lambda i: (i // nt, 0, 0)),
                  pl.BlockSpec(memory_space=pl.ANY)],
        out_specs=pl.BlockSpec((1, t, d), lambda i: (i // nt, i % nt, 0)),
        out_shape=jax.ShapeDtypeStruct((b, s, d), F32),
        scratch_shapes=[pltpu.VMEM((2, TOP_K, t, w), F32), pltpu.SemaphoreType.DMA((2,))],
        compiler_params=_cparams("arbitrary"),
        name="combine",
    )(dest, dest, wts, x1, mod, ys)


def _layer(x, c, pos_row, w_ada, b_ada, norm1_g, w_in, q_a_norm, w_q_b, kv_a_norm, w_kv_b, q_norm,
           k_norm, conv_w, conv_b, lru_wa, lru_ba, lru_wx, lru_bx, lru_lambda, attn_out_norm,
           lru_out_norm, w_out, norm2_g, w_rg, b_rg, w_re, b_re, w_gate, w_up, w_down):
    b, s, d = x.shape
    n = b * s
    q_lora = q_a_norm.shape[0]
    kv_lora = kv_a_norm.shape[0]
    lru_w = lru_lambda.shape[0]
    attn_w = ATTN_HEADS * V_HEAD_DIM
    n_experts = w_gate.shape[0]
    row = lambda v: v.reshape(1, -1)

    mod = _adaln(c, w_ada, b_ada).reshape(b, 6, d)

    o1 = q_lora
    o2 = o1 + kv_lora
    o3 = o2 + QK_ROPE_DIM
    o4 = o3 + lru_w
    w_in_p = jnp.concatenate(
        [w_in[:, :o2], w_in[:, o3:o4], w_in[:, o4:], w_in[:, o2:o3],
         jnp.zeros((d, LANES - QK_ROPE_DIM), w_in.dtype)], axis=1).astype(BF16)
    pad_h = QK_PAD_DIM - QK_HEAD_DIM
    w_q_p = jnp.pad(w_q_b.reshape(q_lora, ATTN_HEADS, QK_HEAD_DIM),
                    ((0, 0), (0, 0), (0, pad_h))).reshape(q_lora, ATTN_HEADS * QK_PAD_DIM).astype(BF16)
    q_norm_p = row(jnp.pad(q_norm, (0, pad_h)))
    k_norm_nope = row(k_norm[:QK_NOPE_DIM])
    k_norm_rope_p = row(jnp.pad(k_norm[QK_NOPE_DIM:], (0, LANES - QK_ROPE_DIM)))
    inv = 1.0 / (ROPE_THETA ** (np.arange(0, QK_ROPE_DIM, 2, dtype=np.float32) / QK_ROPE_DIM))
    inv_col = jnp.asarray(inv.reshape(-1, 1))

    q, k, v, x_lru, gelu_g = _in_proj(
        x, pos_row, mod, row(norm1_g), w_in_p, row(q_a_norm), w_q_p, row(kv_a_norm),
        w_kv_b.astype(BF16), q_norm_p, k_norm_nope, k_norm_rope_p, inv_col,
        q_lora=q_lora, kv_lora=kv_lora, lru_w=lru_w)

    attn = _attention(q, k, v)

    w_ax = jnp.concatenate([lru_wa, lru_wx], axis=-1).astype(BF16)
    lru_n = _rg_lru(x_lru, gelu_g, pos_row, conv_w, row(conv_b), w_ax, row(lru_ba), row(lru_bx),
                    row(lru_lambda), row(lru_out_norm))

    group_row0, expert_row0 = 0, SUBLANES
    w_r_t = jnp.concatenate([w_rg.T, jnp.zeros((expert_row0 - N_GROUPS, d), w_rg.dtype), w_re.T],
                            axis=0).astype(BF16)
    b_r_col = jnp.concatenate([b_rg, jnp.zeros((expert_row0 - N_GROUPS,), b_rg.dtype),
                               b_re]).reshape(-1, 1)
    x1, ids, ranks, wts, cnt = _out_route(
        x, attn, lru_n, mod, row(attn_out_norm), w_out[:attn_w].astype(BF16),
        w_out[attn_w:].astype(BF16), row(norm2_g), w_r_t, b_r_col,
        n_experts=n_experts, group_row0=group_row0, expert_row0=expert_row0)

    blk = EXPERT_BLOCK
    counts = cnt[:, 0].astype(jnp.int32)
    padded = (counts + blk - 1) // blk * blk
    pad_ends = jnp.cumsum(padded)
    starts = (pad_ends - padded).astype(jnp.int32)
    dest = _slots(starts, ids.reshape(-1, LANES), ranks.reshape(-1, LANES)).reshape(ids.shape)
    cap = (n * TOP_K + n_experts * (blk - 1) + blk - 1) // blk * blk
    n_blocks = cap // blk
    block_first_row = jnp.arange(n_blocks, dtype=jnp.int32) * blk
    block_e = jnp.minimum(jnp.sum(pad_ends[None, :] <= block_first_row[:, None], axis=1),
                          n_experts - 1).astype(jnp.int32)
    n_used = (pad_ends[-1:] // blk).astype(jnp.int32)

    xs = _dispatch(counts, starts, n_used, dest, x1, mod, row(norm2_g), cap,
                   block=blk, n_experts=n_experts)
    ys = _experts(block_e, n_used, xs, w_gate, w_up, w_down, block=blk)
    return _combine(dest, wts, x1, mod, ys)


def kernel(x, c, positions, w_ada, b_ada, norm1_g, w_in, q_a_norm, w_q_b, kv_a_norm, w_kv_b, q_norm, k_norm, conv_w, conv_b, lru_wa, lru_ba, lru_wx, lru_bx, lru_lambda, attn_out_norm, lru_out_norm, w_out, norm2_g, w_router_group, b_router_group, w_router_expert, b_router_expert, w_gate, w_up, w_down):
    b, s, _ = x.shape
    pos_row = positions.reshape(b, 1, s)
    for l in range(w_ada.shape[0]):
        x = _layer(x, c, pos_row, w_ada[l], b_ada[l], norm1_g[l], w_in[l], q_a_norm[l], w_q_b[l],
                   kv_a_norm[l], w_kv_b[l], q_norm[l], k_norm[l], conv_w[l], conv_b[l], lru_wa[l],
                   lru_ba[l], lru_wx[l], lru_bx[l], lru_lambda[l], attn_out_norm[l], lru_out_norm[l],
                   w_out[l], norm2_g[l], w_router_group[l], b_router_group[l], w_router_expert[l],
                   b_router_expert[l], w_gate[l], w_up[l], w_down[l])
    return x
```

```python
import functools
import math

import numpy as np
import jax
import jax.numpy as jnp
from jax import lax
from jax.experimental import pallas as pl
from jax.experimental.pallas import tpu as pltpu

F32 = jnp.float32
BF16 = jnp.bfloat16

ATTN_HEADS = 4
QK_NOPE_DIM = 128
QK_ROPE_DIM = 64
QK_HEAD_DIM = QK_NOPE_DIM + QK_ROPE_DIM
V_HEAD_DIM = 128
ROPE_THETA = 10000.0
LRU_HEADS = 4
CONV_WIDTH = 4
LRU_C = 8.0
N_GROUPS = 4
TOP_K = 2
EPS = 1e-6

LANES = 128
SUBLANES = 8
QK_PAD_DIM = 256
V_PAD_DIM = 256
VMEM_LIMIT = 56 * 1024 * 1024

PROJ_TILE = 1024
PROJ_CHUNK = 128
ATTN_TILE = 1024
ATTN_CHUNK = 256
LRU_TILE = 1024
ROUTE_TILE = 1024
ROUTE_DEPTH = 3
DISPATCH_TILE = 512
COMBINE_TILE = 512
EXPERT_BLOCK = 512
EXPERT_DEPTH = 3


def _cparams(*sem):
    return pltpu.CompilerParams(dimension_semantics=sem, vmem_limit_bytes=VMEM_LIMIT)


def _rms(x, n):
    return lax.rsqrt(jnp.sum(x * x, axis=-1, keepdims=True) * (1.0 / n) + EPS)


def _sigmoid(x):
    return 0.5 * jnp.tanh(0.5 * x) + 0.5


def _column(row, width=LANES):
    t = row.shape[-1]
    return jnp.broadcast_to(row, (width, t)).T


def _adaln_kernel(c_ref, w_ref, b_ref, o_ref):
    c = c_ref[...]
    sc = c * jax.nn.sigmoid(c)
    o_ref[...] = jnp.dot(sc.astype(BF16), w_ref[...].astype(BF16),
                         preferred_element_type=F32) + b_ref[...]


def _adaln(c, w_ada, b_ada):
    b, d = c.shape
    n_out = w_ada.shape[1]
    return pl.pallas_call(
        _adaln_kernel,
        grid=(n_out // d,),
        in_specs=[pl.BlockSpec((b, d), lambda j: (0, 0)),
                  pl.BlockSpec((d, d), lambda j: (0, j)),
                  pl.BlockSpec((1, d), lambda j: (0, j))],
        out_specs=pl.BlockSpec((b, d), lambda j: (0, j)),
        out_shape=jax.ShapeDtypeStruct((b, n_out), F32),
        compiler_params=_cparams("parallel"),
        name="adaln",
    )(c, w_ada, b_ada.reshape(1, n_out))


def _rope_tables(pos_row, inv_col):
    half = QK_ROPE_DIM // 2
    ang = inv_col * pos_row
    cos = jnp.cos(ang)
    sin = jnp.sin(ang)
    t = ang.shape[-1]
    zeros = lambda n: jnp.zeros((n, t), F32)
    c = jnp.concatenate([cos, cos, zeros(LANES - QK_ROPE_DIM)], axis=0)
    sa = jnp.concatenate([-sin, zeros(LANES - half)], axis=0)
    sb = jnp.concatenate([zeros(half), sin, zeros(LANES - QK_ROPE_DIM)], axis=0)
    return c.T, sa.T, sb.T


def _rope(x, c, sa, sb):
    return x * c + pltpu.roll(x, LANES - QK_ROPE_DIM // 2, 1) * sa + pltpu.roll(x, QK_ROPE_DIM // 2, 1) * sb


def _in_proj_kernel(x_ref, pos_ref, mod_ref, g1_ref, w_in_ref, qa_ref, wq_ref, kva_ref, wkv_ref,
                    qn_ref, kn_nope_ref, kn_rope_ref, inv_ref,
                    q_ref, k_ref, v_ref, xl_ref, gl_ref, *, q_lora, kv_lora, lru_w, chunk):
    t, d = x_ref.shape[1:]
    shift = mod_ref[0, 0:1, :]
    gain = g1_ref[...] * (1.0 + mod_ref[0, 1:2, :])
    c_all, sa_all, sb_all = _rope_tables(pos_ref[0].astype(F32), inv_ref[...])
    sm_scale = QK_HEAD_DIM ** -0.5 * math.log2(math.e)
    ones = jnp.ones((chunk, V_PAD_DIM - V_HEAD_DIM), BF16)

    for r0 in range(0, t, chunk):
        rows = slice(r0, r0 + chunk)
        x = x_ref[0, rows, :]
        h = x * _rms(x, d) * gain + shift
        proj = jnp.dot(h.astype(BF16), w_in_ref[...], preferred_element_type=F32)
        o = 0
        q_lat = proj[:, o:o + q_lora]; o += q_lora
        kv_lat = proj[:, o:o + kv_lora]; o += kv_lora
        x_lru = proj[:, o:o + lru_w]; o += lru_w
        g_lru = proj[:, o:o + lru_w]; o += lru_w
        k_pe = proj[:, o:o + LANES]

        xl_ref[0, rows, :] = x_lru
        gl_ref[0, rows, :] = jax.nn.gelu(g_lru).astype(BF16)

        c, sa, sb = c_all[rows], sa_all[rows], sb_all[rows]
        qn = q_lat * _rms(q_lat, q_lora) * qa_ref[...]
        q_full = jnp.dot(qn.astype(BF16), wq_ref[...], preferred_element_type=F32)
        kvn = kv_lat * _rms(kv_lat, kv_lora) * kva_ref[...]
        kv_full = jnp.dot(kvn.astype(BF16), wkv_ref[...], preferred_element_type=F32)

        pe_ss = jnp.sum(k_pe * k_pe, axis=-1, keepdims=True)
        pe_rot = _rope(k_pe * kn_rope_ref[...], c, sa, sb)
        for hd in range(ATTN_HEADS):
            qh = q_full[:, hd * QK_PAD_DIM:(hd + 1) * QK_PAD_DIM]
            qh = qh * (_rms(qh, QK_HEAD_DIM) * sm_scale) * qn_ref[...]
            q_ref[0, hd, rows, 0:QK_NOPE_DIM] = qh[:, 0:QK_NOPE_DIM].astype(BF16)
            q_ref[0, hd, rows, QK_NOPE_DIM:QK_PAD_DIM] = _rope(qh[:, QK_NOPE_DIM:QK_PAD_DIM],
                                                               c, sa, sb).astype(BF16)
            base = hd * (QK_NOPE_DIM + V_HEAD_DIM)
            kn = kv_full[:, base:base + QK_NOPE_DIM]
            vv = kv_full[:, base + QK_NOPE_DIM:base + QK_NOPE_DIM + V_HEAD_DIM]
            r = lax.rsqrt((jnp.sum(kn * kn, axis=-1, keepdims=True) + pe_ss) * (1.0 / QK_HEAD_DIM) + EPS)
            k_ref[0, hd, rows, 0:QK_NOPE_DIM] = (kn * r * kn_nope_ref[...]).astype(BF16)
            k_ref[0, hd, rows, QK_NOPE_DIM:QK_PAD_DIM] = (pe_rot * r).astype(BF16)
            v_ref[0, hd, rows, 0:V_HEAD_DIM] = vv.astype(BF16)
            v_ref[0, hd, rows, V_HEAD_DIM:V_PAD_DIM] = ones


def _in_proj(x, pos_row, mod, norm1_g, w_in_p, q_a_norm, w_q_p, kv_a_norm, w_kv, q_norm_p,
             k_norm_nope, k_norm_rope_p, inv_col, *, q_lora, kv_lora, lru_w):
    b, s, d = x.shape
    t = min(PROJ_TILE, s)
    full = lambda a: pl.BlockSpec(a.shape, lambda i, j: (0,) * a.ndim)
    kern = functools.partial(_in_proj_kernel, q_lora=q_lora, kv_lora=kv_lora, lru_w=lru_w,
                             chunk=min(PROJ_CHUNK, t))
    head_out = lambda w: pl.BlockSpec((1, ATTN_HEADS, t, w), lambda i, j: (i, 0, j, 0))
    return pl.pallas_call(
        kern,
        grid=(b, s // t),
        in_specs=[pl.BlockSpec((1, t, d), lambda i, j: (i, j, 0)),
                  pl.BlockSpec((1, 1, t), lambda i, j: (i, 0, j)),
                  pl.BlockSpec((1,) + mod.shape[1:], lambda i, j: (i, 0, 0)),
                  full(norm1_g), full(w_in_p), full(q_a_norm), full(w_q_p), full(kv_a_norm),
                  full(w_kv), full(q_norm_p), full(k_norm_nope), full(k_norm_rope_p), full(inv_col)],
        out_specs=[head_out(QK_PAD_DIM), head_out(QK_PAD_DIM), head_out(V_PAD_DIM),
                   pl.BlockSpec((1, t, lru_w), lambda i, j: (i, j, 0)),
                   pl.BlockSpec((1, t, lru_w), lambda i, j: (i, j, 0))],
        out_shape=[jax.ShapeDtypeStruct((b, ATTN_HEADS, s, QK_PAD_DIM), BF16),
                   jax.ShapeDtypeStruct((b, ATTN_HEADS, s, QK_PAD_DIM), BF16),
                   jax.ShapeDtypeStruct((b, ATTN_HEADS, s, V_PAD_DIM), BF16),
                   jax.ShapeDtypeStruct((b, s, lru_w), F32),
                   jax.ShapeDtypeStruct((b, s, lru_w), BF16)],
        compiler_params=_cparams("parallel", "parallel"),
        name="in_proj",
    )(x, pos_row, mod, norm1_g, w_in_p, q_a_norm, w_q_p, kv_a_norm, w_kv, q_norm_p,
      k_norm_nope, k_norm_rope_p, inv_col)


def _attn_kernel(q_ref, k_ref, v_ref, o_ref, m_sc, acc_sc, s_sc, *, tile, chunk):
    qi = pl.program_id(2)
    m_sc[...] = jnp.full(m_sc.shape, -jnp.inf, F32)
    acc_sc[...] = jnp.zeros(acc_sc.shape, F32)

    def scores(j, slot):
        start = pl.multiple_of(j * tile, tile)
        k = k_ref[0, 0, pl.ds(start, tile), :]
        s_sc[slot] = lax.dot_general(q_ref[0, 0], k, (((1,), (1,)), ((), ())),
                                     preferred_element_type=F32)

    def accumulate(j, slot, masked):
        start = pl.multiple_of(j * tile, tile)
        for r in range(tile // chunk):
            rows = slice(r * chunk, (r + 1) * chunk)
            cols = -(-(r + 1) * chunk // LANES) * LANES if masked else tile
            s = s_sc[slot, rows, 0:cols]
            if masked:
                row = lax.broadcasted_iota(jnp.int32, s.shape, 0) + r * chunk
                col = lax.broadcasted_iota(jnp.int32, s.shape, 1)
                s = jnp.where(col <= row, s, -jnp.inf)
            v = v_ref[0, 0, pl.ds(start, cols), :]
            m_old = m_sc[rows, :]
            m_new = jnp.maximum(m_old, jnp.max(s, axis=-1, keepdims=True))
            alpha = jnp.exp2(m_old - m_new)
            p = jnp.exp2(s - jnp.tile(m_new, (1, cols // LANES)))
            acc_sc[rows, :] = (jnp.tile(alpha, (1, V_PAD_DIM // LANES)) * acc_sc[rows, :]
                               + jnp.dot(p.astype(BF16), v, preferred_element_type=F32))
            m_sc[rows, :] = m_new

    scores(0, 0)

    def pair(p, carry):
        scores(2 * p + 1, 1)
        accumulate(2 * p, 0, False)
        scores(2 * p + 2, 0)
        accumulate(2 * p + 1, 1, False)
        return carry

    lax.fori_loop(0, qi // 2, pair, 0)

    @pl.when(qi % 2 == 0)
    def _():
        accumulate(qi, 0, True)

    @pl.when(qi % 2 == 1)
    def _():
        scores(qi, 1)
        accumulate(qi - 1, 0, False)
        accumulate(qi, 1, True)

    o_ref[0] = (acc_sc[:, 0:V_HEAD_DIM] / acc_sc[:, V_HEAD_DIM:2 * V_HEAD_DIM]).astype(o_ref.dtype)


def _attention(q, k, v):
    b, h, s, _ = q.shape
    t = min(ATTN_TILE, s)
    assert V_HEAD_DIM == LANES and V_PAD_DIM == 2 * V_HEAD_DIM
    kern = functools.partial(_attn_kernel, tile=t, chunk=min(ATTN_CHUNK, t))
    return pl.pallas_call(
        kern,
        grid=(b, h, s // t),
        in_specs=[pl.BlockSpec((1, 1, t, QK_PAD_DIM), lambda i, j, n: (i, j, n, 0)),
                  pl.BlockSpec((1, 1, s, QK_PAD_DIM), lambda i, j, n: (i, j, 0, 0)),
                  pl.BlockSpec((1, 1, s, V_PAD_DIM), lambda i, j, n: (i, j, 0, 0))],
        out_specs=pl.BlockSpec((1, t, V_HEAD_DIM), lambda i, j, n: (i, n, j)),
        out_shape=jax.ShapeDtypeStruct((b, s, h * V_HEAD_DIM), BF16),
        scratch_shapes=[pltpu.VMEM((t, LANES), F32), pltpu.VMEM((t, V_PAD_DIM), F32),
                        pltpu.VMEM((2, t, t), F32)],
        compiler_params=_cparams("parallel", "parallel", "arbitrary"),
        name="attention",
    )(q, k, v)


def _lru_kernel(xl_ref, gl_ref, pos_ref, cw_ref, cb_ref, wax_ref, ba_ref, bx_ref, lam_ref, gn_ref,
                o_ref, ext_sc, a_sc, b_sc, h_sc, carry_sc):
    t, c = a_sc.shape
    head_w = c // LRU_HEADS

    @pl.when(pl.program_id(1) == 0)
    def _():
        ext_sc[0:SUBLANES, :] = jnp.zeros((SUBLANES, c), F32)
        carry_sc[...] = jnp.zeros(carry_sc.shape, F32)

    x = xl_ref[0]
    ext_sc[SUBLANES:SUBLANES + t, :] = x
    xc = cb_ref[...] + x * cw_ref[CONV_WIDTH - 1:CONV_WIDTH, :]
    for back in range(1, CONV_WIDTH):
        tap = CONV_WIDTH - 1 - back
        xc = xc + ext_sc[SUBLANES - back:SUBLANES - back + t, :] * cw_ref[tap:tap + 1, :]
    ext_sc[0:SUBLANES, :] = x[t - SUBLANES:t, :]

    reset = _column((pos_ref[0] == 0).astype(F32)) > 0.5
    lam = lam_ref[...]
    neg_sp = -LRU_C * (jnp.maximum(-lam, 0.0) + jnp.log1p(jnp.exp(-jnp.abs(lam))))
    for hd in range(LRU_HEADS):
        sl = slice(hd * head_w, (hd + 1) * head_w)
        xh = xc[:, sl]
        gates = jnp.dot(xh.astype(BF16), wax_ref[hd], preferred_element_type=F32)
        r = _sigmoid(gates[:, :head_w] + ba_ref[:, sl])
        i = _sigmoid(gates[:, head_w:] + bx_ref[:, sl])
        a = jnp.exp(r * neg_sp[:, sl])
        gap = jnp.maximum(1.0 - a * a, 0.0)
        mult = jnp.where(gap > 0.0, gap * lax.rsqrt(gap), 0.0)
        a = jnp.where(reset, 0.0, a)
        mult = jnp.where(reset, 1.0, mult)
        a_sc[:, sl] = a
        b_sc[:, sl] = mult * (i * xh)

    srow = lax.broadcasted_iota(jnp.int32, (SUBLANES, c), 0)

    def group(g, h_prev):
        start = pl.multiple_of(g * SUBLANES, SUBLANES)
        a = a_sc[pl.ds(start, SUBLANES), :]
        bb = b_sc[pl.ds(start, SUBLANES), :]
        sh = 1
        while sh < SUBLANES:
            keep = srow >= sh
            a_prev = jnp.where(keep, pltpu.roll(a, sh, 0), 1.0)
            b_prev = jnp.where(keep, pltpu.roll(bb, sh, 0), 0.0)
            bb = bb + a * b_prev
            a = a * a_prev
            sh *= 2
        hh = bb + a * h_prev
        h_sc[pl.ds(start, SUBLANES), :] = hh
        return jnp.broadcast_to(hh[SUBLANES - 1:SUBLANES, :], (SUBLANES, c))

    carry_sc[...] = lax.fori_loop(0, t // SUBLANES, group, carry_sc[...], unroll=8)

    y = h_sc[...] * gl_ref[0].astype(F32)
    o_ref[0] = (y * _rms(y, c) * gn_ref[...]).astype(o_ref.dtype)


def _rg_lru(x_lru, gelu_g, pos_row, conv_w, conv_b, w_ax, b_a, b_x, lam, out_norm):
    b, s, c = x_lru.shape
    t = min(LRU_TILE, s)
    full = lambda a: pl.BlockSpec(a.shape, lambda i, j: (0,) * a.ndim)
    return pl.pallas_call(
        _lru_kernel,
        grid=(b, s // t),
        in_specs=[pl.BlockSpec((1, t, c), lambda i, j: (i, j, 0)),
                  pl.BlockSpec((1, t, c), lambda i, j: (i, j, 0)),
                  pl.BlockSpec((1, 1, t), lambda i, j: (i, 0, j)),
                  full(conv_w), full(conv_b), full(w_ax), full(b_a), full(b_x), full(lam),
                  full(out_norm)],
        out_specs=pl.BlockSpec((1, t, c), lambda i, j: (i, j, 0)),
        out_shape=jax.ShapeDtypeStruct((b, s, c), BF16),
        scratch_shapes=[pltpu.VMEM((t + SUBLANES, c), F32), pltpu.VMEM((t, c), F32),
                        pltpu.VMEM((t, c), F32), pltpu.VMEM((t, c), F32),
                        pltpu.VMEM((SUBLANES, c), F32)],
        compiler_params=_cparams("parallel", "arbitrary"),
        name="rg_lru",
    )(x_lru, gelu_g, pos_row, conv_w, conv_b, w_ax, b_a, b_x, lam, out_norm)


def _route_kernel(x_ref, at_ref, lr_ref, mod_ref, an_ref, wa_ref, wl_ref, g2_ref, wr_ref, br_ref,
                  x1_ref, ids_ref, rank_ref, wts_ref, cnt_ref, carry_sc, xbuf, sem,
                  *, n_experts, group_row0, expert_row0):
    i = pl.program_id(0)
    n_steps = pl.num_programs(0)
    t, d = xbuf.shape[1:]

    def fetch(step, slot):
        rows = pl.ds(pl.multiple_of(step * t, t), t)
        return pltpu.make_async_copy(x_ref.at[rows, :], xbuf.at[slot], sem.at[slot])

    @pl.when(i == 0)
    def _():
        carry_sc[...] = jnp.zeros(carry_sc.shape, F32)
        for ahead in range(ROUTE_DEPTH - 1):
            @pl.when(ahead < n_steps)
            def _(ahead=ahead):
                fetch(ahead, ahead).start()

    @pl.when(i + ROUTE_DEPTH - 1 < n_steps)
    def _():
        fetch(i + ROUTE_DEPTH - 1, (i + ROUTE_DEPTH - 1) % ROUTE_DEPTH).start()

    fetch(i, i % ROUTE_DEPTH).wait()
    x = xbuf[i % ROUTE_DEPTH]
    gate1 = mod_ref[0, 2:3, :]
    shift2 = mod_ref[0, 3:4, :]
    gain2 = g2_ref[...] * (1.0 + mod_ref[0, 4:5, :])
    at = at_ref[0].astype(F32)
    at_n = at * _rms(at, at.shape[-1]) * an_ref[...]
    mix = jnp.dot(at_n.astype(BF16), wa_ref[...], preferred_element_type=F32)
    mix = mix + jnp.dot(lr_ref[0], wl_ref[...], preferred_element_type=F32)
    x1 = x + gate1 * mix
    x1_ref[0] = x1
    h2 = x1 * _rms(x1, d) * gain2 + shift2

    logits = lax.dot_general(wr_ref[...], h2.astype(BF16), (((1,), (1,)), ((), ())),
                             preferred_element_type=F32) + br_ref[...]
    per_group = n_experts // N_GROUPS
    lg = logits[group_row0:group_row0 + N_GROUPS, :]
    grow = lax.broadcasted_iota(jnp.int32, lg.shape, 0)
    gmax = jnp.max(lg, axis=0, keepdims=True)
    g_idx = jnp.min(jnp.where(lg == gmax, grow, N_GROUPS), axis=0, keepdims=True)
    g_p = 1.0 / jnp.sum(jnp.exp(lg - gmax), axis=0, keepdims=True)
    sel = jnp.zeros((per_group, t), F32)
    for g in range(N_GROUPS):
        r0 = expert_row0 + g * per_group
        sel = sel + jnp.where(g_idx == g, logits[r0:r0 + per_group, :], 0.0)
    erow = lax.broadcasted_iota(jnp.int32, sel.shape, 0)
    m1 = jnp.max(sel, axis=0, keepdims=True)
    i1 = jnp.min(jnp.where(sel == m1, erow, per_group), axis=0, keepdims=True)
    sel2 = jnp.where(erow == i1, -jnp.inf, sel)
    m2 = jnp.max(sel2, axis=0, keepdims=True)
    i2 = jnp.min(jnp.where(sel2 == m2, erow, per_group), axis=0, keepdims=True)
    e21 = jnp.exp(m2 - m1)
    w1 = g_p / (1.0 + e21)
    w2 = g_p * e21 / (1.0 + e21)
    e1 = g_idx * per_group + i1
    e2 = g_idx * per_group + i2
    ids_ref[0:1, :] = e1
    ids_ref[1:2, :] = e2
    wts_ref[0:1, :] = w1
    wts_ref[1:2, :] = w2

    xrow = lax.broadcasted_iota(jnp.int32, (n_experts, t), 0)
    o1 = (xrow == e1).astype(F32)
    o2 = (xrow == e2).astype(F32)
    both = o1 + o2
    tr = lax.broadcasted_iota(jnp.int32, (t, t), 0)
    tc = lax.broadcasted_iota(jnp.int32, (t, t), 1)
    upper = jnp.where(tr < tc, 1.0, 0.0).astype(BF16)
    before = jnp.dot(both.astype(BF16), upper, preferred_element_type=F32) + carry_sc[...]
    rank_ref[0:1, :] = jnp.sum(o1 * before, axis=0, keepdims=True).astype(jnp.int32)
    rank_ref[1:2, :] = jnp.sum(o2 * before, axis=0, keepdims=True).astype(jnp.int32)
    carry_sc[...] = carry_sc[...] + jnp.sum(both, axis=1, keepdims=True)
    cnt_ref[...] = jnp.broadcast_to(carry_sc[...], cnt_ref.shape)


def _out_route(x, attn, lru_n, mod, attn_norm, w_out_a, w_out_l, norm2_g, w_r_t, b_r_col,
               *, n_experts, group_row0, expert_row0):
    b, s, d = x.shape
    t = min(ROUTE_TILE, s)
    nt = s // t
    n = b * s
    full = lambda a: pl.BlockSpec(a.shape, lambda i: (0,) * a.ndim)
    tok = lambda w: pl.BlockSpec((1, t, w), lambda i: (i // nt, i % nt, 0))
    kern = functools.partial(_route_kernel, n_experts=n_experts, group_row0=group_row0,
                             expert_row0=expert_row0)
    return pl.pallas_call(
        kern,
        grid=(b * nt,),
        in_specs=[pl.BlockSpec(memory_space=pl.ANY), tok(attn.shape[-1]), tok(lru_n.shape[-1]),
                  pl.BlockSpec((1,) + mod.shape[1:], lambda i: (i // nt, 0, 0)),
                  full(attn_norm), full(w_out_a), full(w_out_l), full(norm2_g), full(w_r_t),
                  full(b_r_col)],
        out_specs=[tok(d),
                   pl.BlockSpec((TOP_K, t), lambda i: (0, i)),
                   pl.BlockSpec((TOP_K, t), lambda i: (0, i)),
                   pl.BlockSpec((TOP_K, t), lambda i: (0, i)),
                   pl.BlockSpec((n_experts, LANES), lambda i: (0, 0))],
        out_shape=[jax.ShapeDtypeStruct((b, s, d), F32),
                   jax.ShapeDtypeStruct((TOP_K, n), jnp.int32),
                   jax.ShapeDtypeStruct((TOP_K, n), jnp.int32),
                   jax.ShapeDtypeStruct((TOP_K, n), F32),
                   jax.ShapeDtypeStruct((n_experts, LANES), F32)],
        scratch_shapes=[pltpu.VMEM((n_experts, 1), F32), pltpu.VMEM((ROUTE_DEPTH, t, d), F32),
                        pltpu.SemaphoreType.DMA((ROUTE_DEPTH,))],
        compiler_params=_cparams("arbitrary"),
        name="out_route",
    )(x.reshape(n, d), attn, lru_n, mod, attn_norm, w_out_a, w_out_l, norm2_g, w_r_t, b_r_col)


def _slots_kernel(start_ref, ids_ref, rank_ref, o_ref):
    ids = ids_ref[...]
    slot = rank_ref[...]
    for e in range(start_ref.shape[0]):
        slot = slot + jnp.where(ids == e, start_ref[e], 0)
    o_ref[...] = slot


def _slots(starts, ids, ranks):
    whole = pl.BlockSpec(ids.shape, lambda i, st: (0, 0))
    return pl.pallas_call(
        _slots_kernel,
        grid_spec=pltpu.PrefetchScalarGridSpec(num_scalar_prefetch=1, grid=(1,),
                                               in_specs=[whole, whole], out_specs=whole),
        out_shape=jax.ShapeDtypeStruct(ids.shape, jnp.int32),
        compiler_params=_cparams("arbitrary"),
        name="slots",
    )(starts, ids, ranks)


def _row_copy(src, src_row, dst, dst_row, sem):
    return pltpu.make_async_copy(src.at[pl.ds(src_row, 1), :], dst.at[pl.ds(dst_row, 1), :], sem)


def _dispatch_kernel(cnt_ref, start_ref, nb_ref, dest_ref, x1_ref, mod_ref, g2_ref, xs_ref, h2_sc,
                     zero_sc, sem, *, tile, block, n_experts):
    i = pl.program_id(0)

    x1 = x1_ref[0]
    gain2 = g2_ref[...] * (1.0 + mod_ref[0, 4:5, :])
    h2_sc[...] = x1 * _rms(x1, x1.shape[-1]) * gain2 + mod_ref[0, 3:4, :]

    @pl.when(i == 0)
    def _():
        zero_sc[...] = jnp.zeros(zero_sc.shape, zero_sc.dtype)

        def tail_copy(j):
            return pltpu.make_async_copy(zero_sc, xs_ref.at[pl.ds(j * block, block), :], sem)

        def tail_issue(j, carry):
            tail_copy(j).start()
            return carry

        def tail_drain(j, carry):
            tail_copy(j).wait()
            return carry

        n_blocks = xs_ref.shape[0] // block
        lax.fori_loop(nb_ref[0], n_blocks, tail_issue, 0)
        lax.fori_loop(nb_ref[0], n_blocks, tail_drain, 0)

        def pad_copies(e, act):
            cnt = cnt_ref[e]
            first = start_ref[e] + cnt
            n_pad = (block - cnt % block) % block
            n_single = (SUBLANES - first % SUBLANES) % SUBLANES
            for r in range(SUBLANES - 1):
                @pl.when(r < n_single)
                def _(r=r):
                    act(_row_copy(zero_sc, 0, xs_ref, first + r, sem))

            row = first + n_single
            rest = n_pad - n_single
            for bit in range(SUBLANES.bit_length() - 1, block.bit_length() - 1):
                size = 1 << bit
                take = (rest >> bit) & 1

                @pl.when(take == 1)
                def _(row=row, size=size):
                    act(pltpu.make_async_copy(
                        zero_sc.at[pl.ds(0, size), :],
                        xs_ref.at[pl.ds(pl.multiple_of(row, SUBLANES), size), :], sem))

                row = row + take * size

        def pad_issue(e, carry):
            pad_copies(e, lambda cp: cp.start())
            return carry

        def pad_drain(e, carry):
            pad_copies(e, lambda cp: cp.wait())
            return carry

        lax.fori_loop(0, n_experts, pad_issue, 0)
        lax.fori_loop(0, n_experts, pad_drain, 0)

    for tk in range(tile):
        for k in range(TOP_K):
            _row_copy(h2_sc, tk, xs_ref, dest_ref[k, tk], sem).start(priority=k % 2)

    def drain(tk, carry):
        for _ in range(TOP_K):
            _row_copy(h2_sc, 0, xs_ref, 0, sem).wait()
        return carry

    lax.fori_loop(0, tile, drain, 0, unroll=8)


def _dispatch(counts, starts, n_blocks_used, dest, x1, mod, norm2_g, cap, *, block, n_experts):
    b, s, w = x1.shape
    t = min(DISPATCH_TILE, s)
    nt = s // t
    kern = functools.partial(_dispatch_kernel, tile=t, block=block, n_experts=n_experts)
    smem_tile = pl.BlockSpec((TOP_K, t), lambda i, c, st, nb: (0, i), memory_space=pltpu.SMEM)
    return pl.pallas_call(
        kern,
        grid_spec=pltpu.PrefetchScalarGridSpec(
            num_scalar_prefetch=3,
            grid=(b * nt,),
            in_specs=[smem_tile,
                      pl.BlockSpec((1, t, w), lambda i, c, st, nb: (i // nt, i % nt, 0)),
                      pl.BlockSpec((1,) + mod.shape[1:], lambda i, c, st, nb: (i // nt, 0, 0)),
                      pl.BlockSpec(norm2_g.shape, lambda i, c, st, nb: (0, 0))],
            out_specs=pl.BlockSpec(memory_space=pl.ANY),
            scratch_shapes=[pltpu.VMEM((t, w), F32), pltpu.VMEM((block, w), F32),
                            pltpu.SemaphoreType.DMA(())]),
        out_shape=jax.ShapeDtypeStruct((cap, w), F32),
        compiler_params=_cparams("arbitrary"),
        name="dispatch",
    )(counts, starts, n_blocks_used, dest, x1, mod, norm2_g)


def _expert_kernel(be_ref, nb_ref, xs_ref, wg_ref, wu_ref, wd_ref, ys_ref, xbuf, wgu_sc, wdn_sc, sem,
                   *, d_ff, block):
    j = pl.program_id(0)
    n_live = nb_ref[0]
    live = jnp.minimum(j, n_live - 1)

    def fetch(blk, slot):
        rows = pl.ds(pl.multiple_of(blk * block, block), block)
        return pltpu.make_async_copy(xs_ref.at[rows, :], xbuf.at[slot], sem.at[slot])

    @pl.when(j == 0)
    def _():
        for ahead in range(EXPERT_DEPTH - 1):
            @pl.when(ahead < n_live)
            def _(ahead=ahead):
                fetch(ahead, ahead).start()

    @pl.when(j + EXPERT_DEPTH - 1 < n_live)
    def _():
        fetch(j + EXPERT_DEPTH - 1, (j + EXPERT_DEPTH - 1) % EXPERT_DEPTH).start()

    @pl.when((j == 0) | (be_ref[live] != be_ref[jnp.maximum(live - 1, 0)]))
    def _():
        wgu_sc[:, 0:d_ff] = wg_ref[0].astype(BF16)
        wgu_sc[:, d_ff:2 * d_ff] = wu_ref[0].astype(BF16)
        wdn_sc[...] = wd_ref[0].astype(BF16)

    @pl.when(j < n_live)
    def _():
        slot = j % EXPERT_DEPTH
        fetch(j, slot).wait()
        xb = xbuf[slot].astype(BF16)
        gu = jnp.dot(xb, wgu_sc[...], preferred_element_type=F32)
        gt = gu[:, :d_ff]
        hid = gt * jax.nn.sigmoid(gt) * gu[:, d_ff:]
        ys_ref[...] = jnp.dot(hid.astype(BF16), wdn_sc[...], preferred_element_type=F32)

    @pl.when(pl.program_id(0) >= nb_ref[0])
    def _():
        ys_ref[...] = jnp.zeros(ys_ref.shape, ys_ref.dtype)


def _experts(block_e, n_blocks_used, xs, w_gate, w_up, w_down, *, block):
    cap, w = xs.shape
    d = w_down.shape[2]
    d_ff = w_down.shape[1]
    kern = functools.partial(_expert_kernel, d_ff=d_ff, block=block)
    live = lambda j, nb: jnp.minimum(j, nb[0] - 1)
    return pl.pallas_call(
        kern,
        grid_spec=pltpu.PrefetchScalarGridSpec(
            num_scalar_prefetch=2,
            grid=(cap // block,),
            in_specs=[pl.BlockSpec(memory_space=pl.ANY),
                      pl.BlockSpec((1, d, d_ff), lambda j, be, nb: (be[live(j, nb)], 0, 0)),
                      pl.BlockSpec((1, d, d_ff), lambda j, be, nb: (be[live(j, nb)], 0, 0)),
                      pl.BlockSpec((1, d_ff, d), lambda j, be, nb: (be[live(j, nb)], 0, 0))],
            out_specs=pl.BlockSpec((block, w), lambda j, be, nb: (j, 0)),
            scratch_shapes=[pltpu.VMEM((EXPERT_DEPTH, block, w), F32),
                            pltpu.VMEM((d, 2 * d_ff), BF16), pltpu.VMEM((d_ff, d), BF16),
                            pltpu.SemaphoreType.DMA((EXPERT_DEPTH,))]),
        out_shape=jax.ShapeDtypeStruct((cap, w), F32),
        compiler_params=_cparams("arbitrary"),
        name="experts",
    )(block_e, n_blocks_used, xs, w_gate, w_up, w_down)


def _combine_kernel(dest_ref, dest_nx_ref, wts_ref, x1_ref, mod_ref, ys_ref, o_ref, buf, sem, *,
                    tile):
    i = pl.program_id(0)
    slot = i % 2

    def gather(d_ref, sl):
        for tk in range(tile):
            for k in range(TOP_K):
                _row_copy(ys_ref, d_ref[k, tk], buf.at[sl, k], tk, sem.at[sl]).start(priority=k % 2)

    @pl.when(i == 0)
    def _():
        gather(dest_ref, slot)

    @pl.when(i + 1 < pl.num_programs(0))
    def _():
        gather(dest_nx_ref, 1 - slot)

    for k in range(TOP_K):
        pltpu.make_async_copy(ys_ref.at[pl.ds(0, tile), :], buf.at[slot, k], sem.at[slot]).wait()

    d = o_ref.shape[-1]
    w0 = jnp.tile(_column(wts_ref[0:1, :]), (1, d // LANES))
    w1 = jnp.tile(_column(wts_ref[1:2, :]), (1, d // LANES))
    gate2 = mod_ref[0, 5:6, :]
    o_ref[0] = x1_ref[0] + gate2 * (w0 * buf[slot, 0] + w1 * buf[slot, 1])


def _combine(dest, wts, x1, mod, ys):
    b, s, d = x1.shape
    w = ys.shape[1]
    t = min(COMBINE_TILE, s)
    nt = s // t
    last = b * nt - 1
    kern = functools.partial(_combine_kernel, tile=t)
    smem_now = pl.BlockSpec((TOP_K, t), lambda i: (0, i), memory_space=pltpu.SMEM)
    smem_next = pl.BlockSpec((TOP_K, t), lambda i: (0, jnp.minimum(i + 1, last)),
                             memory_space=pltpu.SMEM)
    return pl.pallas_call(
        kern,
        grid=(b * nt,),
        in_specs=[smem_now, smem_next,
                  pl.BlockSpec((TOP_K, t), lambda i: (0, i)),
                  pl.BlockSpec((1, t, d), lambda i: (i // nt, i % nt, 0)),
                  pl.BlockSpec((1,) + mod.shape[1:], lambda i: (i // nt, 0, 0)),
                  pl.BlockSpec(memory_space=pl.ANY)],
        out_specs=pl.BlockSpec((1, t, d), lambda i: (i // nt, i % nt, 0)),
        out_shape=jax.ShapeDtypeStruct((b, s, d), F32),
        scratch_shapes=[pltpu.VMEM((2, TOP_K, t, w), F32), pltpu.SemaphoreType.DMA((2,))],
        compiler_params=_cparams("arbitrary"),
        name="combine",
    )(dest, dest, wts, x1, mod, ys)


def _layer(x, c, pos_row, w_ada, b_ada, norm1_g, w_in, q_a_norm, w_q_b, kv_a_norm, w_kv_b, q_norm,
           k_norm, conv_w, conv_b, lru_wa, lru_ba, lru_wx, lru_bx, lru_lambda, attn_out_norm,
           lru_out_norm, w_out, norm2_g, w_rg, b_rg, w_re, b_re, w_gate, w_up, w_down):
    b, s, d = x.shape
    n = b * s
    q_lora = q_a_norm.shape[0]
    kv_lora = kv_a_norm.shape[0]
    lru_w = lru_lambda.shape[0]
    attn_w = ATTN_HEADS * V_HEAD_DIM
    n_experts = w_gate.shape[0]
    row = lambda v: v.reshape(1, -1)

    mod = _adaln(c, w_ada, b_ada).reshape(b, 6, d)

    o1 = q_lora
    o2 = o1 + kv_lora
    o3 = o2 + QK_ROPE_DIM
    o4 = o3 + lru_w
    w_in_p = jnp.concatenate(
        [w_in[:, :o2], w_in[:, o3:o4], w_in[:, o4:], w_in[:, o2:o3],
         jnp.zeros((d, LANES - QK_ROPE_DIM), w_in.dtype)], axis=1).astype(BF16)
    pad_h = QK_PAD_DIM - QK_HEAD_DIM
    w_q_p = jnp.pad(w_q_b.reshape(q_lora, ATTN_HEADS, QK_HEAD_DIM),
                    ((0, 0), (0, 0), (0, pad_h))).reshape(q_lora, ATTN_HEADS * QK_PAD_DIM).astype(BF16)
    q_norm_p = row(jnp.pad(q_norm, (0, pad_h)))
    k_norm_nope = row(k_norm[:QK_NOPE_DIM])
    k_norm_rope_p = row(jnp.pad(k_norm[QK_NOPE_DIM:], (0, LANES - QK_ROPE_DIM)))
    inv = 1.0 / (ROPE_THETA ** (np.arange(0, QK_ROPE_DIM, 2, dtype=np.float32) / QK_ROPE_DIM))
    inv_col = jnp.asarray(inv.reshape(-1, 1))

    q, k, v, x_lru, gelu_g = _in_proj(
        x, pos_row, mod, row(norm1_g), w_in_p, row(q_a_norm), w_q_p, row(kv_a_norm),
        w_kv_b.astype(BF16), q_norm_p, k_norm_nope, k_norm_rope_p, inv_col,
        q_lora=q_lora, kv_lora=kv_lora, lru_w=lru_w)

    attn = _attention(q, k, v)

    w_ax = jnp.concatenate([lru_wa, lru_wx], axis=-1).astype(BF16)
    lru_n = _rg_lru(x_lru, gelu_g, pos_row, conv_w, row(conv_b), w_ax, row(lru_ba), row(lru_bx),
                    row(lru_lambda), row(lru_out_norm))

    group_row0, expert_row0 = 0, SUBLANES
    w_r_t = jnp.concatenate([w_rg.T, jnp.zeros((expert_row0 - N_GROUPS, d), w_rg.dtype), w_re.T],
                            axis=0).astype(BF16)
    b_r_col = jnp.concatenate([b_rg, jnp.zeros((expert_row0 - N_GROUPS,), b_rg.dtype),
                               b_re]).reshape(-1, 1)
    x1, ids, ranks, wts, cnt = _out_route(
        x, attn, lru_n, mod, row(attn_out_norm), w_out[:attn_w].astype(BF16),
        w_out[attn_w:].astype(BF16), row(norm2_g), w_r_t, b_r_col,
        n_experts=n_experts, group_row0=group_row0, expert_row0=expert_row0)

    blk = EXPERT_BLOCK
    counts = cnt[:, 0].astype(jnp.int32)
    padded = (counts + blk - 1) // blk * blk
    pad_ends = jnp.cumsum(padded)
    starts = (pad_ends - padded).astype(jnp.int32)
    dest = _slots(starts, ids.reshape(-1, LANES), ranks.reshape(-1, LANES)).reshape(ids.shape)
    cap = (n * TOP_K + n_experts * (blk - 1) + blk - 1) // blk * blk
    n_blocks = cap // blk
    block_first_row = jnp.arange(n_blocks, dtype=jnp.int32) * blk
    block_e = jnp.minimum(jnp.sum(pad_ends[None, :] <= block_first_row[:, None], axis=1),
                          n_experts - 1).astype(jnp.int32)
    n_used = (pad_ends[-1:] // blk).astype(jnp.int32)

    xs = _dispatch(counts, starts, n_used, dest, x1, mod, row(norm2_g), cap,
                   block=blk, n_experts=n_experts)
    ys = _experts(block_e, n_used, xs, w_gate, w_up, w_down, block=blk)
    return _combine(dest, wts, x1, mod, ys)


def kernel(x, c, positions, w_ada, b_ada, norm1_g, w_in, q_a_norm, w_q_b, kv_a_norm, w_kv_b, q_norm, k_norm, conv_w, conv_b, lru_wa, lru_ba, lru_wx, lru_bx, lru_lambda, attn_out_norm, lru_out_norm, w_out, norm2_g, w_router_group, b_router_group, w_router_expert, b_router_expert, w_gate, w_up, w_down):
    b, s, _ = x.shape
    pos_row = positions.reshape(b, 1, s)
    for l in range(w_ada.shape[0]):
        x = _layer(x, c, pos_row, w_ada[l], b_ada[l], norm1_g[l], w_in[l], q_a_norm[l], w_q_b[l],
                   kv_a_norm[l], w_kv_b[l], q_norm[l], k_norm[l], conv_w[l], conv_b[l], lru_wa[l],
                   lru_ba[l], lru_wx[l], lru_bx[l], lru_lambda[l], attn_out_norm[l], lru_out_norm[l],
                   w_out[l], norm2_g[l], w_router_group[l], b_router_group[l], w_router_expert[l],
                   b_router_expert[l], w_gate[l], w_up[l], w_down[l])
    return x
```
